```python
import jax, jax.numpy as jnp
from jax import lax
import numpy as np

D_MODEL = 1024
BATCH = 2
SEQ = 8192
DEPTH = 1
DEC_BATCH = 128
DEC_SEQ = 1
PAST_LEN = 16384
PAGE_SIZE = 128

HEAD_DIM = 64
N_Q_HEADS = 8
N_KV_HEADS = 2
Q_PER_KV = N_Q_HEADS // N_KV_HEADS
WINDOW = 128
ATTN_BLOCK = 128
ROPE_THETA = 10000.0
SCALE = HEAD_DIM ** -0.5
NEG_INF = -1e30
D_RNN = 1280
N_RNN_BLOCKS = 10
RNN_BLOCK = D_RNN // N_RNN_BLOCKS
CONV_W = 4
LRU_C = 8.0
N_GROUPS = 4
EXPERTS_PER_GROUP = 8
N_EXPERTS = N_GROUPS * EXPERTS_PER_GROUP
TOP_K_IN_GROUP = 2
D_EXPERT = 256
MOE_TOKEN_BLOCK = 512
EPS = 1e-6
D_Q = N_Q_HEADS * HEAD_DIM
D_KV = N_KV_HEADS * HEAD_DIM
D_IN = D_Q + 2 * D_KV + 2 * D_RNN + 2 * D_MODEL

kernel_name = "hybrid_swa_rglru_hmoe_step"


def rms_norm(x, g):
    x32 = x.astype(jnp.float32)
    inv = lax.rsqrt(jnp.mean(x32 * x32, axis=-1, keepdims=True) + EPS)
    return (x32 * inv * g.astype(jnp.float32)).astype(x.dtype)


def rope(x, pos):
    half = HEAD_DIM // 2
    inv_freq = ROPE_THETA ** (-jnp.arange(half, dtype=jnp.float32) / half)
    ang = pos[:, None] * inv_freq[None, :]
    cos = jnp.cos(ang)[:, None, :]
    sin = jnp.sin(ang)[:, None, :]
    x32 = x.astype(jnp.float32)
    x1, x2 = x32[..., :half], x32[..., half:]
    return jnp.concatenate([x1 * cos - x2 * sin, x2 * cos + x1 * sin], axis=-1).astype(x.dtype)


def sink_softmax(scores, mask, sink):
    scores = jnp.where(mask, scores, NEG_INF)
    m = jnp.maximum(jnp.max(scores, axis=-1, keepdims=True), sink)
    p = jnp.exp(scores - m)
    denom = jnp.sum(p, axis=-1, keepdims=True) + jnp.exp(sink - m)
    return p / denom


def window_attn_prompt(q, k, v, sinks):
    B, T = q.shape[:2]
    nb = T // ATTN_BLOCK
    qb = q.reshape(B, nb, ATTN_BLOCK, N_KV_HEADS, Q_PER_KV, HEAD_DIM)
    pad = ((0, 0), (ATTN_BLOCK, 0), (0, 0), (0, 0))
    kb = jnp.pad(k, pad).reshape(B, nb + 1, ATTN_BLOCK, N_KV_HEADS, HEAD_DIM)
    vb = jnp.pad(v, pad).reshape(B, nb + 1, ATTN_BLOCK, N_KV_HEADS, HEAD_DIM)
    k2 = jnp.concatenate([kb[:, :-1], kb[:, 1:]], axis=2)
    v2 = jnp.concatenate([vb[:, :-1], vb[:, 1:]], axis=2)
    blk = jnp.arange(nb)[:, None] * ATTN_BLOCK
    qpos = blk + jnp.arange(ATTN_BLOCK)[None, :]
    kpos = blk - ATTN_BLOCK + jnp.arange(2 * ATTN_BLOCK)[None, :]
    diff = qpos[:, :, None] - kpos[:, None, :]
    mask = (diff >= 0) & (diff < WINDOW) & (kpos[:, None, :] >= 0)
    scores = jnp.einsum("bnqhgd,bnkhd->bnhgqk", qb, k2, preferred_element_type=jnp.float32) * SCALE
    sink = sinks.astype(jnp.float32).reshape(N_KV_HEADS, Q_PER_KV)[None, None, :, :, None, None]
    probs = sink_softmax(scores, mask[None, :, None, None], sink)
    out = jnp.einsum("bnhgqk,bnkhd->bnqhgd", probs.astype(v.dtype), v2)
    return out.reshape(B, T, D_Q)


def window_attn_sample(q, k_new, v_new, k_buf, v_buf, sinks):
    Bd, S = q.shape[:2]
    k_all = jnp.concatenate([k_buf.astype(k_new.dtype), k_new], axis=1)
    v_all = jnp.concatenate([v_buf.astype(v_new.dtype), v_new], axis=1)
    qpos = PAST_LEN + jnp.arange(S)
    kpos = PAST_LEN - WINDOW + jnp.arange(WINDOW + S)
    diff = qpos[:, None] - kpos[None, :]
    mask = (diff >= 0) & (diff < WINDOW)
    qg = q.reshape(Bd, S, N_KV_HEADS, Q_PER_KV, HEAD_DIM)
    scores = jnp.einsum("bqhgd,bkhd->bhgqk", qg, k_all, preferred_element_type=jnp.float32) * SCALE
    sink = sinks.astype(jnp.float32).reshape(N_KV_HEADS, Q_PER_KV)[None, :, :, None, None]
    probs = sink_softmax(scores, mask[None, None, None], sink)
    out = jnp.einsum("bhgqk,bkhd->bqhgd", probs.astype(v_all.dtype), v_all)
    return (out.reshape(Bd, S, D_Q),
            k_all[:, -WINDOW:].astype(k_buf.dtype), v_all[:, -WINDOW:].astype(v_buf.dtype))


def causal_conv(x, prev, w, b):
    T = x.shape[1]
    xp = jnp.concatenate([prev.astype(x.dtype), x], axis=1)
    y = b + sum(w[j] * xp[:, j:j + T] for j in range(CONV_W))
    return y, xp[:, -(CONV_W - 1):].astype(prev.dtype)


def block_diag_linear(x, w, b):
    xb = x.reshape(x.shape[:-1] + (N_RNN_BLOCKS, RNN_BLOCK))
    y = jnp.einsum("btni,nio->btno", xb, w)
    return y.reshape(x.shape) + b


def lru_combine(left, right):
    a1, b1 = left
    a2, b2 = right
    return a1 * a2, a2 * b1 + b2


def rg_lru(x, h0, w_a, b_a, w_i, b_i, lam):
    x32 = x.astype(jnp.float32)
    r = jax.nn.sigmoid(block_diag_linear(x, w_a, b_a).astype(jnp.float32))
    i = jax.nn.sigmoid(block_diag_linear(x, w_i, b_i).astype(jnp.float32))
    log_a = -LRU_C * r * jax.nn.softplus(-lam.astype(jnp.float32))
    a = jnp.exp(log_a)
    bterm = jnp.sqrt(-jnp.expm1(2.0 * log_a)) * (i * x32)
    a_cum, b_cum = lax.associative_scan(lru_combine, (a, bterm), axis=1)
    h = a_cum * h0.astype(jnp.float32)[:, None, :] + b_cum
    return h.astype(x.dtype), h[:, -1].astype(h0.dtype)


def routed_experts(x, combine, w_gate, w_up, w_down):
    n = x.shape[0]
    nb = -(-n // MOE_TOKEN_BLOCK)
    pad = nb * MOE_TOKEN_BLOCK - n
    xb = jnp.pad(x, ((0, pad), (0, 0))).reshape(nb, MOE_TOKEN_BLOCK, D_MODEL)
    cb = jnp.pad(combine, ((0, pad), (0, 0))).reshape(nb, MOE_TOKEN_BLOCK, N_EXPERTS)

    def one_block(args):
        xt, ct = args
        hg = jnp.einsum("nd,edf->nef", xt, w_gate)
        hu = jnp.einsum("nd,edf->nef", xt, w_up)
        h = jax.nn.silu(hg) * hu * ct[..., None].astype(xt.dtype)
        return jnp.einsum("nef,efd->nd", h, w_down)

    out = lax.map(one_block, (xb, cb))
    return out.reshape(nb * MOE_TOKEN_BLOCK, D_MODEL)[:n]


def hier_moe(x, w_rg, b_rg, w_re, b_re, w_gate, w_up, w_down):
    n = x.shape[0]
    gl = jnp.dot(x, w_rg).astype(jnp.float32) + b_rg.astype(jnp.float32)
    g = jnp.argmax(gl, axis=-1)
    p_g = jnp.take_along_axis(jax.nn.softmax(gl, axis=-1), g[:, None], axis=-1)
    el = (jnp.dot(x, w_re).astype(jnp.float32) + b_re.astype(jnp.float32)).reshape(n, N_GROUPS, EXPERTS_PER_GROUP)
    el_g = jnp.take_along_axis(el, g[:, None, None], axis=1)[:, 0]
    top_v, top_i = lax.top_k(el_g, TOP_K_IN_GROUP)
    w_k = jax.nn.softmax(top_v, axis=-1) * p_g
    eidx = g[:, None] * EXPERTS_PER_GROUP + top_i
    combine = jnp.einsum("nk,nke->ne", w_k, jax.nn.one_hot(eidx, N_EXPERTS, dtype=jnp.float32))
    return routed_experts(x, combine, w_gate, w_up, w_down)


def trunk_layer(x, pos0, k_buf, v_buf, conv_prev, h_prev, p):
    B, T = x.shape[:2]
    xn = rms_norm(x, p["attn_norm_g"])
    splits = [int(s) for s in np.cumsum([D_Q, D_KV, D_KV, D_RNN, D_RNN, D_MODEL])]
    q, k, v, xr, yr, ga, gr = jnp.split(jnp.dot(xn, p["w_in"]), splits, axis=-1)
    pos = pos0 + jnp.arange(T, dtype=jnp.float32)
    q = rope(rms_norm(q.reshape(B, T, N_Q_HEADS, HEAD_DIM), p["q_norm_g"]), pos)
    k = rope(rms_norm(k.reshape(B, T, N_KV_HEADS, HEAD_DIM), p["k_norm_g"]), pos)
    v = v.reshape(B, T, N_KV_HEADS, HEAD_DIM)
    if k_buf is None:
        attn = window_attn_prompt(q, k, v, p["attn_sinks"])
        new_k, new_v = k[:, -WINDOW:], v[:, -WINDOW:]
    else:
        attn, new_k, new_v = window_attn_sample(q, k, v, k_buf, v_buf, p["attn_sinks"])
    xc, new_conv = causal_conv(xr, conv_prev, p["conv_w"], p["conv_b"])
    h, new_h = rg_lru(xc, h_prev, p["w_lru_a"], p["b_lru_a"], p["w_lru_i"], p["b_lru_i"], p["lru_lambda"])
    rnn = h * jax.nn.gelu(yr)
    merged = (jax.nn.sigmoid(ga) * jnp.dot(attn, p["w_br_attn"])
              + jax.nn.sigmoid(gr) * jnp.dot(rnn, p["w_br_rnn"]))
    x = x + jnp.dot(merged, p["w_out"])
    xn2 = rms_norm(x, p["ffn_norm_g"]).reshape(B * T, D_MODEL)
    moe = hier_moe(xn2, p["w_route_group"], p["b_route_group"], p["w_route_expert"], p["b_route_expert"],
                   p["w_exp_gate"], p["w_exp_up"], p["w_exp_down"])
    x = x + moe.reshape(B, T, D_MODEL)
    return x, new_k, new_v, new_conv, new_h


def setup_inputs(seed: int = 0) -> dict:
    key = jax.random.key(seed)
    ks = jax.random.split(key, 32)
    f32 = jnp.float32

    def nrm(k, shape, s):
        return s * jax.random.normal(k, shape, f32)

    def gain(k, shape):
        return 1.0 + 0.02 * jax.random.normal(k, shape, f32)

    u = jax.random.uniform(ks[14], (DEPTH, D_RNN), f32, 0.9, 0.999)
    a0 = u ** (1.0 / LRU_C)
    lru_lambda = jnp.log(a0) - jnp.log1p(-a0)
    return {
        "x_prompt": nrm(ks[0], (BATCH, SEQ, D_MODEL), 1.0),
        "x_sample": nrm(ks[1], (DEC_BATCH, DEC_SEQ, D_MODEL), 1.0),
        "cache_k_win": nrm(ks[2], (DEPTH, DEC_BATCH, WINDOW, N_KV_HEADS, HEAD_DIM), 1.0),
        "cache_v_win": nrm(ks[3], (DEPTH, DEC_BATCH, WINDOW, N_KV_HEADS, HEAD_DIM), 1.0),
        "state_conv": nrm(ks[4], (DEPTH, DEC_BATCH, CONV_W - 1, D_RNN), 1.0),
        "state_lru_h": nrm(ks[5], (DEPTH, DEC_BATCH, D_RNN), 0.5),
        "attn_norm_g": gain(ks[6], (DEPTH, D_MODEL)),
        "w_in": nrm(ks[7], (DEPTH, D_MODEL, D_IN), D_MODEL ** -0.5),
        "q_norm_g": gain(ks[8], (DEPTH, HEAD_DIM)),
        "k_norm_g": gain(ks[9], (DEPTH, HEAD_DIM)),
        "attn_sinks": nrm(ks[10], (DEPTH, N_Q_HEADS), 0.5),
        "conv_w": nrm(ks[11], (DEPTH, CONV_W, D_RNN), CONV_W ** -0.5),
        "conv_b": nrm(ks[12], (DEPTH, D_RNN), 0.01),
        "w_lru_a": nrm(ks[13], (DEPTH, N_RNN_BLOCKS, RNN_BLOCK, RNN_BLOCK), RNN_BLOCK ** -0.5),
        "b_lru_a": nrm(ks[15], (DEPTH, D_RNN), 0.01),
        "w_lru_i": nrm(ks[16], (DEPTH, N_RNN_BLOCKS, RNN_BLOCK, RNN_BLOCK), RNN_BLOCK ** -0.5),
        "b_lru_i": nrm(ks[17], (DEPTH, D_RNN), 0.01),
        "lru_lambda": lru_lambda,
        "w_br_attn": nrm(ks[18], (DEPTH, D_Q, D_MODEL), D_Q ** -0.5),
        "w_br_rnn": nrm(ks[19], (DEPTH, D_RNN, D_MODEL), D_RNN ** -0.5),
        "w_out": nrm(ks[20], (DEPTH, D_MODEL, D_MODEL), D_MODEL ** -0.5),
        "ffn_norm_g": gain(ks[21], (DEPTH, D_MODEL)),
        "w_route_group": nrm(ks[22], (DEPTH, D_MODEL, N_GROUPS), D_MODEL ** -0.5),
        "b_route_group": nrm(ks[23], (DEPTH, N_GROUPS), 0.01),
        "w_route_expert": nrm(ks[24], (DEPTH, D_MODEL, N_EXPERTS), D_MODEL ** -0.5),
        "b_route_expert": nrm(ks[25], (DEPTH, N_EXPERTS), 0.01),
        "w_exp_gate": nrm(ks[26], (DEPTH, N_EXPERTS, D_MODEL, D_EXPERT), D_MODEL ** -0.5),
        "w_exp_up": nrm(ks[27], (DEPTH, N_EXPERTS, D_MODEL, D_EXPERT), D_MODEL ** -0.5),
        "w_exp_down": nrm(ks[28], (DEPTH, N_EXPERTS, D_EXPERT, D_MODEL), D_EXPERT ** -0.5),
    }


def reference(x_prompt, x_sample, cache_k_win, cache_v_win, state_conv, state_lru_h,
              attn_norm_g, w_in, q_norm_g, k_norm_g, attn_sinks, conv_w, conv_b,
              w_lru_a, b_lru_a, w_lru_i, b_lru_i, lru_lambda, w_br_attn, w_br_rnn, w_out,
              ffn_norm_g, w_route_group, b_route_group, w_route_expert, b_route_expert,
              w_exp_gate, w_exp_up, w_exp_down):
    xp, xs = x_prompt, x_sample
    kp_l, vp_l, cp_l, hp_l = [], [], [], []
    ks_l, vs_l, cs_l, hs_l = [], [], [], []
    for l in range(DEPTH):
        p = {
            "attn_norm_g": attn_norm_g[l], "w_in": w_in[l], "q_norm_g": q_norm_g[l],
            "k_norm_g": k_norm_g[l], "attn_sinks": attn_sinks[l], "conv_w": conv_w[l],
            "conv_b": conv_b[l], "w_lru_a": w_lru_a[l], "b_lru_a": b_lru_a[l],
            "w_lru_i": w_lru_i[l], "b_lru_i": b_lru_i[l], "lru_lambda": lru_lambda[l],
            "w_br_attn": w_br_attn[l], "w_br_rnn": w_br_rnn[l], "w_out": w_out[l],
            "ffn_norm_g": ffn_norm_g[l], "w_route_group": w_route_group[l],
            "b_route_group": b_route_group[l], "w_route_expert": w_route_expert[l],
            "b_route_expert": b_route_expert[l], "w_exp_gate": w_exp_gate[l],
            "w_exp_up": w_exp_up[l], "w_exp_down": w_exp_down[l],
        }
        conv0 = jnp.zeros((BATCH, CONV_W - 1, D_RNN), state_conv.dtype)
        h0 = jnp.zeros((BATCH, D_RNN), state_lru_h.dtype)
        xp, kp, vp, cp, hp = trunk_layer(xp, 0, None, None, conv0, h0, p)
        xs, kss, vss, css, hss = trunk_layer(xs, PAST_LEN, cache_k_win[l], cache_v_win[l],
                                             state_conv[l], state_lru_h[l], p)
        kp_l.append(kp); vp_l.append(vp); cp_l.append(cp); hp_l.append(hp)
        ks_l.append(kss); vs_l.append(vss); cs_l.append(css); hs_l.append(hss)
    return (xp, xs,
            jnp.stack(kp_l), jnp.stack(vp_l), jnp.stack(cp_l), jnp.stack(hp_l),
            jnp.stack(ks_l), jnp.stack(vs_l), jnp.stack(cs_l), jnp.stack(hs_l))
```

```python
import functools

import jax
import jax.numpy as jnp
from jax import lax
from jax.experimental import pallas as pl
from jax.experimental.pallas import tpu as pltpu

D_MODEL = 1024
HEAD_DIM = 64
N_Q_HEADS = 8
N_KV_HEADS = 2
Q_PER_KV = N_Q_HEADS // N_KV_HEADS
WINDOW = 128
ATTN_BLOCK = 128
ROPE_THETA = 10000.0
SCALE = HEAD_DIM ** -0.5
NEG_INF = -1e30
D_RNN = 1280
N_RNN_BLOCKS = 10
RNN_BLOCK = D_RNN // N_RNN_BLOCKS
CONV_W = 4
LRU_C = 8.0
N_GROUPS = 4
EXPERTS_PER_GROUP = 8
N_EXPERTS = N_GROUPS * EXPERTS_PER_GROUP
D_EXPERT = 256
PAST_LEN = 16384
EPS = 1e-6
D_Q = N_Q_HEADS * HEAD_DIM
D_KV = N_KV_HEADS * HEAD_DIM
D_IN = D_Q + 2 * D_KV + 2 * D_RNN + 2 * D_MODEL
OFF_K = D_Q
OFF_V = OFF_K + D_KV
OFF_XR = OFF_V + D_KV
OFF_YR = OFF_XR + D_RNN
OFF_GA = OFF_YR + D_RNN
OFF_GR = OFF_GA + D_MODEL

LANES = 128
SUBLANES = 8
VMEM_LIMIT = 56 * 1024 * 1024

F32 = jnp.float32
BF16 = jnp.bfloat16


def _params(*sem):
    return pltpu.CompilerParams(dimension_semantics=sem, vmem_limit_bytes=VMEM_LIMIT)


def _sigmoid(x):
    return 1.0 / (1.0 + jnp.exp(-x))


def _gelu_tanh(x):
    c = 0.7978845608028654
    return 0.5 * x * (1.0 + jnp.tanh(c * (x + 0.044715 * (x * x * x))))


def _full(shape, single_buffer=False):
    index_map = lambda *_: (0,) * len(shape)
    if single_buffer:
        return pl.BlockSpec(shape, index_map, pipeline_mode=pl.Buffered(1))
    return pl.BlockSpec(shape, index_map)


def _mm(a, b, precise, dims=None):
    if precise:
        a, b, prec = a.astype(F32), b.astype(F32), lax.Precision.HIGHEST
    else:
        a, b, prec = a.astype(BF16), b.astype(BF16), None
    if dims is None:
        return jnp.dot(a, b, preferred_element_type=F32, precision=prec)
    return lax.dot_general(a, b, dims, preferred_element_type=F32, precision=prec)


_NT = (((1,), (1,)), ((), ()))


def _proj_kernel(x_ref, g_ref, w_ref, cos_ref, sin_ref, qkg_ref,
                 q_ref, k_ref, v_ref, xr_ref, gy_ref, sga_ref, sgr_ref, *, precise):
    x = x_ref[...]
    inv = lax.rsqrt(jnp.mean(x * x, axis=-1, keepdims=True) + EPS)
    xn = x * inv * g_ref[...]
    if not precise:
        xn = xn.astype(BF16)

    def proj(lo, hi):
        return _mm(xn, w_ref[:, lo:hi], precise)

    qk = proj(0, OFF_V)
    tm = qk.shape[0]
    lane = lax.broadcasted_iota(jnp.int32, (tm, LANES), 1)
    lo_head = lane < HEAD_DIM
    first_half = (lane % HEAD_DIM) < (HEAD_DIM // 2)
    cos = cos_ref[...]
    sin = sin_ref[...]
    for g in range(OFF_V // LANES):
        seg = qk[:, g * LANES:(g + 1) * LANES]
        sq = seg * seg
        s_lo = jnp.sum(jnp.where(lo_head, sq, 0.0), axis=-1, keepdims=True)
        s_hi = jnp.sum(jnp.where(lo_head, 0.0, sq), axis=-1, keepdims=True)
        ms = jnp.where(lo_head, s_lo, s_hi) * (1.0 / HEAD_DIM)
        normed = seg * lax.rsqrt(ms + EPS) * qkg_ref[:, g * LANES:(g + 1) * LANES]
        partner = jnp.where(first_half,
                            pltpu.roll(normed, LANES - HEAD_DIM // 2, axis=1),
                            pltpu.roll(normed, HEAD_DIM // 2, axis=1))
        roped = normed * cos + partner * sin
        if g < D_Q // LANES:
            q_ref[:, g * LANES:(g + 1) * LANES] = (roped * SCALE).astype(q_ref.dtype)
        else:
            k_ref[...] = roped
    v_ref[...] = proj(OFF_V, OFF_XR)
    xr_ref[...] = proj(OFF_XR, OFF_YR)
    gy_ref[...] = _gelu_tanh(proj(OFF_YR, OFF_GA)).astype(gy_ref.dtype)
    sga_ref[...] = _sigmoid(proj(OFF_GA, OFF_GR)).astype(sga_ref.dtype)
    sgr_ref[...] = _sigmoid(proj(OFF_GR, D_IN)).astype(sgr_ref.dtype)


def _proj(x, g, w_in, cos_t, sin_t, qkg, tm, precise):
    n = x.shape[0]
    t_blocks = cos_t.shape[0] // tm
    row = lambda i: (i, 0)
    act = F32 if precise else BF16
    out_shape = (
        jax.ShapeDtypeStruct((n, D_Q), act),
        jax.ShapeDtypeStruct((n, D_KV), F32),
        jax.ShapeDtypeStruct((n, D_KV), F32),
        jax.ShapeDtypeStruct((n, D_RNN), F32),
        jax.ShapeDtypeStruct((n, D_RNN), act),
        jax.ShapeDtypeStruct((n, D_MODEL), act),
        jax.ShapeDtypeStruct((n, D_MODEL), act),
    )
    return pl.pallas_call(
        functools.partial(_proj_kernel, precise=precise),
        grid=(n // tm,),
        in_specs=[
            pl.BlockSpec((tm, D_MODEL), row),
            _full((1, D_MODEL)),
            _full((D_MODEL, D_IN), single_buffer=True),
            pl.BlockSpec((tm, LANES), lambda i: (i % t_blocks, 0)),
            pl.BlockSpec((tm, LANES), lambda i: (i % t_blocks, 0)),
            _full((1, OFF_V)),
        ],
        out_specs=[pl.BlockSpec((tm, s.shape[1]), row) for s in out_shape],
        out_shape=out_shape,
        compiler_params=_params("parallel"),
        name="proj",
    )(x, g, w_in, cos_t, sin_t, qkg)


def _softmax_pv(s, sink, v2, precise):
    m = jnp.maximum(jnp.max(s, axis=-1, keepdims=True), sink)
    p = jnp.exp(s - m)
    denom = jnp.sum(p, axis=-1, keepdims=True) + jnp.exp(sink - m)
    return _mm(p, v2, precise) * (1.0 / denom)


def _sink_column(sink_ref, h, rows, rows_per_head):
    r = lax.broadcasted_iota(jnp.int32, (rows, 1), 0) // rows_per_head
    col = jnp.full((rows, 1), sink_ref[h * Q_PER_KV], F32)
    for g in range(1, Q_PER_KV):
        col = jnp.where(r == g, sink_ref[h * Q_PER_KV + g], col)
    return col


def _attn_prompt_kernel(sink_ref, q_ref, kc_ref, kp_ref, vc_ref, vp_ref, o_ref):
    n = pl.program_id(1)
    blk = ATTN_BLOCK
    q = q_ref[...]
    k2 = jnp.concatenate([kp_ref[...], kc_ref[...]], axis=0).astype(BF16)
    v2 = jnp.concatenate([vp_ref[...], vc_ref[...]], axis=0).astype(BF16)
    rows = Q_PER_KV * blk
    i = lax.broadcasted_iota(jnp.int32, (rows, 2 * blk), 0) % blk
    j = lax.broadcasted_iota(jnp.int32, (rows, 2 * blk), 1)
    d = j - i
    dmin = jnp.where(n > 0, 1, jnp.maximum(1, blk - i))
    valid = (d >= dmin) & (d <= WINDOW)
    for h in range(N_KV_HEADS):
        kh = k2[:, h * HEAD_DIM:(h + 1) * HEAD_DIM]
        vh = v2[:, h * HEAD_DIM:(h + 1) * HEAD_DIM]
        qs = jnp.concatenate(
            [q[:, (h * Q_PER_KV + g) * HEAD_DIM:(h * Q_PER_KV + g + 1) * HEAD_DIM]
             for g in range(Q_PER_KV)], axis=0)
        s = _mm(qs, kh, False, _NT)
        s = jnp.where(valid, s, NEG_INF)
        o = _softmax_pv(s, _sink_column(sink_ref, h, rows, blk), vh, False)
        for g in range(Q_PER_KV):
            c = (h * Q_PER_KV + g) * HEAD_DIM
            o_ref[:, c:c + HEAD_DIM] = o[g * blk:(g + 1) * blk].astype(o_ref.dtype)


def _attn_prompt(q, k, v, sinks, batch, seq):
    nb = seq // ATTN_BLOCK
    cur = lambda b, n: (b * nb + n, 0)
    prev = lambda b, n: (b * nb + jnp.maximum(n - 1, 0), 0)
    return pl.pallas_call(
        _attn_prompt_kernel,
        grid=(batch, nb),
        in_specs=[
            pl.BlockSpec(memory_space=pltpu.SMEM),
            pl.BlockSpec((ATTN_BLOCK, D_Q), cur),
            pl.BlockSpec((ATTN_BLOCK, D_KV), cur),
            pl.BlockSpec((ATTN_BLOCK, D_KV), prev),
            pl.BlockSpec((ATTN_BLOCK, D_KV), cur),
            pl.BlockSpec((ATTN_BLOCK, D_KV), prev),
        ],
        out_specs=pl.BlockSpec((ATTN_BLOCK, D_Q), cur),
        out_shape=jax.ShapeDtypeStruct((batch * seq, D_Q), BF16),
        compiler_params=_params("parallel", "parallel"),
        name="attn_prompt",
    )(sinks, q, k, k, v, v)


SAMPLE_BT = 8


def _attn_sample_kernel(sink_ref, q_ref, kn_ref, vn_ref, kc_ref, vc_ref, o_ref, ko_ref, vo_ref):
    bt = SAMPLE_BT
    w = lax.broadcasted_iota(jnp.int32, (bt, WINDOW, D_KV), 1)

    def shifted(cache_ref, new_ref):
        rolled = pltpu.roll(cache_ref[...], WINDOW - 1, axis=1)
        return jnp.where(w == WINDOW - 1, new_ref[...][:, None, :], rolled)

    k_win = shifted(kc_ref, kn_ref)
    v_win = shifted(vc_ref, vn_ref)
    ko_ref[...] = k_win
    vo_ref[...] = v_win
    k2 = k_win.reshape(bt * WINDOW, D_KV)
    v2 = v_win.reshape(bt * WINDOW, D_KV)
    q = q_ref[...]
    rows = Q_PER_KV * bt
    rb = lax.broadcasted_iota(jnp.int32, (rows, bt * WINDOW), 0) % bt
    cb = lax.broadcasted_iota(jnp.int32, (rows, bt * WINDOW), 1) // WINDOW
    valid = rb == cb
    for h in range(N_KV_HEADS):
        kh = k2[:, h * HEAD_DIM:(h + 1) * HEAD_DIM]
        vh = v2[:, h * HEAD_DIM:(h + 1) * HEAD_DIM]
        qs = jnp.concatenate(
            [q[:, (h * Q_PER_KV + g) * HEAD_DIM:(h * Q_PER_KV + g + 1) * HEAD_DIM]
             for g in range(Q_PER_KV)], axis=0)
        s = _mm(qs, kh, True, _NT)
        s = jnp.where(valid, s, NEG_INF)
        o = _softmax_pv(s, _sink_column(sink_ref, h, rows, bt), vh, True)
        for g in range(Q_PER_KV):
            c = (h * Q_PER_KV + g) * HEAD_DIM
            o_ref[:, c:c + HEAD_DIM] = o[g * bt:(g + 1) * bt].astype(o_ref.dtype)


def _attn_sample(q, k_new, v_new, cache_k, cache_v, sinks):
    nbatch = q.shape[0]
    bt = SAMPLE_BT
    row = lambda i: (i, 0)
    win = lambda i: (i, 0, 0)
    return pl.pallas_call(
        _attn_sample_kernel,
        grid=(nbatch // bt,),
        in_specs=[
            pl.BlockSpec(memory_space=pltpu.SMEM),
            pl.BlockSpec((bt, D_Q), row),
            pl.BlockSpec((bt, D_KV), row),
            pl.BlockSpec((bt, D_KV), row),
            pl.BlockSpec((bt, WINDOW, D_KV), win),
            pl.BlockSpec((bt, WINDOW, D_KV), win),
        ],
        out_specs=[
            pl.BlockSpec((bt, D_Q), row),
            pl.BlockSpec((bt, WINDOW, D_KV), win),
            pl.BlockSpec((bt, WINDOW, D_KV), win),
        ],
        out_shape=(
            jax.ShapeDtypeStruct((nbatch, D_Q), F32),
            jax.ShapeDtypeStruct((nbatch, WINDOW, D_KV), F32),
            jax.ShapeDtypeStruct((nbatch, WINDOW, D_KV), F32),
        ),
        compiler_params=_params("parallel"),
        name="attn_sample",
    )(sinks, q, k_new, v_new, cache_k, cache_v)


def _lru_terms(xc, wcat_ref, ba_ref, bi_ref, lam_ref, precise):
    xcb = xc if precise else xc.astype(BF16)
    ya, yi = [], []
    for n in range(N_RNN_BLOCKS):
        y = _mm(xcb[:, n * RNN_BLOCK:(n + 1) * RNN_BLOCK], wcat_ref[n], precise)
        ya.append(y[:, :RNN_BLOCK])
        yi.append(y[:, RNN_BLOCK:])
    r = _sigmoid(jnp.concatenate(ya, axis=-1) + ba_ref[...])
    gate_i = _sigmoid(jnp.concatenate(yi, axis=-1) + bi_ref[...])
    neg_lam = -lam_ref[...]
    softplus = jnp.maximum(neg_lam, 0.0) + jnp.log1p(jnp.exp(-jnp.abs(neg_lam)))
    log_a = (-LRU_C * softplus) * r
    a = jnp.exp(log_a)
    b = jnp.sqrt(jnp.maximum(1.0 - a * a, 0.0)) * (gate_i * xc)
    return a, b


def _rnn_prompt_kernel(xr_ref, gy_ref, cw_ref, cb_ref, wcat_ref, ba_ref, bi_ref, lam_ref,
                       o_ref, hl_ref, xbuf, a_scr, b_scr, h_scr, hcar):
    tt = xr_ref.shape[0]
    groups = tt // SUBLANES

    @pl.when(pl.program_id(1) == 0)
    def _():
        xbuf[0:SUBLANES, :] = jnp.zeros((SUBLANES, D_RNN), F32)
        hcar[...] = jnp.zeros((SUBLANES, D_RNN), F32)

    x = xr_ref[...]
    xbuf[SUBLANES:, :] = x
    xc = cb_ref[...] + cw_ref[CONV_W - 1:CONV_W, :] * x
    for j in range(CONV_W - 1):
        s = CONV_W - 1 - j
        xc = xc + cw_ref[j:j + 1, :] * xbuf[SUBLANES - s:SUBLANES - s + tt, :]
    xbuf[0:SUBLANES, :] = x[tt - SUBLANES:, :]

    a, b = _lru_terms(xc, wcat_ref, ba_ref, bi_ref, lam_ref, False)
    a = a.reshape(groups, SUBLANES, D_RNN)
    b = b.reshape(groups, SUBLANES, D_RNN)
    step = lax.broadcasted_iota(jnp.int32, (groups, SUBLANES, D_RNN), 1)
    k = 1
    while k < SUBLANES:
        keep = step >= k
        a_sh = jnp.where(keep, pltpu.roll(a, k, axis=1), 1.0)
        b_sh = jnp.where(keep, pltpu.roll(b, k, axis=1), 0.0)
        b = a * b_sh + b
        a = a * a_sh
        k *= 2
    a_scr[...] = a
    b_scr[...] = b

    def chain(g, h_in):
        h = a_scr[g] * h_in + b_scr[g]
        h_scr[g] = h
        return jnp.broadcast_to(h[SUBLANES - 1:SUBLANES, :], (SUBLANES, D_RNN))

    h_last = lax.fori_loop(0, groups, chain, hcar[...])
    hcar[...] = h_last
    hl_ref[0] = h_last
    h = h_scr[...].reshape(tt, D_RNN)
    o_ref[...] = (h * gy_ref[...].astype(F32)).astype(o_ref.dtype)


def _rnn_prompt(xr, gy, conv_w, conv_b, wcat, b_a, b_i, lam, batch, seq, tt):
    nt = seq // tt
    row = lambda b, t: (b * nt + t, 0)
    groups = tt // SUBLANES
    return pl.pallas_call(
        _rnn_prompt_kernel,
        grid=(batch, nt),
        in_specs=[
            pl.BlockSpec((tt, D_RNN), row),
            pl.BlockSpec((tt, D_RNN), row),
            _full((CONV_W, D_RNN)),
            _full((1, D_RNN)),
            _full((N_RNN_BLOCKS, RNN_BLOCK, 2 * RNN_BLOCK)),
            _full((1, D_RNN)),
            _full((1, D_RNN)),
            _full((1, D_RNN)),
        ],
        out_specs=[
            pl.BlockSpec((tt, D_RNN), row),
            pl.BlockSpec((1, SUBLANES, D_RNN), lambda b, t: (b, 0, 0)),
        ],
        out_shape=(
            jax.ShapeDtypeStruct((batch * seq, D_RNN), BF16),
            jax.ShapeDtypeStruct((batch, SUBLANES, D_RNN), F32),
        ),
        scratch_shapes=[
            pltpu.VMEM((tt + SUBLANES, D_RNN), F32),
            pltpu.VMEM((groups, SUBLANES, D_RNN), F32),
            pltpu.VMEM((groups, SUBLANES, D_RNN), F32),
            pltpu.VMEM((groups, SUBLANES, D_RNN), F32),
            pltpu.VMEM((SUBLANES, D_RNN), F32),
        ],
        compiler_params=_params("parallel", "arbitrary"),
        name="rnn_prompt",
    )(xr, gy, conv_w, conv_b, wcat, b_a, b_i, lam)


def _rnn_sample_kernel(xr_ref, gy_ref, s0_ref, s1_ref, s2_ref, h_ref, cw_ref, cb_ref,
                       wcat_ref, ba_ref, bi_ref, lam_ref, o_ref, hn_ref):
    x = xr_ref[...]
    xc = (cb_ref[...] + cw_ref[0:1, :] * s0_ref[...] + cw_ref[1:2, :] * s1_ref[...]
          + cw_ref[2:3, :] * s2_ref[...] + cw_ref[3:4, :] * x)
    a, b = _lru_terms(xc, wcat_ref, ba_ref, bi_ref, lam_ref, True)
    h = a * h_ref[...] + b
    hn_ref[...] = h
    o_ref[...] = (h * gy_ref[...].astype(F32)).astype(o_ref.dtype)


def _rnn_sample(xr, gy, s0, s1, s2, h_prev, conv_w, conv_b, wcat, b_a, b_i, lam):
    n = xr.shape[0]
    act = _full((n, D_RNN))
    return pl.pallas_call(
        _rnn_sample_kernel,
        grid=(1,),
        in_specs=[act, act, act, act, act, act,
                  _full((CONV_W, D_RNN)), _full((1, D_RNN)),
                  _full((N_RNN_BLOCKS, RNN_BLOCK, 2 * RNN_BLOCK)),
                  _full((1, D_RNN)), _full((1, D_RNN)), _full((1, D_RNN))],
        out_specs=[act, act],
        out_shape=(jax.ShapeDtypeStruct((n, D_RNN), F32),
                   jax.ShapeDtypeStruct((n, D_RNN), F32)),
        compiler_params=_params("arbitrary"),
        name="rnn_sample",
    )(xr, gy, s0, s1, s2, h_prev, conv_w, conv_b, wcat, b_a, b_i, lam)


def _merge_kernel(x_ref, at_ref, rn_ref, sga_ref, sgr_ref, wa_ref, wr_ref, wo_ref, g_ref,
                  wrt_ref, brt_ref, x2_ref, xn_ref, cmb_ref, *, precise):
    ya = _mm(at_ref[...], wa_ref[...], precise)
    yr = _mm(rn_ref[...], wr_ref[...], precise)
    merged = sga_ref[...].astype(F32) * ya + sgr_ref[...].astype(F32) * yr
    x2 = x_ref[...] + _mm(merged, wo_ref[...], precise)
    x2_ref[...] = x2
    inv = lax.rsqrt(jnp.mean(x2 * x2, axis=-1, keepdims=True) + EPS)
    xn = x2 * inv * g_ref[...]
    xn_ref[...] = xn.astype(xn_ref.dtype)

    logits = _mm(xn, wrt_ref[...], precise) + brt_ref[...]
    tm = logits.shape[0]
    lane = lax.broadcasted_iota(jnp.int32, (tm, LANES), 1)
    big = jnp.int32(LANES)
    is_grp = (lane >= N_EXPERTS) & (lane < N_EXPERTS + N_GROUPS)
    gl = jnp.where(is_grp, logits, NEG_INF)
    gmax = jnp.max(gl, axis=-1, keepdims=True)
    g_idx = jnp.min(jnp.where(gl == gmax, lane, big), axis=-1, keepdims=True) - N_EXPERTS
    p_g = 1.0 / jnp.sum(jnp.exp(gl - gmax), axis=-1, keepdims=True)
    in_grp = (lane // EXPERTS_PER_GROUP) == g_idx
    el = jnp.where(in_grp, logits, NEG_INF)
    v1 = jnp.max(el, axis=-1, keepdims=True)
    i1 = jnp.min(jnp.where(el == v1, lane, big), axis=-1, keepdims=True)
    el2 = jnp.where(lane == i1, NEG_INF, el)
    v2 = jnp.max(el2, axis=-1, keepdims=True)
    i2 = jnp.min(jnp.where(el2 == v2, lane, big), axis=-1, keepdims=True)
    e2 = jnp.exp(v2 - v1)
    w1 = p_g / (1.0 + e2)
    w2 = p_g * e2 / (1.0 + e2)
    cmb_ref[...] = jnp.where(lane == i1, w1, jnp.where(lane == i2, w2, 0.0))


def _merge(x, attn, rnn, sga, sgr, wa, wr, wo, g2, w_route, b_route, tm, precise):
    n = x.shape[0]
    row = lambda i: (i, 0)
    return pl.pallas_call(
        functools.partial(_merge_kernel, precise=precise),
        grid=(n // tm,),
        in_specs=[
            pl.BlockSpec((tm, D_MODEL), row),
            pl.BlockSpec((tm, D_Q), row),
            pl.BlockSpec((tm, D_RNN), row),
            pl.BlockSpec((tm, D_MODEL), row),
            pl.BlockSpec((tm, D_MODEL), row),
            _full((D_Q, D_MODEL)),
            _full((D_RNN, D_MODEL)),
            _full((D_MODEL, D_MODEL)),
            _full((1, D_MODEL)),
            _full((D_MODEL, LANES)),
            _full((1, LANES)),
        ],
        out_specs=[
            pl.BlockSpec((tm, D_MODEL), row),
            pl.BlockSpec((tm, D_MODEL), row),
            pl.BlockSpec((tm, LANES), row),
        ],
        out_shape=(
            jax.ShapeDtypeStruct((n, D_MODEL), F32),
            jax.ShapeDtypeStruct((n, D_MODEL), BF16),
            jax.ShapeDtypeStruct((n, LANES), F32),
        ),
        compiler_params=_params("parallel"),
        name="merge",
    )(x, attn, rnn, sga, sgr, wa, wr, wo, g2, w_route, b_route)


def _moe_kernel(x2_ref, xn_ref, cmb_ref, wgu_ref, wd_ref, o_ref, acc):
    e = pl.program_id(1)

    @pl.when(e == 0)
    def _():
        acc[...] = jnp.zeros_like(acc)

    hgu = jnp.dot(xn_ref[...], wgu_ref[0], preferred_element_type=F32)
    hg = hgu[:, :D_EXPERT]
    hu = hgu[:, D_EXPERT:]
    cmb = cmb_ref[...]
    lane = lax.broadcasted_iota(jnp.int32, cmb.shape, 1)
    c = jnp.sum(jnp.where(lane == e, cmb, 0.0), axis=-1, keepdims=True)
    h = (hg * _sigmoid(hg)) * hu * c
    acc[...] += jnp.dot(h.astype(BF16), wd_ref[0], preferred_element_type=F32)

    @pl.when(e == N_EXPERTS - 1)
    def _():
        o_ref[...] = x2_ref[...] + acc[...]


def _moe(x2, xn, cmb, wgu, wd, tm):
    n = x2.shape[0]
    row = lambda i, e: (i, 0)
    return pl.pallas_call(
        _moe_kernel,
        grid=(n // tm, N_EXPERTS),
        in_specs=[
            pl.BlockSpec((tm, D_MODEL), row),
            pl.BlockSpec((tm, D_MODEL), row),
            pl.BlockSpec((tm, LANES), row),
            pl.BlockSpec((1, D_MODEL, 2 * D_EXPERT), lambda i, e: (e, 0, 0)),
            pl.BlockSpec((1, D_EXPERT, D_MODEL), lambda i, e: (e, 0, 0)),
        ],
        out_specs=pl.BlockSpec((tm, D_MODEL), row),
        out_shape=jax.ShapeDtypeStruct((n, D_MODEL), F32),
        scratch_shapes=[pltpu.VMEM((tm, D_MODEL), F32)],
        compiler_params=_params("parallel", "arbitrary"),
        name="moe",
    )(x2, xn, cmb, wgu, wd)


def _rope_tables(pos):
    half = HEAD_DIM // 2
    inv_freq = ROPE_THETA ** (-jnp.arange(half, dtype=F32) / half)
    ang = pos[:, None] * inv_freq[None, :]
    cos = jnp.cos(ang)
    sin = jnp.sin(ang)
    reps = LANES // HEAD_DIM
    cos_t = jnp.tile(jnp.concatenate([cos, cos], axis=-1), (1, reps))
    sin_t = jnp.tile(jnp.concatenate([-sin, sin], axis=-1), (1, reps))
    return cos_t, sin_t


def kernel(x_prompt, x_sample, cache_k_win, cache_v_win, state_conv, state_lru_h, attn_norm_g, w_in, q_norm_g, k_norm_g, attn_sinks, conv_w, conv_b, w_lru_a, b_lru_a, w_lru_i, b_lru_i, lru_lambda, w_br_attn, w_br_rnn, w_out, ffn_norm_g, w_route_group, b_route_group, w_route_expert, b_route_expert, w_exp_gate, w_exp_up, w_exp_down):
    batch, seq, _ = x_prompt.shape
    dec_batch, dec_seq, _ = x_sample.shape
    depth = w_in.shape[0]
    assert depth == 1 and dec_seq == 1
    l = 0

    w_in_f = w_in[l]
    qkg = jnp.concatenate([jnp.tile(q_norm_g[l], N_Q_HEADS), jnp.tile(k_norm_g[l], N_KV_HEADS)])[None, :]
    wcat_f = jnp.concatenate([w_lru_a[l], w_lru_i[l]], axis=-1)
    wa_f, wr_f, wo_f = w_br_attn[l], w_br_rnn[l], w_out[l]
    w_route_f = jnp.concatenate(
        [w_route_expert[l], w_route_group[l],
         jnp.zeros((D_MODEL, LANES - N_EXPERTS - N_GROUPS), F32)], axis=-1)
    w_in_b, wcat = w_in_f.astype(BF16), wcat_f.astype(BF16)
    wa_b, wr_b, wo_b, w_route = (w.astype(BF16) for w in (wa_f, wr_f, wo_f, w_route_f))
    b_route = jnp.concatenate(
        [b_route_expert[l], b_route_group[l], jnp.zeros((LANES - N_EXPERTS - N_GROUPS,), F32)])[None, :]
    wgu = jnp.concatenate([w_exp_gate[l], w_exp_up[l]], axis=-1).astype(BF16)
    wd = w_exp_down[l].astype(BF16)
    g1 = attn_norm_g[l][None, :]
    g2 = ffn_norm_g[l][None, :]
    cw, cb = conv_w[l], conv_b[l][None, :]
    b_a, b_i, lam = b_lru_a[l][None, :], b_lru_i[l][None, :], lru_lambda[l][None, :]
    sinks = attn_sinks[l]

    def tail(x, attn, rnn, sga, sgr, tm, tm_moe, precise):
        wa, wr, wo, wrt = (wa_f, wr_f, wo_f, w_route_f) if precise else (wa_b, wr_b, wo_b, w_route)
        x2, xn2, cmb = _merge(x, attn, rnn, sga, sgr, wa, wr, wo, g2, wrt, b_route, tm, precise)
        return _moe(x2, xn2, cmb, wgu, wd, tm_moe)

    xp = x_prompt.reshape(batch * seq, D_MODEL)
    cos_p, sin_p = _rope_tables(jnp.arange(seq, dtype=F32))
    q, k, v, xr, gy, sga, sgr = _proj(xp, g1, w_in_b, cos_p, sin_p, qkg, 512, False)
    attn = _attn_prompt(q, k, v, sinks, batch, seq)
    rnn, h_last = _rnn_prompt(xr, gy, cw, cb, wcat, b_a, b_i, lam, batch, seq, 256)
    y_prompt = tail(xp, attn, rnn, sga, sgr, 512, 1024, False).reshape(batch, seq, D_MODEL)
    k_win_p = k.reshape(batch, seq, N_KV_HEADS, HEAD_DIM)[None, :, seq - WINDOW:]
    v_win_p = v.reshape(batch, seq, N_KV_HEADS, HEAD_DIM)[None, :, seq - WINDOW:]
    conv_p = xr.reshape(batch, seq, D_RNN)[None, :, seq - (CONV_W - 1):]
    h_p = h_last[None, :, 0, :]

    xs = x_sample.reshape(dec_batch, D_MODEL)
    cos_s, sin_s = _rope_tables(jnp.full((dec_batch,), PAST_LEN, F32))
    qs, ks, vs, xrs, gys, sgas, sgrs = _proj(xs, g1, w_in_f, cos_s, sin_s, qkg, dec_batch, True)
    ck = cache_k_win[l].reshape(dec_batch, WINDOW, D_KV)
    cv = cache_v_win[l].reshape(dec_batch, WINDOW, D_KV)
    attn_s, k_win_s, v_win_s = _attn_sample(qs, ks, vs, ck, cv, sinks)
    sc = state_conv[l]
    rnn_s, h_s = _rnn_sample(xrs, gys, sc[:, 0], sc[:, 1], sc[:, 2], state_lru_h[l],
                             cw, cb, wcat_f, b_a, b_i, lam)
    y_sample = tail(xs, attn_s, rnn_s, sgas, sgrs, dec_batch, dec_batch, True).reshape(dec_batch, 1, D_MODEL)
    conv_s = jnp.stack([sc[:, 1], sc[:, 2], xrs], axis=1)[None]

    return (y_prompt, y_sample, k_win_p, v_win_p, conv_p, h_p,
            k_win_s.reshape(1, dec_batch, WINDOW, N_KV_HEADS, HEAD_DIM),
            v_win_s.reshape(1, dec_batch, WINDOW, N_KV_HEADS, HEAD_DIM),
            conv_s, h_s[None])
```

```python
import functools

import jax
import jax.numpy as jnp
from jax import lax
from jax.experimental import pallas as pl
from jax.experimental.pallas import tpu as pltpu

D_MODEL = 1024
HEAD_DIM = 64
N_Q_HEADS = 8
N_KV_HEADS = 2
Q_PER_KV = N_Q_HEADS // N_KV_HEADS
WINDOW = 128
ATTN_BLOCK = 128
ROPE_THETA = 10000.0
SCALE = HEAD_DIM ** -0.5
NEG_INF = -1e30
D_RNN = 1280
N_RNN_BLOCKS = 10
RNN_BLOCK = D_RNN // N_RNN_BLOCKS
CONV_W = 4
LRU_C = 8.0
N_GROUPS = 4
EXPERTS_PER_GROUP = 8
N_EXPERTS = N_GROUPS * EXPERTS_PER_GROUP
D_EXPERT = 256
PAST_LEN = 16384
EPS = 1e-6
D_Q = N_Q_HEADS * HEAD_DIM
D_KV = N_KV_HEADS * HEAD_DIM
D_IN = D_Q + 2 * D_KV + 2 * D_RNN + 2 * D_MODEL
OFF_K = D_Q
OFF_V = OFF_K + D_KV
OFF_XR = OFF_V + D_KV
OFF_YR = OFF_XR + D_RNN
OFF_GA = OFF_YR + D_RNN
OFF_GR = OFF_GA + D_MODEL

LANES = 128
SUBLANES = 8
VMEM_LIMIT = 56 * 1024 * 1024

F32 = jnp.float32
BF16 = jnp.bfloat16


def _params(*sem):
    return pltpu.CompilerParams(dimension_semantics=sem, vmem_limit_bytes=VMEM_LIMIT)


def _sigmoid(x):
    return 1.0 / (1.0 + jnp.exp(-x))


def _gelu_tanh(x):
    c = 0.7978845608028654
    return 0.5 * x * (1.0 + jnp.tanh(c * (x + 0.044715 * (x * x * x))))


def _full(shape, single_buffer=False):
    index_map = lambda *_: (0,) * len(shape)
    if single_buffer:
        return pl.BlockSpec(shape, index_map, pipeline_mode=pl.Buffered(1))
    return pl.BlockSpec(shape, index_map)


def _mm(a, b, precise, dims=None):
    if precise:
        a, b, prec = a.astype(F32), b.astype(F32), lax.Precision.HIGHEST
    else:
        a, b, prec = a.astype(BF16), b.astype(BF16), None
    if dims is None:
        return jnp.dot(a, b, preferred_element_type=F32, precision=prec)
    return lax.dot_general(a, b, dims, preferred_element_type=F32, precision=prec)


_NT = (((1,), (1,)), ((), ()))


def _proj_kernel(x_ref, g_ref, w_ref, cos_ref, sin_ref, qkg_ref,
                 q_ref, k_ref, v_ref, xr_ref, gy_ref, sga_ref, sgr_ref, *, precise):
    x = x_ref[...]
    inv = lax.rsqrt(jnp.mean(x * x, axis=-1, keepdims=True) + EPS)
    xn = x * inv * g_ref[...]
    if not precise:
        xn = xn.astype(BF16)

    def proj(lo, hi):
        return _mm(xn, w_ref[:, lo:hi], precise)

    qk = proj(0, OFF_V)
    tm = qk.shape[0]
    lane = lax.broadcasted_iota(jnp.int32, (tm, LANES), 1)
    lo_head = lane < HEAD_DIM
    first_half = (lane % HEAD_DIM) < (HEAD_DIM // 2)
    cos = cos_ref[...]
    sin = sin_ref[...]
    for g in range(OFF_V // LANES):
        seg = qk[:, g * LANES:(g + 1) * LANES]
        sq = seg * seg
        s_lo = jnp.sum(jnp.where(lo_head, sq, 0.0), axis=-1, keepdims=True)
        s_hi = jnp.sum(jnp.where(lo_head, 0.0, sq), axis=-1, keepdims=True)
        ms = jnp.where(lo_head, s_lo, s_hi) * (1.0 / HEAD_DIM)
        normed = seg * lax.rsqrt(ms + EPS) * qkg_ref[:, g * LANES:(g + 1) * LANES]
        partner = jnp.where(first_half,
                            pltpu.roll(normed, LANES - HEAD_DIM // 2, axis=1),
                            pltpu.roll(normed, HEAD_DIM // 2, axis=1))
        roped = normed * cos + partner * sin
        if g < D_Q // LANES:
            q_ref[:, g * LANES:(g + 1) * LANES] = (roped * SCALE).astype(q_ref.dtype)
        else:
            k_ref[...] = roped
    v_ref[...] = proj(OFF_V, OFF_XR)
    xr_ref[...] = proj(OFF_XR, OFF_YR)
    gy_ref[...] = _gelu_tanh(proj(OFF_YR, OFF_GA)).astype(gy_ref.dtype)
    sga_ref[...] = _sigmoid(proj(OFF_GA, OFF_GR)).astype(sga_ref.dtype)
    sgr_ref[...] = _sigmoid(proj(OFF_GR, D_IN)).astype(sgr_ref.dtype)


def _proj(x, g, w_in, cos_t, sin_t, qkg, tm, precise):
    n = x.shape[0]
    t_blocks = cos_t.shape[0] // tm
    row = lambda i: (i, 0)
    act = F32 if precise else BF16
    out_shape = (
        jax.ShapeDtypeStruct((n, D_Q), act),
        jax.ShapeDtypeStruct((n, D_KV), F32),
        jax.ShapeDtypeStruct((n, D_KV), F32),
        jax.ShapeDtypeStruct((n, D_RNN), F32),
        jax.ShapeDtypeStruct((n, D_RNN), act),
        jax.ShapeDtypeStruct((n, D_MODEL), act),
        jax.ShapeDtypeStruct((n, D_MODEL), act),
    )
    return pl.pallas_call(
        functools.partial(_proj_kernel, precise=precise),
        grid=(n // tm,),
        in_specs=[
            pl.BlockSpec((tm, D_MODEL), row),
            _full((1, D_MODEL)),
            _full((D_MODEL, D_IN), single_buffer=True),
            pl.BlockSpec((tm, LANES), lambda i: (i % t_blocks, 0)),
            pl.BlockSpec((tm, LANES), lambda i: (i % t_blocks, 0)),
            _full((1, OFF_V)),
        ],
        out_specs=[pl.BlockSpec((tm, s.shape[1]), row) for s in out_shape],
        out_shape=out_shape,
        compiler_params=_params("parallel"),
        name="proj",
    )(x, g, w_in, cos_t, sin_t, qkg)


def _softmax_pv(s, sink, v2, precise):
    m = jnp.maximum(jnp.max(s, axis=-1, keepdims=True), sink)
    p = jnp.exp(s - m)
    denom = jnp.sum(p, axis=-1, keepdims=True) + jnp.exp(sink - m)
    return _mm(p, v2, precise) * (1.0 / denom)


def _sink_column(sink_ref, h, rows, rows_per_head):
    r = lax.broadcasted_iota(jnp.int32, (rows, 1), 0) // rows_per_head
    col = jnp.full((rows, 1), sink_ref[h * Q_PER_KV], F32)
    for g in range(1, Q_PER_KV):
        col = jnp.where(r == g, sink_ref[h * Q_PER_KV + g], col)
    return col


def _attn_prompt_kernel(sink_ref, q_ref, kc_ref, kp_ref, vc_ref, vp_ref, o_ref):
    n = pl.program_id(1)
    blk = ATTN_BLOCK
    q = q_ref[...]
    k2 = jnp.concatenate([kp_ref[...], kc_ref[...]], axis=0).astype(BF16)
    v2 = jnp.concatenate([vp_ref[...], vc_ref[...]], axis=0).astype(BF16)
    rows = Q_PER_KV * blk
    i = lax.broadcasted_iota(jnp.int32, (rows, 2 * blk), 0) % blk
    j = lax.broadcasted_iota(jnp.int32, (rows, 2 * blk), 1)
    d = j - i
    dmin = jnp.where(n > 0, 1, jnp.maximum(1, blk - i))
    valid = (d >= dmin) & (d <= WINDOW)
    for h in range(N_KV_HEADS):
        kh = k2[:, h * HEAD_DIM:(h + 1) * HEAD_DIM]
        vh = v2[:, h * HEAD_DIM:(h + 1) * HEAD_DIM]
        qs = jnp.concatenate(
            [q[:, (h * Q_PER_KV + g) * HEAD_DIM:(h * Q_PER_KV + g + 1) * HEAD_DIM]
             for g in range(Q_PER_KV)], axis=0)
        s = _mm(qs, kh, False, _NT)
        s = jnp.where(valid, s, NEG_INF)
        o = _softmax_pv(s, _sink_column(sink_ref, h, rows, blk), vh, False)
        for g in range(Q_PER_KV):
            c = (h * Q_PER_KV + g) * HEAD_DIM
            o_ref[:, c:c + HEAD_DIM] = o[g * blk:(g + 1) * blk].astype(o_ref.dtype)


def _attn_prompt(q, k, v, sinks, batch, seq):
    nb = seq // ATTN_BLOCK
    cur = lambda b, n: (b * nb + n, 0)
    prev = lambda b, n: (b * nb + jnp.maximum(n - 1, 0), 0)
    return pl.pallas_call(
        _attn_prompt_kernel,
        grid=(batch, nb),
        in_specs=[
            pl.BlockSpec(memory_space=pltpu.SMEM),
            pl.BlockSpec((ATTN_BLOCK, D_Q), cur),
            pl.BlockSpec((ATTN_BLOCK, D_KV), cur),
            pl.BlockSpec((ATTN_BLOCK, D_KV), prev),
            pl.BlockSpec((ATTN_BLOCK, D_KV), cur),
            pl.BlockSpec((ATTN_BLOCK, D_KV), prev),
        ],
        out_specs=pl.BlockSpec((ATTN_BLOCK, D_Q), cur),
        out_shape=jax.ShapeDtypeStruct((batch * seq, D_Q), BF16),
        compiler_params=_params("parallel", "parallel"),
        name="attn_prompt",
    )(sinks, q, k, k, v, v)


SAMPLE_BT = 8


def _attn_sample_kernel(sink_ref, q_ref, kn_ref, vn_ref, kc_ref, vc_ref, o_ref, ko_ref, vo_ref):
    bt = SAMPLE_BT
    w = lax.broadcasted_iota(jnp.int32, (bt, WINDOW, D_KV), 1)

    def shifted(cache_ref, new_ref):
        rolled = pltpu.roll(cache_ref[...], WINDOW - 1, axis=1)
        return jnp.where(w == WINDOW - 1, new_ref[...][:, None, :], rolled)

    k_win = shifted(kc_ref, kn_ref)
    v_win = shifted(vc_ref, vn_ref)
    ko_ref[...] = k_win
    vo_ref[...] = v_win
    k2 = k_win.reshape(bt * WINDOW, D_KV)
    v2 = v_win.reshape(bt * WINDOW, D_KV)
    q = q_ref[...]
    rows = Q_PER_KV * bt
    rb = lax.broadcasted_iota(jnp.int32, (rows, bt * WINDOW), 0) % bt
    cb = lax.broadcasted_iota(jnp.int32, (rows, bt * WINDOW), 1) // WINDOW
    valid = rb == cb
    for h in range(N_KV_HEADS):
        kh = k2[:, h * HEAD_DIM:(h + 1) * HEAD_DIM]
        vh = v2[:, h * HEAD_DIM:(h + 1) * HEAD_DIM]
        qs = jnp.concatenate(
            [q[:, (h * Q_PER_KV + g) * HEAD_DIM:(h * Q_PER_KV + g + 1) * HEAD_DIM]
             for g in range(Q_PER_KV)], axis=0)
        s = _mm(qs, kh, True, _NT)
        s = jnp.where(valid, s, NEG_INF)
        o = _softmax_pv(s, _sink_column(sink_ref, h, rows, bt), vh, True)
        for g in range(Q_PER_KV):
            c = (h * Q_PER_KV + g) * HEAD_DIM
            o_ref[:, c:c + HEAD_DIM] = o[g * bt:(g + 1) * bt].astype(o_ref.dtype)


def _attn_sample(q, k_new, v_new, cache_k, cache_v, sinks):
    nbatch = q.shape[0]
    bt = SAMPLE_BT
    row = lambda i: (i, 0)
    win = lambda i: (i, 0, 0)
    return pl.pallas_call(
        _attn_sample_kernel,
        grid=(nbatch // bt,),
        in_specs=[
            pl.BlockSpec(memory_space=pltpu.SMEM),
            pl.BlockSpec((bt, D_Q), row),
            pl.BlockSpec((bt, D_KV), row),
            pl.BlockSpec((bt, D_KV), row),
            pl.BlockSpec((bt, WINDOW, D_KV), win),
            pl.BlockSpec((bt, WINDOW, D_KV), win),
        ],
        out_specs=[
            pl.BlockSpec((bt, D_Q), row),
            pl.BlockSpec((bt, WINDOW, D_KV), win),
            pl.BlockSpec((bt, WINDOW, D_KV), win),
        ],
        out_shape=(
            jax.ShapeDtypeStruct((nbatch, D_Q), F32),
            jax.ShapeDtypeStruct((nbatch, WINDOW, D_KV), F32),
            jax.ShapeDtypeStruct((nbatch, WINDOW, D_KV), F32),
        ),
        compiler_params=_params("parallel"),
        name="attn_sample",
    )(sinks, q, k_new, v_new, cache_k, cache_v)


def _lru_terms(xc, wcat_ref, ba_ref, bi_ref, lam_ref, precise):
    xcb = xc if precise else xc.astype(BF16)
    ya, yi = [], []
    for n in range(N_RNN_BLOCKS):
        y = _mm(xcb[:, n * RNN_BLOCK:(n + 1) * RNN_BLOCK], wcat_ref[n], precise)
        ya.append(y[:, :RNN_BLOCK])
        yi.append(y[:, RNN_BLOCK:])
    r = _sigmoid(jnp.concatenate(ya, axis=-1) + ba_ref[...])
    gate_i = _sigmoid(jnp.concatenate(yi, axis=-1) + bi_ref[...])
    neg_lam = -lam_ref[...]
    softplus = jnp.maximum(neg_lam, 0.0) + jnp.log1p(jnp.exp(-jnp.abs(neg_lam)))
    log_a = (-LRU_C * softplus) * r
    a = jnp.exp(log_a)
    b = jnp.sqrt(jnp.maximum(1.0 - a * a, 0.0)) * (gate_i * xc)
    return a, b


def _rnn_prompt_kernel(xr_ref, gy_ref, cw_ref, cb_ref, wcat_ref, ba_ref, bi_ref, lam_ref,
                       o_ref, hl_ref, xbuf, a_scr, b_scr, h_scr, hcar):
    tt = xr_ref.shape[0]
    groups = tt // SUBLANES

    @pl.when(pl.program_id(1) == 0)
    def _():
        xbuf[0:SUBLANES, :] = jnp.zeros((SUBLANES, D_RNN), F32)
        hcar[...] = jnp.zeros((SUBLANES, D_RNN), F32)

    x = xr_ref[...]
    xbuf[SUBLANES:, :] = x
    xc = cb_ref[...] + cw_ref[CONV_W - 1:CONV_W, :] * x
    for j in range(CONV_W - 1):
        s = CONV_W - 1 - j
        xc = xc + cw_ref[j:j + 1, :] * xbuf[SUBLANES - s:SUBLANES - s + tt, :]
    xbuf[0:SUBLANES, :] = x[tt - SUBLANES:, :]

    a, b = _lru_terms(xc, wcat_ref, ba_ref, bi_ref, lam_ref, False)
    a = a.reshape(groups, SUBLANES, D_RNN)
    b = b.reshape(groups, SUBLANES, D_RNN)
    step = lax.broadcasted_iota(jnp.int32, (groups, SUBLANES, D_RNN), 1)
    k = 1
    while k < SUBLANES:
        keep = step >= k
        a_sh = jnp.where(keep, pltpu.roll(a, k, axis=1), 1.0)
        b_sh = jnp.where(keep, pltpu.roll(b, k, axis=1), 0.0)
        b = a * b_sh + b
        a = a * a_sh
        k *= 2
    a_scr[...] = a
    b_scr[...] = b

    def chain(g, h_in):
        h = a_scr[g] * h_in + b_scr[g]
        h_scr[g] = h
        return jnp.broadcast_to(h[SUBLANES - 1:SUBLANES, :], (SUBLANES, D_RNN))

    h_last = lax.fori_loop(0, groups, chain, hcar[...])
    hcar[...] = h_last
    hl_ref[0] = h_last
    h = h_scr[...].reshape(tt, D_RNN)
    o_ref[...] = (h * gy_ref[...].astype(F32)).astype(o_ref.dtype)


def _rnn_prompt(xr, gy, conv_w, conv_b, wcat, b_a, b_i, lam, batch, seq, tt):
    nt = seq // tt
    row = lambda b, t: (b * nt + t, 0)
    groups = tt // SUBLANES
    return pl.pallas_call(
        _rnn_prompt_kernel,
        grid=(batch, nt),
        in_specs=[
            pl.BlockSpec((tt, D_RNN), row),
            pl.BlockSpec((tt, D_RNN), row),
            _full((CONV_W, D_RNN)),
            _full((1, D_RNN)),
            _full((N_RNN_BLOCKS, RNN_BLOCK, 2 * RNN_BLOCK)),
            _full((1, D_RNN)),
            _full((1, D_RNN)),
            _full((1, D_RNN)),
        ],
        out_specs=[
            pl.BlockSpec((tt, D_RNN), row),
            pl.BlockSpec((1, SUBLANES, D_RNN), lambda b, t: (b, 0, 0)),
        ],
        out_shape=(
            jax.ShapeDtypeStruct((batch * seq, D_RNN), BF16),
            jax.ShapeDtypeStruct((batch, SUBLANES, D_RNN), F32),
        ),
        scratch_shapes=[
            pltpu.VMEM((tt + SUBLANES, D_RNN), F32),
            pltpu.VMEM((groups, SUBLANES, D_RNN), F32),
            pltpu.VMEM((groups, SUBLANES, D_RNN), F32),
            pltpu.VMEM((groups, SUBLANES, D_RNN), F32),
            pltpu.VMEM((SUBLANES, D_RNN), F32),
        ],
        compiler_params=_params("parallel", "arbitrary"),
        name="rnn_prompt",
    )(xr, gy, conv_w, conv_b, wcat, b_a, b_i, lam)


def _rnn_sample_kernel(xr_ref, gy_ref, s0_ref, s1_ref, s2_ref, h_ref, cw_ref, cb_ref,
                       wcat_ref, ba_ref, bi_ref, lam_ref, o_ref, hn_ref):
    x = xr_ref[...]
    xc = (cb_ref[...] + cw_ref[0:1, :] * s0_ref[...] + cw_ref[1:2, :] * s1_ref[...]
          + cw_ref[2:3, :] * s2_ref[...] + cw_ref[3:4, :] * x)
    a, b = _lru_terms(xc, wcat_ref, ba_ref, bi_ref, lam_ref, True)
    h = a * h_ref[...] + b
    hn_ref[...] = h
    o_ref[...] = (h * gy_ref[...].astype(F32)).astype(o_ref.dtype)


def _rnn_sample(xr, gy, s0, s1, s2, h_prev, conv_w, conv_b, wcat, b_a, b_i, lam):
    n = xr.shape[0]
    act = _full((n, D_RNN))
    return pl.pallas_call(
        _rnn_sample_kernel,
        grid=(1,),
        in_specs=[act, act, act, act, act, act,
                  _full((CONV_W, D_RNN)), _full((1, D_RNN)),
                  _full((N_RNN_BLOCKS, RNN_BLOCK, 2 * RNN_BLOCK)),
                  _full((1, D_RNN)), _full((1, D_RNN)), _full((1, D_RNN))],
        out_specs=[act, act],
        out_shape=(jax.ShapeDtypeStruct((n, D_RNN), F32),
                   jax.ShapeDtypeStruct((n, D_RNN), F32)),
        compiler_params=_params("arbitrary"),
        name="rnn_sample",
    )(xr, gy, s0, s1, s2, h_prev, conv_w, conv_b, wcat, b_a, b_i, lam)


def _merge_kernel(x_ref, at_ref, rn_ref, sga_ref, sgr_ref, wa_ref, wr_ref, wo_ref, g_ref,
                  wrt_ref, brt_ref, tri_ref, x2_ref, xtm_ref, rec_ref, cnt_ref, cnt_scr,
                  *, precise, tiles_per_seg):
    ya = _mm(at_ref[...], wa_ref[...], precise)
    yr = _mm(rn_ref[...], wr_ref[...], precise)
    merged = sga_ref[...].astype(F32) * ya + sgr_ref[...].astype(F32) * yr
    x2 = x_ref[...] + _mm(merged, wo_ref[...], precise)
    x2_ref[...] = x2
    inv = lax.rsqrt(jnp.mean(x2 * x2, axis=-1, keepdims=True) + EPS)
    xn = x2 * inv * g_ref[...]

    logits = _mm(xn, wrt_ref[...], precise) + brt_ref[...]
    tm = logits.shape[0]
    lane = lax.broadcasted_iota(jnp.int32, (tm, LANES), 1)
    big = jnp.int32(LANES)
    is_grp = (lane >= N_EXPERTS) & (lane < N_EXPERTS + N_GROUPS)
    gl = jnp.where(is_grp, logits, NEG_INF)
    gmax = jnp.max(gl, axis=-1, keepdims=True)
    g_idx = jnp.min(jnp.where(gl == gmax, lane, big), axis=-1, keepdims=True) - N_EXPERTS
    p_g = 1.0 / jnp.sum(jnp.exp(gl - gmax), axis=-1, keepdims=True)
    in_grp = (lane // EXPERTS_PER_GROUP) == g_idx
    el = jnp.where(in_grp, logits, NEG_INF)
    v1 = jnp.max(el, axis=-1, keepdims=True)
    i1 = jnp.min(jnp.where(el == v1, lane, big), axis=-1, keepdims=True)
    el2 = jnp.where(lane == i1, NEG_INF, el)
    v2 = jnp.max(el2, axis=-1, keepdims=True)
    i2 = jnp.min(jnp.where(el2 == v2, lane, big), axis=-1, keepdims=True)
    e2 = jnp.exp(v2 - v1)
    w1 = p_g / (1.0 + e2)
    w2 = p_g * e2 / (1.0 + e2)

    @pl.when(pl.program_id(0) % tiles_per_seg == 0)
    def _():
        cnt_scr[...] = jnp.zeros_like(cnt_scr)

    hit = jnp.where(lane == i1, 1.0, jnp.where(lane == i2, 1.0, 0.0))
    before = jnp.dot(tri_ref[...], hit.astype(BF16), preferred_element_type=F32) + cnt_scr[0:1, :]
    r1 = jnp.sum(jnp.where(lane == i1, before, 0.0), axis=-1, keepdims=True)
    r2 = jnp.sum(jnp.where(lane == i2, before, 0.0), axis=-1, keepdims=True)
    counts = cnt_scr[...] + jnp.sum(hit, axis=0, keepdims=True)
    cnt_scr[...] = counts
    cnt_ref[0] = counts
    rec = jnp.where(lane == REC_W2, w2, 0.0)
    for field, val in ((REC_W1, w1), (REC_R2, r2), (REC_R1, r1),
                       (REC_E2, i2.astype(F32)), (REC_E1, i1.astype(F32))):
        rec = jnp.where(lane == field, val, rec)
    rec_ref[...] = rec

    half = D_MODEL // 2
    words = _pack_bf16_pair(xn[:, :half], xn[:, half:])
    for c in range(TOKEN_ROWS):
        xtm_ref[pl.ds(c, tm, stride=TOKEN_ROWS), :] = words[:, c * LANES:(c + 1) * LANES]


REC_E1, REC_E2, REC_R1, REC_R2, REC_W1, REC_W2 = range(6)
TOKEN_ROWS = D_MODEL // 2 // LANES


def _pack_bf16_pair(lo, hi):
    lo_bits = pltpu.bitcast(lo.astype(BF16).astype(F32), jnp.uint32)
    hi_bits = pltpu.bitcast(hi.astype(BF16).astype(F32), jnp.uint32)
    return (lo_bits >> 16) | (hi_bits & jnp.uint32(0xFFFF0000))


def _unpack_bf16_pair(words):
    lo = pltpu.bitcast(words << 16, F32)
    hi = pltpu.bitcast(words & jnp.uint32(0xFFFF0000), F32)
    return lo, hi


def _merge(x, attn, rnn, sga, sgr, wa, wr, wo, g2, w_route, b_route, tri, tm, seg, precise):
    n = x.shape[0]
    tiles_per_seg = seg // tm
    row = lambda i: (i, 0)
    return pl.pallas_call(
        functools.partial(_merge_kernel, precise=precise, tiles_per_seg=tiles_per_seg),
        grid=(n // tm,),
        in_specs=[
            pl.BlockSpec((tm, D_MODEL), row),
            pl.BlockSpec((tm, D_Q), row),
            pl.BlockSpec((tm, D_RNN), row),
            pl.BlockSpec((tm, D_MODEL), row),
            pl.BlockSpec((tm, D_MODEL), row),
            _full((D_Q, D_MODEL)),
            _full((D_RNN, D_MODEL)),
            _full((D_MODEL, D_MODEL)),
            _full((1, D_MODEL)),
            _full((D_MODEL, LANES)),
            _full((1, LANES)),
            _full((tm, tm)),
        ],
        out_specs=[
            pl.BlockSpec((tm, D_MODEL), row),
            pl.BlockSpec((tm * TOKEN_ROWS, LANES), row),
            pl.BlockSpec((tm, LANES), row),
            pl.BlockSpec((1, SUBLANES, LANES), lambda i: (i // tiles_per_seg, 0, 0)),
        ],
        out_shape=(
            jax.ShapeDtypeStruct((n, D_MODEL), F32),
            jax.ShapeDtypeStruct((n * TOKEN_ROWS, LANES), jnp.uint32),
            jax.ShapeDtypeStruct((n, LANES), F32),
            jax.ShapeDtypeStruct((n // seg, SUBLANES, LANES), F32),
        ),
        scratch_shapes=[pltpu.VMEM((SUBLANES, LANES), F32)],
        compiler_params=_params("arbitrary"),
        name="merge",
    )(x, attn, rnn, sga, sgr, wa, wr, wo, g2, w_route, b_route, tri)


MOE_CHUNK = 128


def _seg_rows(seg):
    return 2 * seg + N_EXPERTS * SUBLANES + MOE_CHUNK


def _expert_kernel(off_ref, nch_ref, slot_ref, xtm_ref, wgu_ref, wd_ref, ys_ref, xs_scr, *, seg):
    s = pl.program_id(0)
    e = pl.program_id(1)
    half = D_MODEL // 2

    @pl.when((s == 0) & (e == 0))
    def _():
        xs_scr[...] = jnp.zeros_like(xs_scr)

    @pl.when(e == 0)
    def _():
        def dispatch(g, carry):
            for j in range(SUBLANES):
                t = g * SUBLANES + j
                row = xtm_ref[pl.ds(pl.multiple_of(t * TOKEN_ROWS, TOKEN_ROWS), TOKEN_ROWS), :]
                for k in range(2):
                    dst = pl.multiple_of(slot_ref[0, 0, 2 * t + k] * TOKEN_ROWS, TOKEN_ROWS)
                    xs_scr[pl.ds(dst, TOKEN_ROWS), :] = row
            return carry

        lax.fori_loop(0, seg // SUBLANES, dispatch, 0)

    base = off_ref[s * N_EXPERTS + e]

    def chunk(c, carry):
        r0 = pl.multiple_of((base + c * MOE_CHUNK) * TOKEN_ROWS, SUBLANES * TOKEN_ROWS)
        words = jnp.concatenate(
            [xs_scr[pl.ds(r0 + j, MOE_CHUNK, stride=TOKEN_ROWS), :] for j in range(TOKEN_ROWS)], axis=-1)
        x_lo, x_hi = _unpack_bf16_pair(words)
        hgu = (jnp.dot(x_lo.astype(BF16), wgu_ref[0, :half, :], preferred_element_type=F32)
               + jnp.dot(x_hi.astype(BF16), wgu_ref[0, half:, :], preferred_element_type=F32))
        hg = hgu[:, :D_EXPERT]
        h = (hg * _sigmoid(hg)) * hgu[:, D_EXPERT:]
        y = jnp.dot(h.astype(BF16), wd_ref[0], preferred_element_type=F32)
        yw = _pack_bf16_pair(y[:, :half], y[:, half:])
        for j in range(TOKEN_ROWS):
            ys_ref[pl.ds(r0 + j, MOE_CHUNK, stride=TOKEN_ROWS), :] = yw[:, j * LANES:(j + 1) * LANES]
        return carry

    lax.fori_loop(0, nch_ref[s * N_EXPERTS + e], chunk, 0)


def _experts(off, nch, slot, xtm, wgu, wd, seg):
    n_seg = xtm.shape[0] // (seg * TOKEN_ROWS)
    rows = _seg_rows(seg) * TOKEN_ROWS
    grid_spec = pltpu.PrefetchScalarGridSpec(
        num_scalar_prefetch=2,
        grid=(n_seg, N_EXPERTS),
        in_specs=[
            pl.BlockSpec((1, 1, 2 * seg), lambda s, e, *_: (s, 0, 0), memory_space=pltpu.SMEM),
            pl.BlockSpec((seg * TOKEN_ROWS, LANES), lambda s, e, *_: (s, 0),
                         pipeline_mode=pl.Buffered(1)),
            pl.BlockSpec((1, D_MODEL, 2 * D_EXPERT), lambda s, e, *_: (e, 0, 0)),
            pl.BlockSpec((1, D_EXPERT, D_MODEL), lambda s, e, *_: (e, 0, 0)),
        ],
        out_specs=pl.BlockSpec((rows, LANES), lambda s, e, *_: (s, 0)),
        scratch_shapes=[pltpu.VMEM((rows, LANES), jnp.uint32)],
    )
    return pl.pallas_call(
        functools.partial(_expert_kernel, seg=seg),
        grid_spec=grid_spec,
        out_shape=jax.ShapeDtypeStruct((n_seg * rows, LANES), jnp.uint32),
        compiler_params=_params("arbitrary", "arbitrary"),
        name="experts",
    )(off, nch, slot.reshape(n_seg, 1, 2 * seg), xtm, wgu, wd)


def _combine_kernel(slot_ref, x2_ref, rec_ref, ys_ref, o_ref, g1, g2):
    tc = x2_ref.shape[0]
    half = D_MODEL // 2

    def gather(g, carry):
        for j in range(SUBLANES):
            t = g * SUBLANES + j
            dst = pl.multiple_of(t * TOKEN_ROWS, TOKEN_ROWS)
            s1 = pl.multiple_of(slot_ref[0, 0, 2 * t] * TOKEN_ROWS, TOKEN_ROWS)
            s2 = pl.multiple_of(slot_ref[0, 0, 2 * t + 1] * TOKEN_ROWS, TOKEN_ROWS)
            g1[pl.ds(dst, TOKEN_ROWS), :] = ys_ref[pl.ds(s1, TOKEN_ROWS), :]
            g2[pl.ds(dst, TOKEN_ROWS), :] = ys_ref[pl.ds(s2, TOKEN_ROWS), :]
        return carry

    lax.fori_loop(0, tc // SUBLANES, gather, 0)
    rec = rec_ref[...]
    lane = lax.broadcasted_iota(jnp.int32, rec.shape, 1)
    w1 = jnp.sum(jnp.where(lane == REC_W1, rec, 0.0), axis=-1, keepdims=True)
    w2 = jnp.sum(jnp.where(lane == REC_W2, rec, 0.0), axis=-1, keepdims=True)
    for j in range(TOKEN_ROWS):
        a_lo, a_hi = _unpack_bf16_pair(g1[pl.ds(j, tc, stride=TOKEN_ROWS), :])
        b_lo, b_hi = _unpack_bf16_pair(g2[pl.ds(j, tc, stride=TOKEN_ROWS), :])
        c_lo = slice(j * LANES, (j + 1) * LANES)
        c_hi = slice(half + j * LANES, half + (j + 1) * LANES)
        o_ref[:, c_lo] = x2_ref[:, c_lo] + w1 * a_lo + w2 * b_lo
        o_ref[:, c_hi] = x2_ref[:, c_hi] + w1 * a_hi + w2 * b_hi


def _combine(slot, x2, rec, ys, seg, tc):
    n = x2.shape[0]
    tiles_per_seg = seg // tc
    rows = _seg_rows(seg) * TOKEN_ROWS
    tile = lambda s, i: (s * tiles_per_seg + i, 0)
    return pl.pallas_call(
        _combine_kernel,
        grid=(n // seg, tiles_per_seg),
        in_specs=[
            pl.BlockSpec((1, 1, 2 * tc), lambda s, i: (s * tiles_per_seg + i, 0, 0),
                         memory_space=pltpu.SMEM),
            pl.BlockSpec((tc, D_MODEL), tile),
            pl.BlockSpec((tc, LANES), tile),
            pl.BlockSpec((rows, LANES), lambda s, i: (s, 0), pipeline_mode=pl.Buffered(1)),
        ],
        out_specs=pl.BlockSpec((tc, D_MODEL), tile),
        out_shape=jax.ShapeDtypeStruct((n, D_MODEL), F32),
        scratch_shapes=[pltpu.VMEM((tc * TOKEN_ROWS, LANES), jnp.uint32),
                        pltpu.VMEM((tc * TOKEN_ROWS, LANES), jnp.uint32)],
        compiler_params=_params("arbitrary", "arbitrary"),
        name="combine",
    )(slot.reshape(n // tc, 1, 2 * tc), x2, rec, ys)


def _plan(rec, cnt, seg):
    n = rec.shape[0]
    expert = rec[:, REC_E1:REC_E2 + 1].astype(jnp.int32)
    rank = rec[:, REC_R1:REC_R2 + 1].astype(jnp.int32)
    counts = cnt[:, 0, :N_EXPERTS].astype(jnp.int32)
    padded = (counts + SUBLANES - 1) // SUBLANES * SUBLANES
    off = jnp.cumsum(padded, axis=1) - padded
    nch = (counts + MOE_CHUNK - 1) // MOE_CHUNK
    off_tok = jnp.repeat(off, seg, axis=0)
    hit = expert[:, :, None] == jnp.arange(N_EXPERTS, dtype=jnp.int32)[None, None, :]
    slot = rank + jnp.sum(jnp.where(hit, off_tok[:, None, :], 0), axis=-1)
    return off.reshape(-1), nch.reshape(-1), slot


def _rope_tables(pos):
    half = HEAD_DIM // 2
    inv_freq = ROPE_THETA ** (-jnp.arange(half, dtype=F32) / half)
    ang = pos[:, None] * inv_freq[None, :]
    cos = jnp.cos(ang)
    sin = jnp.sin(ang)
    reps = LANES // HEAD_DIM
    cos_t = jnp.tile(jnp.concatenate([cos, cos], axis=-1), (1, reps))
    sin_t = jnp.tile(jnp.concatenate([-sin, sin], axis=-1), (1, reps))
    return cos_t, sin_t


def kernel(x_prompt, x_sample, cache_k_win, cache_v_win, state_conv, state_lru_h, attn_norm_g, w_in, q_norm_g, k_norm_g, attn_sinks, conv_w, conv_b, w_lru_a, b_lru_a, w_lru_i, b_lru_i, lru_lambda, w_br_attn, w_br_rnn, w_out, ffn_norm_g, w_route_group, b_route_group, w_route_expert, b_route_expert, w_exp_gate, w_exp_up, w_exp_down):
    batch, seq, _ = x_prompt.shape
    dec_batch, dec_seq, _ = x_sample.shape
    depth = w_in.shape[0]
    assert depth == 1 and dec_seq == 1
    l = 0

    w_in_f = w_in[l]
    qkg = jnp.concatenate([jnp.tile(q_norm_g[l], N_Q_HEADS), jnp.tile(k_norm_g[l], N_KV_HEADS)])[None, :]
    wcat_f = jnp.concatenate([w_lru_a[l], w_lru_i[l]], axis=-1)
    wa_f, wr_f, wo_f = w_br_attn[l], w_br_rnn[l], w_out[l]
    w_route_f = jnp.concatenate(
        [w_route_expert[l], w_route_group[l],
         jnp.zeros((D_MODEL, LANES - N_EXPERTS - N_GROUPS), F32)], axis=-1)
    w_in_b, wcat = w_in_f.astype(BF16), wcat_f.astype(BF16)
    wa_b, wr_b, wo_b, w_route = (w.astype(BF16) for w in (wa_f, wr_f, wo_f, w_route_f))
    b_route = jnp.concatenate(
        [b_route_expert[l], b_route_group[l], jnp.zeros((LANES - N_EXPERTS - N_GROUPS,), F32)])[None, :]
    wgu = jnp.concatenate([w_exp_gate[l], w_exp_up[l]], axis=-1).astype(BF16)
    wd = w_exp_down[l].astype(BF16)
    g1 = attn_norm_g[l][None, :]
    g2 = ffn_norm_g[l][None, :]
    cw, cb = conv_w[l], conv_b[l][None, :]
    b_a, b_i, lam = b_lru_a[l][None, :], b_lru_i[l][None, :], lru_lambda[l][None, :]
    sinks = attn_sinks[l]

    def tail(x, attn, rnn, sga, sgr, tm, seg, precise):
        wa, wr, wo, wrt = (wa_f, wr_f, wo_f, w_route_f) if precise else (wa_b, wr_b, wo_b, w_route)
        tri = jnp.tril(jnp.ones((tm, tm), BF16), -1)
        x2, xtm, rec, cnt = _merge(x, attn, rnn, sga, sgr, wa, wr, wo, g2, wrt, b_route, tri,
                                   tm, seg, precise)
        off, nch, slot = _plan(rec, cnt, seg)
        ys = _experts(off, nch, slot, xtm, wgu, wd, seg)
        return _combine(slot, x2, rec, ys, seg, tm)

    xp = x_prompt.reshape(batch * seq, D_MODEL)
    cos_p, sin_p = _rope_tables(jnp.arange(seq, dtype=F32))
    q, k, v, xr, gy, sga, sgr = _proj(xp, g1, w_in_b, cos_p, sin_p, qkg, 512, False)
    attn = _attn_prompt(q, k, v, sinks, batch, seq)
    rnn, h_last = _rnn_prompt(xr, gy, cw, cb, wcat, b_a, b_i, lam, batch, seq, 256)
    y_prompt = tail(xp, attn, rnn, sga, sgr, 512, 2048, False).reshape(batch, seq, D_MODEL)
    k_win_p = k.reshape(batch, seq, N_KV_HEADS, HEAD_DIM)[None, :, seq - WINDOW:]
    v_win_p = v.reshape(batch, seq, N_KV_HEADS, HEAD_DIM)[None, :, seq - WINDOW:]
    conv_p = xr.reshape(batch, seq, D_RNN)[None, :, seq - (CONV_W - 1):]
    h_p = h_last[None, :, 0, :]

    xs = x_sample.reshape(dec_batch, D_MODEL)
    cos_s, sin_s = _rope_tables(jnp.full((dec_batch,), PAST_LEN, F32))
    qs, ks, vs, xrs, gys, sgas, sgrs = _proj(xs, g1, w_in_f, cos_s, sin_s, qkg, dec_batch, True)
    ck = cache_k_win[l].reshape(dec_batch, WINDOW, D_KV)
    cv = cache_v_win[l].reshape(dec_batch, WINDOW, D_KV)
    attn_s, k_win_s, v_win_s = _attn_sample(qs, ks, vs, ck, cv, sinks)
    sc = state_conv[l]
    rnn_s, h_s = _rnn_sample(xrs, gys, sc[:, 0], sc[:, 1], sc[:, 2], state_lru_h[l],
                             cw, cb, wcat_f, b_a, b_i, lam)
    y_sample = tail(xs, attn_s, rnn_s, sgas, sgrs, dec_batch, dec_batch, True).reshape(dec_batch, 1, D_MODEL)
    conv_s = jnp.stack([sc[:, 1], sc[:, 2], xrs], axis=1)[None]

    return (y_prompt, y_sample, k_win_p, v_win_p, conv_p, h_p,
            k_win_s.reshape(1, dec_batch, WINDOW, N_KV_HEADS, HEAD_DIM),
            v_win_s.reshape(1, dec_batch, WINDOW, N_KV_HEADS, HEAD_DIM),
            conv_s, h_s[None])
```

```python
import functools

import jax
import jax.numpy as jnp
from jax import lax
from jax.experimental import pallas as pl
from jax.experimental.pallas import tpu as pltpu

D_MODEL = 1024
HEAD_DIM = 64
N_Q_HEADS = 8
N_KV_HEADS = 2
Q_PER_KV = N_Q_HEADS // N_KV_HEADS
WINDOW = 128
ATTN_BLOCK = 128
ROPE_THETA = 10000.0
SCALE = HEAD_DIM ** -0.5
NEG_INF = -1e30
D_RNN = 1280
N_RNN_BLOCKS = 10
RNN_BLOCK = D_RNN // N_RNN_BLOCKS
CONV_W = 4
LRU_C = 8.0
N_GROUPS = 4
EXPERTS_PER_GROUP = 8
N_EXPERTS = N_GROUPS * EXPERTS_PER_GROUP
D_EXPERT = 256
PAST_LEN = 16384
EPS = 1e-6
D_Q = N_Q_HEADS * HEAD_DIM
D_KV = N_KV_HEADS * HEAD_DIM
D_IN = D_Q + 2 * D_KV + 2 * D_RNN + 2 * D_MODEL
OFF_K = D_Q
OFF_V = OFF_K + D_KV
OFF_XR = OFF_V + D_KV
OFF_YR = OFF_XR + D_RNN
OFF_GA = OFF_YR + D_RNN
OFF_GR = OFF_GA + D_MODEL

LANES = 128
SUBLANES = 8
VMEM_LIMIT = 56 * 1024 * 1024

F32 = jnp.float32
BF16 = jnp.bfloat16


def _params(*sem):
    return pltpu.CompilerParams(dimension_semantics=sem, vmem_limit_bytes=VMEM_LIMIT)


def _sigmoid(x):
    return 1.0 / (1.0 + jnp.exp(-x))


def _gelu_tanh(x):
    c = 0.7978845608028654
    return 0.5 * x * (1.0 + jnp.tanh(c * (x + 0.044715 * (x * x * x))))


def _full(shape, single_buffer=False):
    index_map = lambda *_: (0,) * len(shape)
    if single_buffer:
        return pl.BlockSpec(shape, index_map, pipeline_mode=pl.Buffered(1))
    return pl.BlockSpec(shape, index_map)


def _mm(a, b, precise, dims=None):
    if precise:
        a, b, prec = a.astype(F32), b.astype(F32), lax.Precision.HIGHEST
    else:
        a, b, prec = a.astype(BF16), b.astype(BF16), None
    if dims is None:
        return jnp.dot(a, b, preferred_element_type=F32, precision=prec)
    return lax.dot_general(a, b, dims, preferred_element_type=F32, precision=prec)


_NT = (((1,), (1,)), ((), ()))


def _proj_kernel(x_ref, g_ref, w_ref, cos_ref, sin_ref, qkg_ref,
                 q_ref, k_ref, v_ref, xr_ref, gy_ref, sga_ref, sgr_ref, *, precise):
    x = x_ref[...]
    inv = lax.rsqrt(jnp.mean(x * x, axis=-1, keepdims=True) + EPS)
    xn = x * inv * g_ref[...]
    if not precise:
        xn = xn.astype(BF16)

    def proj(lo, hi):
        return _mm(xn, w_ref[:, lo:hi], precise)

    qk = proj(0, OFF_V)
    tm = qk.shape[0]
    lane = lax.broadcasted_iota(jnp.int32, (tm, LANES), 1)
    lo_head = lane < HEAD_DIM
    first_half = (lane % HEAD_DIM) < (HEAD_DIM // 2)
    cos = cos_ref[...]
    sin = sin_ref[...]
    for g in range(OFF_V // LANES):
        seg = qk[:, g * LANES:(g + 1) * LANES]
        sq = seg * seg
        s_lo = jnp.sum(jnp.where(lo_head, sq, 0.0), axis=-1, keepdims=True)
        s_hi = jnp.sum(jnp.where(lo_head, 0.0, sq), axis=-1, keepdims=True)
        ms = jnp.where(lo_head, s_lo, s_hi) * (1.0 / HEAD_DIM)
        normed = seg * lax.rsqrt(ms + EPS) * qkg_ref[:, g * LANES:(g + 1) * LANES]
        partner = jnp.where(first_half,
                            pltpu.roll(normed, LANES - HEAD_DIM // 2, axis=1),
                            pltpu.roll(normed, HEAD_DIM // 2, axis=1))
        roped = normed * cos + partner * sin
        if g < D_Q // LANES:
            q_ref[:, g * LANES:(g + 1) * LANES] = (roped * SCALE).astype(q_ref.dtype)
        else:
            k_ref[...] = roped
    v_ref[...] = proj(OFF_V, OFF_XR)
    xr_ref[...] = proj(OFF_XR, OFF_YR)
    gy_ref[...] = _gelu_tanh(proj(OFF_YR, OFF_GA)).astype(gy_ref.dtype)
    sga_ref[...] = _sigmoid(proj(OFF_GA, OFF_GR)).astype(sga_ref.dtype)
    sgr_ref[...] = _sigmoid(proj(OFF_GR, D_IN)).astype(sgr_ref.dtype)


def _proj(x, g, w_in, cos_t, sin_t, qkg, tm, precise):
    n = x.shape[0]
    t_blocks = cos_t.shape[0] // tm
    row = lambda i: (i, 0)
    act = F32 if precise else BF16
    out_shape = (
        jax.ShapeDtypeStruct((n, D_Q), act),
        jax.ShapeDtypeStruct((n, D_KV), F32),
        jax.ShapeDtypeStruct((n, D_KV), F32),
        jax.ShapeDtypeStruct((n, D_RNN), F32),
        jax.ShapeDtypeStruct((n, D_RNN), act),
        jax.ShapeDtypeStruct((n, D_MODEL), act),
        jax.ShapeDtypeStruct((n, D_MODEL), act),
    )
    return pl.pallas_call(
        functools.partial(_proj_kernel, precise=precise),
        grid=(n // tm,),
        in_specs=[
            pl.BlockSpec((tm, D_MODEL), row),
            _full((1, D_MODEL)),
            _full((D_MODEL, D_IN), single_buffer=True),
            pl.BlockSpec((tm, LANES), lambda i: (i % t_blocks, 0)),
            pl.BlockSpec((tm, LANES), lambda i: (i % t_blocks, 0)),
            _full((1, OFF_V)),
        ],
        out_specs=[pl.BlockSpec((tm, s.shape[1]), row) for s in out_shape],
        out_shape=out_shape,
        compiler_params=_params("parallel"),
        name="proj",
    )(x, g, w_in, cos_t, sin_t, qkg)


def _softmax_pv(s, sink, v2, precise):
    m = jnp.maximum(jnp.max(s, axis=-1, keepdims=True), sink)
    p = jnp.exp(s - m)
    denom = jnp.sum(p, axis=-1, keepdims=True) + jnp.exp(sink - m)
    return _mm(p, v2, precise) * (1.0 / denom)


def _sink_column(sink_ref, h, rows, rows_per_head):
    r = lax.broadcasted_iota(jnp.int32, (rows, 1), 0) // rows_per_head
    col = jnp.full((rows, 1), sink_ref[h * Q_PER_KV], F32)
    for g in range(1, Q_PER_KV):
        col = jnp.where(r == g, sink_ref[h * Q_PER_KV + g], col)
    return col


def _attn_prompt_kernel(sink_ref, q_ref, kc_ref, kp_ref, vc_ref, vp_ref, o_ref):
    n = pl.program_id(1)
    blk = ATTN_BLOCK
    q = q_ref[...]
    k2 = jnp.concatenate([kp_ref[...], kc_ref[...]], axis=0).astype(BF16)
    v2 = jnp.concatenate([vp_ref[...], vc_ref[...]], axis=0).astype(BF16)
    rows = Q_PER_KV * blk
    i = lax.broadcasted_iota(jnp.int32, (rows, 2 * blk), 0) % blk
    j = lax.broadcasted_iota(jnp.int32, (rows, 2 * blk), 1)
    d = j - i
    dmin = jnp.where(n > 0, 1, jnp.maximum(1, blk - i))
    valid = (d >= dmin) & (d <= WINDOW)
    for h in range(N_KV_HEADS):
        kh = k2[:, h * HEAD_DIM:(h + 1) * HEAD_DIM]
        vh = v2[:, h * HEAD_DIM:(h + 1) * HEAD_DIM]
        qs = jnp.concatenate(
            [q[:, (h * Q_PER_KV + g) * HEAD_DIM:(h * Q_PER_KV + g + 1) * HEAD_DIM]
             for g in range(Q_PER_KV)], axis=0)
        s = _mm(qs, kh, False, _NT)
        s = jnp.where(valid, s, NEG_INF)
        o = _softmax_pv(s, _sink_column(sink_ref, h, rows, blk), vh, False)
        for g in range(Q_PER_KV):
            c = (h * Q_PER_KV + g) * HEAD_DIM
            o_ref[:, c:c + HEAD_DIM] = o[g * blk:(g + 1) * blk].astype(o_ref.dtype)


def _attn_prompt(q, k, v, sinks, batch, seq):
    nb = seq // ATTN_BLOCK
    cur = lambda b, n: (b * nb + n, 0)
    prev = lambda b, n: (b * nb + jnp.maximum(n - 1, 0), 0)
    return pl.pallas_call(
        _attn_prompt_kernel,
        grid=(batch, nb),
        in_specs=[
            pl.BlockSpec(memory_space=pltpu.SMEM),
            pl.BlockSpec((ATTN_BLOCK, D_Q), cur),
            pl.BlockSpec((ATTN_BLOCK, D_KV), cur),
            pl.BlockSpec((ATTN_BLOCK, D_KV), prev),
            pl.BlockSpec((ATTN_BLOCK, D_KV), cur),
            pl.BlockSpec((ATTN_BLOCK, D_KV), prev),
        ],
        out_specs=pl.BlockSpec((ATTN_BLOCK, D_Q), cur),
        out_shape=jax.ShapeDtypeStruct((batch * seq, D_Q), BF16),
        compiler_params=_params("parallel", "parallel"),
        name="attn_prompt",
    )(sinks, q, k, k, v, v)


SAMPLE_BT = 8


def _attn_sample_kernel(sink_ref, q_ref, kn_ref, vn_ref, kc_ref, vc_ref, o_ref, ko_ref, vo_ref):
    bt = SAMPLE_BT
    w = lax.broadcasted_iota(jnp.int32, (bt, WINDOW, D_KV), 1)

    def shifted(cache_ref, new_ref):
        rolled = pltpu.roll(cache_ref[...], WINDOW - 1, axis=1)
        return jnp.where(w == WINDOW - 1, new_ref[...][:, None, :], rolled)

    k_win = shifted(kc_ref, kn_ref)
    v_win = shifted(vc_ref, vn_ref)
    ko_ref[...] = k_win
    vo_ref[...] = v_win
    k2 = k_win.reshape(bt * WINDOW, D_KV)
    v2 = v_win.reshape(bt * WINDOW, D_KV)
    q = q_ref[...]
    rows = Q_PER_KV * bt
    rb = lax.broadcasted_iota(jnp.int32, (rows, bt * WINDOW), 0) % bt
    cb = lax.broadcasted_iota(jnp.int32, (rows, bt * WINDOW), 1) // WINDOW
    valid = rb == cb
    for h in range(N_KV_HEADS):
        kh = k2[:, h * HEAD_DIM:(h + 1) * HEAD_DIM]
        vh = v2[:, h * HEAD_DIM:(h + 1) * HEAD_DIM]
        qs = jnp.concatenate(
            [q[:, (h * Q_PER_KV + g) * HEAD_DIM:(h * Q_PER_KV + g + 1) * HEAD_DIM]
             for g in range(Q_PER_KV)], axis=0)
        s = _mm(qs, kh, True, _NT)
        s = jnp.where(valid, s, NEG_INF)
        o = _softmax_pv(s, _sink_column(sink_ref, h, rows, bt), vh, True)
        for g in range(Q_PER_KV):
            c = (h * Q_PER_KV + g) * HEAD_DIM
            o_ref[:, c:c + HEAD_DIM] = o[g * bt:(g + 1) * bt].astype(o_ref.dtype)


def _attn_sample(q, k_new, v_new, cache_k, cache_v, sinks):
    nbatch = q.shape[0]
    bt = SAMPLE_BT
    row = lambda i: (i, 0)
    win = lambda i: (i, 0, 0)
    return pl.pallas_call(
        _attn_sample_kernel,
        grid=(nbatch // bt,),
        in_specs=[
            pl.BlockSpec(memory_space=pltpu.SMEM),
            pl.BlockSpec((bt, D_Q), row),
            pl.BlockSpec((bt, D_KV), row),
            pl.BlockSpec((bt, D_KV), row),
            pl.BlockSpec((bt, WINDOW, D_KV), win),
            pl.BlockSpec((bt, WINDOW, D_KV), win),
        ],
        out_specs=[
            pl.BlockSpec((bt, D_Q), row),
            pl.BlockSpec((bt, WINDOW, D_KV), win),
            pl.BlockSpec((bt, WINDOW, D_KV), win),
        ],
        out_shape=(
            jax.ShapeDtypeStruct((nbatch, D_Q), F32),
            jax.ShapeDtypeStruct((nbatch, WINDOW, D_KV), F32),
            jax.ShapeDtypeStruct((nbatch, WINDOW, D_KV), F32),
        ),
        compiler_params=_params("parallel"),
        name="attn_sample",
    )(sinks, q, k_new, v_new, cache_k, cache_v)


def _lru_terms(xc, wcat_ref, ba_ref, bi_ref, lam_ref, precise):
    xcb = xc if precise else xc.astype(BF16)
    ya, yi = [], []
    for n in range(N_RNN_BLOCKS):
        y = _mm(xcb[:, n * RNN_BLOCK:(n + 1) * RNN_BLOCK], wcat_ref[n], precise)
        ya.append(y[:, :RNN_BLOCK])
        yi.append(y[:, RNN_BLOCK:])
    r = _sigmoid(jnp.concatenate(ya, axis=-1) + ba_ref[...])
    gate_i = _sigmoid(jnp.concatenate(yi, axis=-1) + bi_ref[...])
    neg_lam = -lam_ref[...]
    softplus = jnp.maximum(neg_lam, 0.0) + jnp.log1p(jnp.exp(-jnp.abs(neg_lam)))
    log_a = (-LRU_C * softplus) * r
    a = jnp.exp(log_a)
    b = jnp.sqrt(jnp.maximum(1.0 - a * a, 0.0)) * (gate_i * xc)
    return a, b


def _rnn_prompt_kernel(xr_ref, gy_ref, cw_ref, cb_ref, wcat_ref, ba_ref, bi_ref, lam_ref,
                       o_ref, hl_ref, xbuf, a_scr, b_scr, h_scr, hcar):
    tt = xr_ref.shape[0]
    groups = tt // SUBLANES

    @pl.when(pl.program_id(1) == 0)
    def _():
        xbuf[0:SUBLANES, :] = jnp.zeros((SUBLANES, D_RNN), F32)
        hcar[...] = jnp.zeros((SUBLANES, D_RNN), F32)

    x = xr_ref[...]
    xbuf[SUBLANES:, :] = x
    xc = cb_ref[...] + cw_ref[CONV_W - 1:CONV_W, :] * x
    for j in range(CONV_W - 1):
        s = CONV_W - 1 - j
        xc = xc + cw_ref[j:j + 1, :] * xbuf[SUBLANES - s:SUBLANES - s + tt, :]
    xbuf[0:SUBLANES, :] = x[tt - SUBLANES:, :]

    a, b = _lru_terms(xc, wcat_ref, ba_ref, bi_ref, lam_ref, False)
    a = a.reshape(groups, SUBLANES, D_RNN)
    b = b.reshape(groups, SUBLANES, D_RNN)
    step = lax.broadcasted_iota(jnp.int32, (groups, SUBLANES, D_RNN), 1)
    k = 1
    while k < SUBLANES:
        keep = step >= k
        a_sh = jnp.where(keep, pltpu.roll(a, k, axis=1), 1.0)
        b_sh = jnp.where(keep, pltpu.roll(b, k, axis=1), 0.0)
        b = a * b_sh + b
        a = a * a_sh
        k *= 2
    a_scr[...] = a
    b_scr[...] = b

    def chain(g, h_in):
        h = a_scr[g] * h_in + b_scr[g]
        h_scr[g] = h
        return jnp.broadcast_to(h[SUBLANES - 1:SUBLANES, :], (SUBLANES, D_RNN))

    h_last = lax.fori_loop(0, groups, chain, hcar[...])
    hcar[...] = h_last
    hl_ref[0] = h_last
    h = h_scr[...].reshape(tt, D_RNN)
    o_ref[...] = (h * gy_ref[...].astype(F32)).astype(o_ref.dtype)


def _rnn_prompt(xr, gy, conv_w, conv_b, wcat, b_a, b_i, lam, batch, seq, tt):
    nt = seq // tt
    row = lambda b, t: (b * nt + t, 0)
    groups = tt // SUBLANES
    return pl.pallas_call(
        _rnn_prompt_kernel,
        grid=(batch, nt),
        in_specs=[
            pl.BlockSpec((tt, D_RNN), row),
            pl.BlockSpec((tt, D_RNN), row),
            _full((CONV_W, D_RNN)),
            _full((1, D_RNN)),
            _full((N_RNN_BLOCKS, RNN_BLOCK, 2 * RNN_BLOCK)),
            _full((1, D_RNN)),
            _full((1, D_RNN)),
            _full((1, D_RNN)),
        ],
        out_specs=[
            pl.BlockSpec((tt, D_RNN), row),
            pl.BlockSpec((1, SUBLANES, D_RNN), lambda b, t: (b, 0, 0)),
        ],
        out_shape=(
            jax.ShapeDtypeStruct((batch * seq, D_RNN), BF16),
            jax.ShapeDtypeStruct((batch, SUBLANES, D_RNN), F32),
        ),
        scratch_shapes=[
            pltpu.VMEM((tt + SUBLANES, D_RNN), F32),
            pltpu.VMEM((groups, SUBLANES, D_RNN), F32),
            pltpu.VMEM((groups, SUBLANES, D_RNN), F32),
            pltpu.VMEM((groups, SUBLANES, D_RNN), F32),
            pltpu.VMEM((SUBLANES, D_RNN), F32),
        ],
        compiler_params=_params("parallel", "arbitrary"),
        name="rnn_prompt",
    )(xr, gy, conv_w, conv_b, wcat, b_a, b_i, lam)


def _rnn_sample_kernel(xr_ref, gy_ref, s0_ref, s1_ref, s2_ref, h_ref, cw_ref, cb_ref,
                       wcat_ref, ba_ref, bi_ref, lam_ref, o_ref, hn_ref):
    x = xr_ref[...]
    xc = (cb_ref[...] + cw_ref[0:1, :] * s0_ref[...] + cw_ref[1:2, :] * s1_ref[...]
          + cw_ref[2:3, :] * s2_ref[...] + cw_ref[3:4, :] * x)
    a, b = _lru_terms(xc, wcat_ref, ba_ref, bi_ref, lam_ref, True)
    h = a * h_ref[...] + b
    hn_ref[...] = h
    o_ref[...] = (h * gy_ref[...].astype(F32)).astype(o_ref.dtype)


def _rnn_sample(xr, gy, s0, s1, s2, h_prev, conv_w, conv_b, wcat, b_a, b_i, lam):
    n = xr.shape[0]
    act = _full((n, D_RNN))
    return pl.pallas_call(
        _rnn_sample_kernel,
        grid=(1,),
        in_specs=[act, act, act, act, act, act,
                  _full((CONV_W, D_RNN)), _full((1, D_RNN)),
                  _full((N_RNN_BLOCKS, RNN_BLOCK, 2 * RNN_BLOCK)),
                  _full((1, D_RNN)), _full((1, D_RNN)), _full((1, D_RNN))],
        out_specs=[act, act],
        out_shape=(jax.ShapeDtypeStruct((n, D_RNN), F32),
                   jax.ShapeDtypeStruct((n, D_RNN), F32)),
        compiler_params=_params("arbitrary"),
        name="rnn_sample",
    )(xr, gy, s0, s1, s2, h_prev, conv_w, conv_b, wcat, b_a, b_i, lam)


def _merge_kernel(x_ref, at_ref, rn_ref, sga_ref, sgr_ref, wa_ref, wr_ref, wo_ref, g_ref,
                  wrt_ref, brt_ref, tri_ref, x2_ref, xtm_ref, rec_ref, rect_ref, cnt_ref, cnt_scr,
                  *, precise, tiles_per_seg):
    ya = _mm(at_ref[...], wa_ref[...], precise)
    yr = _mm(rn_ref[...], wr_ref[...], precise)
    merged = sga_ref[...].astype(F32) * ya + sgr_ref[...].astype(F32) * yr
    x2 = x_ref[...] + _mm(merged, wo_ref[...], precise)
    x2_ref[...] = x2
    inv = lax.rsqrt(jnp.mean(x2 * x2, axis=-1, keepdims=True) + EPS)
    xn = x2 * inv * g_ref[...]

    logits = _mm(xn, wrt_ref[...], precise) + brt_ref[...]
    tm = logits.shape[0]
    lane = lax.broadcasted_iota(jnp.int32, (tm, LANES), 1)
    big = jnp.int32(LANES)
    is_grp = (lane >= N_EXPERTS) & (lane < N_EXPERTS + N_GROUPS)
    gl = jnp.where(is_grp, logits, NEG_INF)
    gmax = jnp.max(gl, axis=-1, keepdims=True)
    g_idx = jnp.min(jnp.where(gl == gmax, lane, big), axis=-1, keepdims=True) - N_EXPERTS
    p_g = 1.0 / jnp.sum(jnp.exp(gl - gmax), axis=-1, keepdims=True)
    in_grp = (lane // EXPERTS_PER_GROUP) == g_idx
    el = jnp.where(in_grp, logits, NEG_INF)
    v1 = jnp.max(el, axis=-1, keepdims=True)
    i1 = jnp.min(jnp.where(el == v1, lane, big), axis=-1, keepdims=True)
    el2 = jnp.where(lane == i1, NEG_INF, el)
    v2 = jnp.max(el2, axis=-1, keepdims=True)
    i2 = jnp.min(jnp.where(el2 == v2, lane, big), axis=-1, keepdims=True)
    e2 = jnp.exp(v2 - v1)
    w1 = p_g / (1.0 + e2)
    w2 = p_g * e2 / (1.0 + e2)

    @pl.when(pl.program_id(0) % tiles_per_seg == 0)
    def _():
        cnt_scr[...] = jnp.zeros_like(cnt_scr)

    hit = jnp.where(lane == i1, 1.0, jnp.where(lane == i2, 1.0, 0.0))
    before = jnp.dot(tri_ref[...], hit.astype(BF16), preferred_element_type=F32) + cnt_scr[0:1, :]
    r1 = jnp.sum(jnp.where(lane == i1, before, 0.0), axis=-1, keepdims=True)
    r2 = jnp.sum(jnp.where(lane == i2, before, 0.0), axis=-1, keepdims=True)
    counts = cnt_scr[...] + jnp.sum(hit, axis=0, keepdims=True)
    cnt_scr[...] = counts
    cnt_ref[0] = counts
    rec = jnp.where(lane == REC_W2, w2, 0.0)
    for field, val in ((REC_W1, w1), (REC_R2, r2), (REC_R1, r1),
                       (REC_E2, i2.astype(F32)), (REC_E1, i1.astype(F32))):
        rec = jnp.where(lane == field, val, rec)
    rec_ref[...] = rec
    rect_ref[...] = rec.T

    for c in range(TOKEN_ROWS):
        xtm_ref[pl.ds(c, tm, stride=TOKEN_ROWS), :] = xn[:, c * LANES:(c + 1) * LANES]


REC_E1, REC_E2, REC_R1, REC_R2, REC_W1, REC_W2 = range(6)
TOKEN_ROWS = D_MODEL // LANES


def _merge(x, attn, rnn, sga, sgr, wa, wr, wo, g2, w_route, b_route, tri, tm, seg, precise):
    n = x.shape[0]
    tiles_per_seg = seg // tm
    row = lambda i: (i, 0)
    return pl.pallas_call(
        functools.partial(_merge_kernel, precise=precise, tiles_per_seg=tiles_per_seg),
        grid=(n // tm,),
        in_specs=[
            pl.BlockSpec((tm, D_MODEL), row),
            pl.BlockSpec((tm, D_Q), row),
            pl.BlockSpec((tm, D_RNN), row),
            pl.BlockSpec((tm, D_MODEL), row),
            pl.BlockSpec((tm, D_MODEL), row),
            _full((D_Q, D_MODEL)),
            _full((D_RNN, D_MODEL)),
            _full((D_MODEL, D_MODEL)),
            _full((1, D_MODEL)),
            _full((D_MODEL, LANES)),
            _full((1, LANES)),
            _full((tm, tm)),
        ],
        out_specs=[
            pl.BlockSpec((tm, D_MODEL), row),
            pl.BlockSpec((tm * TOKEN_ROWS, LANES), row),
            pl.BlockSpec((tm, LANES), row),
            pl.BlockSpec((LANES, tm), lambda i: (0, i)),
            pl.BlockSpec((1, SUBLANES, LANES), lambda i: (i // tiles_per_seg, 0, 0)),
        ],
        out_shape=(
            jax.ShapeDtypeStruct((n, D_MODEL), F32),
            jax.ShapeDtypeStruct((n * TOKEN_ROWS, LANES), F32),
            jax.ShapeDtypeStruct((n, LANES), F32),
            jax.ShapeDtypeStruct((LANES, n), F32),
            jax.ShapeDtypeStruct((n // seg, SUBLANES, LANES), F32),
        ),
        scratch_shapes=[pltpu.VMEM((SUBLANES, LANES), F32)],
        compiler_params=_params("arbitrary"),
        name="merge",
    )(x, attn, rnn, sga, sgr, wa, wr, wo, g2, w_route, b_route, tri)


MOE_CHUNK = 128
MOE_TAIL = 64


def _seg_rows(seg):
    return 2 * seg + N_EXPERTS * SUBLANES + MOE_CHUNK


def _token_rows(i):
    return pl.ds(pl.multiple_of(i * TOKEN_ROWS, TOKEN_ROWS), TOKEN_ROWS)


def _expert_kernel(off_ref, nch_ref, tail_ref, slot_ref, xtm_ref, wg_ref, wu_ref, wd_ref,
                   ys_ref, xs_scr, *, seg):
    s = pl.program_id(0)
    e = pl.program_id(1)

    @pl.when((s == 0) & (e == 0))
    def _():
        xs_scr[...] = jnp.zeros_like(xs_scr)

    @pl.when(e == 0)
    def _():
        def dispatch(g, carry):
            for j in range(SUBLANES):
                t = g * SUBLANES + j
                row = xtm_ref[_token_rows(t), :]
                for k in range(2):
                    xs_scr[_token_rows(slot_ref[0, 0, k * seg + t]), :] = row
            return carry

        lax.fori_loop(0, seg // SUBLANES, dispatch, 0)

    base = off_ref[s * N_EXPERTS + e]

    def run_chunk(row0, rows):
        r0 = pl.multiple_of(row0 * TOKEN_ROWS, SUBLANES * TOKEN_ROWS)
        x = jnp.concatenate(
            [xs_scr[pl.ds(r0 + j, rows, stride=TOKEN_ROWS), :] for j in range(TOKEN_ROWS)],
            axis=-1).astype(BF16)
        hg = jnp.dot(x, wg_ref[0], preferred_element_type=F32)
        hu = jnp.dot(x, wu_ref[0], preferred_element_type=F32)
        h = (hg * _sigmoid(hg)) * hu
        y = jnp.dot(h.astype(BF16), wd_ref[0], preferred_element_type=F32)
        for j in range(TOKEN_ROWS):
            ys_ref[pl.ds(r0 + j, rows, stride=TOKEN_ROWS), :] = y[:, j * LANES:(j + 1) * LANES]

    n_full = nch_ref[s * N_EXPERTS + e]

    def chunk(c, carry):
        run_chunk(base + c * MOE_CHUNK, MOE_CHUNK)
        return carry

    lax.fori_loop(0, n_full, chunk, 0)

    @pl.when(tail_ref[s * N_EXPERTS + e] > 0)
    def _():
        run_chunk(base + n_full * MOE_CHUNK, MOE_TAIL)


def _experts(off, nch, tail, slot, xtm, wg, wu, wd, seg):
    n_seg = xtm.shape[0] // (seg * TOKEN_ROWS)
    rows = _seg_rows(seg) * TOKEN_ROWS
    expert = lambda s, e, *_: (e, 0, 0)
    grid_spec = pltpu.PrefetchScalarGridSpec(
        num_scalar_prefetch=3,
        grid=(n_seg, N_EXPERTS),
        in_specs=[
            pl.BlockSpec((1, 1, 2 * seg), lambda s, e, *_: (s, 0, 0), memory_space=pltpu.SMEM),
            pl.BlockSpec((seg * TOKEN_ROWS, LANES), lambda s, e, *_: (s, 0),
                         pipeline_mode=pl.Buffered(1)),
            pl.BlockSpec((1, D_MODEL, D_EXPERT), expert),
            pl.BlockSpec((1, D_MODEL, D_EXPERT), expert),
            pl.BlockSpec((1, D_EXPERT, D_MODEL), expert),
        ],
        out_specs=pl.BlockSpec((rows, LANES), lambda s, e, *_: (s, 0), pipeline_mode=pl.Buffered(1)),
        scratch_shapes=[pltpu.VMEM((rows, LANES), F32)],
    )
    return pl.pallas_call(
        functools.partial(_expert_kernel, seg=seg),
        grid_spec=grid_spec,
        out_shape=jax.ShapeDtypeStruct((n_seg * rows, LANES), F32),
        compiler_params=_params("arbitrary", "arbitrary"),
        name="experts",
    )(off, nch, tail, slot, xtm, wg, wu, wd)


def _combine_kernel(slot_ref, x2_ref, rec_ref, ys_ref, o_ref, g1, g2):
    tc = x2_ref.shape[0]

    def gather(g, carry):
        for j in range(SUBLANES):
            t = g * SUBLANES + j
            g1[_token_rows(t), :] = ys_ref[_token_rows(slot_ref[0, 0, t]), :]
            g2[_token_rows(t), :] = ys_ref[_token_rows(slot_ref[0, 0, tc + t]), :]
        return carry

    lax.fori_loop(0, tc // SUBLANES, gather, 0)
    rec = rec_ref[...]
    lane = lax.broadcasted_iota(jnp.int32, rec.shape, 1)
    w1 = jnp.sum(jnp.where(lane == REC_W1, rec, 0.0), axis=-1, keepdims=True)
    w2 = jnp.sum(jnp.where(lane == REC_W2, rec, 0.0), axis=-1, keepdims=True)
    for j in range(TOKEN_ROWS):
        cols = slice(j * LANES, (j + 1) * LANES)
        o_ref[:, cols] = (x2_ref[:, cols] + w1 * g1[pl.ds(j, tc, stride=TOKEN_ROWS), :]
                          + w2 * g2[pl.ds(j, tc, stride=TOKEN_ROWS), :])


def _combine(slot, x2, rec, ys, seg, tc):
    n = x2.shape[0]
    tiles_per_seg = seg // tc
    rows = _seg_rows(seg) * TOKEN_ROWS
    tile = lambda s, i: (s * tiles_per_seg + i, 0)
    return pl.pallas_call(
        _combine_kernel,
        grid=(n // seg, tiles_per_seg),
        in_specs=[
            pl.BlockSpec((1, 1, 2 * tc), lambda s, i: (s * tiles_per_seg + i, 0, 0),
                         memory_space=pltpu.SMEM),
            pl.BlockSpec((tc, D_MODEL), tile),
            pl.BlockSpec((tc, LANES), tile),
            pl.BlockSpec((rows, LANES), lambda s, i: (s, 0), pipeline_mode=pl.Buffered(1)),
        ],
        out_specs=pl.BlockSpec((tc, D_MODEL), tile),
        out_shape=jax.ShapeDtypeStruct((n, D_MODEL), F32),
        scratch_shapes=[pltpu.VMEM((tc * TOKEN_ROWS, LANES), F32),
                        pltpu.VMEM((tc * TOKEN_ROWS, LANES), F32)],
        compiler_params=_params("arbitrary", "arbitrary"),
        name="combine",
    )(slot, x2, rec, ys)


def _plan(rect, cnt, seg, tc):
    n = rect.shape[1]
    expert = rect[REC_E1:REC_E2 + 1].astype(jnp.int32)
    rank = rect[REC_R1:REC_R2 + 1].astype(jnp.int32)
    counts = cnt[:, 0, :N_EXPERTS].astype(jnp.int32)
    padded = (counts + SUBLANES - 1) // SUBLANES * SUBLANES
    off = jnp.cumsum(padded, axis=1) - padded
    n_full = counts // MOE_CHUNK
    rem = counts - n_full * MOE_CHUNK
    n_full = n_full + (rem > MOE_TAIL)
    tail = ((rem > 0) & (rem <= MOE_TAIL)).astype(jnp.int32)
    off_tok = jnp.repeat(off.T, seg, axis=1)
    hit = expert[:, None, :] == jnp.arange(N_EXPERTS, dtype=jnp.int32)[None, :, None]
    slot = rank + jnp.sum(jnp.where(hit, off_tok[None], 0), axis=1)

    def blocked(size):
        return slot.reshape(2, n // size, size).transpose(1, 0, 2).reshape(n // size, 1, 2 * size)

    return off.reshape(-1), n_full.reshape(-1), tail.reshape(-1), blocked(seg), blocked(tc)


def _rope_tables(pos):
    half = HEAD_DIM // 2
    inv_freq = ROPE_THETA ** (-jnp.arange(half, dtype=F32) / half)
    ang = pos[:, None] * inv_freq[None, :]
    cos = jnp.cos(ang)
    sin = jnp.sin(ang)
    reps = LANES // HEAD_DIM
    cos_t = jnp.tile(jnp.concatenate([cos, cos], axis=-1), (1, reps))
    sin_t = jnp.tile(jnp.concatenate([-sin, sin], axis=-1), (1, reps))
    return cos_t, sin_t


def kernel(x_prompt, x_sample, cache_k_win, cache_v_win, state_conv, state_lru_h, attn_norm_g, w_in, q_norm_g, k_norm_g, attn_sinks, conv_w, conv_b, w_lru_a, b_lru_a, w_lru_i, b_lru_i, lru_lambda, w_br_attn, w_br_rnn, w_out, ffn_norm_g, w_route_group, b_route_group, w_route_expert, b_route_expert, w_exp_gate, w_exp_up, w_exp_down):
    batch, seq, _ = x_prompt.shape
    dec_batch, dec_seq, _ = x_sample.shape
    depth = w_in.shape[0]
    assert depth == 1 and dec_seq == 1
    l = 0

    w_in_f = w_in[l]
    qkg = jnp.concatenate([jnp.tile(q_norm_g[l], N_Q_HEADS), jnp.tile(k_norm_g[l], N_KV_HEADS)])[None, :]
    wcat_f = jnp.concatenate([w_lru_a[l], w_lru_i[l]], axis=-1)
    wa_f, wr_f, wo_f = w_br_attn[l], w_br_rnn[l], w_out[l]
    w_route_f = jnp.concatenate(
        [w_route_expert[l], w_route_group[l],
         jnp.zeros((D_MODEL, LANES - N_EXPERTS - N_GROUPS), F32)], axis=-1)
    w_in_b, wcat = w_in_f.astype(BF16), wcat_f.astype(BF16)
    wa_b, wr_b, wo_b, w_route = (w.astype(BF16) for w in (wa_f, wr_f, wo_f, w_route_f))
    b_route = jnp.concatenate(
        [b_route_expert[l], b_route_group[l], jnp.zeros((LANES - N_EXPERTS - N_GROUPS,), F32)])[None, :]
    wg, wu, wd = (w[l].astype(BF16) for w in (w_exp_gate, w_exp_up, w_exp_down))
    g1 = attn_norm_g[l][None, :]
    g2 = ffn_norm_g[l][None, :]
    cw, cb = conv_w[l], conv_b[l][None, :]
    b_a, b_i, lam = b_lru_a[l][None, :], b_lru_i[l][None, :], lru_lambda[l][None, :]
    sinks = attn_sinks[l]

    def tail(x, attn, rnn, sga, sgr, tm, seg, precise):
        wa, wr, wo, wrt = (wa_f, wr_f, wo_f, w_route_f) if precise else (wa_b, wr_b, wo_b, w_route)
        tri = jnp.tril(jnp.ones((tm, tm), BF16), -1)
        x2, xtm, rec, rect, cnt = _merge(x, attn, rnn, sga, sgr, wa, wr, wo, g2, wrt, b_route, tri,
                                         tm, seg, precise)
        off, nch, tail_flag, slot_seg, slot_tile = _plan(rect, cnt, seg, tm)
        ys = _experts(off, nch, tail_flag, slot_seg, xtm, wg, wu, wd, seg)
        return _combine(slot_tile, x2, rec, ys, seg, tm)

    xp = x_prompt.reshape(batch * seq, D_MODEL)
    cos_p, sin_p = _rope_tables(jnp.arange(seq, dtype=F32))
    q, k, v, xr, gy, sga, sgr = _proj(xp, g1, w_in_b, cos_p, sin_p, qkg, 512, False)
    attn = _attn_prompt(q, k, v, sinks, batch, seq)
    rnn, h_last = _rnn_prompt(xr, gy, cw, cb, wcat, b_a, b_i, lam, batch, seq, 256)
    y_prompt = tail(xp, attn, rnn, sga, sgr, 512, 2048, False).reshape(batch, seq, D_MODEL)
    def last_rows(a, rows):
        return a.reshape(batch, seq, a.shape[-1])[:, seq - rows:]

    k_win_p = last_rows(k, WINDOW).reshape(1, batch, WINDOW, N_KV_HEADS, HEAD_DIM)
    v_win_p = last_rows(v, WINDOW).reshape(1, batch, WINDOW, N_KV_HEADS, HEAD_DIM)
    conv_p = last_rows(xr, CONV_W - 1)[None]
    h_p = h_last[None, :, 0, :]

    xs = x_sample.reshape(dec_batch, D_MODEL)
    cos_s, sin_s = _rope_tables(jnp.full((dec_batch,), PAST_LEN, F32))
    qs, ks, vs, xrs, gys, sgas, sgrs = _proj(xs, g1, w_in_f, cos_s, sin_s, qkg, dec_batch, True)
    ck = cache_k_win[l].reshape(dec_batch, WINDOW, D_KV)
    cv = cache_v_win[l].reshape(dec_batch, WINDOW, D_KV)
    attn_s, k_win_s, v_win_s = _attn_sample(qs, ks, vs, ck, cv, sinks)
    sc = state_conv[l]
    rnn_s, h_s = _rnn_sample(xrs, gys, sc[:, 0], sc[:, 1], sc[:, 2], state_lru_h[l],
                             cw, cb, wcat_f, b_a, b_i, lam)
    y_sample = tail(xs, attn_s, rnn_s, sgas, sgrs, dec_batch, dec_batch, True).reshape(dec_batch, 1, D_MODEL)
    conv_s = jnp.stack([sc[:, 1], sc[:, 2], xrs], axis=1)[None]

    return (y_prompt, y_sample, k_win_p, v_win_p, conv_p, h_p,
            k_win_s.reshape(1, dec_batch, WINDOW, N_KV_HEADS, HEAD_DIM),
            v_win_s.reshape(1, dec_batch, WINDOW, N_KV_HEADS, HEAD_DIM),
            conv_s, h_s[None])
```

```python
import functools

import jax
import jax.numpy as jnp
from jax import lax
from jax.experimental import pallas as pl
from jax.experimental.pallas import tpu as pltpu

D_MODEL = 1024
HEAD_DIM = 64
N_Q_HEADS = 8
N_KV_HEADS = 2
Q_PER_KV = N_Q_HEADS // N_KV_HEADS
WINDOW = 128
ATTN_BLOCK = 128
ROPE_THETA = 10000.0
SCALE = HEAD_DIM ** -0.5
NEG_INF = -1e30
D_RNN = 1280
N_RNN_BLOCKS = 10
RNN_BLOCK = D_RNN // N_RNN_BLOCKS
CONV_W = 4
LRU_C = 8.0
N_GROUPS = 4
EXPERTS_PER_GROUP = 8
N_EXPERTS = N_GROUPS * EXPERTS_PER_GROUP
D_EXPERT = 256
PAST_LEN = 16384
EPS = 1e-6
D_Q = N_Q_HEADS * HEAD_DIM
D_KV = N_KV_HEADS * HEAD_DIM
D_IN = D_Q + 2 * D_KV + 2 * D_RNN + 2 * D_MODEL
OFF_K = D_Q
OFF_V = OFF_K + D_KV
OFF_XR = OFF_V + D_KV
OFF_YR = OFF_XR + D_RNN
OFF_GA = OFF_YR + D_RNN
OFF_GR = OFF_GA + D_MODEL

LANES = 128
SUBLANES = 8
VMEM_LIMIT = 56 * 1024 * 1024

F32 = jnp.float32
BF16 = jnp.bfloat16


def _params(*sem):
    return pltpu.CompilerParams(dimension_semantics=sem, vmem_limit_bytes=VMEM_LIMIT)


def _sigmoid(x):
    return 1.0 / (1.0 + jnp.exp(-x))


def _gelu_tanh(x):
    c = 0.7978845608028654
    return 0.5 * x * (1.0 + jnp.tanh(c * (x + 0.044715 * (x * x * x))))


def _full(shape, single_buffer=False):
    index_map = lambda *_: (0,) * len(shape)
    if single_buffer:
        return pl.BlockSpec(shape, index_map, pipeline_mode=pl.Buffered(1))
    return pl.BlockSpec(shape, index_map)


def _mm(a, b, precise, dims=None):
    if precise:
        a, b, prec = a.astype(F32), b.astype(F32), lax.Precision.HIGHEST
    else:
        a, b, prec = a.astype(BF16), b.astype(BF16), None
    if dims is None:
        return jnp.dot(a, b, preferred_element_type=F32, precision=prec)
    return lax.dot_general(a, b, dims, preferred_element_type=F32, precision=prec)


_NT = (((1,), (1,)), ((), ()))


def _proj_kernel(x_ref, g_ref, w_ref, cos_ref, sin_ref, qkg_ref,
                 q_ref, k_ref, v_ref, xr_ref, gy_ref, sga_ref, sgr_ref, *, precise):
    x = x_ref[...]
    inv = lax.rsqrt(jnp.mean(x * x, axis=-1, keepdims=True) + EPS)
    xn = x * inv * g_ref[...]
    if not precise:
        xn = xn.astype(BF16)

    def proj(lo, hi):
        return _mm(xn, w_ref[:, lo:hi], precise)

    qk = proj(0, OFF_V)
    tm = qk.shape[0]
    lane = lax.broadcasted_iota(jnp.int32, (tm, LANES), 1)
    lo_head = lane < HEAD_DIM
    first_half = (lane % HEAD_DIM) < (HEAD_DIM // 2)
    cos = cos_ref[...]
    sin = sin_ref[...]
    for g in range(OFF_V // LANES):
        seg = qk[:, g * LANES:(g + 1) * LANES]
        sq = seg * seg
        s_lo = jnp.sum(jnp.where(lo_head, sq, 0.0), axis=-1, keepdims=True)
        s_hi = jnp.sum(jnp.where(lo_head, 0.0, sq), axis=-1, keepdims=True)
        ms = jnp.where(lo_head, s_lo, s_hi) * (1.0 / HEAD_DIM)
        normed = seg * lax.rsqrt(ms + EPS) * qkg_ref[:, g * LANES:(g + 1) * LANES]
        partner = jnp.where(first_half,
                            pltpu.roll(normed, LANES - HEAD_DIM // 2, axis=1),
                            pltpu.roll(normed, HEAD_DIM // 2, axis=1))
        roped = normed * cos + partner * sin
        if g < D_Q // LANES:
            q_ref[:, g * LANES:(g + 1) * LANES] = (roped * SCALE).astype(q_ref.dtype)
        else:
            k_ref[...] = roped
    v_ref[...] = proj(OFF_V, OFF_XR)
    xr_ref[...] = proj(OFF_XR, OFF_YR)
    gy_ref[...] = _gelu_tanh(proj(OFF_YR, OFF_GA)).astype(gy_ref.dtype)
    sga_ref[...] = _sigmoid(proj(OFF_GA, OFF_GR)).astype(sga_ref.dtype)
    sgr_ref[...] = _sigmoid(proj(OFF_GR, D_IN)).astype(sgr_ref.dtype)


def _proj(x, g, w_in, cos_t, sin_t, qkg, tm, precise):
    n = x.shape[0]
    t_blocks = cos_t.shape[0] // tm
    row = lambda i: (i, 0)
    act = F32 if precise else BF16
    out_shape = (
        jax.ShapeDtypeStruct((n, D_Q), act),
        jax.ShapeDtypeStruct((n, D_KV), F32),
        jax.ShapeDtypeStruct((n, D_KV), F32),
        jax.ShapeDtypeStruct((n, D_RNN), F32),
        jax.ShapeDtypeStruct((n, D_RNN), act),
        jax.ShapeDtypeStruct((n, D_MODEL), act),
        jax.ShapeDtypeStruct((n, D_MODEL), act),
    )
    return pl.pallas_call(
        functools.partial(_proj_kernel, precise=precise),
        grid=(n // tm,),
        in_specs=[
            pl.BlockSpec((tm, D_MODEL), row),
            _full((1, D_MODEL)),
            _full((D_MODEL, D_IN), single_buffer=True),
            pl.BlockSpec((tm, LANES), lambda i: (i % t_blocks, 0)),
            pl.BlockSpec((tm, LANES), lambda i: (i % t_blocks, 0)),
            _full((1, OFF_V)),
        ],
        out_specs=[pl.BlockSpec((tm, s.shape[1]), row) for s in out_shape],
        out_shape=out_shape,
        compiler_params=_params("parallel"),
        name="proj",
    )(x, g, w_in, cos_t, sin_t, qkg)


def _softmax_pv(s, sink, v2, precise):
    m = jnp.maximum(jnp.max(s, axis=-1, keepdims=True), sink)
    p = jnp.exp(s - m)
    denom = jnp.sum(p, axis=-1, keepdims=True) + jnp.exp(sink - m)
    return _mm(p, v2, precise) * (1.0 / denom)


def _sink_column(sink_ref, h, rows, rows_per_head):
    r = lax.broadcasted_iota(jnp.int32, (rows, 1), 0) // rows_per_head
    col = jnp.full((rows, 1), sink_ref[h * Q_PER_KV], F32)
    for g in range(1, Q_PER_KV):
        col = jnp.where(r == g, sink_ref[h * Q_PER_KV + g], col)
    return col


def _attn_prompt_kernel(sink_ref, q_ref, kc_ref, kp_ref, vc_ref, vp_ref, o_ref):
    n = pl.program_id(1)
    blk = ATTN_BLOCK
    q = q_ref[...]
    k2 = jnp.concatenate([kp_ref[...], kc_ref[...]], axis=0).astype(BF16)
    v2 = jnp.concatenate([vp_ref[...], vc_ref[...]], axis=0).astype(BF16)
    rows = Q_PER_KV * blk
    i = lax.broadcasted_iota(jnp.int32, (rows, 2 * blk), 0) % blk
    j = lax.broadcasted_iota(jnp.int32, (rows, 2 * blk), 1)
    d = j - i
    dmin = jnp.where(n > 0, 1, jnp.maximum(1, blk - i))
    valid = (d >= dmin) & (d <= WINDOW)
    for h in range(N_KV_HEADS):
        kh = k2[:, h * HEAD_DIM:(h + 1) * HEAD_DIM]
        vh = v2[:, h * HEAD_DIM:(h + 1) * HEAD_DIM]
        qs = jnp.concatenate(
            [q[:, (h * Q_PER_KV + g) * HEAD_DIM:(h * Q_PER_KV + g + 1) * HEAD_DIM]
             for g in range(Q_PER_KV)], axis=0)
        s = _mm(qs, kh, False, _NT)
        s = jnp.where(valid, s, NEG_INF)
        o = _softmax_pv(s, _sink_column(sink_ref, h, rows, blk), vh, False)
        for g in range(Q_PER_KV):
            c = (h * Q_PER_KV + g) * HEAD_DIM
            o_ref[:, c:c + HEAD_DIM] = o[g * blk:(g + 1) * blk].astype(o_ref.dtype)


def _attn_prompt(q, k, v, sinks, batch, seq):
    nb = seq // ATTN_BLOCK
    cur = lambda b, n: (b * nb + n, 0)
    prev = lambda b, n: (b * nb + jnp.maximum(n - 1, 0), 0)
    return pl.pallas_call(
        _attn_prompt_kernel,
        grid=(batch, nb),
        in_specs=[
            pl.BlockSpec(memory_space=pltpu.SMEM),
            pl.BlockSpec((ATTN_BLOCK, D_Q), cur),
            pl.BlockSpec((ATTN_BLOCK, D_KV), cur),
            pl.BlockSpec((ATTN_BLOCK, D_KV), prev),
            pl.BlockSpec((ATTN_BLOCK, D_KV), cur),
            pl.BlockSpec((ATTN_BLOCK, D_KV), prev),
        ],
        out_specs=pl.BlockSpec((ATTN_BLOCK, D_Q), cur),
        out_shape=jax.ShapeDtypeStruct((batch * seq, D_Q), BF16),
        compiler_params=_params("parallel", "parallel"),
        name="attn_prompt",
    )(sinks, q, k, k, v, v)


SAMPLE_BT = 8


def _attn_sample_kernel(sink_ref, q_ref, kn_ref, vn_ref, kc_ref, vc_ref, o_ref, ko_ref, vo_ref):
    bt = SAMPLE_BT
    w = lax.broadcasted_iota(jnp.int32, (bt, WINDOW, D_KV), 1)

    def shifted(cache_ref, new_ref):
        rolled = pltpu.roll(cache_ref[...], WINDOW - 1, axis=1)
        return jnp.where(w == WINDOW - 1, new_ref[...][:, None, :], rolled)

    k_win = shifted(kc_ref, kn_ref)
    v_win = shifted(vc_ref, vn_ref)
    ko_ref[...] = k_win
    vo_ref[...] = v_win
    k2 = k_win.reshape(bt * WINDOW, D_KV)
    v2 = v_win.reshape(bt * WINDOW, D_KV)
    q = q_ref[...]
    rows = Q_PER_KV * bt
    rb = lax.broadcasted_iota(jnp.int32, (rows, bt * WINDOW), 0) % bt
    cb = lax.broadcasted_iota(jnp.int32, (rows, bt * WINDOW), 1) // WINDOW
    valid = rb == cb
    for h in range(N_KV_HEADS):
        kh = k2[:, h * HEAD_DIM:(h + 1) * HEAD_DIM]
        vh = v2[:, h * HEAD_DIM:(h + 1) * HEAD_DIM]
        qs = jnp.concatenate(
            [q[:, (h * Q_PER_KV + g) * HEAD_DIM:(h * Q_PER_KV + g + 1) * HEAD_DIM]
             for g in range(Q_PER_KV)], axis=0)
        s = _mm(qs, kh, True, _NT)
        s = jnp.where(valid, s, NEG_INF)
        o = _softmax_pv(s, _sink_column(sink_ref, h, rows, bt), vh, True)
        for g in range(Q_PER_KV):
            c = (h * Q_PER_KV + g) * HEAD_DIM
            o_ref[:, c:c + HEAD_DIM] = o[g * bt:(g + 1) * bt].astype(o_ref.dtype)


def _attn_sample(q, k_new, v_new, cache_k, cache_v, sinks):
    nbatch = q.shape[0]
    bt = SAMPLE_BT
    row = lambda i: (i, 0)
    win = lambda i: (i, 0, 0)
    return pl.pallas_call(
        _attn_sample_kernel,
        grid=(nbatch // bt,),
        in_specs=[
            pl.BlockSpec(memory_space=pltpu.SMEM),
            pl.BlockSpec((bt, D_Q), row),
            pl.BlockSpec((bt, D_KV), row),
            pl.BlockSpec((bt, D_KV), row),
            pl.BlockSpec((bt, WINDOW, D_KV), win),
            pl.BlockSpec((bt, WINDOW, D_KV), win),
        ],
        out_specs=[
            pl.BlockSpec((bt, D_Q), row),
            pl.BlockSpec((bt, WINDOW, D_KV), win),
            pl.BlockSpec((bt, WINDOW, D_KV), win),
        ],
        out_shape=(
            jax.ShapeDtypeStruct((nbatch, D_Q), F32),
            jax.ShapeDtypeStruct((nbatch, WINDOW, D_KV), F32),
            jax.ShapeDtypeStruct((nbatch, WINDOW, D_KV), F32),
        ),
        compiler_params=_params("parallel"),
        name="attn_sample",
    )(sinks, q, k_new, v_new, cache_k, cache_v)


def _lru_terms(xc, wcat_ref, ba_ref, bi_ref, lam_ref, precise):
    xcb = xc if precise else xc.astype(BF16)
    ya, yi = [], []
    for n in range(N_RNN_BLOCKS):
        y = _mm(xcb[:, n * RNN_BLOCK:(n + 1) * RNN_BLOCK], wcat_ref[n], precise)
        ya.append(y[:, :RNN_BLOCK])
        yi.append(y[:, RNN_BLOCK:])
    r = _sigmoid(jnp.concatenate(ya, axis=-1) + ba_ref[...])
    gate_i = _sigmoid(jnp.concatenate(yi, axis=-1) + bi_ref[...])
    neg_lam = -lam_ref[...]
    softplus = jnp.maximum(neg_lam, 0.0) + jnp.log1p(jnp.exp(-jnp.abs(neg_lam)))
    log_a = (-LRU_C * softplus) * r
    a = jnp.exp(log_a)
    b = jnp.sqrt(jnp.maximum(1.0 - a * a, 0.0)) * (gate_i * xc)
    return a, b


def _rnn_prompt_kernel(xr_ref, gy_ref, cw_ref, cb_ref, wcat_ref, ba_ref, bi_ref, lam_ref,
                       o_ref, hl_ref, xbuf, a_scr, b_scr, h_scr, hcar):
    tt = xr_ref.shape[0]
    groups = tt // SUBLANES

    @pl.when(pl.program_id(1) == 0)
    def _():
        xbuf[0:SUBLANES, :] = jnp.zeros((SUBLANES, D_RNN), F32)
        hcar[...] = jnp.zeros((SUBLANES, D_RNN), F32)

    x = xr_ref[...]
    xbuf[SUBLANES:, :] = x
    xc = cb_ref[...] + cw_ref[CONV_W - 1:CONV_W, :] * x
    for j in range(CONV_W - 1):
        s = CONV_W - 1 - j
        xc = xc + cw_ref[j:j + 1, :] * xbuf[SUBLANES - s:SUBLANES - s + tt, :]
    xbuf[0:SUBLANES, :] = x[tt - SUBLANES:, :]

    a, b = _lru_terms(xc, wcat_ref, ba_ref, bi_ref, lam_ref, False)
    a = a.reshape(groups, SUBLANES, D_RNN)
    b = b.reshape(groups, SUBLANES, D_RNN)
    step = lax.broadcasted_iota(jnp.int32, (groups, SUBLANES, D_RNN), 1)
    k = 1
    while k < SUBLANES:
        keep = step >= k
        a_sh = jnp.where(keep, pltpu.roll(a, k, axis=1), 1.0)
        b_sh = jnp.where(keep, pltpu.roll(b, k, axis=1), 0.0)
        b = a * b_sh + b
        a = a * a_sh
        k *= 2
    a_scr[...] = a
    b_scr[...] = b

    def chain(g, h_in):
        h = a_scr[g] * h_in + b_scr[g]
        h_scr[g] = h
        return jnp.broadcast_to(h[SUBLANES - 1:SUBLANES, :], (SUBLANES, D_RNN))

    h_last = lax.fori_loop(0, groups, chain, hcar[...])
    hcar[...] = h_last
    hl_ref[0] = h_last
    h = h_scr[...].reshape(tt, D_RNN)
    o_ref[...] = (h * gy_ref[...].astype(F32)).astype(o_ref.dtype)


def _rnn_prompt(xr, gy, conv_w, conv_b, wcat, b_a, b_i, lam, batch, seq, tt):
    nt = seq // tt
    row = lambda b, t: (b * nt + t, 0)
    groups = tt // SUBLANES
    return pl.pallas_call(
        _rnn_prompt_kernel,
        grid=(batch, nt),
        in_specs=[
            pl.BlockSpec((tt, D_RNN), row),
            pl.BlockSpec((tt, D_RNN), row),
            _full((CONV_W, D_RNN)),
            _full((1, D_RNN)),
            _full((N_RNN_BLOCKS, RNN_BLOCK, 2 * RNN_BLOCK)),
            _full((1, D_RNN)),
            _full((1, D_RNN)),
            _full((1, D_RNN)),
        ],
        out_specs=[
            pl.BlockSpec((tt, D_RNN), row),
            pl.BlockSpec((1, SUBLANES, D_RNN), lambda b, t: (b, 0, 0)),
        ],
        out_shape=(
            jax.ShapeDtypeStruct((batch * seq, D_RNN), BF16),
            jax.ShapeDtypeStruct((batch, SUBLANES, D_RNN), F32),
        ),
        scratch_shapes=[
            pltpu.VMEM((tt + SUBLANES, D_RNN), F32),
            pltpu.VMEM((groups, SUBLANES, D_RNN), F32),
            pltpu.VMEM((groups, SUBLANES, D_RNN), F32),
            pltpu.VMEM((groups, SUBLANES, D_RNN), F32),
            pltpu.VMEM((SUBLANES, D_RNN), F32),
        ],
        compiler_params=_params("parallel", "arbitrary"),
        name="rnn_prompt",
    )(xr, gy, conv_w, conv_b, wcat, b_a, b_i, lam)


def _rnn_sample_kernel(xr_ref, gy_ref, s0_ref, s1_ref, s2_ref, h_ref, cw_ref, cb_ref,
                       wcat_ref, ba_ref, bi_ref, lam_ref, o_ref, hn_ref):
    x = xr_ref[...]
    xc = (cb_ref[...] + cw_ref[0:1, :] * s0_ref[...] + cw_ref[1:2, :] * s1_ref[...]
          + cw_ref[2:3, :] * s2_ref[...] + cw_ref[3:4, :] * x)
    a, b = _lru_terms(xc, wcat_ref, ba_ref, bi_ref, lam_ref, True)
    h = a * h_ref[...] + b
    hn_ref[...] = h
    o_ref[...] = (h * gy_ref[...].astype(F32)).astype(o_ref.dtype)


def _rnn_sample(xr, gy, s0, s1, s2, h_prev, conv_w, conv_b, wcat, b_a, b_i, lam):
    n = xr.shape[0]
    act = _full((n, D_RNN))
    return pl.pallas_call(
        _rnn_sample_kernel,
        grid=(1,),
        in_specs=[act, act, act, act, act, act,
                  _full((CONV_W, D_RNN)), _full((1, D_RNN)),
                  _full((N_RNN_BLOCKS, RNN_BLOCK, 2 * RNN_BLOCK)),
                  _full((1, D_RNN)), _full((1, D_RNN)), _full((1, D_RNN))],
        out_specs=[act, act],
        out_shape=(jax.ShapeDtypeStruct((n, D_RNN), F32),
                   jax.ShapeDtypeStruct((n, D_RNN), F32)),
        compiler_params=_params("arbitrary"),
        name="rnn_sample",
    )(xr, gy, s0, s1, s2, h_prev, conv_w, conv_b, wcat, b_a, b_i, lam)


def _merge_kernel(x_ref, at_ref, rn_ref, sga_ref, sgr_ref, wa_ref, wr_ref, wo_ref, g_ref,
                  wrt_ref, brt_ref, tri_ref, x2_ref, xtm_ref, rec_ref, rect_ref, cnt_ref, cnt_scr,
                  *, precise, tiles_per_seg):
    ya = _mm(at_ref[...], wa_ref[...], precise)
    yr = _mm(rn_ref[...], wr_ref[...], precise)
    merged = sga_ref[...].astype(F32) * ya + sgr_ref[...].astype(F32) * yr
    x2 = x_ref[...] + _mm(merged, wo_ref[...], precise)
    x2_ref[...] = x2
    inv = lax.rsqrt(jnp.mean(x2 * x2, axis=-1, keepdims=True) + EPS)
    xn = x2 * inv * g_ref[...]

    logits = _mm(xn, wrt_ref[...], precise) + brt_ref[...]
    tm = logits.shape[0]
    lane = lax.broadcasted_iota(jnp.int32, (tm, LANES), 1)
    big = jnp.int32(LANES)
    is_grp = (lane >= N_EXPERTS) & (lane < N_EXPERTS + N_GROUPS)
    gl = jnp.where(is_grp, logits, NEG_INF)
    gmax = jnp.max(gl, axis=-1, keepdims=True)
    g_idx = jnp.min(jnp.where(gl == gmax, lane, big), axis=-1, keepdims=True) - N_EXPERTS
    p_g = 1.0 / jnp.sum(jnp.exp(gl - gmax), axis=-1, keepdims=True)
    in_grp = (lane // EXPERTS_PER_GROUP) == g_idx
    el = jnp.where(in_grp, logits, NEG_INF)
    v1 = jnp.max(el, axis=-1, keepdims=True)
    i1 = jnp.min(jnp.where(el == v1, lane, big), axis=-1, keepdims=True)
    el2 = jnp.where(lane == i1, NEG_INF, el)
    v2 = jnp.max(el2, axis=-1, keepdims=True)
    i2 = jnp.min(jnp.where(el2 == v2, lane, big), axis=-1, keepdims=True)
    e2 = jnp.exp(v2 - v1)
    w1 = p_g / (1.0 + e2)
    w2 = p_g * e2 / (1.0 + e2)

    @pl.when(pl.program_id(0) % tiles_per_seg == 0)
    def _():
        cnt_scr[...] = jnp.zeros_like(cnt_scr)

    hit = jnp.where(lane == i1, 1.0, jnp.where(lane == i2, 1.0, 0.0))
    before = jnp.dot(tri_ref[...], hit.astype(BF16), preferred_element_type=F32) + cnt_scr[0:1, :]
    r1 = jnp.sum(jnp.where(lane == i1, before, 0.0), axis=-1, keepdims=True)
    r2 = jnp.sum(jnp.where(lane == i2, before, 0.0), axis=-1, keepdims=True)
    counts = cnt_scr[...] + jnp.sum(hit, axis=0, keepdims=True)
    cnt_scr[...] = counts
    cnt_ref[0] = counts
    rec = jnp.where(lane == REC_W2, w2, 0.0)
    for field, val in ((REC_W1, w1), (REC_R2, r2), (REC_R1, r1),
                       (REC_E2, i2.astype(F32)), (REC_E1, i1.astype(F32))):
        rec = jnp.where(lane == field, val, rec)
    rec_ref[...] = rec
    rect_ref[...] = rec.T

    for c in range(TOKEN_ROWS):
        xtm_ref[pl.ds(c, tm, stride=TOKEN_ROWS), :] = xn[:, c * LANES:(c + 1) * LANES]


REC_E1, REC_E2, REC_R1, REC_R2, REC_W1, REC_W2 = range(6)
TOKEN_ROWS = D_MODEL // LANES


def _merge(x, attn, rnn, sga, sgr, wa, wr, wo, g2, w_route, b_route, tri, tm, seg, precise):
    n = x.shape[0]
    tiles_per_seg = seg // tm
    row = lambda i: (i, 0)
    return pl.pallas_call(
        functools.partial(_merge_kernel, precise=precise, tiles_per_seg=tiles_per_seg),
        grid=(n // tm,),
        in_specs=[
            pl.BlockSpec((tm, D_MODEL), row),
            pl.BlockSpec((tm, D_Q), row),
            pl.BlockSpec((tm, D_RNN), row),
            pl.BlockSpec((tm, D_MODEL), row),
            pl.BlockSpec((tm, D_MODEL), row),
            _full((D_Q, D_MODEL)),
            _full((D_RNN, D_MODEL)),
            _full((D_MODEL, D_MODEL)),
            _full((1, D_MODEL)),
            _full((D_MODEL, LANES)),
            _full((1, LANES)),
            _full((tm, tm)),
        ],
        out_specs=[
            pl.BlockSpec((tm, D_MODEL), row),
            pl.BlockSpec((tm * TOKEN_ROWS, LANES), row),
            pl.BlockSpec((tm, LANES), row),
            pl.BlockSpec((LANES, tm), lambda i: (0, i)),
            pl.BlockSpec((1, SUBLANES, LANES), lambda i: (i // tiles_per_seg, 0, 0)),
        ],
        out_shape=(
            jax.ShapeDtypeStruct((n, D_MODEL), F32),
            jax.ShapeDtypeStruct((n * TOKEN_ROWS, LANES), F32),
            jax.ShapeDtypeStruct((n, LANES), F32),
            jax.ShapeDtypeStruct((LANES, n), F32),
            jax.ShapeDtypeStruct((n // seg, SUBLANES, LANES), F32),
        ),
        scratch_shapes=[pltpu.VMEM((SUBLANES, LANES), F32)],
        compiler_params=_params("arbitrary"),
        name="merge",
    )(x, attn, rnn, sga, sgr, wa, wr, wo, g2, w_route, b_route, tri)


MOE_CHUNK = 256
MOE_TAIL = 128
MOE_VMEM_LIMIT = 60 * 1024 * 1024


def _seg_rows(seg):
    return 2 * seg + N_EXPERTS * SUBLANES + MOE_CHUNK


def _token_rows(i):
    return pl.ds(pl.multiple_of(i * TOKEN_ROWS, TOKEN_ROWS), TOKEN_ROWS)


def _moe_kernel(off_ref, nfull_ref, rem_ref, slot_ref, xtm_ref, wg_ref, wu_ref, wd_ref,
                x2_ref, rec_ref, o_ref, buf, g1, g2, *, seg, td, tc):
    s = pl.program_id(0)
    p = pl.program_id(1)
    n_disp = seg // td

    @pl.when((s == 0) & (p == 0))
    def _():
        buf[...] = jnp.zeros_like(buf)

    @pl.when(p < n_disp)
    def _():
        def dispatch(g, carry):
            for j in range(SUBLANES):
                t = g * SUBLANES + j
                row = xtm_ref[_token_rows(t), :]
                for k in range(2):
                    buf[_token_rows(slot_ref[0, 0, k * seg + p * td + t]), :] = row
            return carry

        lax.fori_loop(0, td // SUBLANES, dispatch, 0)

    def run_chunk(row0, rows, valid):
        r0 = pl.multiple_of(row0 * TOKEN_ROWS, SUBLANES * TOKEN_ROWS)
        xf = [buf[pl.ds(r0 + j, rows, stride=TOKEN_ROWS), :] for j in range(TOKEN_ROWS)]
        x = jnp.concatenate(xf, axis=-1).astype(BF16)
        hg = jnp.dot(x, wg_ref[0], preferred_element_type=F32)
        hu = jnp.dot(x, wu_ref[0], preferred_element_type=F32)
        h = (hg * _sigmoid(hg)) * hu
        y = jnp.dot(h.astype(BF16), wd_ref[0], preferred_element_type=F32)
        if valid is not None:
            mine = lax.broadcasted_iota(jnp.int32, (rows, LANES), 0) < valid
        for j in range(TOKEN_ROWS):
            yj = y[:, j * LANES:(j + 1) * LANES]
            if valid is not None:
                yj = jnp.where(mine, yj, xf[j])
            buf[pl.ds(r0 + j, rows, stride=TOKEN_ROWS), :] = yj

    @pl.when((p >= n_disp) & (p < n_disp + N_EXPERTS))
    def _():
        idx = s * N_EXPERTS + p - n_disp
        base = off_ref[idx]
        n_full = nfull_ref[idx]
        rem = rem_ref[idx]

        def chunk(c, carry):
            run_chunk(base + c * MOE_CHUNK, MOE_CHUNK, None)
            return carry

        lax.fori_loop(0, n_full, chunk, 0)

        @pl.when(rem > MOE_TAIL)
        def _():
            run_chunk(base + n_full * MOE_CHUNK, MOE_CHUNK, rem)

        @pl.when((rem > 0) & (rem <= MOE_TAIL))
        def _():
            run_chunk(base + n_full * MOE_CHUNK, MOE_TAIL, rem)

    @pl.when(p >= n_disp + N_EXPERTS)
    def _():
        t0 = (p - n_disp - N_EXPERTS) * tc

        def gather(g, carry):
            for j in range(SUBLANES):
                t = g * SUBLANES + j
                g1[_token_rows(t), :] = buf[_token_rows(slot_ref[0, 0, t0 + t]), :]
                g2[_token_rows(t), :] = buf[_token_rows(slot_ref[0, 0, seg + t0 + t]), :]
            return carry

        lax.fori_loop(0, tc // SUBLANES, gather, 0)
        rec = rec_ref[...]
        lane = lax.broadcasted_iota(jnp.int32, rec.shape, 1)
        w1 = jnp.sum(jnp.where(lane == REC_W1, rec, 0.0), axis=-1, keepdims=True)
        w2 = jnp.sum(jnp.where(lane == REC_W2, rec, 0.0), axis=-1, keepdims=True)
        for j in range(TOKEN_ROWS):
            cols = slice(j * LANES, (j + 1) * LANES)
            o_ref[:, cols] = (x2_ref[:, cols] + w1 * g1[pl.ds(j, tc, stride=TOKEN_ROWS), :]
                              + w2 * g2[pl.ds(j, tc, stride=TOKEN_ROWS), :])


def _moe(off, nfull, rem, slot, xtm, wg, wu, wd, x2, rec, seg, td, tc):
    n = x2.shape[0]
    n_seg = n // seg
    n_disp, n_comb = seg // td, seg // tc
    rows = _seg_rows(seg) * TOKEN_ROWS
    disp_tile = lambda s, p, *_: (s * n_disp + jnp.minimum(p, n_disp - 1), 0)
    expert = lambda s, p, *_: (jnp.clip(p - n_disp, 0, N_EXPERTS - 1), 0, 0)
    comb_tile = lambda s, p, *_: (s * n_comb + jnp.clip(p - n_disp - N_EXPERTS, 0, n_comb - 1), 0)
    grid_spec = pltpu.PrefetchScalarGridSpec(
        num_scalar_prefetch=3,
        grid=(n_seg, n_disp + N_EXPERTS + n_comb),
        in_specs=[
            pl.BlockSpec((1, 1, 2 * seg), lambda s, p, *_: (s, 0, 0), memory_space=pltpu.SMEM),
            pl.BlockSpec((td * TOKEN_ROWS, LANES), disp_tile),
            pl.BlockSpec((1, D_MODEL, D_EXPERT), expert),
            pl.BlockSpec((1, D_MODEL, D_EXPERT), expert),
            pl.BlockSpec((1, D_EXPERT, D_MODEL), expert),
            pl.BlockSpec((tc, D_MODEL), comb_tile),
            pl.BlockSpec((tc, LANES), comb_tile),
        ],
        out_specs=pl.BlockSpec((tc, D_MODEL), comb_tile),
        scratch_shapes=[pltpu.VMEM((rows, LANES), F32),
                        pltpu.VMEM((tc * TOKEN_ROWS, LANES), F32),
                        pltpu.VMEM((tc * TOKEN_ROWS, LANES), F32)],
    )
    return pl.pallas_call(
        functools.partial(_moe_kernel, seg=seg, td=td, tc=tc),
        grid_spec=grid_spec,
        out_shape=jax.ShapeDtypeStruct((n, D_MODEL), F32),
        compiler_params=pltpu.CompilerParams(
            dimension_semantics=("arbitrary", "arbitrary"), vmem_limit_bytes=MOE_VMEM_LIMIT),
        name="moe",
    )(off, nfull, rem, slot, xtm, wg, wu, wd, x2, rec)


def _plan(rect, cnt, seg):
    n = rect.shape[1]
    expert = rect[REC_E1:REC_E2 + 1].astype(jnp.int32)
    rank = rect[REC_R1:REC_R2 + 1].astype(jnp.int32)
    counts = cnt[:, 0, :N_EXPERTS].astype(jnp.int32)
    padded = (counts + SUBLANES - 1) // SUBLANES * SUBLANES
    off = jnp.cumsum(padded, axis=1) - padded
    n_full = counts // MOE_CHUNK
    rem = counts - n_full * MOE_CHUNK
    off_tok = jnp.repeat(off.T, seg, axis=1)
    hit = expert[:, None, :] == jnp.arange(N_EXPERTS, dtype=jnp.int32)[None, :, None]
    slot = rank + jnp.sum(jnp.where(hit, off_tok[None], 0), axis=1)
    slot = slot.reshape(2, n // seg, seg).transpose(1, 0, 2).reshape(n // seg, 1, 2 * seg)
    return off.reshape(-1), n_full.reshape(-1), rem.reshape(-1), slot


def _rope_tables(pos):
    half = HEAD_DIM // 2
    inv_freq = ROPE_THETA ** (-jnp.arange(half, dtype=F32) / half)
    ang = pos[:, None] * inv_freq[None, :]
    cos = jnp.cos(ang)
    sin = jnp.sin(ang)
    reps = LANES // HEAD_DIM
    cos_t = jnp.tile(jnp.concatenate([cos, cos], axis=-1), (1, reps))
    sin_t = jnp.tile(jnp.concatenate([-sin, sin], axis=-1), (1, reps))
    return cos_t, sin_t


def kernel(x_prompt, x_sample, cache_k_win, cache_v_win, state_conv, state_lru_h, attn_norm_g, w_in, q_norm_g, k_norm_g, attn_sinks, conv_w, conv_b, w_lru_a, b_lru_a, w_lru_i, b_lru_i, lru_lambda, w_br_attn, w_br_rnn, w_out, ffn_norm_g, w_route_group, b_route_group, w_route_expert, b_route_expert, w_exp_gate, w_exp_up, w_exp_down):
    batch, seq, _ = x_prompt.shape
    dec_batch, dec_seq, _ = x_sample.shape
    depth = w_in.shape[0]
    assert depth == 1 and dec_seq == 1
    l = 0

    w_in_f = w_in[l]
    qkg = jnp.concatenate([jnp.tile(q_norm_g[l], N_Q_HEADS), jnp.tile(k_norm_g[l], N_KV_HEADS)])[None, :]
    wcat_f = jnp.concatenate([w_lru_a[l], w_lru_i[l]], axis=-1)
    wa_f, wr_f, wo_f = w_br_attn[l], w_br_rnn[l], w_out[l]
    w_route_f = jnp.concatenate(
        [w_route_expert[l], w_route_group[l],
         jnp.zeros((D_MODEL, LANES - N_EXPERTS - N_GROUPS), F32)], axis=-1)
    w_in_b, wcat = w_in_f.astype(BF16), wcat_f.astype(BF16)
    wa_b, wr_b, wo_b, w_route = (w.astype(BF16) for w in (wa_f, wr_f, wo_f, w_route_f))
    b_route = jnp.concatenate(
        [b_route_expert[l], b_route_group[l], jnp.zeros((LANES - N_EXPERTS - N_GROUPS,), F32)])[None, :]
    wg, wu, wd = (w[l].astype(BF16) for w in (w_exp_gate, w_exp_up, w_exp_down))
    g1 = attn_norm_g[l][None, :]
    g2 = ffn_norm_g[l][None, :]
    cw, cb = conv_w[l], conv_b[l][None, :]
    b_a, b_i, lam = b_lru_a[l][None, :], b_lru_i[l][None, :], lru_lambda[l][None, :]
    sinks = attn_sinks[l]

    def tail(x, attn, rnn, sga, sgr, tm, seg, precise):
        wa, wr, wo, wrt = (wa_f, wr_f, wo_f, w_route_f) if precise else (wa_b, wr_b, wo_b, w_route)
        tri = jnp.tril(jnp.ones((tm, tm), BF16), -1)
        x2, xtm, rec, rect, cnt = _merge(x, attn, rnn, sga, sgr, wa, wr, wo, g2, wrt, b_route, tri,
                                         tm, seg, precise)
        off, nfull, rem, slot = _plan(rect, cnt, seg)
        return _moe(off, nfull, rem, slot, xtm, wg, wu, wd, x2, rec, seg, tm, min(tm, 256))

    xp = x_prompt.reshape(batch * seq, D_MODEL)
    cos_p, sin_p = _rope_tables(jnp.arange(seq, dtype=F32))
    q, k, v, xr, gy, sga, sgr = _proj(xp, g1, w_in_b, cos_p, sin_p, qkg, 512, False)
    attn = _attn_prompt(q, k, v, sinks, batch, seq)
    rnn, h_last = _rnn_prompt(xr, gy, cw, cb, wcat, b_a, b_i, lam, batch, seq, 256)
    y_prompt = tail(xp, attn, rnn, sga, sgr, 512, 4096, False).reshape(batch, seq, D_MODEL)
    def last_rows(a, rows):
        return a.reshape(batch, seq, a.shape[-1])[:, seq - rows:]

    k_win_p = last_rows(k, WINDOW).reshape(1, batch, WINDOW, N_KV_HEADS, HEAD_DIM)
    v_win_p = last_rows(v, WINDOW).reshape(1, batch, WINDOW, N_KV_HEADS, HEAD_DIM)
    conv_p = last_rows(xr, CONV_W - 1)[None]
    h_p = h_last[None, :, 0, :]

    xs = x_sample.reshape(dec_batch, D_MODEL)
    cos_s, sin_s = _rope_tables(jnp.full((dec_batch,), PAST_LEN, F32))
    qs, ks, vs, xrs, gys, sgas, sgrs = _proj(xs, g1, w_in_f, cos_s, sin_s, qkg, dec_batch, True)
    ck = cache_k_win[l].reshape(dec_batch, WINDOW, D_KV)
    cv = cache_v_win[l].reshape(dec_batch, WINDOW, D_KV)
    attn_s, k_win_s, v_win_s = _attn_sample(qs, ks, vs, ck, cv, sinks)
    sc = state_conv[l]
    rnn_s, h_s = _rnn_sample(xrs, gys, sc[:, 0], sc[:, 1], sc[:, 2], state_lru_h[l],
                             cw, cb, wcat_f, b_a, b_i, lam)
    y_sample = tail(xs, attn_s, rnn_s, sgas, sgrs, dec_batch, dec_batch, True).reshape(dec_batch, 1, D_MODEL)
    conv_s = jnp.stack([sc[:, 1], sc[:, 2], xrs], axis=1)[None]

    return (y_prompt, y_sample, k_win_p, v_win_p, conv_p, h_p,
            k_win_s.reshape(1, dec_batch, WINDOW, N_KV_HEADS, HEAD_DIM),
            v_win_s.reshape(1, dec_batch, WINDOW, N_KV_HEADS, HEAD_DIM),
            conv_s, h_s[None])
```

```python
import functools

import jax
import jax.numpy as jnp
from jax import lax
from jax.experimental import pallas as pl
from jax.experimental.pallas import tpu as pltpu

D_MODEL = 1024
HEAD_DIM = 64
N_Q_HEADS = 8
N_KV_HEADS = 2
Q_PER_KV = N_Q_HEADS // N_KV_HEADS
WINDOW = 128
ATTN_BLOCK = 128
ROPE_THETA = 10000.0
SCALE = HEAD_DIM ** -0.5
NEG_INF = -1e30
D_RNN = 1280
N_RNN_BLOCKS = 10
RNN_BLOCK = D_RNN // N_RNN_BLOCKS
CONV_W = 4
LRU_C = 8.0
N_GROUPS = 4
EXPERTS_PER_GROUP = 8
N_EXPERTS = N_GROUPS * EXPERTS_PER_GROUP
D_EXPERT = 256
PAST_LEN = 16384
EPS = 1e-6
D_Q = N_Q_HEADS * HEAD_DIM
D_KV = N_KV_HEADS * HEAD_DIM
D_IN = D_Q + 2 * D_KV + 2 * D_RNN + 2 * D_MODEL
OFF_K = D_Q
OFF_V = OFF_K + D_KV
OFF_XR = OFF_V + D_KV
OFF_YR = OFF_XR + D_RNN
OFF_GA = OFF_YR + D_RNN
OFF_GR = OFF_GA + D_MODEL

LANES = 128
SUBLANES = 8
VMEM_LIMIT = 56 * 1024 * 1024

F32 = jnp.float32
BF16 = jnp.bfloat16


def _params(*sem):
    return pltpu.CompilerParams(dimension_semantics=sem, vmem_limit_bytes=VMEM_LIMIT)


def _sigmoid(x):
    return 1.0 / (1.0 + jnp.exp(-x))


def _gelu_tanh(x):
    c = 0.7978845608028654
    return 0.5 * x * (1.0 + jnp.tanh(c * (x + 0.044715 * (x * x * x))))


def _full(shape, single_buffer=False):
    index_map = lambda *_: (0,) * len(shape)
    if single_buffer:
        return pl.BlockSpec(shape, index_map, pipeline_mode=pl.Buffered(1))
    return pl.BlockSpec(shape, index_map)


def _mm(a, b, precise, dims=None):
    if precise:
        a, b, prec = a.astype(F32), b.astype(F32), lax.Precision.HIGHEST
    else:
        a, b, prec = a.astype(BF16), b.astype(BF16), None
    if dims is None:
        return jnp.dot(a, b, preferred_element_type=F32, precision=prec)
    return lax.dot_general(a, b, dims, preferred_element_type=F32, precision=prec)


_NT = (((1,), (1,)), ((), ()))


def _proj_kernel(x_ref, g_ref, w_ref, cos_ref, sin_ref, qkg_ref,
                 q_ref, k_ref, v_ref, xr_ref, gy_ref, sga_ref, sgr_ref, *, precise):
    x = x_ref[...]
    inv = lax.rsqrt(jnp.mean(x * x, axis=-1, keepdims=True) + EPS)
    xn = x * inv * g_ref[...]
    if not precise:
        xn = xn.astype(BF16)

    def proj(lo, hi):
        return _mm(xn, w_ref[:, lo:hi], precise)

    qk = proj(0, OFF_V)
    tm = qk.shape[0]
    lane = lax.broadcasted_iota(jnp.int32, (tm, LANES), 1)
    lo_head = lane < HEAD_DIM
    first_half = (lane % HEAD_DIM) < (HEAD_DIM // 2)
    cos = cos_ref[...]
    sin = sin_ref[...]
    for g in range(OFF_V // LANES):
        seg = qk[:, g * LANES:(g + 1) * LANES]
        sq = seg * seg
        s_lo = jnp.sum(jnp.where(lo_head, sq, 0.0), axis=-1, keepdims=True)
        s_hi = jnp.sum(jnp.where(lo_head, 0.0, sq), axis=-1, keepdims=True)
        ms = jnp.where(lo_head, s_lo, s_hi) * (1.0 / HEAD_DIM)
        normed = seg * lax.rsqrt(ms + EPS) * qkg_ref[:, g * LANES:(g + 1) * LANES]
        partner = jnp.where(first_half,
                            pltpu.roll(normed, LANES - HEAD_DIM // 2, axis=1),
                            pltpu.roll(normed, HEAD_DIM // 2, axis=1))
        roped = normed * cos + partner * sin
        if g < D_Q // LANES:
            q_ref[:, g * LANES:(g + 1) * LANES] = (roped * SCALE).astype(q_ref.dtype)
        else:
            k_ref[...] = roped
    v_ref[...] = proj(OFF_V, OFF_XR)
    xr_ref[...] = proj(OFF_XR, OFF_YR)
    gy_ref[...] = _gelu_tanh(proj(OFF_YR, OFF_GA)).astype(gy_ref.dtype)
    sga_ref[...] = _sigmoid(proj(OFF_GA, OFF_GR)).astype(sga_ref.dtype)
    sgr_ref[...] = _sigmoid(proj(OFF_GR, D_IN)).astype(sgr_ref.dtype)


def _proj(x, g, w_in, cos_t, sin_t, qkg, tm, precise):
    n = x.shape[0]
    t_blocks = cos_t.shape[0] // tm
    row = lambda i: (i, 0)
    act = F32 if precise else BF16
    out_shape = (
        jax.ShapeDtypeStruct((n, D_Q), act),
        jax.ShapeDtypeStruct((n, D_KV), F32),
        jax.ShapeDtypeStruct((n, D_KV), F32),
        jax.ShapeDtypeStruct((n, D_RNN), F32),
        jax.ShapeDtypeStruct((n, D_RNN), act),
        jax.ShapeDtypeStruct((n, D_MODEL), act),
        jax.ShapeDtypeStruct((n, D_MODEL), act),
    )
    return pl.pallas_call(
        functools.partial(_proj_kernel, precise=precise),
        grid=(n // tm,),
        in_specs=[
            pl.BlockSpec((tm, D_MODEL), row),
            _full((1, D_MODEL)),
            _full((D_MODEL, D_IN), single_buffer=True),
            pl.BlockSpec((tm, LANES), lambda i: (i % t_blocks, 0)),
            pl.BlockSpec((tm, LANES), lambda i: (i % t_blocks, 0)),
            _full((1, OFF_V)),
        ],
        out_specs=[pl.BlockSpec((tm, s.shape[1]), row) for s in out_shape],
        out_shape=out_shape,
        compiler_params=_params("parallel"),
        name="proj",
    )(x, g, w_in, cos_t, sin_t, qkg)


def _softmax_pv(s, sink, v2, precise):
    m = jnp.maximum(jnp.max(s, axis=-1, keepdims=True), sink)
    p = jnp.exp(s - m)
    denom = jnp.sum(p, axis=-1, keepdims=True) + jnp.exp(sink - m)
    return _mm(p, v2, precise) * (1.0 / denom)


def _sink_column(sink_ref, h, rows, rows_per_head):
    r = lax.broadcasted_iota(jnp.int32, (rows, 1), 0) // rows_per_head
    col = jnp.full((rows, 1), sink_ref[h * Q_PER_KV], F32)
    for g in range(1, Q_PER_KV):
        col = jnp.where(r == g, sink_ref[h * Q_PER_KV + g], col)
    return col


ATTN_BLOCKS_PER_STEP = 4


def _attn_prompt_kernel(sink_ref, q_ref, kc_ref, kp_ref, vc_ref, vp_ref, o_ref):
    n = pl.program_id(1)
    blk = ATTN_BLOCK
    k_all = jnp.concatenate([kp_ref[...], kc_ref[...]], axis=0).astype(BF16)
    v_all = jnp.concatenate([vp_ref[...], vc_ref[...]], axis=0).astype(BF16)
    rows = Q_PER_KV * blk
    i = lax.broadcasted_iota(jnp.int32, (rows, 2 * blk), 0) % blk
    j = lax.broadcasted_iota(jnp.int32, (rows, 2 * blk), 1)
    d = j - i
    in_window = (d >= 1) & (d <= WINDOW)
    first_valid = (d >= jnp.where(n > 0, 1, jnp.maximum(1, blk - i))) & (d <= WINDOW)
    for sub in range(ATTN_BLOCKS_PER_STEP):
        q = q_ref[sub * blk:(sub + 1) * blk, :]
        k2 = k_all[sub * blk:(sub + 2) * blk]
        v2 = v_all[sub * blk:(sub + 2) * blk]
        valid = first_valid if sub == 0 else in_window
        for h in range(N_KV_HEADS):
            kh = k2[:, h * HEAD_DIM:(h + 1) * HEAD_DIM]
            vh = v2[:, h * HEAD_DIM:(h + 1) * HEAD_DIM]
            qs = jnp.concatenate(
                [q[:, (h * Q_PER_KV + g) * HEAD_DIM:(h * Q_PER_KV + g + 1) * HEAD_DIM]
                 for g in range(Q_PER_KV)], axis=0)
            s = _mm(qs, kh, False, _NT)
            s = jnp.where(valid, s, NEG_INF)
            o = _softmax_pv(s, _sink_column(sink_ref, h, rows, blk), vh, False)
            for g in range(Q_PER_KV):
                c = (h * Q_PER_KV + g) * HEAD_DIM
                o_ref[sub * blk:(sub + 1) * blk, c:c + HEAD_DIM] = (
                    o[g * blk:(g + 1) * blk].astype(o_ref.dtype))


def _attn_prompt(q, k, v, sinks, batch, seq):
    step = ATTN_BLOCKS_PER_STEP * ATTN_BLOCK
    ns = seq // step
    cur = lambda b, n: (b * ns + n, 0)
    prev = lambda b, n: (jnp.maximum((b * ns + n) * ATTN_BLOCKS_PER_STEP - 1, 0), 0)
    return pl.pallas_call(
        _attn_prompt_kernel,
        grid=(batch, ns),
        in_specs=[
            pl.BlockSpec(memory_space=pltpu.SMEM),
            pl.BlockSpec((step, D_Q), cur),
            pl.BlockSpec((step, D_KV), cur),
            pl.BlockSpec((ATTN_BLOCK, D_KV), prev),
            pl.BlockSpec((step, D_KV), cur),
            pl.BlockSpec((ATTN_BLOCK, D_KV), prev),
        ],
        out_specs=pl.BlockSpec((step, D_Q), cur),
        out_shape=jax.ShapeDtypeStruct((batch * seq, D_Q), BF16),
        compiler_params=_params("parallel", "parallel"),
        name="attn_prompt",
    )(sinks, q, k, k, v, v)


SAMPLE_BT = 8


def _attn_sample_kernel(sink_ref, q_ref, kn_ref, vn_ref, kc_ref, vc_ref, o_ref, ko_ref, vo_ref):
    bt = SAMPLE_BT
    w = lax.broadcasted_iota(jnp.int32, (bt, WINDOW, D_KV), 1)

    def shifted(cache_ref, new_ref):
        rolled = pltpu.roll(cache_ref[...], WINDOW - 1, axis=1)
        return jnp.where(w == WINDOW - 1, new_ref[...][:, None, :], rolled)

    k_win = shifted(kc_ref, kn_ref)
    v_win = shifted(vc_ref, vn_ref)
    ko_ref[...] = k_win
    vo_ref[...] = v_win
    k2 = k_win.reshape(bt * WINDOW, D_KV)
    v2 = v_win.reshape(bt * WINDOW, D_KV)
    q = q_ref[...]
    rows = Q_PER_KV * bt
    rb = lax.broadcasted_iota(jnp.int32, (rows, bt * WINDOW), 0) % bt
    cb = lax.broadcasted_iota(jnp.int32, (rows, bt * WINDOW), 1) // WINDOW
    valid = rb == cb
    for h in range(N_KV_HEADS):
        kh = k2[:, h * HEAD_DIM:(h + 1) * HEAD_DIM]
        vh = v2[:, h * HEAD_DIM:(h + 1) * HEAD_DIM]
        qs = jnp.concatenate(
            [q[:, (h * Q_PER_KV + g) * HEAD_DIM:(h * Q_PER_KV + g + 1) * HEAD_DIM]
             for g in range(Q_PER_KV)], axis=0)
        s = _mm(qs, kh, True, _NT)
        s = jnp.where(valid, s, NEG_INF)
        o = _softmax_pv(s, _sink_column(sink_ref, h, rows, bt), vh, True)
        for g in range(Q_PER_KV):
            c = (h * Q_PER_KV + g) * HEAD_DIM
            o_ref[:, c:c + HEAD_DIM] = o[g * bt:(g + 1) * bt].astype(o_ref.dtype)


def _attn_sample(q, k_new, v_new, cache_k, cache_v, sinks):
    nbatch = q.shape[0]
    bt = SAMPLE_BT
    row = lambda i: (i, 0)
    win = lambda i: (i, 0, 0)
    return pl.pallas_call(
        _attn_sample_kernel,
        grid=(nbatch // bt,),
        in_specs=[
            pl.BlockSpec(memory_space=pltpu.SMEM),
            pl.BlockSpec((bt, D_Q), row),
            pl.BlockSpec((bt, D_KV), row),
            pl.BlockSpec((bt, D_KV), row),
            pl.BlockSpec((bt, WINDOW, D_KV), win),
            pl.BlockSpec((bt, WINDOW, D_KV), win),
        ],
        out_specs=[
            pl.BlockSpec((bt, D_Q), row),
            pl.BlockSpec((bt, WINDOW, D_KV), win),
            pl.BlockSpec((bt, WINDOW, D_KV), win),
        ],
        out_shape=(
            jax.ShapeDtypeStruct((nbatch, D_Q), F32),
            jax.ShapeDtypeStruct((nbatch, WINDOW, D_KV), F32),
            jax.ShapeDtypeStruct((nbatch, WINDOW, D_KV), F32),
        ),
        compiler_params=_params("parallel"),
        name="attn_sample",
    )(sinks, q, k_new, v_new, cache_k, cache_v)


def _lru_terms(xc, wcat_ref, ba_ref, bi_ref, lam_ref, precise):
    xcb = xc if precise else xc.astype(BF16)
    ya, yi = [], []
    for n in range(N_RNN_BLOCKS):
        y = _mm(xcb[:, n * RNN_BLOCK:(n + 1) * RNN_BLOCK], wcat_ref[n], precise)
        ya.append(y[:, :RNN_BLOCK])
        yi.append(y[:, RNN_BLOCK:])
    r = _sigmoid(jnp.concatenate(ya, axis=-1) + ba_ref[...])
    gate_i = _sigmoid(jnp.concatenate(yi, axis=-1) + bi_ref[...])
    neg_lam = -lam_ref[...]
    softplus = jnp.maximum(neg_lam, 0.0) + jnp.log1p(jnp.exp(-jnp.abs(neg_lam)))
    log_a = (-LRU_C * softplus) * r
    a = jnp.exp(log_a)
    m = 1.0 - a * a
    b = jnp.where(m > 0.0, m * lax.rsqrt(m), 0.0) * (gate_i * xc)
    return a, b


def _rnn_prompt_kernel(xr_ref, gy_ref, cw_ref, cb_ref, wcat_ref, ba_ref, bi_ref, lam_ref,
                       o_ref, hl_ref, xbuf, a_scr, b_scr, h_scr, hcar):
    tt = xr_ref.shape[0]
    groups = tt // SUBLANES

    @pl.when(pl.program_id(1) == 0)
    def _():
        xbuf[0:SUBLANES, :] = jnp.zeros((SUBLANES, D_RNN), F32)
        hcar[...] = jnp.zeros((SUBLANES, D_RNN), F32)

    x = xr_ref[...]
    xbuf[SUBLANES:, :] = x
    xc = cb_ref[...] + cw_ref[CONV_W - 1:CONV_W, :] * x
    for j in range(CONV_W - 1):
        s = CONV_W - 1 - j
        xc = xc + cw_ref[j:j + 1, :] * xbuf[SUBLANES - s:SUBLANES - s + tt, :]
    xbuf[0:SUBLANES, :] = x[tt - SUBLANES:, :]

    a, b = _lru_terms(xc, wcat_ref, ba_ref, bi_ref, lam_ref, False)
    a = a.reshape(groups, SUBLANES, D_RNN)
    b = b.reshape(groups, SUBLANES, D_RNN)
    step = lax.broadcasted_iota(jnp.int32, (groups, SUBLANES, D_RNN), 1)
    k = 1
    while k < SUBLANES:
        keep = step >= k
        a_sh = jnp.where(keep, pltpu.roll(a, k, axis=1), 1.0)
        b_sh = jnp.where(keep, pltpu.roll(b, k, axis=1), 0.0)
        b = a * b_sh + b
        a = a * a_sh
        k *= 2
    a_scr[...] = a
    b_scr[...] = b

    def chain(g, h_in):
        h = a_scr[g] * h_in + b_scr[g]
        h_scr[g] = h
        return jnp.broadcast_to(h[SUBLANES - 1:SUBLANES, :], (SUBLANES, D_RNN))

    h_last = lax.fori_loop(0, groups, chain, hcar[...])
    hcar[...] = h_last
    hl_ref[0] = h_last
    h = h_scr[...].reshape(tt, D_RNN)
    o_ref[...] = (h * gy_ref[...].astype(F32)).astype(o_ref.dtype)


def _rnn_prompt(xr, gy, conv_w, conv_b, wcat, b_a, b_i, lam, batch, seq, tt):
    nt = seq // tt
    row = lambda b, t: (b * nt + t, 0)
    groups = tt // SUBLANES
    return pl.pallas_call(
        _rnn_prompt_kernel,
        grid=(batch, nt),
        in_specs=[
            pl.BlockSpec((tt, D_RNN), row),
            pl.BlockSpec((tt, D_RNN), row),
            _full((CONV_W, D_RNN)),
            _full((1, D_RNN)),
            _full((N_RNN_BLOCKS, RNN_BLOCK, 2 * RNN_BLOCK)),
            _full((1, D_RNN)),
            _full((1, D_RNN)),
            _full((1, D_RNN)),
        ],
        out_specs=[
            pl.BlockSpec((tt, D_RNN), row),
            pl.BlockSpec((1, SUBLANES, D_RNN), lambda b, t: (b, 0, 0)),
        ],
        out_shape=(
            jax.ShapeDtypeStruct((batch * seq, D_RNN), BF16),
            jax.ShapeDtypeStruct((batch, SUBLANES, D_RNN), F32),
        ),
        scratch_shapes=[
            pltpu.VMEM((tt + SUBLANES, D_RNN), F32),
            pltpu.VMEM((groups, SUBLANES, D_RNN), F32),
            pltpu.VMEM((groups, SUBLANES, D_RNN), F32),
            pltpu.VMEM((groups, SUBLANES, D_RNN), F32),
            pltpu.VMEM((SUBLANES, D_RNN), F32),
        ],
        compiler_params=_params("parallel", "arbitrary"),
        name="rnn_prompt",
    )(xr, gy, conv_w, conv_b, wcat, b_a, b_i, lam)


def _rnn_sample_kernel(xr_ref, gy_ref, s0_ref, s1_ref, s2_ref, h_ref, cw_ref, cb_ref,
                       wcat_ref, ba_ref, bi_ref, lam_ref, o_ref, hn_ref):
    x = xr_ref[...]
    xc = (cb_ref[...] + cw_ref[0:1, :] * s0_ref[...] + cw_ref[1:2, :] * s1_ref[...]
          + cw_ref[2:3, :] * s2_ref[...] + cw_ref[3:4, :] * x)
    a, b = _lru_terms(xc, wcat_ref, ba_ref, bi_ref, lam_ref, True)
    h = a * h_ref[...] + b
    hn_ref[...] = h
    o_ref[...] = (h * gy_ref[...].astype(F32)).astype(o_ref.dtype)


def _rnn_sample(xr, gy, s0, s1, s2, h_prev, conv_w, conv_b, wcat, b_a, b_i, lam):
    n = xr.shape[0]
    act = _full((n, D_RNN))
    return pl.pallas_call(
        _rnn_sample_kernel,
        grid=(1,),
        in_specs=[act, act, act, act, act, act,
                  _full((CONV_W, D_RNN)), _full((1, D_RNN)),
                  _full((N_RNN_BLOCKS, RNN_BLOCK, 2 * RNN_BLOCK)),
                  _full((1, D_RNN)), _full((1, D_RNN)), _full((1, D_RNN))],
        out_specs=[act, act],
        out_shape=(jax.ShapeDtypeStruct((n, D_RNN), F32),
                   jax.ShapeDtypeStruct((n, D_RNN), F32)),
        compiler_params=_params("arbitrary"),
        name="rnn_sample",
    )(xr, gy, s0, s1, s2, h_prev, conv_w, conv_b, wcat, b_a, b_i, lam)


def _merge_kernel(x_ref, at_ref, rn_ref, sga_ref, sgr_ref, wa_ref, wr_ref, wo_ref, g_ref,
                  wrt_ref, brt_ref, tri_ref, x2_ref, xtm_ref, rec_ref, rect_ref, cnt_ref, cnt_scr,
                  *, precise, tiles_per_seg):
    ya = _mm(at_ref[...], wa_ref[...], precise)
    yr = _mm(rn_ref[...], wr_ref[...], precise)
    merged = sga_ref[...].astype(F32) * ya + sgr_ref[...].astype(F32) * yr
    x2 = x_ref[...] + _mm(merged, wo_ref[...], precise)
    x2_ref[...] = x2
    inv = lax.rsqrt(jnp.mean(x2 * x2, axis=-1, keepdims=True) + EPS)
    xn = x2 * inv * g_ref[...]

    logits = _mm(xn, wrt_ref[...], precise) + brt_ref[...]
    tm = logits.shape[0]
    lane = lax.broadcasted_iota(jnp.int32, (tm, LANES), 1)
    big = jnp.int32(LANES)
    is_grp = (lane >= N_EXPERTS) & (lane < N_EXPERTS + N_GROUPS)
    gl = jnp.where(is_grp, logits, NEG_INF)
    gmax = jnp.max(gl, axis=-1, keepdims=True)
    g_idx = jnp.min(jnp.where(gl == gmax, lane, big), axis=-1, keepdims=True) - N_EXPERTS
    p_g = 1.0 / jnp.sum(jnp.exp(gl - gmax), axis=-1, keepdims=True)
    in_grp = (lane // EXPERTS_PER_GROUP) == g_idx
    el = jnp.where(in_grp, logits, NEG_INF)
    v1 = jnp.max(el, axis=-1, keepdims=True)
    i1 = jnp.min(jnp.where(el == v1, lane, big), axis=-1, keepdims=True)
    el2 = jnp.where(lane == i1, NEG_INF, el)
    v2 = jnp.max(el2, axis=-1, keepdims=True)
    i2 = jnp.min(jnp.where(el2 == v2, lane, big), axis=-1, keepdims=True)
    e2 = jnp.exp(v2 - v1)
    w1 = p_g / (1.0 + e2)
    w2 = p_g * e2 / (1.0 + e2)

    @pl.when(pl.program_id(0) % tiles_per_seg == 0)
    def _():
        cnt_scr[...] = jnp.zeros_like(cnt_scr)

    hit = jnp.where(lane == i1, 1.0, jnp.where(lane == i2, 1.0, 0.0))
    before = jnp.dot(tri_ref[...], hit.astype(BF16), preferred_element_type=F32) + cnt_scr[0:1, :]
    r1 = jnp.sum(jnp.where(lane == i1, before, 0.0), axis=-1, keepdims=True)
    r2 = jnp.sum(jnp.where(lane == i2, before, 0.0), axis=-1, keepdims=True)
    counts = cnt_scr[...] + jnp.sum(hit, axis=0, keepdims=True)
    cnt_scr[...] = counts
    cnt_ref[0] = counts
    rec = jnp.where(lane == REC_W2, w2, 0.0)
    for field, val in ((REC_W1, w1), (REC_R2, r2), (REC_R1, r1),
                       (REC_E2, i2.astype(F32)), (REC_E1, i1.astype(F32))):
        rec = jnp.where(lane == field, val, rec)
    rec_ref[...] = rec
    rect_ref[...] = rec.T

    for c in range(TOKEN_ROWS):
        xtm_ref[pl.ds(c, tm, stride=TOKEN_ROWS), :] = xn[:, c * LANES:(c + 1) * LANES]


REC_E1, REC_E2, REC_R1, REC_R2, REC_W1, REC_W2 = range(6)
TOKEN_ROWS = D_MODEL // LANES


def _merge(x, attn, rnn, sga, sgr, wa, wr, wo, g2, w_route, b_route, tri, tm, seg, precise):
    n = x.shape[0]
    tiles_per_seg = seg // tm
    row = lambda i: (i, 0)
    return pl.pallas_call(
        functools.partial(_merge_kernel, precise=precise, tiles_per_seg=tiles_per_seg),
        grid=(n // tm,),
        in_specs=[
            pl.BlockSpec((tm, D_MODEL), row),
            pl.BlockSpec((tm, D_Q), row),
            pl.BlockSpec((tm, D_RNN), row),
            pl.BlockSpec((tm, D_MODEL), row),
            pl.BlockSpec((tm, D_MODEL), row),
            _full((D_Q, D_MODEL)),
            _full((D_RNN, D_MODEL)),
            _full((D_MODEL, D_MODEL)),
            _full((1, D_MODEL)),
            _full((D_MODEL, LANES)),
            _full((1, LANES)),
            _full((tm, tm)),
        ],
        out_specs=[
            pl.BlockSpec((tm, D_MODEL), row),
            pl.BlockSpec((tm * TOKEN_ROWS, LANES), row),
            pl.BlockSpec((tm, LANES), row),
            pl.BlockSpec((LANES, tm), lambda i: (0, i)),
            pl.BlockSpec((1, SUBLANES, LANES), lambda i: (i // tiles_per_seg, 0, 0)),
        ],
        out_shape=(
            jax.ShapeDtypeStruct((n, D_MODEL), F32),
            jax.ShapeDtypeStruct((n * TOKEN_ROWS, LANES), F32),
            jax.ShapeDtypeStruct((n, LANES), F32),
            jax.ShapeDtypeStruct((LANES, n), F32),
            jax.ShapeDtypeStruct((n // seg, SUBLANES, LANES), F32),
        ),
        scratch_shapes=[pltpu.VMEM((SUBLANES, LANES), F32)],
        compiler_params=_params("arbitrary"),
        name="merge",
    )(x, attn, rnn, sga, sgr, wa, wr, wo, g2, w_route, b_route, tri)


MOE_CHUNK = 256
MOE_TAIL = 128
MOE_VMEM_LIMIT = 60 * 1024 * 1024


def _seg_rows(seg):
    return 2 * seg + N_EXPERTS * SUBLANES + MOE_CHUNK


def _token_rows(i):
    return pl.ds(pl.multiple_of(i * TOKEN_ROWS, TOKEN_ROWS), TOKEN_ROWS)


def _sorted_rows(first_row):
    return pl.ds(pl.multiple_of(first_row, TOKEN_ROWS), TOKEN_ROWS)


def _moe_kernel(off_ref, nfull_ref, rem_ref, slot_ref, xtm_ref, wg_ref, wu_ref, wd_ref,
                x2_ref, rec_ref, o_ref, buf, g1, g2, *, seg, td, tc):
    s = pl.program_id(0)
    p = pl.program_id(1)
    n_disp = seg // td

    @pl.when((s == 0) & (p == 0))
    def _():
        buf[...] = jnp.zeros_like(buf)

    @pl.when(p < n_disp)
    def _():
        def dispatch(g, carry):
            for j in range(SUBLANES):
                t = g * SUBLANES + j
                row = xtm_ref[_token_rows(t), :]
                for k in range(2):
                    buf[_sorted_rows(slot_ref[0, 0, k * seg + p * td + t]), :] = row
            return carry

        lax.fori_loop(0, td // SUBLANES, dispatch, 0)

    def run_chunk(row0, rows, valid):
        r0 = pl.multiple_of(row0 * TOKEN_ROWS, SUBLANES * TOKEN_ROWS)
        xf = [buf[pl.ds(r0 + j, rows, stride=TOKEN_ROWS), :] for j in range(TOKEN_ROWS)]
        x = jnp.concatenate(xf, axis=-1).astype(BF16)
        hg = jnp.dot(x, wg_ref[0], preferred_element_type=F32)
        hu = jnp.dot(x, wu_ref[0], preferred_element_type=F32)
        h = (hg * _sigmoid(hg)) * hu
        y = jnp.dot(h.astype(BF16), wd_ref[0], preferred_element_type=F32)
        if valid is not None:
            mine = lax.broadcasted_iota(jnp.int32, (rows, LANES), 0) < valid
        for j in range(TOKEN_ROWS):
            yj = y[:, j * LANES:(j + 1) * LANES]
            if valid is not None:
                yj = jnp.where(mine, yj, xf[j])
            buf[pl.ds(r0 + j, rows, stride=TOKEN_ROWS), :] = yj

    @pl.when((p >= n_disp) & (p < n_disp + N_EXPERTS))
    def _():
        idx = s * N_EXPERTS + p - n_disp
        base = off_ref[idx]
        n_full = nfull_ref[idx]
        rem = rem_ref[idx]

        def chunk(c, carry):
            run_chunk(base + c * MOE_CHUNK, MOE_CHUNK, None)
            return carry

        lax.fori_loop(0, n_full, chunk, 0)
        last = base + n_full * MOE_CHUNK
        for units in range(1, MOE_CHUNK // MOE_TAIL + 2):
            @pl.when((rem > (units - 1) * MOE_TAIL) & (rem <= units * MOE_TAIL))
            def _():
                run_chunk(last, units * MOE_TAIL, rem)

    @pl.when(p >= n_disp + N_EXPERTS)
    def _():
        t0 = (p - n_disp - N_EXPERTS) * tc

        def gather(g, carry):
            for j in range(SUBLANES):
                t = g * SUBLANES + j
                g1[_token_rows(t), :] = buf[_sorted_rows(slot_ref[0, 0, t0 + t]), :]
                g2[_token_rows(t), :] = buf[_sorted_rows(slot_ref[0, 0, seg + t0 + t]), :]
            return carry

        lax.fori_loop(0, tc // SUBLANES, gather, 0)
        rec = rec_ref[...]
        lane = lax.broadcasted_iota(jnp.int32, rec.shape, 1)
        w1 = jnp.sum(jnp.where(lane == REC_W1, rec, 0.0), axis=-1, keepdims=True)
        w2 = jnp.sum(jnp.where(lane == REC_W2, rec, 0.0), axis=-1, keepdims=True)
        for j in range(TOKEN_ROWS):
            cols = slice(j * LANES, (j + 1) * LANES)
            o_ref[:, cols] = (x2_ref[:, cols] + w1 * g1[pl.ds(j, tc, stride=TOKEN_ROWS), :]
                              + w2 * g2[pl.ds(j, tc, stride=TOKEN_ROWS), :])


def _moe(off, nfull, rem, slot, xtm, wg, wu, wd, x2, rec, seg, td, tc):
    n = x2.shape[0]
    n_seg = n // seg
    n_disp, n_comb = seg // td, seg // tc
    rows = _seg_rows(seg) * TOKEN_ROWS
    disp_tile = lambda s, p, *_: (s * n_disp + jnp.minimum(p, n_disp - 1), 0)
    expert = lambda s, p, *_: (jnp.clip(p - n_disp, 0, N_EXPERTS - 1), 0, 0)
    comb_tile = lambda s, p, *_: (s * n_comb + jnp.clip(p - n_disp - N_EXPERTS, 0, n_comb - 1), 0)
    grid_spec = pltpu.PrefetchScalarGridSpec(
        num_scalar_prefetch=3,
        grid=(n_seg, n_disp + N_EXPERTS + n_comb),
        in_specs=[
            pl.BlockSpec((1, 1, 2 * seg), lambda s, p, *_: (s, 0, 0), memory_space=pltpu.SMEM),
            pl.BlockSpec((td * TOKEN_ROWS, LANES), disp_tile),
            pl.BlockSpec((1, D_MODEL, D_EXPERT), expert),
            pl.BlockSpec((1, D_MODEL, D_EXPERT), expert),
            pl.BlockSpec((1, D_EXPERT, D_MODEL), expert),
            pl.BlockSpec((tc, D_MODEL), comb_tile),
            pl.BlockSpec((tc, LANES), comb_tile),
        ],
        out_specs=pl.BlockSpec((tc, D_MODEL), comb_tile),
        scratch_shapes=[pltpu.VMEM((rows, LANES), F32),
                        pltpu.VMEM((tc * TOKEN_ROWS, LANES), F32),
                        pltpu.VMEM((tc * TOKEN_ROWS, LANES), F32)],
    )
    return pl.pallas_call(
        functools.partial(_moe_kernel, seg=seg, td=td, tc=tc),
        grid_spec=grid_spec,
        out_shape=jax.ShapeDtypeStruct((n, D_MODEL), F32),
        compiler_params=pltpu.CompilerParams(
            dimension_semantics=("arbitrary", "arbitrary"), vmem_limit_bytes=MOE_VMEM_LIMIT),
        name="moe",
    )(off, nfull, rem, slot, xtm, wg, wu, wd, x2, rec)


def _plan(rect, cnt, seg):
    n = rect.shape[1]
    expert = rect[REC_E1:REC_E2 + 1].astype(jnp.int32)
    rank = rect[REC_R1:REC_R2 + 1].astype(jnp.int32)
    counts = cnt[:, 0, :N_EXPERTS].astype(jnp.int32)
    padded = (counts + SUBLANES - 1) // SUBLANES * SUBLANES
    off = jnp.cumsum(padded, axis=1) - padded
    n_full = counts // MOE_CHUNK
    rem = counts - n_full * MOE_CHUNK
    join = (rem > 0) & (rem <= MOE_TAIL) & (n_full > 0)
    n_full = n_full - join
    rem = rem + join * MOE_CHUNK
    off_tok = jnp.repeat(off.T, seg, axis=1)
    hit = expert[:, None, :] == jnp.arange(N_EXPERTS, dtype=jnp.int32)[None, :, None]
    slot = (rank + jnp.sum(jnp.where(hit, off_tok[None], 0), axis=1)) * TOKEN_ROWS
    slot = slot.reshape(2, n // seg, seg).transpose(1, 0, 2).reshape(n // seg, 1, 2 * seg)
    return off.reshape(-1), n_full.reshape(-1), rem.reshape(-1), slot


def _rope_tables(pos):
    half = HEAD_DIM // 2
    inv_freq = ROPE_THETA ** (-jnp.arange(half, dtype=F32) / half)
    ang = pos[:, None] * inv_freq[None, :]
    cos = jnp.cos(ang)
    sin = jnp.sin(ang)
    reps = LANES // HEAD_DIM
    cos_t = jnp.tile(jnp.concatenate([cos, cos], axis=-1), (1, reps))
    sin_t = jnp.tile(jnp.concatenate([-sin, sin], axis=-1), (1, reps))
    return cos_t, sin_t


def kernel(x_prompt, x_sample, cache_k_win, cache_v_win, state_conv, state_lru_h, attn_norm_g, w_in, q_norm_g, k_norm_g, attn_sinks, conv_w, conv_b, w_lru_a, b_lru_a, w_lru_i, b_lru_i, lru_lambda, w_br_attn, w_br_rnn, w_out, ffn_norm_g, w_route_group, b_route_group, w_route_expert, b_route_expert, w_exp_gate, w_exp_up, w_exp_down):
    batch, seq, _ = x_prompt.shape
    dec_batch, dec_seq, _ = x_sample.shape
    depth = w_in.shape[0]
    assert depth == 1 and dec_seq == 1
    l = 0

    w_in_f = w_in[l]
    qkg = jnp.concatenate([jnp.tile(q_norm_g[l], N_Q_HEADS), jnp.tile(k_norm_g[l], N_KV_HEADS)])[None, :]
    wcat_f = jnp.concatenate([w_lru_a[l], w_lru_i[l]], axis=-1)
    wa_f, wr_f, wo_f = w_br_attn[l], w_br_rnn[l], w_out[l]
    w_route_f = jnp.concatenate(
        [w_route_expert[l], w_route_group[l],
         jnp.zeros((D_MODEL, LANES - N_EXPERTS - N_GROUPS), F32)], axis=-1)
    w_in_b, wcat = w_in_f.astype(BF16), wcat_f.astype(BF16)
    wa_b, wr_b, wo_b, w_route = (w.astype(BF16) for w in (wa_f, wr_f, wo_f, w_route_f))
    b_route = jnp.concatenate(
        [b_route_expert[l], b_route_group[l], jnp.zeros((LANES - N_EXPERTS - N_GROUPS,), F32)])[None, :]
    wg, wu, wd = (w[l].astype(BF16) for w in (w_exp_gate, w_exp_up, w_exp_down))
    g1 = attn_norm_g[l][None, :]
    g2 = ffn_norm_g[l][None, :]
    cw, cb = conv_w[l], conv_b[l][None, :]
    b_a, b_i, lam = b_lru_a[l][None, :], b_lru_i[l][None, :], lru_lambda[l][None, :]
    sinks = attn_sinks[l]

    def tail(x, attn, rnn, sga, sgr, tm, seg, precise):
        wa, wr, wo, wrt = (wa_f, wr_f, wo_f, w_route_f) if precise else (wa_b, wr_b, wo_b, w_route)
        tri = jnp.tril(jnp.ones((tm, tm), BF16), -1)
        x2, xtm, rec, rect, cnt = _merge(x, attn, rnn, sga, sgr, wa, wr, wo, g2, wrt, b_route, tri,
                                         tm, seg, precise)
        off, nfull, rem, slot = _plan(rect, cnt, seg)
        return _moe(off, nfull, rem, slot, xtm, wg, wu, wd, x2, rec, seg, tm, min(tm, 256))

    xp = x_prompt.reshape(batch * seq, D_MODEL)
    cos_p, sin_p = _rope_tables(jnp.arange(seq, dtype=F32))
    q, k, v, xr, gy, sga, sgr = _proj(xp, g1, w_in_b, cos_p, sin_p, qkg, 512, False)
    attn = _attn_prompt(q, k, v, sinks, batch, seq)
    rnn, h_last = _rnn_prompt(xr, gy, cw, cb, wcat, b_a, b_i, lam, batch, seq, 512)
    y_prompt = tail(xp, attn, rnn, sga, sgr, 512, 4096, False).reshape(batch, seq, D_MODEL)
    def last_rows(a, rows):
        return a.reshape(batch, seq, a.shape[-1])[:, seq - rows:]

    k_win_p = last_rows(k, WINDOW).reshape(1, batch, WINDOW, N_KV_HEADS, HEAD_DIM)
    v_win_p = last_rows(v, WINDOW).reshape(1, batch, WINDOW, N_KV_HEADS, HEAD_DIM)
    conv_p = last_rows(xr, CONV_W - 1)[None]
    h_p = h_last[None, :, 0, :]

    xs = x_sample.reshape(dec_batch, D_MODEL)
    cos_s, sin_s = _rope_tables(jnp.full((dec_batch,), PAST_LEN, F32))
    qs, ks, vs, xrs, gys, sgas, sgrs = _proj(xs, g1, w_in_f, cos_s, sin_s, qkg, dec_batch, True)
    ck = cache_k_win[l].reshape(dec_batch, WINDOW, D_KV)
    cv = cache_v_win[l].reshape(dec_batch, WINDOW, D_KV)
    attn_s, k_win_s, v_win_s = _attn_sample(qs, ks, vs, ck, cv, sinks)
    sc = state_conv[l]
    rnn_s, h_s = _rnn_sample(xrs, gys, sc[:, 0], sc[:, 1], sc[:, 2], state_lru_h[l],
                             cw, cb, wcat_f, b_a, b_i, lam)
    y_sample = tail(xs, attn_s, rnn_s, sgas, sgrs, dec_batch, dec_batch, True).reshape(dec_batch, 1, D_MODEL)
    conv_s = jnp.stack([sc[:, 1], sc[:, 2], xrs], axis=1)[None]

    return (y_prompt, y_sample, k_win_p, v_win_p, conv_p, h_p,
            k_win_s.reshape(1, dec_batch, WINDOW, N_KV_HEADS, HEAD_DIM),
            v_win_s.reshape(1, dec_batch, WINDOW, N_KV_HEADS, HEAD_DIM),
            conv_s, h_s[None])
```

```python
import functools

import jax
import jax.numpy as jnp
from jax import lax
from jax.experimental import pallas as pl
from jax.experimental.pallas import tpu as pltpu

D_MODEL = 1024
HEAD_DIM = 64
N_Q_HEADS = 8
N_KV_HEADS = 2
Q_PER_KV = N_Q_HEADS // N_KV_HEADS
WINDOW = 128
ATTN_BLOCK = 128
ROPE_THETA = 10000.0
SCALE = HEAD_DIM ** -0.5
NEG_INF = -1e30
D_RNN = 1280
N_RNN_BLOCKS = 10
RNN_BLOCK = D_RNN // N_RNN_BLOCKS
CONV_W = 4
LRU_C = 8.0
N_GROUPS = 4
EXPERTS_PER_GROUP = 8
N_EXPERTS = N_GROUPS * EXPERTS_PER_GROUP
D_EXPERT = 256
PAST_LEN = 16384
EPS = 1e-6
D_Q = N_Q_HEADS * HEAD_DIM
D_KV = N_KV_HEADS * HEAD_DIM
D_IN = D_Q + 2 * D_KV + 2 * D_RNN + 2 * D_MODEL
OFF_K = D_Q
OFF_V = OFF_K + D_KV
OFF_XR = OFF_V + D_KV
OFF_YR = OFF_XR + D_RNN
OFF_GA = OFF_YR + D_RNN
OFF_GR = OFF_GA + D_MODEL

LANES = 128
SUBLANES = 8
VMEM_LIMIT = 56 * 1024 * 1024

F32 = jnp.float32
BF16 = jnp.bfloat16


def _params(*sem):
    return pltpu.CompilerParams(dimension_semantics=sem, vmem_limit_bytes=VMEM_LIMIT)


def _sigmoid(x):
    return 1.0 / (1.0 + jnp.exp(-x))


def _gelu_tanh(x):
    c = 0.7978845608028654
    return 0.5 * x * (1.0 + jnp.tanh(c * (x + 0.044715 * (x * x * x))))


def _full(shape, single_buffer=False):
    index_map = lambda *_: (0,) * len(shape)
    if single_buffer:
        return pl.BlockSpec(shape, index_map, pipeline_mode=pl.Buffered(1))
    return pl.BlockSpec(shape, index_map)


def _mm(a, b, precise, dims=None):
    if precise:
        a, b, prec = a.astype(F32), b.astype(F32), lax.Precision.HIGHEST
    else:
        a, b, prec = a.astype(BF16), b.astype(BF16), None
    if dims is None:
        return jnp.dot(a, b, preferred_element_type=F32, precision=prec)
    return lax.dot_general(a, b, dims, preferred_element_type=F32, precision=prec)


_NT = (((1,), (1,)), ((), ()))


def _proj_kernel(x_ref, g_ref, w_ref, cos_ref, sin_ref, qkg_ref,
                 q_ref, k_ref, v_ref, xr_ref, gy_ref, sga_ref, sgr_ref, *, precise):
    x = x_ref[...]
    inv = lax.rsqrt(jnp.mean(x * x, axis=-1, keepdims=True) + EPS)
    xn = x * inv * g_ref[...]
    if not precise:
        xn = xn.astype(BF16)

    def proj(lo, hi):
        return _mm(xn, w_ref[:, lo:hi], precise)

    qk = proj(0, OFF_V)
    tm = qk.shape[0]
    lane = lax.broadcasted_iota(jnp.int32, (tm, LANES), 1)
    lo_head = lane < HEAD_DIM
    first_half = (lane % HEAD_DIM) < (HEAD_DIM // 2)
    cos = cos_ref[...]
    sin = sin_ref[...]
    for g in range(OFF_V // LANES):
        seg = qk[:, g * LANES:(g + 1) * LANES]
        sq = seg * seg
        s_lo = jnp.sum(jnp.where(lo_head, sq, 0.0), axis=-1, keepdims=True)
        s_hi = jnp.sum(jnp.where(lo_head, 0.0, sq), axis=-1, keepdims=True)
        ms = jnp.where(lo_head, s_lo, s_hi) * (1.0 / HEAD_DIM)
        normed = seg * lax.rsqrt(ms + EPS) * qkg_ref[:, g * LANES:(g + 1) * LANES]
        partner = jnp.where(first_half,
                            pltpu.roll(normed, LANES - HEAD_DIM // 2, axis=1),
                            pltpu.roll(normed, HEAD_DIM // 2, axis=1))
        roped = normed * cos + partner * sin
        if g < D_Q // LANES:
            q_ref[:, g * LANES:(g + 1) * LANES] = (roped * SCALE).astype(q_ref.dtype)
        else:
            k_ref[...] = roped
    v_ref[...] = proj(OFF_V, OFF_XR)
    xr_ref[...] = proj(OFF_XR, OFF_YR)
    gy_ref[...] = _gelu_tanh(proj(OFF_YR, OFF_GA)).astype(gy_ref.dtype)
    sga_ref[...] = _sigmoid(proj(OFF_GA, OFF_GR)).astype(sga_ref.dtype)
    sgr_ref[...] = _sigmoid(proj(OFF_GR, D_IN)).astype(sgr_ref.dtype)


def _proj(x, g, w_in, cos_t, sin_t, qkg, tm, precise):
    n = x.shape[0]
    t_blocks = cos_t.shape[0] // tm
    row = lambda i: (i, 0)
    act = F32 if precise else BF16
    out_shape = (
        jax.ShapeDtypeStruct((n, D_Q), act),
        jax.ShapeDtypeStruct((n, D_KV), F32),
        jax.ShapeDtypeStruct((n, D_KV), F32),
        jax.ShapeDtypeStruct((n, D_RNN), F32),
        jax.ShapeDtypeStruct((n, D_RNN), act),
        jax.ShapeDtypeStruct((n, D_MODEL), act),
        jax.ShapeDtypeStruct((n, D_MODEL), act),
    )
    return pl.pallas_call(
        functools.partial(_proj_kernel, precise=precise),
        grid=(n // tm,),
        in_specs=[
            pl.BlockSpec((tm, D_MODEL), row),
            _full((1, D_MODEL)),
            _full((D_MODEL, D_IN), single_buffer=True),
            pl.BlockSpec((tm, LANES), lambda i: (i % t_blocks, 0)),
            pl.BlockSpec((tm, LANES), lambda i: (i % t_blocks, 0)),
            _full((1, OFF_V)),
        ],
        out_specs=[pl.BlockSpec((tm, s.shape[1]), row) for s in out_shape],
        out_shape=out_shape,
        compiler_params=_params("parallel"),
        name="proj",
    )(x, g, w_in, cos_t, sin_t, qkg)


def _softmax_pv(s, sink, v2, precise):
    m = jnp.maximum(jnp.max(s, axis=-1, keepdims=True), sink)
    p = jnp.exp(s - m)
    denom = jnp.sum(p, axis=-1, keepdims=True) + jnp.exp(sink - m)
    return _mm(p, v2, precise) * (1.0 / denom)


def _sink_column(sink_ref, h, rows, rows_per_head):
    r = lax.broadcasted_iota(jnp.int32, (rows, 1), 0) // rows_per_head
    col = jnp.full((rows, 1), sink_ref[h * Q_PER_KV], F32)
    for g in range(1, Q_PER_KV):
        col = jnp.where(r == g, sink_ref[h * Q_PER_KV + g], col)
    return col


ATTN_BLOCKS_PER_STEP = 1


def _attn_prompt_kernel(sink_ref, q_ref, kc_ref, kp_ref, vc_ref, vp_ref, o_ref):
    n = pl.program_id(1)
    blk = ATTN_BLOCK
    k_all = jnp.concatenate([kp_ref[...], kc_ref[...]], axis=0).astype(BF16)
    v_all = jnp.concatenate([vp_ref[...], vc_ref[...]], axis=0).astype(BF16)
    rows = Q_PER_KV * blk
    i = lax.broadcasted_iota(jnp.int32, (rows, 2 * blk), 0) % blk
    j = lax.broadcasted_iota(jnp.int32, (rows, 2 * blk), 1)
    d = j - i
    in_window = (d >= 1) & (d <= WINDOW)
    first_valid = (d >= jnp.where(n > 0, 1, jnp.maximum(1, blk - i))) & (d <= WINDOW)
    for sub in range(ATTN_BLOCKS_PER_STEP):
        q = q_ref[sub * blk:(sub + 1) * blk, :]
        k2 = k_all[sub * blk:(sub + 2) * blk]
        v2 = v_all[sub * blk:(sub + 2) * blk]
        valid = first_valid if sub == 0 else in_window
        for h in range(N_KV_HEADS):
            kh = k2[:, h * HEAD_DIM:(h + 1) * HEAD_DIM]
            vh = v2[:, h * HEAD_DIM:(h + 1) * HEAD_DIM]
            qs = jnp.concatenate(
                [q[:, (h * Q_PER_KV + g) * HEAD_DIM:(h * Q_PER_KV + g + 1) * HEAD_DIM]
                 for g in range(Q_PER_KV)], axis=0)
            s = _mm(qs, kh, False, _NT)
            s = jnp.where(valid, s, NEG_INF)
            o = _softmax_pv(s, _sink_column(sink_ref, h, rows, blk), vh, False)
            for g in range(Q_PER_KV):
                c = (h * Q_PER_KV + g) * HEAD_DIM
                o_ref[sub * blk:(sub + 1) * blk, c:c + HEAD_DIM] = (
                    o[g * blk:(g + 1) * blk].astype(o_ref.dtype))


def _attn_prompt(q, k, v, sinks, batch, seq):
    step = ATTN_BLOCKS_PER_STEP * ATTN_BLOCK
    ns = seq // step
    cur = lambda b, n: (b * ns + n, 0)
    prev = lambda b, n: (jnp.maximum((b * ns + n) * ATTN_BLOCKS_PER_STEP - 1, 0), 0)
    return pl.pallas_call(
        _attn_prompt_kernel,
        grid=(batch, ns),
        in_specs=[
            pl.BlockSpec(memory_space=pltpu.SMEM),
            pl.BlockSpec((step, D_Q), cur),
            pl.BlockSpec((step, D_KV), cur),
            pl.BlockSpec((ATTN_BLOCK, D_KV), prev),
            pl.BlockSpec((step, D_KV), cur),
            pl.BlockSpec((ATTN_BLOCK, D_KV), prev),
        ],
        out_specs=pl.BlockSpec((step, D_Q), cur),
        out_shape=jax.ShapeDtypeStruct((batch * seq, D_Q), BF16),
        compiler_params=_params("parallel", "parallel"),
        name="attn_prompt",
    )(sinks, q, k, k, v, v)


SAMPLE_BT = 8


def _attn_sample_kernel(sink_ref, q_ref, kn_ref, vn_ref, kc_ref, vc_ref, o_ref, ko_ref, vo_ref):
    bt = SAMPLE_BT
    w = lax.broadcasted_iota(jnp.int32, (bt, WINDOW, D_KV), 1)

    def shifted(cache_ref, new_ref):
        rolled = pltpu.roll(cache_ref[...], WINDOW - 1, axis=1)
        return jnp.where(w == WINDOW - 1, new_ref[...][:, None, :], rolled)

    k_win = shifted(kc_ref, kn_ref)
    v_win = shifted(vc_ref, vn_ref)
    ko_ref[...] = k_win
    vo_ref[...] = v_win
    k2 = k_win.reshape(bt * WINDOW, D_KV)
    v2 = v_win.reshape(bt * WINDOW, D_KV)
    q = q_ref[...]
    rows = Q_PER_KV * bt
    rb = lax.broadcasted_iota(jnp.int32, (rows, bt * WINDOW), 0) % bt
    cb = lax.broadcasted_iota(jnp.int32, (rows, bt * WINDOW), 1) // WINDOW
    valid = rb == cb
    for h in range(N_KV_HEADS):
        kh = k2[:, h * HEAD_DIM:(h + 1) * HEAD_DIM]
        vh = v2[:, h * HEAD_DIM:(h + 1) * HEAD_DIM]
        qs = jnp.concatenate(
            [q[:, (h * Q_PER_KV + g) * HEAD_DIM:(h * Q_PER_KV + g + 1) * HEAD_DIM]
             for g in range(Q_PER_KV)], axis=0)
        s = _mm(qs, kh, True, _NT)
        s = jnp.where(valid, s, NEG_INF)
        o = _softmax_pv(s, _sink_column(sink_ref, h, rows, bt), vh, True)
        for g in range(Q_PER_KV):
            c = (h * Q_PER_KV + g) * HEAD_DIM
            o_ref[:, c:c + HEAD_DIM] = o[g * bt:(g + 1) * bt].astype(o_ref.dtype)


def _attn_sample(q, k_new, v_new, cache_k, cache_v, sinks):
    nbatch = q.shape[0]
    bt = SAMPLE_BT
    row = lambda i: (i, 0)
    win = lambda i: (i, 0, 0)
    return pl.pallas_call(
        _attn_sample_kernel,
        grid=(nbatch // bt,),
        in_specs=[
            pl.BlockSpec(memory_space=pltpu.SMEM),
            pl.BlockSpec((bt, D_Q), row),
            pl.BlockSpec((bt, D_KV), row),
            pl.BlockSpec((bt, D_KV), row),
            pl.BlockSpec((bt, WINDOW, D_KV), win),
            pl.BlockSpec((bt, WINDOW, D_KV), win),
        ],
        out_specs=[
            pl.BlockSpec((bt, D_Q), row),
            pl.BlockSpec((bt, WINDOW, D_KV), win),
            pl.BlockSpec((bt, WINDOW, D_KV), win),
        ],
        out_shape=(
            jax.ShapeDtypeStruct((nbatch, D_Q), F32),
            jax.ShapeDtypeStruct((nbatch, WINDOW, D_KV), F32),
            jax.ShapeDtypeStruct((nbatch, WINDOW, D_KV), F32),
        ),
        compiler_params=_params("parallel"),
        name="attn_sample",
    )(sinks, q, k_new, v_new, cache_k, cache_v)


def _lru_terms(xc, wcat_ref, ba_ref, bi_ref, lam_ref, precise):
    xcb = xc if precise else xc.astype(BF16)
    ya, yi = [], []
    for n in range(N_RNN_BLOCKS):
        y = _mm(xcb[:, n * RNN_BLOCK:(n + 1) * RNN_BLOCK], wcat_ref[n], precise)
        ya.append(y[:, :RNN_BLOCK])
        yi.append(y[:, RNN_BLOCK:])
    r = _sigmoid(jnp.concatenate(ya, axis=-1) + ba_ref[...])
    gate_i = _sigmoid(jnp.concatenate(yi, axis=-1) + bi_ref[...])
    neg_lam = -lam_ref[...]
    softplus = jnp.maximum(neg_lam, 0.0) + jnp.log1p(jnp.exp(-jnp.abs(neg_lam)))
    log_a = (-LRU_C * softplus) * r
    a = jnp.exp(log_a)
    m = 1.0 - a * a
    b = jnp.where(m > 0.0, m * lax.rsqrt(m), 0.0) * (gate_i * xc)
    return a, b


def _rnn_prompt_kernel(xr_ref, gy_ref, cw_ref, cb_ref, wcat_ref, ba_ref, bi_ref, lam_ref,
                       o_ref, hl_ref, xbuf, a_scr, b_scr, h_scr, hcar):
    tt = xr_ref.shape[0]
    groups = tt // SUBLANES

    @pl.when(pl.program_id(1) == 0)
    def _():
        xbuf[0:SUBLANES, :] = jnp.zeros((SUBLANES, D_RNN), F32)
        hcar[...] = jnp.zeros((SUBLANES, D_RNN), F32)

    x = xr_ref[...]
    xbuf[SUBLANES:, :] = x
    xc = cb_ref[...] + cw_ref[CONV_W - 1:CONV_W, :] * x
    for j in range(CONV_W - 1):
        s = CONV_W - 1 - j
        xc = xc + cw_ref[j:j + 1, :] * xbuf[SUBLANES - s:SUBLANES - s + tt, :]
    xbuf[0:SUBLANES, :] = x[tt - SUBLANES:, :]

    a, b = _lru_terms(xc, wcat_ref, ba_ref, bi_ref, lam_ref, False)
    a = a.reshape(groups, SUBLANES, D_RNN)
    b = b.reshape(groups, SUBLANES, D_RNN)
    step = lax.broadcasted_iota(jnp.int32, (groups, SUBLANES, D_RNN), 1)
    k = 1
    while k < SUBLANES:
        keep = step >= k
        a_sh = jnp.where(keep, pltpu.roll(a, k, axis=1), 1.0)
        b_sh = jnp.where(keep, pltpu.roll(b, k, axis=1), 0.0)
        b = a * b_sh + b
        a = a * a_sh
        k *= 2
    a_scr[...] = a
    b_scr[...] = b

    def chain(g, h_in):
        h = a_scr[g] * h_in + b_scr[g]
        h_scr[g] = h
        return jnp.broadcast_to(h[SUBLANES - 1:SUBLANES, :], (SUBLANES, D_RNN))

    h_last = lax.fori_loop(0, groups, chain, hcar[...])
    hcar[...] = h_last
    hl_ref[0] = h_last
    h = h_scr[...].reshape(tt, D_RNN)
    o_ref[...] = (h * gy_ref[...].astype(F32)).astype(o_ref.dtype)


def _rnn_prompt(xr, gy, conv_w, conv_b, wcat, b_a, b_i, lam, batch, seq, tt):
    nt = seq // tt
    row = lambda b, t: (b * nt + t, 0)
    groups = tt // SUBLANES
    return pl.pallas_call(
        _rnn_prompt_kernel,
        grid=(batch, nt),
        in_specs=[
            pl.BlockSpec((tt, D_RNN), row),
            pl.BlockSpec((tt, D_RNN), row),
            _full((CONV_W, D_RNN)),
            _full((1, D_RNN)),
            _full((N_RNN_BLOCKS, RNN_BLOCK, 2 * RNN_BLOCK)),
            _full((1, D_RNN)),
            _full((1, D_RNN)),
            _full((1, D_RNN)),
        ],
        out_specs=[
            pl.BlockSpec((tt, D_RNN), row),
            pl.BlockSpec((1, SUBLANES, D_RNN), lambda b, t: (b, 0, 0)),
        ],
        out_shape=(
            jax.ShapeDtypeStruct((batch * seq, D_RNN), BF16),
            jax.ShapeDtypeStruct((batch, SUBLANES, D_RNN), F32),
        ),
        scratch_shapes=[
            pltpu.VMEM((tt + SUBLANES, D_RNN), F32),
            pltpu.VMEM((groups, SUBLANES, D_RNN), F32),
            pltpu.VMEM((groups, SUBLANES, D_RNN), F32),
            pltpu.VMEM((groups, SUBLANES, D_RNN), F32),
            pltpu.VMEM((SUBLANES, D_RNN), F32),
        ],
        compiler_params=_params("parallel", "arbitrary"),
        name="rnn_prompt",
    )(xr, gy, conv_w, conv_b, wcat, b_a, b_i, lam)


def _rnn_sample_kernel(xr_ref, gy_ref, s0_ref, s1_ref, s2_ref, h_ref, cw_ref, cb_ref,
                       wcat_ref, ba_ref, bi_ref, lam_ref, o_ref, hn_ref):
    x = xr_ref[...]
    xc = (cb_ref[...] + cw_ref[0:1, :] * s0_ref[...] + cw_ref[1:2, :] * s1_ref[...]
          + cw_ref[2:3, :] * s2_ref[...] + cw_ref[3:4, :] * x)
    a, b = _lru_terms(xc, wcat_ref, ba_ref, bi_ref, lam_ref, True)
    h = a * h_ref[...] + b
    hn_ref[...] = h
    o_ref[...] = (h * gy_ref[...].astype(F32)).astype(o_ref.dtype)


def _rnn_sample(xr, gy, s0, s1, s2, h_prev, conv_w, conv_b, wcat, b_a, b_i, lam):
    n = xr.shape[0]
    act = _full((n, D_RNN))
    return pl.pallas_call(
        _rnn_sample_kernel,
        grid=(1,),
        in_specs=[act, act, act, act, act, act,
                  _full((CONV_W, D_RNN)), _full((1, D_RNN)),
                  _full((N_RNN_BLOCKS, RNN_BLOCK, 2 * RNN_BLOCK)),
                  _full((1, D_RNN)), _full((1, D_RNN)), _full((1, D_RNN))],
        out_specs=[act, act],
        out_shape=(jax.ShapeDtypeStruct((n, D_RNN), F32),
                   jax.ShapeDtypeStruct((n, D_RNN), F32)),
        compiler_params=_params("arbitrary"),
        name="rnn_sample",
    )(xr, gy, s0, s1, s2, h_prev, conv_w, conv_b, wcat, b_a, b_i, lam)


def _merge_kernel(x_ref, at_ref, rn_ref, sga_ref, sgr_ref, wa_ref, wr_ref, wo_ref, g_ref,
                  wrt_ref, brt_ref, tri_ref, *rest, precise, tiles_per_seg, cast_experts):
    if cast_experts:
        wg_ref, wu_ref, wd_ref, *rest = rest
        x2_ref, xtm_ref, rec_ref, rect_ref, cnt_ref, wg_o, wu_o, wd_o, cnt_scr = rest
        wg_o[...] = wg_ref[...].astype(BF16)
        wu_o[...] = wu_ref[...].astype(BF16)
        wd_o[...] = wd_ref[...].astype(BF16)
    else:
        x2_ref, xtm_ref, rec_ref, rect_ref, cnt_ref, cnt_scr = rest
    ya = _mm(at_ref[...], wa_ref[...], precise)
    yr = _mm(rn_ref[...], wr_ref[...], precise)
    merged = sga_ref[...].astype(F32) * ya + sgr_ref[...].astype(F32) * yr
    x2 = x_ref[...] + _mm(merged, wo_ref[...], precise)
    x2_ref[...] = x2
    inv = lax.rsqrt(jnp.mean(x2 * x2, axis=-1, keepdims=True) + EPS)
    xn = x2 * inv * g_ref[...]

    logits = _mm(xn, wrt_ref[...], precise) + brt_ref[...]
    tm = logits.shape[0]
    lane_i = lax.broadcasted_iota(jnp.int32, (tm, LANES), 1)
    lane = lane_i.astype(F32)
    big = float(LANES)
    is_grp = (lane_i >= N_EXPERTS) & (lane_i < N_EXPERTS + N_GROUPS)
    gl = jnp.where(is_grp, logits, NEG_INF)
    gmax = jnp.max(gl, axis=-1, keepdims=True)
    g_idx = jnp.min(jnp.where(gl == gmax, lane, big), axis=-1, keepdims=True) - N_EXPERTS
    p_g = 1.0 / jnp.sum(jnp.exp(gl - gmax), axis=-1, keepdims=True)
    in_grp = (lane_i // EXPERTS_PER_GROUP).astype(F32) == g_idx
    el = jnp.where(in_grp, logits, NEG_INF)
    v1 = jnp.max(el, axis=-1, keepdims=True)
    i1 = jnp.min(jnp.where(el == v1, lane, big), axis=-1, keepdims=True)
    el2 = jnp.where(lane == i1, NEG_INF, el)
    v2 = jnp.max(el2, axis=-1, keepdims=True)
    i2 = jnp.min(jnp.where(el2 == v2, lane, big), axis=-1, keepdims=True)
    e2 = jnp.exp(v2 - v1)
    w1 = p_g / (1.0 + e2)
    w2 = p_g * e2 / (1.0 + e2)

    @pl.when(pl.program_id(0) % tiles_per_seg == 0)
    def _():
        cnt_scr[...] = jnp.zeros_like(cnt_scr)

    hit = jnp.where(lane == i1, 1.0, jnp.where(lane == i2, 1.0, 0.0))
    before = jnp.dot(tri_ref[...], hit.astype(BF16), preferred_element_type=F32) + cnt_scr[0:1, :]
    r1 = jnp.sum(jnp.where(lane == i1, before, 0.0), axis=-1, keepdims=True)
    r2 = jnp.sum(jnp.where(lane == i2, before, 0.0), axis=-1, keepdims=True)
    counts = cnt_scr[...] + jnp.sum(hit, axis=0, keepdims=True)
    cnt_scr[...] = counts
    cnt_ref[0] = counts
    rec = jnp.where(lane == REC_W2, w2, 0.0)
    for field, val in ((REC_W1, w1), (REC_R2, r2), (REC_R1, r1), (REC_E2, i2), (REC_E1, i1)):
        rec = jnp.where(lane == field, val, rec)
    rec_ref[...] = rec
    rect_ref[...] = rec.T

    for c in range(TOKEN_ROWS):
        xtm_ref[pl.ds(c, tm, stride=TOKEN_ROWS), :] = xn[:, c * LANES:(c + 1) * LANES]


REC_E1, REC_E2, REC_R1, REC_R2, REC_W1, REC_W2 = range(6)
TOKEN_ROWS = D_MODEL // LANES


def _merge(x, attn, rnn, sga, sgr, wa, wr, wo, g2, w_route, b_route, tri, tm, seg, precise,
           expert_weights=None):
    n = x.shape[0]
    steps = n // tm
    tiles_per_seg = seg // tm
    row = lambda i: (i, 0)
    in_specs = [
        pl.BlockSpec((tm, D_MODEL), row),
        pl.BlockSpec((tm, D_Q), row),
        pl.BlockSpec((tm, D_RNN), row),
        pl.BlockSpec((tm, D_MODEL), row),
        pl.BlockSpec((tm, D_MODEL), row),
        _full((D_Q, D_MODEL)),
        _full((D_RNN, D_MODEL)),
        _full((D_MODEL, D_MODEL)),
        _full((1, D_MODEL)),
        _full((D_MODEL, LANES)),
        _full((1, LANES)),
        _full((tm, tm)),
    ]
    out_specs = [
        pl.BlockSpec((tm, D_MODEL), row),
        pl.BlockSpec((tm * TOKEN_ROWS, LANES), row),
        pl.BlockSpec((tm, LANES), row),
        pl.BlockSpec((LANES, tm), lambda i: (0, i)),
        pl.BlockSpec((1, SUBLANES, LANES), lambda i: (i // tiles_per_seg, 0, 0)),
    ]
    out_shape = [
        jax.ShapeDtypeStruct((n, D_MODEL), F32),
        jax.ShapeDtypeStruct((n * TOKEN_ROWS, LANES), F32),
        jax.ShapeDtypeStruct((n, LANES), F32),
        jax.ShapeDtypeStruct((LANES, n), F32),
        jax.ShapeDtypeStruct((n // seg, SUBLANES, LANES), F32),
    ]
    args = [x, attn, rnn, sga, sgr, wa, wr, wo, g2, w_route, b_route, tri]
    if expert_weights is not None:
        assert steps == N_EXPERTS
        for w in expert_weights:
            spec = pl.BlockSpec((1,) + w.shape[1:], lambda i: (i, 0, 0))
            in_specs.append(spec)
            out_specs.append(spec)
            out_shape.append(jax.ShapeDtypeStruct(w.shape, BF16))
            args.append(w)
    return pl.pallas_call(
        functools.partial(_merge_kernel, precise=precise, tiles_per_seg=tiles_per_seg,
                          cast_experts=expert_weights is not None),
        grid=(steps,),
        in_specs=in_specs,
        out_specs=out_specs,
        out_shape=out_shape,
        scratch_shapes=[pltpu.VMEM((SUBLANES, LANES), F32)],
        compiler_params=_params("arbitrary"),
        name="merge",
    )(*args)


MOE_CHUNK = 256
MOE_TAIL = 128
MOE_VMEM_LIMIT = 60 * 1024 * 1024


def _seg_rows(seg):
    return 2 * seg + N_EXPERTS * SUBLANES + MOE_CHUNK


def _token_rows(i):
    return pl.ds(pl.multiple_of(i * TOKEN_ROWS, TOKEN_ROWS), TOKEN_ROWS)


def _sorted_rows(first_row):
    return pl.ds(pl.multiple_of(first_row, TOKEN_ROWS), TOKEN_ROWS)


def _moe_kernel(off_ref, nfull_ref, rem_ref, slot_ref, xtm_ref, wg_ref, wu_ref, wd_ref,
                x2_ref, rec_ref, o_ref, buf, g1, g2, *, seg, td, tc):
    s = pl.program_id(0)
    p = pl.program_id(1)
    n_disp = seg // td

    @pl.when((s == 0) & (p == 0))
    def _():
        buf[...] = jnp.zeros_like(buf)

    @pl.when(p < n_disp)
    def _():
        def dispatch(g, carry):
            for j in range(SUBLANES):
                t = g * SUBLANES + j
                row = xtm_ref[_token_rows(t), :]
                for k in range(2):
                    buf[_sorted_rows(slot_ref[0, 0, k * seg + p * td + t]), :] = row
            return carry

        lax.fori_loop(0, td // SUBLANES, dispatch, 0)

    def run_chunk(row0, rows, valid):
        r0 = pl.multiple_of(row0 * TOKEN_ROWS, SUBLANES * TOKEN_ROWS)
        xf = [buf[pl.ds(r0 + j, rows, stride=TOKEN_ROWS), :] for j in range(TOKEN_ROWS)]
        x = jnp.concatenate(xf, axis=-1).astype(BF16)
        hg = jnp.dot(x, wg_ref[0], preferred_element_type=F32)
        hu = jnp.dot(x, wu_ref[0], preferred_element_type=F32)
        h = (hg * _sigmoid(hg)) * hu
        y = jnp.dot(h.astype(BF16), wd_ref[0], preferred_element_type=F32)
        if valid is not None:
            mine = lax.broadcasted_iota(jnp.int32, (rows, LANES), 0) < valid
        for j in range(TOKEN_ROWS):
            yj = y[:, j * LANES:(j + 1) * LANES]
            if valid is not None:
                yj = jnp.where(mine, yj, xf[j])
            buf[pl.ds(r0 + j, rows, stride=TOKEN_ROWS), :] = yj

    @pl.when((p >= n_disp) & (p < n_disp + N_EXPERTS))
    def _():
        idx = s * N_EXPERTS + p - n_disp
        base = off_ref[idx]
        n_full = nfull_ref[idx]
        rem = rem_ref[idx]

        def chunk(c, carry):
            run_chunk(base + c * MOE_CHUNK, MOE_CHUNK, None)
            return carry

        lax.fori_loop(0, n_full, chunk, 0)
        last = base + n_full * MOE_CHUNK
        for units in range(1, MOE_CHUNK // MOE_TAIL + 2):
            @pl.when((rem > (units - 1) * MOE_TAIL) & (rem <= units * MOE_TAIL))
            def _():
                run_chunk(last, units * MOE_TAIL, rem)

    @pl.when(p >= n_disp + N_EXPERTS)
    def _():
        t0 = (p - n_disp - N_EXPERTS) * tc

        def gather(g, carry):
            for j in range(SUBLANES):
                t = g * SUBLANES + j
                g1[_token_rows(t), :] = buf[_sorted_rows(slot_ref[0, 0, t0 + t]), :]
                g2[_token_rows(t), :] = buf[_sorted_rows(slot_ref[0, 0, seg + t0 + t]), :]
            return carry

        lax.fori_loop(0, tc // SUBLANES, gather, 0)
        rec = rec_ref[...]
        lane = lax.broadcasted_iota(jnp.int32, rec.shape, 1)
        w1 = jnp.sum(jnp.where(lane == REC_W1, rec, 0.0), axis=-1, keepdims=True)
        w2 = jnp.sum(jnp.where(lane == REC_W2, rec, 0.0), axis=-1, keepdims=True)
        for j in range(TOKEN_ROWS):
            cols = slice(j * LANES, (j + 1) * LANES)
            o_ref[:, cols] = (x2_ref[:, cols] + w1 * g1[pl.ds(j, tc, stride=TOKEN_ROWS), :]
                              + w2 * g2[pl.ds(j, tc, stride=TOKEN_ROWS), :])


def _moe(off, nfull, rem, slot, xtm, wg, wu, wd, x2, rec, seg, td, tc):
    n = x2.shape[0]
    n_seg = n // seg
    n_disp, n_comb = seg // td, seg // tc
    rows = _seg_rows(seg) * TOKEN_ROWS
    disp_tile = lambda s, p, *_: (s * n_disp + jnp.minimum(p, n_disp - 1), 0)
    expert = lambda s, p, *_: (jnp.clip(p - n_disp, 0, N_EXPERTS - 1), 0, 0)
    comb_tile = lambda s, p, *_: (s * n_comb + jnp.clip(p - n_disp - N_EXPERTS, 0, n_comb - 1), 0)
    grid_spec = pltpu.PrefetchScalarGridSpec(
        num_scalar_prefetch=3,
        grid=(n_seg, n_disp + N_EXPERTS + n_comb),
        in_specs=[
            pl.BlockSpec((1, 1, 2 * seg), lambda s, p, *_: (s, 0, 0), memory_space=pltpu.SMEM),
            pl.BlockSpec((td * TOKEN_ROWS, LANES), disp_tile),
            pl.BlockSpec((1, D_MODEL, D_EXPERT), expert),
            pl.BlockSpec((1, D_MODEL, D_EXPERT), expert),
            pl.BlockSpec((1, D_EXPERT, D_MODEL), expert),
            pl.BlockSpec((tc, D_MODEL), comb_tile),
            pl.BlockSpec((tc, LANES), comb_tile),
        ],
        out_specs=pl.BlockSpec((tc, D_MODEL), comb_tile),
        scratch_shapes=[pltpu.VMEM((rows, LANES), F32),
                        pltpu.VMEM((tc * TOKEN_ROWS, LANES), F32),
                        pltpu.VMEM((tc * TOKEN_ROWS, LANES), F32)],
    )
    return pl.pallas_call(
        functools.partial(_moe_kernel, seg=seg, td=td, tc=tc),
        grid_spec=grid_spec,
        out_shape=jax.ShapeDtypeStruct((n, D_MODEL), F32),
        compiler_params=pltpu.CompilerParams(
            dimension_semantics=("arbitrary", "arbitrary"), vmem_limit_bytes=MOE_VMEM_LIMIT),
        name="moe",
    )(off, nfull, rem, slot, xtm, wg, wu, wd, x2, rec)


def _plan(rect, cnt, seg):
    n = rect.shape[1]
    expert = rect[REC_E1:REC_E2 + 1].astype(jnp.int32)
    rank = rect[REC_R1:REC_R2 + 1].astype(jnp.int32)
    counts = cnt[:, 0, :N_EXPERTS].astype(jnp.int32)
    padded = (counts + SUBLANES - 1) // SUBLANES * SUBLANES
    off = jnp.cumsum(padded, axis=1) - padded
    n_full = counts // MOE_CHUNK
    rem = counts - n_full * MOE_CHUNK
    join = (rem > 0) & (rem <= MOE_TAIL) & (n_full > 0)
    n_full = n_full - join
    rem = rem + join * MOE_CHUNK
    off_tok = jnp.repeat(off.T, seg, axis=1)
    hit = expert[:, None, :] == jnp.arange(N_EXPERTS, dtype=jnp.int32)[None, :, None]
    slot = (rank + jnp.sum(jnp.where(hit, off_tok[None], 0), axis=1)) * TOKEN_ROWS
    slot = slot.reshape(2, n // seg, seg).transpose(1, 0, 2).reshape(n // seg, 1, 2 * seg)
    return off.reshape(-1), n_full.reshape(-1), rem.reshape(-1), slot


def _rope_tables(pos):
    half = HEAD_DIM // 2
    inv_freq = ROPE_THETA ** (-jnp.arange(half, dtype=F32) / half)
    ang = pos[:, None] * inv_freq[None, :]
    cos = jnp.cos(ang)
    sin = jnp.sin(ang)
    reps = LANES // HEAD_DIM
    cos_t = jnp.tile(jnp.concatenate([cos, cos], axis=-1), (1, reps))
    sin_t = jnp.tile(jnp.concatenate([-sin, sin], axis=-1), (1, reps))
    return cos_t, sin_t


def kernel(x_prompt, x_sample, cache_k_win, cache_v_win, state_conv, state_lru_h, attn_norm_g, w_in, q_norm_g, k_norm_g, attn_sinks, conv_w, conv_b, w_lru_a, b_lru_a, w_lru_i, b_lru_i, lru_lambda, w_br_attn, w_br_rnn, w_out, ffn_norm_g, w_route_group, b_route_group, w_route_expert, b_route_expert, w_exp_gate, w_exp_up, w_exp_down):
    batch, seq, _ = x_prompt.shape
    dec_batch, dec_seq, _ = x_sample.shape
    depth = w_in.shape[0]
    assert depth == 1 and dec_seq == 1
    l = 0

    w_in_f = w_in[l]
    qkg = jnp.concatenate([jnp.tile(q_norm_g[l], N_Q_HEADS), jnp.tile(k_norm_g[l], N_KV_HEADS)])[None, :]
    wcat_f = jnp.concatenate([w_lru_a[l], w_lru_i[l]], axis=-1)
    wa_f, wr_f, wo_f = w_br_attn[l], w_br_rnn[l], w_out[l]
    w_route_f = jnp.concatenate(
        [w_route_expert[l], w_route_group[l],
         jnp.zeros((D_MODEL, LANES - N_EXPERTS - N_GROUPS), F32)], axis=-1)
    w_in_b, wcat = w_in_f.astype(BF16), wcat_f.astype(BF16)
    wa_b, wr_b, wo_b, w_route = (w.astype(BF16) for w in (wa_f, wr_f, wo_f, w_route_f))
    b_route = jnp.concatenate(
        [b_route_expert[l], b_route_group[l], jnp.zeros((LANES - N_EXPERTS - N_GROUPS,), F32)])[None, :]
    experts_f = (w_exp_gate[l], w_exp_up[l], w_exp_down[l])
    g1 = attn_norm_g[l][None, :]
    g2 = ffn_norm_g[l][None, :]
    cw, cb = conv_w[l], conv_b[l][None, :]
    b_a, b_i, lam = b_lru_a[l][None, :], b_lru_i[l][None, :], lru_lambda[l][None, :]
    sinks = attn_sinks[l]

    def tail(x, attn, rnn, sga, sgr, tm, seg, precise, experts_b=None):
        wa, wr, wo, wrt = (wa_f, wr_f, wo_f, w_route_f) if precise else (wa_b, wr_b, wo_b, w_route)
        tri = jnp.tril(jnp.ones((tm, tm), BF16), -1)
        outs = _merge(x, attn, rnn, sga, sgr, wa, wr, wo, g2, wrt, b_route, tri, tm, seg, precise,
                      expert_weights=None if experts_b else experts_f)
        x2, xtm, rec, rect, cnt = outs[:5]
        wg, wu, wd = experts_b or outs[5:]
        off, nfull, rem, slot = _plan(rect, cnt, seg)
        y = _moe(off, nfull, rem, slot, xtm, wg, wu, wd, x2, rec, seg, tm, min(tm, 256))
        return y, (wg, wu, wd)

    xp = x_prompt.reshape(batch * seq, D_MODEL)
    cos_p, sin_p = _rope_tables(jnp.arange(seq, dtype=F32))
    q, k, v, xr, gy, sga, sgr = _proj(xp, g1, w_in_b, cos_p, sin_p, qkg, 512, False)
    attn = _attn_prompt(q, k, v, sinks, batch, seq)
    rnn, h_last = _rnn_prompt(xr, gy, cw, cb, wcat, b_a, b_i, lam, batch, seq, 512)
    y_prompt, experts_b = tail(xp, attn, rnn, sga, sgr, 512, 4096, False)
    y_prompt = y_prompt.reshape(batch, seq, D_MODEL)

    def last_rows(a, rows):
        return a.reshape(batch, seq, a.shape[-1])[:, seq - rows:]

    k_win_p = last_rows(k, WINDOW).reshape(1, batch, WINDOW, N_KV_HEADS, HEAD_DIM)
    v_win_p = last_rows(v, WINDOW).reshape(1, batch, WINDOW, N_KV_HEADS, HEAD_DIM)
    conv_p = last_rows(xr, CONV_W - 1)[None]
    h_p = h_last[None, :, 0, :]

    xs = x_sample.reshape(dec_batch, D_MODEL)
    cos_s, sin_s = _rope_tables(jnp.full((dec_batch,), PAST_LEN, F32))
    qs, ks, vs, xrs, gys, sgas, sgrs = _proj(xs, g1, w_in_f, cos_s, sin_s, qkg, dec_batch, True)
    ck = cache_k_win[l].reshape(dec_batch, WINDOW, D_KV)
    cv = cache_v_win[l].reshape(dec_batch, WINDOW, D_KV)
    attn_s, k_win_s, v_win_s = _attn_sample(qs, ks, vs, ck, cv, sinks)
    sc = state_conv[l]
    rnn_s, h_s = _rnn_sample(xrs, gys, sc[:, 0], sc[:, 1], sc[:, 2], state_lru_h[l],
                             cw, cb, wcat_f, b_a, b_i, lam)
    y_sample, _ = tail(xs, attn_s, rnn_s, sgas, sgrs, dec_batch, dec_batch, True, experts_b)
    y_sample = y_sample.reshape(dec_batch, 1, D_MODEL)
    conv_s = jnp.stack([sc[:, 1], sc[:, 2], xrs], axis=1)[None]

    return (y_prompt, y_sample, k_win_p, v_win_p, conv_p, h_p,
            k_win_s.reshape(1, dec_batch, WINDOW, N_KV_HEADS, HEAD_DIM),
            v_win_s.reshape(1, dec_batch, WINDOW, N_KV_HEADS, HEAD_DIM),
            conv_s, h_s[None])
```

```python
import functools

import jax
import jax.numpy as jnp
from jax import lax
from jax.experimental import pallas as pl
from jax.experimental.pallas import tpu as pltpu

D_MODEL = 1024
HEAD_DIM = 64
N_Q_HEADS = 8
N_KV_HEADS = 2
Q_PER_KV = N_Q_HEADS // N_KV_HEADS
WINDOW = 128
ATTN_BLOCK = 128
ROPE_THETA = 10000.0
SCALE = HEAD_DIM ** -0.5
NEG_INF = -1e30
D_RNN = 1280
N_RNN_BLOCKS = 10
RNN_BLOCK = D_RNN // N_RNN_BLOCKS
CONV_W = 4
LRU_C = 8.0
N_GROUPS = 4
EXPERTS_PER_GROUP = 8
N_EXPERTS = N_GROUPS * EXPERTS_PER_GROUP
D_EXPERT = 256
PAST_LEN = 16384
EPS = 1e-6
D_Q = N_Q_HEADS * HEAD_DIM
D_KV = N_KV_HEADS * HEAD_DIM
D_IN = D_Q + 2 * D_KV + 2 * D_RNN + 2 * D_MODEL
OFF_K = D_Q
OFF_V = OFF_K + D_KV
OFF_XR = OFF_V + D_KV
OFF_YR = OFF_XR + D_RNN
OFF_GA = OFF_YR + D_RNN
OFF_GR = OFF_GA + D_MODEL

LANES = 128
SUBLANES = 8
VMEM_LIMIT = 56 * 1024 * 1024

F32 = jnp.float32
BF16 = jnp.bfloat16


def _params(*sem):
    return pltpu.CompilerParams(dimension_semantics=sem, vmem_limit_bytes=VMEM_LIMIT)


def _sigmoid(x):
    return 1.0 / (1.0 + jnp.exp(-x))


def _gelu_tanh(x):
    c = 0.7978845608028654
    return 0.5 * x * (1.0 + jnp.tanh(c * (x + 0.044715 * (x * x * x))))


def _full(shape, single_buffer=False):
    index_map = lambda *_: (0,) * len(shape)
    if single_buffer:
        return pl.BlockSpec(shape, index_map, pipeline_mode=pl.Buffered(1))
    return pl.BlockSpec(shape, index_map)


def _mm(a, b, precise, dims=None):
    if precise:
        a, b, prec = a.astype(F32), b.astype(F32), lax.Precision.HIGHEST
    else:
        a, b, prec = a.astype(BF16), b.astype(BF16), None
    if dims is None:
        return jnp.dot(a, b, preferred_element_type=F32, precision=prec)
    return lax.dot_general(a, b, dims, preferred_element_type=F32, precision=prec)


_NT = (((1,), (1,)), ((), ()))


def _proj_kernel(x_ref, g_ref, w_ref, cos_ref, sin_ref, qkg_ref, *rest, precise, fuse_rnn):
    if fuse_rnn:
        rnn_w = rest[:6]
        q_ref, k_ref, v_ref, rnn_ref, sga_ref, sgr_ref, hl_ref, ct_ref = rest[6:14]
        xbuf, a_scr, b_scr, h_scr, hcar = rest[14:]
        pl.when(pl.program_id(1) == 0)(functools.partial(_rnn_start_sequence, xbuf, hcar))
    else:
        q_ref, k_ref, v_ref, xr_ref, gy_ref, sga_ref, sgr_ref = rest
    x = x_ref[...]
    inv = lax.rsqrt(jnp.mean(x * x, axis=-1, keepdims=True) + EPS)
    xn = x * inv * g_ref[...]
    if not precise:
        xn = xn.astype(BF16)

    def proj(lo, hi):
        return _mm(xn, w_ref[:, lo:hi], precise)

    def qk_heads():
        qk = proj(0, OFF_V)
        tm = qk.shape[0]
        lane = lax.broadcasted_iota(jnp.int32, (tm, LANES), 1)
        lo_head = lane < HEAD_DIM
        first_half = (lane % HEAD_DIM) < (HEAD_DIM // 2)
        cos = cos_ref[...]
        sin = sin_ref[...]
        for g in range(OFF_V // LANES):
            seg = qk[:, g * LANES:(g + 1) * LANES]
            sq = seg * seg
            s_lo = jnp.sum(jnp.where(lo_head, sq, 0.0), axis=-1, keepdims=True)
            s_hi = jnp.sum(jnp.where(lo_head, 0.0, sq), axis=-1, keepdims=True)
            ms = jnp.where(lo_head, s_lo, s_hi) * (1.0 / HEAD_DIM)
            normed = seg * lax.rsqrt(ms + EPS) * qkg_ref[:, g * LANES:(g + 1) * LANES]
            partner = jnp.where(first_half,
                                pltpu.roll(normed, LANES - HEAD_DIM // 2, axis=1),
                                pltpu.roll(normed, HEAD_DIM // 2, axis=1))
            roped = normed * cos + partner * sin
            if g < D_Q // LANES:
                q_ref[:, g * LANES:(g + 1) * LANES] = (roped * SCALE).astype(q_ref.dtype)
            else:
                k_ref[...] = roped

    def values():
        v_ref[...] = proj(OFF_V, OFF_XR)

    def gate(out_ref, off, c0, c1, r0=0, r1=x.shape[0]):
        y = _mm(xn[r0:r1], w_ref[:, off + c0:off + c1], precise)
        out_ref[r0:r1, c0:c1] = _sigmoid(y).astype(out_ref.dtype)

    if not fuse_rnn:
        xr_ref[...] = proj(OFF_XR, OFF_YR)
        gy_ref[...] = _gelu_tanh(proj(OFF_YR, OFF_GA)).astype(gy_ref.dtype)
        qk_heads()
        values()
        gate(sga_ref, OFF_GA, 0, D_MODEL)
        gate(sgr_ref, OFF_GR, 0, D_MODEL)
        return

    slab = RNN_SLAB_BLOCKS * RNN_BLOCK
    half = x.shape[0] // 2
    others = [qk_heads, values]
    for out_ref, off in ((sga_ref, OFF_GA), (sgr_ref, OFF_GR)):
        others += [functools.partial(gate, out_ref, off, c, c + slab, r, r + half)
                   for c in range(0, D_MODEL, slab) for r in (0, half)]
    others = iter(others)
    gy = []
    for i in range(D_RNN // slab):
        c0 = i * slab
        xr = proj(OFF_XR + c0, OFF_XR + c0 + slab)
        gy.append(_gelu_tanh(proj(OFF_YR + c0, OFF_YR + c0 + slab)))
        ct_ref[0, :, c0:c0 + slab] = xr[xr.shape[0] - SUBLANES:, :]
        for _ in _rnn_scan_terms(xr, i * RNN_SLAB_BLOCKS, *rnn_w, xbuf, a_scr, b_scr):
            next(others, lambda: None)()
    for other in others:
        other()
    _rnn_scan_finish(jnp.concatenate(gy, axis=-1), rnn_ref, hl_ref, a_scr, b_scr, h_scr, hcar)


RNN_SLAB_BLOCKS = 2


def _proj(x, g, w_in, cos_t, sin_t, qkg, batch, seq, tm, precise, rnn_weights=None):
    n = batch * seq
    nt = seq // tm
    row = lambda b, t: (b * nt + t, 0)
    per_seq = lambda b, t: (b, 0, 0)
    act = F32 if precise else BF16
    rows_out = lambda width, dtype: (jax.ShapeDtypeStruct((n, width), dtype),
                                     pl.BlockSpec((tm, width), row))
    state_out = (jax.ShapeDtypeStruct((batch, SUBLANES, D_RNN), F32),
                 pl.BlockSpec((1, SUBLANES, D_RNN), per_seq))
    outs = [rows_out(D_Q, act), rows_out(D_KV, F32), rows_out(D_KV, F32)]
    in_specs = [
        pl.BlockSpec((tm, D_MODEL), row),
        _full((1, D_MODEL)),
        _full((D_MODEL, D_IN), single_buffer=True),
        pl.BlockSpec((tm, LANES), lambda b, t: (t, 0)),
        pl.BlockSpec((tm, LANES), lambda b, t: (t, 0)),
        _full((1, OFF_V)),
    ]
    args = [x, g, w_in, cos_t, sin_t, qkg]
    scratch = []
    if rnn_weights is None:
        outs += [rows_out(D_RNN, F32), rows_out(D_RNN, act)]
    else:
        assert not precise
        outs += [rows_out(D_RNN, act)]
        in_specs += [_full(w.shape) for w in rnn_weights]
        args += list(rnn_weights)
        groups = tm // SUBLANES
        scratch = [pltpu.VMEM((tm + SUBLANES, D_RNN), F32)]
        scratch += [pltpu.VMEM((groups, SUBLANES, D_RNN), F32)] * 3
        scratch += [pltpu.VMEM((SUBLANES, D_RNN), F32)]
    outs += [rows_out(D_MODEL, act), rows_out(D_MODEL, act)]
    if rnn_weights is not None:
        outs += [state_out, state_out]
    return pl.pallas_call(
        functools.partial(_proj_kernel, precise=precise, fuse_rnn=rnn_weights is not None),
        grid=(batch, nt),
        in_specs=in_specs,
        out_specs=[spec for _, spec in outs],
        out_shape=[shape for shape, _ in outs],
        scratch_shapes=scratch,
        compiler_params=_params("parallel", "arbitrary"),
        name="proj",
    )(*args)


def _softmax_pv(s, sink, v2, precise):
    m = jnp.maximum(jnp.max(s, axis=-1, keepdims=True), sink)
    p = jnp.exp(s - m)
    denom = jnp.sum(p, axis=-1, keepdims=True) + jnp.exp(sink - m)
    return _mm(p, v2, precise) * (1.0 / denom)


def _sink_column(sink_ref, h, rows, rows_per_head):
    r = lax.broadcasted_iota(jnp.int32, (rows, 1), 0) // rows_per_head
    col = jnp.full((rows, 1), sink_ref[h * Q_PER_KV], F32)
    for g in range(1, Q_PER_KV):
        col = jnp.where(r == g, sink_ref[h * Q_PER_KV + g], col)
    return col


ATTN_BLOCKS_PER_STEP = 1


def _attn_prompt_kernel(sink_ref, q_ref, kc_ref, kp_ref, vc_ref, vp_ref, o_ref):
    n = pl.program_id(1)
    blk = ATTN_BLOCK
    k_all = jnp.concatenate([kp_ref[...], kc_ref[...]], axis=0).astype(BF16)
    v_all = jnp.concatenate([vp_ref[...], vc_ref[...]], axis=0).astype(BF16)
    rows = Q_PER_KV * blk
    i = lax.broadcasted_iota(jnp.int32, (rows, 2 * blk), 0) % blk
    j = lax.broadcasted_iota(jnp.int32, (rows, 2 * blk), 1)
    d = j - i
    in_window = (d >= 1) & (d <= WINDOW)
    first_valid = (d >= jnp.where(n > 0, 1, jnp.maximum(1, blk - i))) & (d <= WINDOW)
    for sub in range(ATTN_BLOCKS_PER_STEP):
        q = q_ref[sub * blk:(sub + 1) * blk, :]
        k2 = k_all[sub * blk:(sub + 2) * blk]
        v2 = v_all[sub * blk:(sub + 2) * blk]
        valid = first_valid if sub == 0 else in_window
        for h in range(N_KV_HEADS):
            kh = k2[:, h * HEAD_DIM:(h + 1) * HEAD_DIM]
            vh = v2[:, h * HEAD_DIM:(h + 1) * HEAD_DIM]
            qs = jnp.concatenate(
                [q[:, (h * Q_PER_KV + g) * HEAD_DIM:(h * Q_PER_KV + g + 1) * HEAD_DIM]
                 for g in range(Q_PER_KV)], axis=0)
            s = _mm(qs, kh, False, _NT)
            s = jnp.where(valid, s, NEG_INF)
            o = _softmax_pv(s, _sink_column(sink_ref, h, rows, blk), vh, False)
            for g in range(Q_PER_KV):
                c = (h * Q_PER_KV + g) * HEAD_DIM
                o_ref[sub * blk:(sub + 1) * blk, c:c + HEAD_DIM] = (
                    o[g * blk:(g + 1) * blk].astype(o_ref.dtype))


def _attn_prompt(q, k, v, sinks, batch, seq):
    step = ATTN_BLOCKS_PER_STEP * ATTN_BLOCK
    ns = seq // step
    cur = lambda b, n: (b * ns + n, 0)
    prev = lambda b, n: (jnp.maximum((b * ns + n) * ATTN_BLOCKS_PER_STEP - 1, 0), 0)
    return pl.pallas_call(
        _attn_prompt_kernel,
        grid=(batch, ns),
        in_specs=[
            pl.BlockSpec(memory_space=pltpu.SMEM),
            pl.BlockSpec((step, D_Q), cur),
            pl.BlockSpec((step, D_KV), cur),
            pl.BlockSpec((ATTN_BLOCK, D_KV), prev),
            pl.BlockSpec((step, D_KV), cur),
            pl.BlockSpec((ATTN_BLOCK, D_KV), prev),
        ],
        out_specs=pl.BlockSpec((step, D_Q), cur),
        out_shape=jax.ShapeDtypeStruct((batch * seq, D_Q), BF16),
        compiler_params=_params("parallel", "parallel"),
        name="attn_prompt",
    )(sinks, q, k, k, v, v)


SAMPLE_BT = 8


def _attn_sample_kernel(sink_ref, q_ref, kn_ref, vn_ref, kc_ref, vc_ref, o_ref, ko_ref, vo_ref):
    bt = SAMPLE_BT
    w = lax.broadcasted_iota(jnp.int32, (bt, WINDOW, D_KV), 1)

    def shifted(cache_ref, new_ref):
        rolled = pltpu.roll(cache_ref[...], WINDOW - 1, axis=1)
        return jnp.where(w == WINDOW - 1, new_ref[...][:, None, :], rolled)

    k_win = shifted(kc_ref, kn_ref)
    v_win = shifted(vc_ref, vn_ref)
    ko_ref[...] = k_win
    vo_ref[...] = v_win
    k2 = k_win.reshape(bt * WINDOW, D_KV)
    v2 = v_win.reshape(bt * WINDOW, D_KV)
    q = q_ref[...]
    rows = Q_PER_KV * bt
    rb = lax.broadcasted_iota(jnp.int32, (rows, bt * WINDOW), 0) % bt
    cb = lax.broadcasted_iota(jnp.int32, (rows, bt * WINDOW), 1) // WINDOW
    valid = rb == cb
    for h in range(N_KV_HEADS):
        kh = k2[:, h * HEAD_DIM:(h + 1) * HEAD_DIM]
        vh = v2[:, h * HEAD_DIM:(h + 1) * HEAD_DIM]
        qs = jnp.concatenate(
            [q[:, (h * Q_PER_KV + g) * HEAD_DIM:(h * Q_PER_KV + g + 1) * HEAD_DIM]
             for g in range(Q_PER_KV)], axis=0)
        s = _mm(qs, kh, True, _NT)
        s = jnp.where(valid, s, NEG_INF)
        o = _softmax_pv(s, _sink_column(sink_ref, h, rows, bt), vh, True)
        for g in range(Q_PER_KV):
            c = (h * Q_PER_KV + g) * HEAD_DIM
            o_ref[:, c:c + HEAD_DIM] = o[g * bt:(g + 1) * bt].astype(o_ref.dtype)


def _attn_sample(q, k_new, v_new, cache_k, cache_v, sinks):
    nbatch = q.shape[0]
    bt = SAMPLE_BT
    row = lambda i: (i, 0)
    win = lambda i: (i, 0, 0)
    return pl.pallas_call(
        _attn_sample_kernel,
        grid=(nbatch // bt,),
        in_specs=[
            pl.BlockSpec(memory_space=pltpu.SMEM),
            pl.BlockSpec((bt, D_Q), row),
            pl.BlockSpec((bt, D_KV), row),
            pl.BlockSpec((bt, D_KV), row),
            pl.BlockSpec((bt, WINDOW, D_KV), win),
            pl.BlockSpec((bt, WINDOW, D_KV), win),
        ],
        out_specs=[
            pl.BlockSpec((bt, D_Q), row),
            pl.BlockSpec((bt, WINDOW, D_KV), win),
            pl.BlockSpec((bt, WINDOW, D_KV), win),
        ],
        out_shape=(
            jax.ShapeDtypeStruct((nbatch, D_Q), F32),
            jax.ShapeDtypeStruct((nbatch, WINDOW, D_KV), F32),
            jax.ShapeDtypeStruct((nbatch, WINDOW, D_KV), F32),
        ),
        compiler_params=_params("parallel"),
        name="attn_sample",
    )(sinks, q, k_new, v_new, cache_k, cache_v)


def _lru_terms(xc, wcat_ref, ba_ref, bi_ref, lam_ref, precise, block0=0):
    cols = slice(block0 * RNN_BLOCK, block0 * RNN_BLOCK + xc.shape[1])
    xcb = xc if precise else xc.astype(BF16)
    ya, yi = [], []
    for n in range(xc.shape[1] // RNN_BLOCK):
        y = _mm(xcb[:, n * RNN_BLOCK:(n + 1) * RNN_BLOCK], wcat_ref[block0 + n], precise)
        ya.append(y[:, :RNN_BLOCK])
        yi.append(y[:, RNN_BLOCK:])
    r = _sigmoid(jnp.concatenate(ya, axis=-1) + ba_ref[:, cols])
    gate_i = _sigmoid(jnp.concatenate(yi, axis=-1) + bi_ref[:, cols])
    neg_lam = -lam_ref[:, cols]
    softplus = jnp.maximum(neg_lam, 0.0) + jnp.log1p(jnp.exp(-jnp.abs(neg_lam)))
    log_a = (-LRU_C * softplus) * r
    a = jnp.exp(log_a)
    m = 1.0 - a * a
    b = jnp.where(m > 0.0, m * lax.rsqrt(m), 0.0) * (gate_i * xc)
    return a, b


def _rnn_start_sequence(xbuf, hcar):
    xbuf[0:SUBLANES, :] = jnp.zeros((SUBLANES, D_RNN), F32)
    hcar[...] = jnp.zeros((SUBLANES, D_RNN), F32)


def _rnn_scan_terms(x, block0, cw_ref, cb_ref, wcat_ref, ba_ref, bi_ref, lam_ref,
                    xbuf, a_scr, b_scr):
    tt, width = x.shape
    groups = tt // SUBLANES
    cols = slice(block0 * RNN_BLOCK, block0 * RNN_BLOCK + width)
    xbuf[SUBLANES:, cols] = x
    xc = cb_ref[:, cols] + cw_ref[CONV_W - 1:CONV_W, cols] * x
    for j in range(CONV_W - 1):
        s = CONV_W - 1 - j
        xc = xc + cw_ref[j:j + 1, cols] * xbuf[SUBLANES - s:SUBLANES - s + tt, cols]
    xbuf[0:SUBLANES, cols] = x[tt - SUBLANES:, :]
    yield
    a, b = _lru_terms(xc, wcat_ref, ba_ref, bi_ref, lam_ref, False, block0)
    yield
    a = a.reshape(groups, SUBLANES, width)
    b = b.reshape(groups, SUBLANES, width)
    step = lax.broadcasted_iota(jnp.int32, (groups, SUBLANES, width), 1)
    k = 1
    while k < SUBLANES:
        keep = step >= k
        a_sh = jnp.where(keep, pltpu.roll(a, k, axis=1), 1.0)
        b_sh = jnp.where(keep, pltpu.roll(b, k, axis=1), 0.0)
        b = a * b_sh + b
        a = a * a_sh
        k *= 2
        if k < SUBLANES:
            yield
    a_scr[:, :, cols] = a
    b_scr[:, :, cols] = b


def _rnn_scan_finish(gy, o_ref, hl_ref, a_scr, b_scr, h_scr, hcar):
    groups = a_scr.shape[0]
    tt = groups * SUBLANES

    def chain(g, h_in):
        h = a_scr[g] * h_in + b_scr[g]
        h_scr[g] = h
        return jnp.broadcast_to(h[SUBLANES - 1:SUBLANES, :], (SUBLANES, D_RNN))

    h_last = lax.fori_loop(0, groups, chain, hcar[...])
    hcar[...] = h_last
    hl_ref[0] = h_last
    h = h_scr[...].reshape(tt, D_RNN)
    o_ref[...] = (h * gy).astype(o_ref.dtype)


def _rnn_sample_kernel(xr_ref, gy_ref, s0_ref, s1_ref, s2_ref, h_ref, cw_ref, cb_ref,
                       wcat_ref, ba_ref, bi_ref, lam_ref, o_ref, hn_ref):
    x = xr_ref[...]
    xc = (cb_ref[...] + cw_ref[0:1, :] * s0_ref[...] + cw_ref[1:2, :] * s1_ref[...]
          + cw_ref[2:3, :] * s2_ref[...] + cw_ref[3:4, :] * x)
    a, b = _lru_terms(xc, wcat_ref, ba_ref, bi_ref, lam_ref, True)
    h = a * h_ref[...] + b
    hn_ref[...] = h
    o_ref[...] = (h * gy_ref[...].astype(F32)).astype(o_ref.dtype)


def _rnn_sample(xr, gy, s0, s1, s2, h_prev, conv_w, conv_b, wcat, b_a, b_i, lam):
    n = xr.shape[0]
    act = _full((n, D_RNN))
    return pl.pallas_call(
        _rnn_sample_kernel,
        grid=(1,),
        in_specs=[act, act, act, act, act, act,
                  _full((CONV_W, D_RNN)), _full((1, D_RNN)),
                  _full((N_RNN_BLOCKS, RNN_BLOCK, 2 * RNN_BLOCK)),
                  _full((1, D_RNN)), _full((1, D_RNN)), _full((1, D_RNN))],
        out_specs=[act, act],
        out_shape=(jax.ShapeDtypeStruct((n, D_RNN), F32),
                   jax.ShapeDtypeStruct((n, D_RNN), F32)),
        compiler_params=_params("arbitrary"),
        name="rnn_sample",
    )(xr, gy, s0, s1, s2, h_prev, conv_w, conv_b, wcat, b_a, b_i, lam)


def _merge_kernel(x_ref, at_ref, rn_ref, sga_ref, sgr_ref, wa_ref, wr_ref, wo_ref, g_ref,
                  wrt_ref, brt_ref, tri_ref, *rest, precise, tiles_per_seg, cast_experts):
    if cast_experts:
        wg_ref, wu_ref, wd_ref, *rest = rest
        x2_ref, xtm_ref, rec_ref, rect_ref, cnt_ref, wg_o, wu_o, wd_o, cnt_scr = rest
    else:
        x2_ref, xtm_ref, rec_ref, rect_ref, cnt_ref, cnt_scr = rest

    @pl.when(pl.program_id(0) % tiles_per_seg == 0)
    def _():
        cnt_scr[...] = jnp.zeros_like(cnt_scr)

    tm = x_ref.shape[0]
    parts = MERGE_PARTS if tm % (MERGE_PARTS * LANES) == 0 else 1
    part = tm // parts
    counts = cnt_scr[...]
    for i in range(parts):
        rows = slice(i * part, (i + 1) * part)
        counts = _merge_rows(rows, counts, x_ref, at_ref, rn_ref, sga_ref, sgr_ref, wa_ref, wr_ref,
                             wo_ref, g_ref, wrt_ref, brt_ref, tri_ref, x2_ref, xtm_ref, rec_ref,
                             rect_ref, precise)
    cnt_scr[...] = counts
    cnt_ref[0] = counts
    if cast_experts:
        wg_o[...] = wg_ref[...].astype(BF16)
        wu_o[...] = wu_ref[...].astype(BF16)
        wd_o[...] = wd_ref[...].astype(BF16)


MERGE_PARTS = 2


def _merge_rows(rows, counts, x_ref, at_ref, rn_ref, sga_ref, sgr_ref, wa_ref, wr_ref, wo_ref, g_ref,
                wrt_ref, brt_ref, tri_ref, x2_ref, xtm_ref, rec_ref, rect_ref, precise):
    ya = _mm(at_ref[rows, :], wa_ref[...], precise)
    yr = _mm(rn_ref[rows, :], wr_ref[...], precise)
    merged = sga_ref[rows, :].astype(F32) * ya + sgr_ref[rows, :].astype(F32) * yr
    x2 = x_ref[rows, :] + _mm(merged, wo_ref[...], precise)
    x2_ref[rows, :] = x2
    inv = lax.rsqrt(jnp.mean(x2 * x2, axis=-1, keepdims=True) + EPS)
    xn = x2 * inv * g_ref[...]

    logits = _mm(xn, wrt_ref[...], precise) + brt_ref[...]
    tm = logits.shape[0]
    lane_i = lax.broadcasted_iota(jnp.int32, (tm, LANES), 1)
    lane = lane_i.astype(F32)
    big = float(LANES)
    is_grp = (lane_i >= N_EXPERTS) & (lane_i < N_EXPERTS + N_GROUPS)
    gl = jnp.where(is_grp, logits, NEG_INF)
    gmax = jnp.max(gl, axis=-1, keepdims=True)
    g_idx = jnp.min(jnp.where(gl == gmax, lane, big), axis=-1, keepdims=True) - N_EXPERTS
    p_g = 1.0 / jnp.sum(jnp.exp(gl - gmax), axis=-1, keepdims=True)
    in_grp = (lane_i // EXPERTS_PER_GROUP).astype(F32) == g_idx
    el = jnp.where(in_grp, logits, NEG_INF)
    v1 = jnp.max(el, axis=-1, keepdims=True)
    i1 = jnp.min(jnp.where(el == v1, lane, big), axis=-1, keepdims=True)
    el2 = jnp.where(lane == i1, NEG_INF, el)
    v2 = jnp.max(el2, axis=-1, keepdims=True)
    i2 = jnp.min(jnp.where(el2 == v2, lane, big), axis=-1, keepdims=True)
    e2 = jnp.exp(v2 - v1)
    w1 = p_g / (1.0 + e2)
    w2 = p_g * e2 / (1.0 + e2)

    hit = jnp.where(lane == i1, 1.0, jnp.where(lane == i2, 1.0, 0.0))
    before = (jnp.dot(tri_ref[0:tm, 0:tm], hit.astype(BF16), preferred_element_type=F32)
              + counts[0:1, :])
    r1 = jnp.sum(jnp.where(lane == i1, before, 0.0), axis=-1, keepdims=True)
    r2 = jnp.sum(jnp.where(lane == i2, before, 0.0), axis=-1, keepdims=True)
    rec = jnp.where(lane == REC_W2, w2, 0.0)
    for field, val in ((REC_W1, w1), (REC_R2, r2), (REC_R1, r1), (REC_E2, i2), (REC_E1, i1)):
        rec = jnp.where(lane == field, val, rec)
    rec_ref[rows, :] = rec
    rect_ref[:, rows] = rec.T

    for c in range(TOKEN_ROWS):
        xtm_ref[pl.ds(rows.start * TOKEN_ROWS + c, tm, stride=TOKEN_ROWS), :] = (
            xn[:, c * LANES:(c + 1) * LANES])
    return counts + jnp.sum(hit, axis=0, keepdims=True)


REC_E1, REC_E2, REC_R1, REC_R2, REC_W1, REC_W2 = range(6)
TOKEN_ROWS = D_MODEL // LANES


def _merge(x, attn, rnn, sga, sgr, wa, wr, wo, g2, w_route, b_route, tri, tm, seg, precise,
           expert_weights=None):
    n = x.shape[0]
    steps = n // tm
    tiles_per_seg = seg // tm
    row = lambda i: (i, 0)
    in_specs = [
        pl.BlockSpec((tm, D_MODEL), row),
        pl.BlockSpec((tm, D_Q), row),
        pl.BlockSpec((tm, D_RNN), row),
        pl.BlockSpec((tm, D_MODEL), row),
        pl.BlockSpec((tm, D_MODEL), row),
        _full((D_Q, D_MODEL)),
        _full((D_RNN, D_MODEL)),
        _full((D_MODEL, D_MODEL)),
        _full((1, D_MODEL)),
        _full((D_MODEL, LANES)),
        _full((1, LANES)),
        _full((tm, tm)),
    ]
    out_specs = [
        pl.BlockSpec((tm, D_MODEL), row),
        pl.BlockSpec((tm * TOKEN_ROWS, LANES), row),
        pl.BlockSpec((tm, LANES), row),
        pl.BlockSpec((LANES, tm), lambda i: (0, i)),
        pl.BlockSpec((1, SUBLANES, LANES), lambda i: (i // tiles_per_seg, 0, 0)),
    ]
    out_shape = [
        jax.ShapeDtypeStruct((n, D_MODEL), F32),
        jax.ShapeDtypeStruct((n * TOKEN_ROWS, LANES), F32),
        jax.ShapeDtypeStruct((n, LANES), F32),
        jax.ShapeDtypeStruct((LANES, n), F32),
        jax.ShapeDtypeStruct((n // seg, SUBLANES, LANES), F32),
    ]
    args = [x, attn, rnn, sga, sgr, wa, wr, wo, g2, w_route, b_route, tri]
    if expert_weights is not None:
        assert steps == N_EXPERTS
        for w in expert_weights:
            spec = pl.BlockSpec((1,) + w.shape[1:], lambda i: (i, 0, 0))
            in_specs.append(spec)
            out_specs.append(spec)
            out_shape.append(jax.ShapeDtypeStruct(w.shape, BF16))
            args.append(w)
    return pl.pallas_call(
        functools.partial(_merge_kernel, precise=precise, tiles_per_seg=tiles_per_seg,
                          cast_experts=expert_weights is not None),
        grid=(steps,),
        in_specs=in_specs,
        out_specs=out_specs,
        out_shape=out_shape,
        scratch_shapes=[pltpu.VMEM((SUBLANES, LANES), F32)],
        compiler_params=_params("arbitrary"),
        name="merge",
    )(*args)


MOE_CHUNK = 256
MOE_TAIL = 128
MOE_VMEM_LIMIT = 60 * 1024 * 1024


def _seg_rows(seg):
    return 2 * seg + N_EXPERTS * SUBLANES + MOE_CHUNK


def _token_rows(i):
    return pl.ds(pl.multiple_of(i * TOKEN_ROWS, TOKEN_ROWS), TOKEN_ROWS)


def _sorted_rows(first_row):
    return pl.ds(pl.multiple_of(first_row, TOKEN_ROWS), TOKEN_ROWS)


def _moe_kernel(off_ref, nfull_ref, rem_ref, slot_ref, xtm_ref, wg_ref, wu_ref, wd_ref,
                x2_ref, rec_ref, o_ref, buf, g1, g2, *, seg, td, tc):
    s = pl.program_id(0)
    p = pl.program_id(1)
    n_disp = seg // td

    @pl.when((s == 0) & (p == 0))
    def _():
        buf[...] = jnp.zeros_like(buf)

    @pl.when(p < n_disp)
    def _():
        def dispatch(g, carry):
            for j in range(SUBLANES):
                t = g * SUBLANES + j
                row = xtm_ref[_token_rows(t), :]
                for k in range(2):
                    buf[_sorted_rows(slot_ref[0, 0, k * seg + p * td + t]), :] = row
            return carry

        lax.fori_loop(0, td // SUBLANES, dispatch, 0)

    def run_chunk(row0, rows, valid):
        r0 = pl.multiple_of(row0 * TOKEN_ROWS, SUBLANES * TOKEN_ROWS)
        xf = [buf[pl.ds(r0 + j, rows, stride=TOKEN_ROWS), :] for j in range(TOKEN_ROWS)]
        x = jnp.concatenate(xf, axis=-1).astype(BF16)
        hg = jnp.dot(x, wg_ref[0], preferred_element_type=F32)
        hu = jnp.dot(x, wu_ref[0], preferred_element_type=F32)
        h = (hg * _sigmoid(hg)) * hu
        y = jnp.dot(h.astype(BF16), wd_ref[0], preferred_element_type=F32)
        if valid is not None:
            mine = lax.broadcasted_iota(jnp.int32, (rows, LANES), 0) < valid
        for j in range(TOKEN_ROWS):
            yj = y[:, j * LANES:(j + 1) * LANES]
            if valid is not None:
                yj = jnp.where(mine, yj, xf[j])
            buf[pl.ds(r0 + j, rows, stride=TOKEN_ROWS), :] = yj

    @pl.when((p >= n_disp) & (p < n_disp + N_EXPERTS))
    def _():
        idx = s * N_EXPERTS + p - n_disp
        base = off_ref[idx]
        n_full = nfull_ref[idx]
        rem = rem_ref[idx]

        def chunk(c, carry):
            run_chunk(base + c * MOE_CHUNK, MOE_CHUNK, None)
            return carry

        lax.fori_loop(0, n_full, chunk, 0)
        last = base + n_full * MOE_CHUNK
        for units in range(1, MOE_CHUNK // MOE_TAIL + 2):
            @pl.when((rem > (units - 1) * MOE_TAIL) & (rem <= units * MOE_TAIL))
            def _():
                run_chunk(last, units * MOE_TAIL, rem)

    @pl.when(p >= n_disp + N_EXPERTS)
    def _():
        t0 = (p - n_disp - N_EXPERTS) * tc

        def gather(g, carry):
            for j in range(SUBLANES):
                t = g * SUBLANES + j
                g1[_token_rows(t), :] = buf[_sorted_rows(slot_ref[0, 0, t0 + t]), :]
                g2[_token_rows(t), :] = buf[_sorted_rows(slot_ref[0, 0, seg + t0 + t]), :]
            return carry

        lax.fori_loop(0, tc // SUBLANES, gather, 0)
        rec = rec_ref[...]
        lane = lax.broadcasted_iota(jnp.int32, rec.shape, 1)
        w1 = jnp.sum(jnp.where(lane == REC_W1, rec, 0.0), axis=-1, keepdims=True)
        w2 = jnp.sum(jnp.where(lane == REC_W2, rec, 0.0), axis=-1, keepdims=True)
        for j in range(TOKEN_ROWS):
            cols = slice(j * LANES, (j + 1) * LANES)
            o_ref[:, cols] = (x2_ref[:, cols] + w1 * g1[pl.ds(j, tc, stride=TOKEN_ROWS), :]
                              + w2 * g2[pl.ds(j, tc, stride=TOKEN_ROWS), :])


def _moe(off, nfull, rem, slot, xtm, wg, wu, wd, x2, rec, seg, td, tc):
    n = x2.shape[0]
    n_seg = n // seg
    n_disp, n_comb = seg // td, seg // tc
    rows = _seg_rows(seg) * TOKEN_ROWS
    disp_tile = lambda s, p, *_: (s * n_disp + jnp.minimum(p, n_disp - 1), 0)
    expert = lambda s, p, *_: (jnp.clip(p - n_disp, 0, N_EXPERTS - 1), 0, 0)
    comb_tile = lambda s, p, *_: (s * n_comb + jnp.clip(p - n_disp - N_EXPERTS, 0, n_comb - 1), 0)
    grid_spec = pltpu.PrefetchScalarGridSpec(
        num_scalar_prefetch=3,
        grid=(n_seg, n_disp + N_EXPERTS + n_comb),
        in_specs=[
            pl.BlockSpec((1, 1, 2 * seg), lambda s, p, *_: (s, 0, 0), memory_space=pltpu.SMEM),
            pl.BlockSpec((td * TOKEN_ROWS, LANES), disp_tile),
            pl.BlockSpec((1, D_MODEL, D_EXPERT), expert),
            pl.BlockSpec((1, D_MODEL, D_EXPERT), expert),
            pl.BlockSpec((1, D_EXPERT, D_MODEL), expert),
            pl.BlockSpec((tc, D_MODEL), comb_tile),
            pl.BlockSpec((tc, LANES), comb_tile),
        ],
        out_specs=pl.BlockSpec((tc, D_MODEL), comb_tile),
        scratch_shapes=[pltpu.VMEM((rows, LANES), F32),
                        pltpu.VMEM((tc * TOKEN_ROWS, LANES), F32),
                        pltpu.VMEM((tc * TOKEN_ROWS, LANES), F32)],
    )
    return pl.pallas_call(
        functools.partial(_moe_kernel, seg=seg, td=td, tc=tc),
        grid_spec=grid_spec,
        out_shape=jax.ShapeDtypeStruct((n, D_MODEL), F32),
        compiler_params=pltpu.CompilerParams(
            dimension_semantics=("arbitrary", "arbitrary"), vmem_limit_bytes=MOE_VMEM_LIMIT),
        name="moe",
    )(off, nfull, rem, slot, xtm, wg, wu, wd, x2, rec)


def _plan(rect, cnt, seg):
    n = rect.shape[1]
    expert = rect[REC_E1:REC_E2 + 1].astype(jnp.int32)
    rank = rect[REC_R1:REC_R2 + 1].astype(jnp.int32)
    counts = cnt[:, 0, :N_EXPERTS].astype(jnp.int32)
    padded = (counts + SUBLANES - 1) // SUBLANES * SUBLANES
    off = jnp.cumsum(padded, axis=1) - padded
    n_full = counts // MOE_CHUNK
    rem = counts - n_full * MOE_CHUNK
    join = (rem > 0) & (rem <= MOE_TAIL) & (n_full > 0)
    n_full = n_full - join
    rem = rem + join * MOE_CHUNK
    off_tok = jnp.repeat(off.T, seg, axis=1)
    hit = expert[:, None, :] == jnp.arange(N_EXPERTS, dtype=jnp.int32)[None, :, None]
    slot = (rank + jnp.sum(jnp.where(hit, off_tok[None], 0), axis=1)) * TOKEN_ROWS
    slot = slot.reshape(2, n // seg, seg).transpose(1, 0, 2).reshape(n // seg, 1, 2 * seg)
    return off.reshape(-1), n_full.reshape(-1), rem.reshape(-1), slot


def _rope_tables(pos):
    half = HEAD_DIM // 2
    inv_freq = ROPE_THETA ** (-jnp.arange(half, dtype=F32) / half)
    ang = pos[:, None] * inv_freq[None, :]
    cos = jnp.cos(ang)
    sin = jnp.sin(ang)
    reps = LANES // HEAD_DIM
    cos_t = jnp.tile(jnp.concatenate([cos, cos], axis=-1), (1, reps))
    sin_t = jnp.tile(jnp.concatenate([-sin, sin], axis=-1), (1, reps))
    return cos_t, sin_t


def kernel(x_prompt, x_sample, cache_k_win, cache_v_win, state_conv, state_lru_h, attn_norm_g, w_in, q_norm_g, k_norm_g, attn_sinks, conv_w, conv_b, w_lru_a, b_lru_a, w_lru_i, b_lru_i, lru_lambda, w_br_attn, w_br_rnn, w_out, ffn_norm_g, w_route_group, b_route_group, w_route_expert, b_route_expert, w_exp_gate, w_exp_up, w_exp_down):
    batch, seq, _ = x_prompt.shape
    dec_batch, dec_seq, _ = x_sample.shape
    depth = w_in.shape[0]
    assert depth == 1 and dec_seq == 1
    l = 0

    w_in_f = w_in[l]
    qkg = jnp.concatenate([jnp.tile(q_norm_g[l], N_Q_HEADS), jnp.tile(k_norm_g[l], N_KV_HEADS)])[None, :]
    wcat_f = jnp.concatenate([w_lru_a[l], w_lru_i[l]], axis=-1)
    wa_f, wr_f, wo_f = w_br_attn[l], w_br_rnn[l], w_out[l]
    w_route_f = jnp.concatenate(
        [w_route_expert[l], w_route_group[l],
         jnp.zeros((D_MODEL, LANES - N_EXPERTS - N_GROUPS), F32)], axis=-1)
    w_in_b, wcat = w_in_f.astype(BF16), wcat_f.astype(BF16)
    wa_b, wr_b, wo_b, w_route = (w.astype(BF16) for w in (wa_f, wr_f, wo_f, w_route_f))
    b_route = jnp.concatenate(
        [b_route_expert[l], b_route_group[l], jnp.zeros((LANES - N_EXPERTS - N_GROUPS,), F32)])[None, :]
    experts_f = (w_exp_gate[l], w_exp_up[l], w_exp_down[l])
    g1 = attn_norm_g[l][None, :]
    g2 = ffn_norm_g[l][None, :]
    cw, cb = conv_w[l], conv_b[l][None, :]
    b_a, b_i, lam = b_lru_a[l][None, :], b_lru_i[l][None, :], lru_lambda[l][None, :]
    sinks = attn_sinks[l]

    def tail(x, attn, rnn, sga, sgr, tm, seg, precise, experts_b=None):
        wa, wr, wo, wrt = (wa_f, wr_f, wo_f, w_route_f) if precise else (wa_b, wr_b, wo_b, w_route)
        tri = jnp.tril(jnp.ones((tm, tm), BF16), -1)
        outs = _merge(x, attn, rnn, sga, sgr, wa, wr, wo, g2, wrt, b_route, tri, tm, seg, precise,
                      expert_weights=None if experts_b else experts_f)
        x2, xtm, rec, rect, cnt = outs[:5]
        wg, wu, wd = experts_b or outs[5:]
        off, nfull, rem, slot = _plan(rect, cnt, seg)
        y = _moe(off, nfull, rem, slot, xtm, wg, wu, wd, x2, rec, seg, tm, min(tm, 256))
        return y, (wg, wu, wd)

    xp = x_prompt.reshape(batch * seq, D_MODEL)
    cos_p, sin_p = _rope_tables(jnp.arange(seq, dtype=F32))
    q, k, v, rnn, sga, sgr, h_last, conv_tail = _proj(
        xp, g1, w_in_b, cos_p, sin_p, qkg, batch, seq, 512, False,
        rnn_weights=(cw, cb, wcat, b_a, b_i, lam))
    attn = _attn_prompt(q, k, v, sinks, batch, seq)
    y_prompt, experts_b = tail(xp, attn, rnn, sga, sgr, 512, 4096, False)
    y_prompt = y_prompt.reshape(batch, seq, D_MODEL)

    def last_rows(a, rows):
        return a.reshape(batch, seq, a.shape[-1])[:, seq - rows:]

    k_win_p = last_rows(k, WINDOW).reshape(1, batch, WINDOW, N_KV_HEADS, HEAD_DIM)
    v_win_p = last_rows(v, WINDOW).reshape(1, batch, WINDOW, N_KV_HEADS, HEAD_DIM)
    conv_p = conv_tail[None, :, SUBLANES - (CONV_W - 1):, :]
    h_p = h_last[None, :, 0, :]

    xs = x_sample.reshape(dec_batch, D_MODEL)
    cos_s, sin_s = _rope_tables(jnp.full((dec_batch,), PAST_LEN, F32))
    qs, ks, vs, xrs, gys, sgas, sgrs = _proj(xs, g1, w_in_f, cos_s, sin_s, qkg, 1, dec_batch,
                                             dec_batch, True)
    ck = cache_k_win[l].reshape(dec_batch, WINDOW, D_KV)
    cv = cache_v_win[l].reshape(dec_batch, WINDOW, D_KV)
    attn_s, k_win_s, v_win_s = _attn_sample(qs, ks, vs, ck, cv, sinks)
    sc = state_conv[l]
    rnn_s, h_s = _rnn_sample(xrs, gys, sc[:, 0], sc[:, 1], sc[:, 2], state_lru_h[l],
                             cw, cb, wcat_f, b_a, b_i, lam)
    y_sample, _ = tail(xs, attn_s, rnn_s, sgas, sgrs, dec_batch, dec_batch, True, experts_b)
    y_sample = y_sample.reshape(dec_batch, 1, D_MODEL)
    conv_s = jnp.stack([sc[:, 1], sc[:, 2], xrs], axis=1)[None]

    return (y_prompt, y_sample, k_win_p, v_win_p, conv_p, h_p,
            k_win_s.reshape(1, dec_batch, WINDOW, N_KV_HEADS, HEAD_DIM),
            v_win_s.reshape(1, dec_batch, WINDOW, N_KV_HEADS, HEAD_DIM),
            conv_s, h_s[None])
```

```python
import functools

import jax
import jax.numpy as jnp
from jax import lax
from jax.experimental import pallas as pl
from jax.experimental.pallas import tpu as pltpu

D_MODEL = 1024
HEAD_DIM = 64
N_Q_HEADS = 8
N_KV_HEADS = 2
Q_PER_KV = N_Q_HEADS // N_KV_HEADS
WINDOW = 128
ATTN_BLOCK = 128
ROPE_THETA = 10000.0
SCALE = HEAD_DIM ** -0.5
NEG_INF = -1e30
D_RNN = 1280
N_RNN_BLOCKS = 10
RNN_BLOCK = D_RNN // N_RNN_BLOCKS
CONV_W = 4
LRU_C = 8.0
N_GROUPS = 4
EXPERTS_PER_GROUP = 8
N_EXPERTS = N_GROUPS * EXPERTS_PER_GROUP
D_EXPERT = 256
PAST_LEN = 16384
EPS = 1e-6
D_Q = N_Q_HEADS * HEAD_DIM
D_KV = N_KV_HEADS * HEAD_DIM
D_IN = D_Q + 2 * D_KV + 2 * D_RNN + 2 * D_MODEL
OFF_K = D_Q
OFF_V = OFF_K + D_KV
OFF_XR = OFF_V + D_KV
OFF_YR = OFF_XR + D_RNN
OFF_GA = OFF_YR + D_RNN
OFF_GR = OFF_GA + D_MODEL

LANES = 128
SUBLANES = 8
VMEM_LIMIT = 56 * 1024 * 1024

F32 = jnp.float32
BF16 = jnp.bfloat16


def _params(*sem):
    return pltpu.CompilerParams(dimension_semantics=sem, vmem_limit_bytes=VMEM_LIMIT)


def _sigmoid(x):
    return 1.0 / (1.0 + jnp.exp(-x))


def _gelu_tanh(x):
    c = 0.7978845608028654
    return 0.5 * x * (1.0 + jnp.tanh(c * (x + 0.044715 * (x * x * x))))


def _full(shape, single_buffer=False):
    index_map = lambda *_: (0,) * len(shape)
    if single_buffer:
        return pl.BlockSpec(shape, index_map, pipeline_mode=pl.Buffered(1))
    return pl.BlockSpec(shape, index_map)


def _mm(a, b, precise, dims=None):
    if precise:
        a, b, prec = a.astype(F32), b.astype(F32), lax.Precision.HIGHEST
    else:
        a, b, prec = a.astype(BF16), b.astype(BF16), None
    if dims is None:
        return jnp.dot(a, b, preferred_element_type=F32, precision=prec)
    return lax.dot_general(a, b, dims, preferred_element_type=F32, precision=prec)


_NT = (((1,), (1,)), ((), ()))


def _proj_kernel(x_ref, g_ref, w_ref, cos_ref, sin_ref, qkg_ref, *rest, precise, fuse_rnn):
    if fuse_rnn:
        rnn_w = rest[:6]
        q_ref, k_ref, v_ref, rnn_ref, sga_ref, sgr_ref, hl_ref, ct_ref = rest[6:14]
        xbuf, a_scr, b_scr, h_scr, hcar = rest[14:]
        pl.when(pl.program_id(1) == 0)(functools.partial(_rnn_start_sequence, xbuf, hcar))
    else:
        q_ref, k_ref, v_ref, xr_ref, gy_ref, sga_ref, sgr_ref, wb_ref = rest
        wb_ref[...] = w_ref[...].astype(BF16)
    x = x_ref[...]
    inv = lax.rsqrt(jnp.mean(x * x, axis=-1, keepdims=True) + EPS)
    xn = x * inv * g_ref[...]
    if not precise:
        xn = xn.astype(BF16)

    def proj(lo, hi):
        return _mm(xn, w_ref[:, lo:hi], precise)

    def qk_heads():
        qk = proj(0, OFF_V)
        tm = qk.shape[0]
        lane = lax.broadcasted_iota(jnp.int32, (tm, LANES), 1)
        lo_head = lane < HEAD_DIM
        first_half = (lane % HEAD_DIM) < (HEAD_DIM // 2)
        cos = cos_ref[...]
        sin = sin_ref[...]
        for g in range(OFF_V // LANES):
            seg = qk[:, g * LANES:(g + 1) * LANES]
            sq = seg * seg
            s_lo = jnp.sum(jnp.where(lo_head, sq, 0.0), axis=-1, keepdims=True)
            s_hi = jnp.sum(jnp.where(lo_head, 0.0, sq), axis=-1, keepdims=True)
            ms = jnp.where(lo_head, s_lo, s_hi) * (1.0 / HEAD_DIM)
            normed = seg * lax.rsqrt(ms + EPS) * qkg_ref[:, g * LANES:(g + 1) * LANES]
            partner = jnp.where(first_half,
                                pltpu.roll(normed, LANES - HEAD_DIM // 2, axis=1),
                                pltpu.roll(normed, HEAD_DIM // 2, axis=1))
            roped = normed * cos + partner * sin
            if g < D_Q // LANES:
                q_ref[:, g * LANES:(g + 1) * LANES] = (roped * SCALE).astype(q_ref.dtype)
            else:
                k_ref[...] = roped

    def values():
        v_ref[...] = proj(OFF_V, OFF_XR)

    def gate(out_ref, off, c0, c1, r0=0, r1=x.shape[0]):
        y = _mm(xn[r0:r1], w_ref[:, off + c0:off + c1], precise)
        out_ref[r0:r1, c0:c1] = _sigmoid(y).astype(out_ref.dtype)

    if not fuse_rnn:
        xr_ref[...] = proj(OFF_XR, OFF_YR)
        gy_ref[...] = _gelu_tanh(proj(OFF_YR, OFF_GA)).astype(gy_ref.dtype)
        qk_heads()
        values()
        gate(sga_ref, OFF_GA, 0, D_MODEL)
        gate(sgr_ref, OFF_GR, 0, D_MODEL)
        return

    slab = RNN_SLAB_BLOCKS * RNN_BLOCK
    half = x.shape[0] // 2
    others = [qk_heads, values]
    for out_ref, off in ((sga_ref, OFF_GA), (sgr_ref, OFF_GR)):
        others += [functools.partial(gate, out_ref, off, c, c + slab, r, r + half)
                   for c in range(0, D_MODEL, slab) for r in (0, half)]
    others = iter(others)
    gy = []
    for i in range(D_RNN // slab):
        c0 = i * slab
        xr = proj(OFF_XR + c0, OFF_XR + c0 + slab)
        gy.append(_gelu_tanh(proj(OFF_YR + c0, OFF_YR + c0 + slab)))
        ct_ref[0, :, c0:c0 + slab] = xr[xr.shape[0] - SUBLANES:, :]
        for _ in _rnn_scan_terms(xr, i * RNN_SLAB_BLOCKS, *rnn_w, xbuf, a_scr, b_scr):
            next(others, lambda: None)()
    for other in others:
        other()
    _rnn_scan_finish(jnp.concatenate(gy, axis=-1), rnn_ref, hl_ref, a_scr, b_scr, h_scr, hcar)


RNN_SLAB_BLOCKS = 2


def _proj(x, g, w_in, cos_t, sin_t, qkg, batch, seq, tm, precise, rnn_weights=None):
    n = batch * seq
    nt = seq // tm
    row = lambda b, t: (b * nt + t, 0)
    per_seq = lambda b, t: (b, 0, 0)
    act = F32 if precise else BF16
    rows_out = lambda width, dtype: (jax.ShapeDtypeStruct((n, width), dtype),
                                     pl.BlockSpec((tm, width), row))
    state_out = (jax.ShapeDtypeStruct((batch, SUBLANES, D_RNN), F32),
                 pl.BlockSpec((1, SUBLANES, D_RNN), per_seq))
    outs = [rows_out(D_Q, act), rows_out(D_KV, F32), rows_out(D_KV, F32)]
    in_specs = [
        pl.BlockSpec((tm, D_MODEL), row),
        _full((1, D_MODEL)),
        _full((D_MODEL, D_IN), single_buffer=True),
        pl.BlockSpec((tm, LANES), lambda b, t: (t, 0)),
        pl.BlockSpec((tm, LANES), lambda b, t: (t, 0)),
        _full((1, OFF_V)),
    ]
    args = [x, g, w_in, cos_t, sin_t, qkg]
    scratch = []
    if rnn_weights is None:
        outs += [rows_out(D_RNN, F32), rows_out(D_RNN, act)]
    else:
        assert not precise
        outs += [rows_out(D_RNN, act)]
        in_specs += [_full(w.shape) for w in rnn_weights]
        args += list(rnn_weights)
        groups = tm // SUBLANES
        scratch = [pltpu.VMEM((tm + SUBLANES, D_RNN), F32)]
        scratch += [pltpu.VMEM((groups, SUBLANES, D_RNN), F32)] * 3
        scratch += [pltpu.VMEM((SUBLANES, D_RNN), F32)]
    outs += [rows_out(D_MODEL, act), rows_out(D_MODEL, act)]
    if rnn_weights is not None:
        outs += [state_out, state_out]
    else:
        assert batch * nt == 1 and w_in.dtype == F32
        outs += [(jax.ShapeDtypeStruct(w_in.shape, BF16), _full(w_in.shape, single_buffer=True))]
    return pl.pallas_call(
        functools.partial(_proj_kernel, precise=precise, fuse_rnn=rnn_weights is not None),
        grid=(batch, nt),
        in_specs=in_specs,
        out_specs=[spec for _, spec in outs],
        out_shape=[shape for shape, _ in outs],
        scratch_shapes=scratch,
        compiler_params=_params("parallel", "arbitrary"),
        name="proj",
    )(*args)


def _softmax_pv(s, sink, v2, precise):
    m = jnp.maximum(jnp.max(s, axis=-1, keepdims=True), sink)
    p = jnp.exp(s - m)
    denom = jnp.sum(p, axis=-1, keepdims=True) + jnp.exp(sink - m)
    return _mm(p, v2, precise) * (1.0 / denom)


def _sink_column(sink_ref, h, rows, rows_per_head):
    r = lax.broadcasted_iota(jnp.int32, (rows, 1), 0) // rows_per_head
    col = jnp.full((rows, 1), sink_ref[h * Q_PER_KV], F32)
    for g in range(1, Q_PER_KV):
        col = jnp.where(r == g, sink_ref[h * Q_PER_KV + g], col)
    return col


ATTN_BLOCKS_PER_STEP = 1


def _attn_prompt_kernel(sink_ref, q_ref, kc_ref, kp_ref, vc_ref, vp_ref, o_ref):
    n = pl.program_id(1)
    blk = ATTN_BLOCK
    k_all = jnp.concatenate([kp_ref[...], kc_ref[...]], axis=0).astype(BF16)
    v_all = jnp.concatenate([vp_ref[...], vc_ref[...]], axis=0).astype(BF16)
    rows = Q_PER_KV * blk
    i = lax.broadcasted_iota(jnp.int32, (rows, 2 * blk), 0) % blk
    j = lax.broadcasted_iota(jnp.int32, (rows, 2 * blk), 1)
    d = j - i
    in_window = (d >= 1) & (d <= WINDOW)
    first_valid = (d >= jnp.where(n > 0, 1, jnp.maximum(1, blk - i))) & (d <= WINDOW)
    for sub in range(ATTN_BLOCKS_PER_STEP):
        q = q_ref[sub * blk:(sub + 1) * blk, :]
        k2 = k_all[sub * blk:(sub + 2) * blk]
        v2 = v_all[sub * blk:(sub + 2) * blk]
        valid = first_valid if sub == 0 else in_window
        for h in range(N_KV_HEADS):
            kh = k2[:, h * HEAD_DIM:(h + 1) * HEAD_DIM]
            vh = v2[:, h * HEAD_DIM:(h + 1) * HEAD_DIM]
            qs = jnp.concatenate(
                [q[:, (h * Q_PER_KV + g) * HEAD_DIM:(h * Q_PER_KV + g + 1) * HEAD_DIM]
                 for g in range(Q_PER_KV)], axis=0)
            s = _mm(qs, kh, False, _NT)
            s = jnp.where(valid, s, NEG_INF)
            o = _softmax_pv(s, _sink_column(sink_ref, h, rows, blk), vh, False)
            for g in range(Q_PER_KV):
                c = (h * Q_PER_KV + g) * HEAD_DIM
                o_ref[sub * blk:(sub + 1) * blk, c:c + HEAD_DIM] = (
                    o[g * blk:(g + 1) * blk].astype(o_ref.dtype))


def _attn_prompt(q, k, v, sinks, batch, seq):
    step = ATTN_BLOCKS_PER_STEP * ATTN_BLOCK
    ns = seq // step
    cur = lambda b, n: (b * ns + n, 0)
    prev = lambda b, n: (jnp.maximum((b * ns + n) * ATTN_BLOCKS_PER_STEP - 1, 0), 0)
    return pl.pallas_call(
        _attn_prompt_kernel,
        grid=(batch, ns),
        in_specs=[
            pl.BlockSpec(memory_space=pltpu.SMEM),
            pl.BlockSpec((step, D_Q), cur),
            pl.BlockSpec((step, D_KV), cur),
            pl.BlockSpec((ATTN_BLOCK, D_KV), prev),
            pl.BlockSpec((step, D_KV), cur),
            pl.BlockSpec((ATTN_BLOCK, D_KV), prev),
        ],
        out_specs=pl.BlockSpec((step, D_Q), cur),
        out_shape=jax.ShapeDtypeStruct((batch * seq, D_Q), BF16),
        compiler_params=_params("parallel", "parallel"),
        name="attn_prompt",
    )(sinks, q, k, k, v, v)


SAMPLE_BT = 8


def _attn_sample_kernel(sink_ref, q_ref, kn_ref, vn_ref, kc_ref, vc_ref, o_ref, ko_ref, vo_ref):
    bt = SAMPLE_BT
    w = lax.broadcasted_iota(jnp.int32, (bt, WINDOW, D_KV), 1)

    def shifted(cache_ref, new_ref):
        rolled = pltpu.roll(cache_ref[...], WINDOW - 1, axis=1)
        return jnp.where(w == WINDOW - 1, new_ref[...][:, None, :], rolled)

    k_win = shifted(kc_ref, kn_ref)
    v_win = shifted(vc_ref, vn_ref)
    ko_ref[...] = k_win
    vo_ref[...] = v_win
    k2 = k_win.reshape(bt * WINDOW, D_KV)
    v2 = v_win.reshape(bt * WINDOW, D_KV)
    q = q_ref[...]
    rows = Q_PER_KV * bt
    rb = lax.broadcasted_iota(jnp.int32, (rows, bt * WINDOW), 0) % bt
    cb = lax.broadcasted_iota(jnp.int32, (rows, bt * WINDOW), 1) // WINDOW
    valid = rb == cb
    for h in range(N_KV_HEADS):
        kh = k2[:, h * HEAD_DIM:(h + 1) * HEAD_DIM]
        vh = v2[:, h * HEAD_DIM:(h + 1) * HEAD_DIM]
        qs = jnp.concatenate(
            [q[:, (h * Q_PER_KV + g) * HEAD_DIM:(h * Q_PER_KV + g + 1) * HEAD_DIM]
             for g in range(Q_PER_KV)], axis=0)
        s = _mm(qs, kh, True, _NT)
        s = jnp.where(valid, s, NEG_INF)
        o = _softmax_pv(s, _sink_column(sink_ref, h, rows, bt), vh, True)
        for g in range(Q_PER_KV):
            c = (h * Q_PER_KV + g) * HEAD_DIM
            o_ref[:, c:c + HEAD_DIM] = o[g * bt:(g + 1) * bt].astype(o_ref.dtype)


def _attn_sample(q, k_new, v_new, cache_k, cache_v, sinks):
    nbatch = q.shape[0]
    bt = SAMPLE_BT
    row = lambda i: (i, 0)
    win = lambda i: (i, 0, 0)
    return pl.pallas_call(
        _attn_sample_kernel,
        grid=(nbatch // bt,),
        in_specs=[
            pl.BlockSpec(memory_space=pltpu.SMEM),
            pl.BlockSpec((bt, D_Q), row),
            pl.BlockSpec((bt, D_KV), row),
            pl.BlockSpec((bt, D_KV), row),
            pl.BlockSpec((bt, WINDOW, D_KV), win),
            pl.BlockSpec((bt, WINDOW, D_KV), win),
        ],
        out_specs=[
            pl.BlockSpec((bt, D_Q), row),
            pl.BlockSpec((bt, WINDOW, D_KV), win),
            pl.BlockSpec((bt, WINDOW, D_KV), win),
        ],
        out_shape=(
            jax.ShapeDtypeStruct((nbatch, D_Q), F32),
            jax.ShapeDtypeStruct((nbatch, WINDOW, D_KV), F32),
            jax.ShapeDtypeStruct((nbatch, WINDOW, D_KV), F32),
        ),
        compiler_params=_params("parallel"),
        name="attn_sample",
    )(sinks, q, k_new, v_new, cache_k, cache_v)


def _lru_terms(xc, wcat_ref, ba_ref, bi_ref, lam_ref, precise, block0=0):
    cols = slice(block0 * RNN_BLOCK, block0 * RNN_BLOCK + xc.shape[1])
    xcb = xc if precise else xc.astype(BF16)
    ya, yi = [], []
    for n in range(xc.shape[1] // RNN_BLOCK):
        y = _mm(xcb[:, n * RNN_BLOCK:(n + 1) * RNN_BLOCK], wcat_ref[block0 + n], precise)
        ya.append(y[:, :RNN_BLOCK])
        yi.append(y[:, RNN_BLOCK:])
    r = _sigmoid(jnp.concatenate(ya, axis=-1) + ba_ref[:, cols])
    gate_i = _sigmoid(jnp.concatenate(yi, axis=-1) + bi_ref[:, cols])
    neg_lam = -lam_ref[:, cols]
    softplus = jnp.maximum(neg_lam, 0.0) + jnp.log1p(jnp.exp(-jnp.abs(neg_lam)))
    log_a = (-LRU_C * softplus) * r
    a = jnp.exp(log_a)
    m = 1.0 - a * a
    b = jnp.where(m > 0.0, m * lax.rsqrt(m), 0.0) * (gate_i * xc)
    return a, b


def _rnn_start_sequence(xbuf, hcar):
    xbuf[0:SUBLANES, :] = jnp.zeros((SUBLANES, D_RNN), F32)
    hcar[...] = jnp.zeros((SUBLANES, D_RNN), F32)


def _rnn_scan_terms(x, block0, cw_ref, cb_ref, wcat_ref, ba_ref, bi_ref, lam_ref,
                    xbuf, a_scr, b_scr):
    tt, width = x.shape
    groups = tt // SUBLANES
    cols = slice(block0 * RNN_BLOCK, block0 * RNN_BLOCK + width)
    xbuf[SUBLANES:, cols] = x
    xc = cb_ref[:, cols] + cw_ref[CONV_W - 1:CONV_W, cols] * x
    for j in range(CONV_W - 1):
        s = CONV_W - 1 - j
        xc = xc + cw_ref[j:j + 1, cols] * xbuf[SUBLANES - s:SUBLANES - s + tt, cols]
    xbuf[0:SUBLANES, cols] = x[tt - SUBLANES:, :]
    yield
    a, b = _lru_terms(xc, wcat_ref, ba_ref, bi_ref, lam_ref, False, block0)
    yield
    a = a.reshape(groups, SUBLANES, width)
    b = b.reshape(groups, SUBLANES, width)
    step = lax.broadcasted_iota(jnp.int32, (groups, SUBLANES, width), 1)
    k = 1
    while k < SUBLANES:
        keep = step >= k
        a_sh = jnp.where(keep, pltpu.roll(a, k, axis=1), 1.0)
        b_sh = jnp.where(keep, pltpu.roll(b, k, axis=1), 0.0)
        b = a * b_sh + b
        a = a * a_sh
        k *= 2
        if k < SUBLANES:
            yield
    a_scr[:, :, cols] = a
    b_scr[:, :, cols] = b


def _rnn_scan_finish(gy, o_ref, hl_ref, a_scr, b_scr, h_scr, hcar):
    groups = a_scr.shape[0]
    tt = groups * SUBLANES

    def chain(g, h_in):
        h = a_scr[g] * h_in + b_scr[g]
        h_scr[g] = h
        return jnp.broadcast_to(h[SUBLANES - 1:SUBLANES, :], (SUBLANES, D_RNN))

    h_last = lax.fori_loop(0, groups, chain, hcar[...])
    hcar[...] = h_last
    hl_ref[0] = h_last
    h = h_scr[...].reshape(tt, D_RNN)
    o_ref[...] = (h * gy).astype(o_ref.dtype)


def _rnn_sample_kernel(xr_ref, gy_ref, s0_ref, s1_ref, s2_ref, h_ref, cw_ref, cb_ref,
                       wcat_ref, ba_ref, bi_ref, lam_ref, o_ref, hn_ref):
    x = xr_ref[...]
    xc = (cb_ref[...] + cw_ref[0:1, :] * s0_ref[...] + cw_ref[1:2, :] * s1_ref[...]
          + cw_ref[2:3, :] * s2_ref[...] + cw_ref[3:4, :] * x)
    a, b = _lru_terms(xc, wcat_ref, ba_ref, bi_ref, lam_ref, True)
    h = a * h_ref[...] + b
    hn_ref[...] = h
    o_ref[...] = (h * gy_ref[...].astype(F32)).astype(o_ref.dtype)


def _rnn_sample(xr, gy, s0, s1, s2, h_prev, conv_w, conv_b, wcat, b_a, b_i, lam):
    n = xr.shape[0]
    act = _full((n, D_RNN))
    return pl.pallas_call(
        _rnn_sample_kernel,
        grid=(1,),
        in_specs=[act, act, act, act, act, act,
                  _full((CONV_W, D_RNN)), _full((1, D_RNN)),
                  _full((N_RNN_BLOCKS, RNN_BLOCK, 2 * RNN_BLOCK)),
                  _full((1, D_RNN)), _full((1, D_RNN)), _full((1, D_RNN))],
        out_specs=[act, act],
        out_shape=(jax.ShapeDtypeStruct((n, D_RNN), F32),
                   jax.ShapeDtypeStruct((n, D_RNN), F32)),
        compiler_params=_params("arbitrary"),
        name="rnn_sample",
    )(xr, gy, s0, s1, s2, h_prev, conv_w, conv_b, wcat, b_a, b_i, lam)


def _merge_kernel(x_ref, at_ref, rn_ref, sga_ref, sgr_ref, wa_ref, wr_ref, wo_ref, g_ref,
                  wrt_ref, brt_ref, tri_ref, *rest, precise, tiles_per_seg, cast_experts):
    if cast_experts:
        wg_ref, wu_ref, wd_ref, *rest = rest
        x2_ref, xtm_ref, rec_ref, rect_ref, cnt_ref, wg_o, wu_o, wd_o, cnt_scr = rest
    else:
        x2_ref, xtm_ref, rec_ref, rect_ref, cnt_ref, cnt_scr = rest

    @pl.when(pl.program_id(0) % tiles_per_seg == 0)
    def _():
        cnt_scr[...] = jnp.zeros_like(cnt_scr)

    tm = x_ref.shape[0]
    parts = MERGE_PARTS if tm % (MERGE_PARTS * LANES) == 0 else 1
    part = tm // parts
    counts = cnt_scr[...]
    for i in range(parts):
        rows = slice(i * part, (i + 1) * part)
        counts = _merge_rows(rows, counts, x_ref, at_ref, rn_ref, sga_ref, sgr_ref, wa_ref, wr_ref,
                             wo_ref, g_ref, wrt_ref, brt_ref, tri_ref, x2_ref, xtm_ref, rec_ref,
                             rect_ref, precise)
    cnt_scr[...] = counts
    cnt_ref[0] = counts
    if cast_experts:
        wg_o[...] = wg_ref[...].astype(BF16)
        wu_o[...] = wu_ref[...].astype(BF16)
        wd_o[...] = wd_ref[...].astype(BF16)


MERGE_PARTS = 2


def _merge_rows(rows, counts, x_ref, at_ref, rn_ref, sga_ref, sgr_ref, wa_ref, wr_ref, wo_ref, g_ref,
                wrt_ref, brt_ref, tri_ref, x2_ref, xtm_ref, rec_ref, rect_ref, precise):
    ya = _mm(at_ref[rows, :], wa_ref[...], precise)
    yr = _mm(rn_ref[rows, :], wr_ref[...], precise)
    merged = sga_ref[rows, :].astype(F32) * ya + sgr_ref[rows, :].astype(F32) * yr
    x2 = x_ref[rows, :] + _mm(merged, wo_ref[...], precise)
    x2_ref[rows, :] = x2
    inv = lax.rsqrt(jnp.mean(x2 * x2, axis=-1, keepdims=True) + EPS)
    xn = x2 * inv * g_ref[...]

    logits = _mm(xn, wrt_ref[...], precise) + brt_ref[...]
    tm = logits.shape[0]
    lane_i = lax.broadcasted_iota(jnp.int32, (tm, LANES), 1)
    lane = lane_i.astype(F32)
    big = float(LANES)
    is_grp = (lane_i >= N_EXPERTS) & (lane_i < N_EXPERTS + N_GROUPS)
    gl = jnp.where(is_grp, logits, NEG_INF)
    gmax = jnp.max(gl, axis=-1, keepdims=True)
    g_idx = jnp.min(jnp.where(gl == gmax, lane, big), axis=-1, keepdims=True) - N_EXPERTS
    p_g = 1.0 / jnp.sum(jnp.exp(gl - gmax), axis=-1, keepdims=True)
    in_grp = (lane_i // EXPERTS_PER_GROUP).astype(F32) == g_idx
    el = jnp.where(in_grp, logits, NEG_INF)
    v1 = jnp.max(el, axis=-1, keepdims=True)
    i1 = jnp.min(jnp.where(el == v1, lane, big), axis=-1, keepdims=True)
    el2 = jnp.where(lane == i1, NEG_INF, el)
    v2 = jnp.max(el2, axis=-1, keepdims=True)
    i2 = jnp.min(jnp.where(el2 == v2, lane, big), axis=-1, keepdims=True)
    e2 = jnp.exp(v2 - v1)
    w1 = p_g / (1.0 + e2)
    w2 = p_g * e2 / (1.0 + e2)

    hit = jnp.where(lane == i1, 1.0, jnp.where(lane == i2, 1.0, 0.0))
    before = (jnp.dot(tri_ref[0:tm, 0:tm], hit.astype(BF16), preferred_element_type=F32)
              + counts[0:1, :])
    r1 = jnp.sum(jnp.where(lane == i1, before, 0.0), axis=-1, keepdims=True)
    r2 = jnp.sum(jnp.where(lane == i2, before, 0.0), axis=-1, keepdims=True)
    rec = jnp.where(lane == REC_W2, w2, 0.0)
    for field, val in ((REC_W1, w1), (REC_R2, r2), (REC_R1, r1), (REC_E2, i2), (REC_E1, i1)):
        rec = jnp.where(lane == field, val, rec)
    rec_ref[rows, :] = rec
    rect_ref[:, rows] = rec.T

    for c in range(TOKEN_ROWS):
        xtm_ref[pl.ds(rows.start * TOKEN_ROWS + c, tm, stride=TOKEN_ROWS), :] = (
            xn[:, c * LANES:(c + 1) * LANES])
    return counts + jnp.sum(hit, axis=0, keepdims=True)


REC_E1, REC_E2, REC_R1, REC_R2, REC_W1, REC_W2 = range(6)
TOKEN_ROWS = D_MODEL // LANES


def _merge(x, attn, rnn, sga, sgr, wa, wr, wo, g2, w_route, b_route, tri, tm, seg, precise,
           expert_weights=None):
    n = x.shape[0]
    steps = n // tm
    tiles_per_seg = seg // tm
    row = lambda i: (i, 0)
    in_specs = [
        pl.BlockSpec((tm, D_MODEL), row),
        pl.BlockSpec((tm, D_Q), row),
        pl.BlockSpec((tm, D_RNN), row),
        pl.BlockSpec((tm, D_MODEL), row),
        pl.BlockSpec((tm, D_MODEL), row),
        _full((D_Q, D_MODEL)),
        _full((D_RNN, D_MODEL)),
        _full((D_MODEL, D_MODEL)),
        _full((1, D_MODEL)),
        _full((D_MODEL, LANES)),
        _full((1, LANES)),
        _full((tm, tm)),
    ]
    out_specs = [
        pl.BlockSpec((tm, D_MODEL), row),
        pl.BlockSpec((tm * TOKEN_ROWS, LANES), row),
        pl.BlockSpec((tm, LANES), row),
        pl.BlockSpec((LANES, tm), lambda i: (0, i)),
        pl.BlockSpec((1, SUBLANES, LANES), lambda i: (i // tiles_per_seg, 0, 0)),
    ]
    out_shape = [
        jax.ShapeDtypeStruct((n, D_MODEL), F32),
        jax.ShapeDtypeStruct((n * TOKEN_ROWS, LANES), F32),
        jax.ShapeDtypeStruct((n, LANES), F32),
        jax.ShapeDtypeStruct((LANES, n), F32),
        jax.ShapeDtypeStruct((n // seg, SUBLANES, LANES), F32),
    ]
    args = [x, attn, rnn, sga, sgr, wa, wr, wo, g2, w_route, b_route, tri]
    if expert_weights is not None:
        assert steps == N_EXPERTS
        for w in expert_weights:
            spec = pl.BlockSpec((1,) + w.shape[1:], lambda i: (i, 0, 0))
            in_specs.append(spec)
            out_specs.append(spec)
            out_shape.append(jax.ShapeDtypeStruct(w.shape, BF16))
            args.append(w)
    return pl.pallas_call(
        functools.partial(_merge_kernel, precise=precise, tiles_per_seg=tiles_per_seg,
                          cast_experts=expert_weights is not None),
        grid=(steps,),
        in_specs=in_specs,
        out_specs=out_specs,
        out_shape=out_shape,
        scratch_shapes=[pltpu.VMEM((SUBLANES, LANES), F32)],
        compiler_params=_params("arbitrary"),
        name="merge",
    )(*args)


MOE_CHUNK = 256
MOE_TAIL = 128
MOE_VMEM_LIMIT = 60 * 1024 * 1024


def _seg_rows(seg):
    return 2 * seg + N_EXPERTS * SUBLANES + MOE_CHUNK


def _token_rows(i):
    return pl.ds(pl.multiple_of(i * TOKEN_ROWS, TOKEN_ROWS), TOKEN_ROWS)


def _sorted_rows(first_row):
    return pl.ds(pl.multiple_of(first_row, TOKEN_ROWS), TOKEN_ROWS)


def _moe_kernel(off_ref, nfull_ref, rem_ref, slot_ref, xtm_ref, wg_ref, wu_ref, wd_ref,
                x2_ref, rec_ref, o_ref, buf, g1, g2, *, seg, td, tc):
    s = pl.program_id(0)
    p = pl.program_id(1)
    n_disp = seg // td

    @pl.when((s == 0) & (p == 0))
    def _():
        buf[...] = jnp.zeros_like(buf)

    @pl.when(p < n_disp)
    def _():
        def dispatch(g, carry):
            for j in range(SUBLANES):
                t = g * SUBLANES + j
                row = xtm_ref[_token_rows(t), :]
                for k in range(2):
                    buf[_sorted_rows(slot_ref[0, 0, k * seg + p * td + t]), :] = row
            return carry

        lax.fori_loop(0, td // SUBLANES, dispatch, 0)

    def run_chunk(row0, rows, valid):
        r0 = pl.multiple_of(row0 * TOKEN_ROWS, SUBLANES * TOKEN_ROWS)
        xf = [buf[pl.ds(r0 + j, rows, stride=TOKEN_ROWS), :] for j in range(TOKEN_ROWS)]
        x = jnp.concatenate(xf, axis=-1).astype(BF16)
        hg = jnp.dot(x, wg_ref[0], preferred_element_type=F32)
        hu = jnp.dot(x, wu_ref[0], preferred_element_type=F32)
        h = (hg * _sigmoid(hg)) * hu
        y = jnp.dot(h.astype(BF16), wd_ref[0], preferred_element_type=F32)
        if valid is not None:
            mine = lax.broadcasted_iota(jnp.int32, (rows, LANES), 0) < valid
        for j in range(TOKEN_ROWS):
            yj = y[:, j * LANES:(j + 1) * LANES]
            if valid is not None:
                yj = jnp.where(mine, yj, xf[j])
            buf[pl.ds(r0 + j, rows, stride=TOKEN_ROWS), :] = yj

    @pl.when((p >= n_disp) & (p < n_disp + N_EXPERTS))
    def _():
        idx = s * N_EXPERTS + p - n_disp
        base = off_ref[idx]
        n_full = nfull_ref[idx]
        rem = rem_ref[idx]

        def chunk(c, carry):
            run_chunk(base + c * MOE_CHUNK, MOE_CHUNK, None)
            return carry

        lax.fori_loop(0, n_full, chunk, 0)
        last = base + n_full * MOE_CHUNK
        for units in range(1, MOE_CHUNK // MOE_TAIL + 2):
            @pl.when((rem > (units - 1) * MOE_TAIL) & (rem <= units * MOE_TAIL))
            def _():
                run_chunk(last, units * MOE_TAIL, rem)

    @pl.when(p >= n_disp + N_EXPERTS)
    def _():
        t0 = (p - n_disp - N_EXPERTS) * tc

        def gather(g, carry):
            for j in range(SUBLANES):
                t = g * SUBLANES + j
                g1[_token_rows(t), :] = buf[_sorted_rows(slot_ref[0, 0, t0 + t]), :]
                g2[_token_rows(t), :] = buf[_sorted_rows(slot_ref[0, 0, seg + t0 + t]), :]
            return carry

        lax.fori_loop(0, tc // SUBLANES, gather, 0)
        rec = rec_ref[...]
        lane = lax.broadcasted_iota(jnp.int32, rec.shape, 1)
        w1 = jnp.broadcast_to(
            jnp.sum(jnp.where(lane == REC_W1, rec, 0.0), axis=-1, keepdims=True), rec.shape)
        w2 = jnp.broadcast_to(
            jnp.sum(jnp.where(lane == REC_W2, rec, 0.0), axis=-1, keepdims=True), rec.shape)
        for j in range(TOKEN_ROWS):
            cols = slice(j * LANES, (j + 1) * LANES)
            o_ref[:, cols] = (x2_ref[:, cols] + w1 * g1[pl.ds(j, tc, stride=TOKEN_ROWS), :]
                              + w2 * g2[pl.ds(j, tc, stride=TOKEN_ROWS), :])


def _moe(off, nfull, rem, slot, xtm, wg, wu, wd, x2, rec, seg, td, tc):
    n = x2.shape[0]
    n_seg = n // seg
    n_disp, n_comb = seg // td, seg // tc
    rows = _seg_rows(seg) * TOKEN_ROWS
    disp_tile = lambda s, p, *_: (s * n_disp + jnp.minimum(p, n_disp - 1), 0)
    expert = lambda s, p, *_: (jnp.clip(p - n_disp, 0, N_EXPERTS - 1), 0, 0)
    comb_tile = lambda s, p, *_: (s * n_comb + jnp.clip(p - n_disp - N_EXPERTS, 0, n_comb - 1), 0)
    grid_spec = pltpu.PrefetchScalarGridSpec(
        num_scalar_prefetch=3,
        grid=(n_seg, n_disp + N_EXPERTS + n_comb),
        in_specs=[
            pl.BlockSpec((1, 1, 2 * seg), lambda s, p, *_: (s, 0, 0), memory_space=pltpu.SMEM),
            pl.BlockSpec((td * TOKEN_ROWS, LANES), disp_tile),
            pl.BlockSpec((1, D_MODEL, D_EXPERT), expert),
            pl.BlockSpec((1, D_MODEL, D_EXPERT), expert),
            pl.BlockSpec((1, D_EXPERT, D_MODEL), expert),
            pl.BlockSpec((tc, D_MODEL), comb_tile),
            pl.BlockSpec((tc, LANES), comb_tile),
        ],
        out_specs=pl.BlockSpec((tc, D_MODEL), comb_tile),
        scratch_shapes=[pltpu.VMEM((rows, LANES), F32),
                        pltpu.VMEM((tc * TOKEN_ROWS, LANES), F32),
                        pltpu.VMEM((tc * TOKEN_ROWS, LANES), F32)],
    )
    return pl.pallas_call(
        functools.partial(_moe_kernel, seg=seg, td=td, tc=tc),
        grid_spec=grid_spec,
        out_shape=jax.ShapeDtypeStruct((n, D_MODEL), F32),
        compiler_params=pltpu.CompilerParams(
            dimension_semantics=("arbitrary", "arbitrary"), vmem_limit_bytes=MOE_VMEM_LIMIT),
        name="moe",
    )(off, nfull, rem, slot, xtm, wg, wu, wd, x2, rec)


def _plan(rect, cnt, seg):
    n = rect.shape[1]
    expert = rect[REC_E1:REC_E2 + 1].astype(jnp.int32)
    rank = rect[REC_R1:REC_R2 + 1].astype(jnp.int32)
    counts = cnt[:, 0, :N_EXPERTS].astype(jnp.int32)
    padded = (counts + SUBLANES - 1) // SUBLANES * SUBLANES
    off = jnp.cumsum(padded, axis=1) - padded
    n_full = counts // MOE_CHUNK
    rem = counts - n_full * MOE_CHUNK
    join = (rem > 0) & (rem <= MOE_TAIL) & (n_full > 0)
    n_full = n_full - join
    rem = rem + join * MOE_CHUNK
    off_tok = jnp.repeat(off.T, seg, axis=1)
    hit = expert[:, None, :] == jnp.arange(N_EXPERTS, dtype=jnp.int32)[None, :, None]
    slot = (rank + jnp.sum(jnp.where(hit, off_tok[None], 0), axis=1)) * TOKEN_ROWS
    slot = slot.reshape(2, n // seg, seg).transpose(1, 0, 2).reshape(n // seg, 1, 2 * seg)
    return off.reshape(-1), n_full.reshape(-1), rem.reshape(-1), slot


def _rope_tables(pos):
    half = HEAD_DIM // 2
    inv_freq = ROPE_THETA ** (-jnp.arange(half, dtype=F32) / half)
    ang = pos[:, None] * inv_freq[None, :]
    cos = jnp.cos(ang)
    sin = jnp.sin(ang)
    reps = LANES // HEAD_DIM
    cos_t = jnp.tile(jnp.concatenate([cos, cos], axis=-1), (1, reps))
    sin_t = jnp.tile(jnp.concatenate([-sin, sin], axis=-1), (1, reps))
    return cos_t, sin_t


def kernel(x_prompt, x_sample, cache_k_win, cache_v_win, state_conv, state_lru_h, attn_norm_g, w_in, q_norm_g, k_norm_g, attn_sinks, conv_w, conv_b, w_lru_a, b_lru_a, w_lru_i, b_lru_i, lru_lambda, w_br_attn, w_br_rnn, w_out, ffn_norm_g, w_route_group, b_route_group, w_route_expert, b_route_expert, w_exp_gate, w_exp_up, w_exp_down):
    batch, seq, _ = x_prompt.shape
    dec_batch, dec_seq, _ = x_sample.shape
    depth = w_in.shape[0]
    assert depth == 1 and dec_seq == 1
    l = 0

    w_in_f = w_in[l]
    qkg = jnp.concatenate([jnp.tile(q_norm_g[l], N_Q_HEADS), jnp.tile(k_norm_g[l], N_KV_HEADS)])[None, :]
    wcat_f = jnp.concatenate([w_lru_a[l], w_lru_i[l]], axis=-1)
    wa_f, wr_f, wo_f = w_br_attn[l], w_br_rnn[l], w_out[l]
    w_route_f = jnp.concatenate(
        [w_route_expert[l], w_route_group[l],
         jnp.zeros((D_MODEL, LANES - N_EXPERTS - N_GROUPS), F32)], axis=-1)
    wcat = wcat_f.astype(BF16)
    wa_b, wr_b, wo_b, w_route = (w.astype(BF16) for w in (wa_f, wr_f, wo_f, w_route_f))
    b_route = jnp.concatenate(
        [b_route_expert[l], b_route_group[l], jnp.zeros((LANES - N_EXPERTS - N_GROUPS,), F32)])[None, :]
    experts_f = (w_exp_gate[l], w_exp_up[l], w_exp_down[l])
    g1 = attn_norm_g[l][None, :]
    g2 = ffn_norm_g[l][None, :]
    cw, cb = conv_w[l], conv_b[l][None, :]
    b_a, b_i, lam = b_lru_a[l][None, :], b_lru_i[l][None, :], lru_lambda[l][None, :]
    sinks = attn_sinks[l]

    def tail(x, attn, rnn, sga, sgr, tm, seg, precise, experts_b=None):
        wa, wr, wo, wrt = (wa_f, wr_f, wo_f, w_route_f) if precise else (wa_b, wr_b, wo_b, w_route)
        tri = jnp.tril(jnp.ones((tm, tm), BF16), -1)
        outs = _merge(x, attn, rnn, sga, sgr, wa, wr, wo, g2, wrt, b_route, tri, tm, seg, precise,
                      expert_weights=None if experts_b else experts_f)
        x2, xtm, rec, rect, cnt = outs[:5]
        wg, wu, wd = experts_b or outs[5:]
        off, nfull, rem, slot = _plan(rect, cnt, seg)
        y = _moe(off, nfull, rem, slot, xtm, wg, wu, wd, x2, rec, seg, tm, min(tm, 256))
        return y, (wg, wu, wd)

    xs = x_sample.reshape(dec_batch, D_MODEL)
    cos_s, sin_s = _rope_tables(jnp.full((dec_batch,), PAST_LEN, F32))
    qs, ks, vs, xrs, gys, sgas, sgrs, w_in_b = _proj(xs, g1, w_in_f, cos_s, sin_s, qkg, 1, dec_batch,
                                                     dec_batch, True)

    xp = x_prompt.reshape(batch * seq, D_MODEL)
    cos_p, sin_p = _rope_tables(jnp.arange(seq, dtype=F32))
    q, k, v, rnn, sga, sgr, h_last, conv_tail = _proj(
        xp, g1, w_in_b, cos_p, sin_p, qkg, batch, seq, 512, False,
        rnn_weights=(cw, cb, wcat, b_a, b_i, lam))
    attn = _attn_prompt(q, k, v, sinks, batch, seq)
    y_prompt, experts_b = tail(xp, attn, rnn, sga, sgr, 512, 4096, False)
    y_prompt = y_prompt.reshape(batch, seq, D_MODEL)

    def last_rows(a, rows):
        return a.reshape(batch, seq, a.shape[-1])[:, seq - rows:]

    k_win_p = last_rows(k, WINDOW).reshape(1, batch, WINDOW, N_KV_HEADS, HEAD_DIM)
    v_win_p = last_rows(v, WINDOW).reshape(1, batch, WINDOW, N_KV_HEADS, HEAD_DIM)
    conv_p = conv_tail[None, :, SUBLANES - (CONV_W - 1):, :]
    h_p = h_last[None, :, 0, :]

    ck = cache_k_win[l].reshape(dec_batch, WINDOW, D_KV)
    cv = cache_v_win[l].reshape(dec_batch, WINDOW, D_KV)
    attn_s, k_win_s, v_win_s = _attn_sample(qs, ks, vs, ck, cv, sinks)
    sc = state_conv[l]
    rnn_s, h_s = _rnn_sample(xrs, gys, sc[:, 0], sc[:, 1], sc[:, 2], state_lru_h[l],
                             cw, cb, wcat_f, b_a, b_i, lam)
    y_sample, _ = tail(xs, attn_s, rnn_s, sgas, sgrs, dec_batch, dec_batch, True, experts_b)
    y_sample = y_sample.reshape(dec_batch, 1, D_MODEL)
    conv_s = jnp.stack([sc[:, 1], sc[:, 2], xrs], axis=1)[None]

    return (y_prompt, y_sample, k_win_p, v_win_p, conv_p, h_p,
            k_win_s.reshape(1, dec_batch, WINDOW, N_KV_HEADS, HEAD_DIM),
            v_win_s.reshape(1, dec_batch, WINDOW, N_KV_HEADS, HEAD_DIM),
            conv_s, h_s[None])
```

```python
import functools

import jax
import jax.numpy as jnp
from jax import lax
from jax.experimental import pallas as pl
from jax.experimental.pallas import tpu as pltpu

D_MODEL = 1024
HEAD_DIM = 64
N_Q_HEADS = 8
N_KV_HEADS = 2
Q_PER_KV = N_Q_HEADS // N_KV_HEADS
WINDOW = 128
ATTN_BLOCK = 128
ROPE_THETA = 10000.0
SCALE = HEAD_DIM ** -0.5
NEG_INF = -1e30
D_RNN = 1280
N_RNN_BLOCKS = 10
RNN_BLOCK = D_RNN // N_RNN_BLOCKS
CONV_W = 4
LRU_C = 8.0
N_GROUPS = 4
EXPERTS_PER_GROUP = 8
N_EXPERTS = N_GROUPS * EXPERTS_PER_GROUP
D_EXPERT = 256
PAST_LEN = 16384
EPS = 1e-6
D_Q = N_Q_HEADS * HEAD_DIM
D_KV = N_KV_HEADS * HEAD_DIM
D_IN = D_Q + 2 * D_KV + 2 * D_RNN + 2 * D_MODEL
OFF_K = D_Q
OFF_V = OFF_K + D_KV
OFF_XR = OFF_V + D_KV
OFF_YR = OFF_XR + D_RNN
OFF_GA = OFF_YR + D_RNN
OFF_GR = OFF_GA + D_MODEL

LANES = 128
SUBLANES = 8
VMEM_LIMIT = 56 * 1024 * 1024

F32 = jnp.float32
BF16 = jnp.bfloat16


def _params(*sem):
    return pltpu.CompilerParams(dimension_semantics=sem, vmem_limit_bytes=VMEM_LIMIT)


def _sigmoid(x):
    return 1.0 / (1.0 + jnp.exp(-x))


def _gelu_tanh(x):
    c = 0.7978845608028654
    return 0.5 * x * (1.0 + jnp.tanh(c * (x + 0.044715 * (x * x * x))))


def _full(shape, single_buffer=False):
    index_map = lambda *_: (0,) * len(shape)
    if single_buffer:
        return pl.BlockSpec(shape, index_map, pipeline_mode=pl.Buffered(1))
    return pl.BlockSpec(shape, index_map)


def _mm(a, b, precise, dims=None):
    if precise:
        a, b, prec = a.astype(F32), b.astype(F32), lax.Precision.HIGHEST
    else:
        a, b, prec = a.astype(BF16), b.astype(BF16), None
    if dims is None:
        return jnp.dot(a, b, preferred_element_type=F32, precision=prec)
    return lax.dot_general(a, b, dims, preferred_element_type=F32, precision=prec)


_NT = (((1,), (1,)), ((), ()))


def _proj_kernel(x_ref, g_ref, w_ref, cos_ref, sin_ref, qkg_ref, *rest, precise, fuse_rnn):
    if fuse_rnn:
        rnn_w = rest[:6]
        q_ref, k_ref, v_ref, rnn_ref, sga_ref, sgr_ref, hl_ref, ct_ref = rest[6:14]
        xbuf, a_scr, b_scr, h_scr, hcar = rest[14:]
        pl.when(pl.program_id(1) == 0)(functools.partial(_rnn_start_sequence, xbuf, hcar))
    else:
        q_ref, k_ref, v_ref, xr_ref, gy_ref, sga_ref, sgr_ref, wb_ref = rest
        wb_ref[...] = w_ref[...].astype(BF16)
    x = x_ref[...]
    inv = lax.rsqrt(jnp.mean(x * x, axis=-1, keepdims=True) + EPS)
    xn = x * inv * g_ref[...]
    if not precise:
        xn = xn.astype(BF16)

    def proj(lo, hi):
        return _mm(xn, w_ref[:, lo:hi], precise)

    def qk_heads():
        qk = proj(0, OFF_V)
        tm = qk.shape[0]
        lane = lax.broadcasted_iota(jnp.int32, (tm, LANES), 1)
        lo_head = lane < HEAD_DIM
        first_half = (lane % HEAD_DIM) < (HEAD_DIM // 2)
        cos = cos_ref[...]
        sin = sin_ref[...]
        for g in range(OFF_V // LANES):
            seg = qk[:, g * LANES:(g + 1) * LANES]
            sq = seg * seg
            s_lo = jnp.sum(jnp.where(lo_head, sq, 0.0), axis=-1, keepdims=True)
            s_hi = jnp.sum(jnp.where(lo_head, 0.0, sq), axis=-1, keepdims=True)
            ms = jnp.where(lo_head, s_lo, s_hi) * (1.0 / HEAD_DIM)
            normed = seg * lax.rsqrt(ms + EPS) * qkg_ref[:, g * LANES:(g + 1) * LANES]
            partner = jnp.where(first_half,
                                pltpu.roll(normed, LANES - HEAD_DIM // 2, axis=1),
                                pltpu.roll(normed, HEAD_DIM // 2, axis=1))
            roped = normed * cos + partner * sin
            if g < D_Q // LANES:
                q_ref[:, g * LANES:(g + 1) * LANES] = (roped * SCALE).astype(q_ref.dtype)
            else:
                k_ref[...] = roped

    def values():
        v_ref[...] = proj(OFF_V, OFF_XR)

    def gate(out_ref, off, c0, c1, r0=0, r1=x.shape[0]):
        y = _mm(xn[r0:r1], w_ref[:, off + c0:off + c1], precise)
        out_ref[r0:r1, c0:c1] = _sigmoid(y).astype(out_ref.dtype)

    if not fuse_rnn:
        xr_ref[...] = proj(OFF_XR, OFF_YR)
        gy_ref[...] = _gelu_tanh(proj(OFF_YR, OFF_GA)).astype(gy_ref.dtype)
        qk_heads()
        values()
        gate(sga_ref, OFF_GA, 0, D_MODEL)
        gate(sgr_ref, OFF_GR, 0, D_MODEL)
        return

    slab = RNN_SLAB_BLOCKS * RNN_BLOCK
    half = x.shape[0] // 2
    others = [qk_heads, values]
    for out_ref, off in ((sga_ref, OFF_GA), (sgr_ref, OFF_GR)):
        others += [functools.partial(gate, out_ref, off, c, c + slab, r, r + half)
                   for c in range(0, D_MODEL, slab) for r in (0, half)]
    others = iter(others)
    gy = []
    for i in range(D_RNN // slab):
        c0 = i * slab
        xr = proj(OFF_XR + c0, OFF_XR + c0 + slab)
        gy.append(_gelu_tanh(proj(OFF_YR + c0, OFF_YR + c0 + slab)))
        ct_ref[0, :, c0:c0 + slab] = xr[xr.shape[0] - SUBLANES:, :]
        for _ in _rnn_scan_terms(xr, i * RNN_SLAB_BLOCKS, *rnn_w, xbuf, a_scr, b_scr):
            next(others, lambda: None)()
    for other in others:
        other()
    _rnn_scan_finish(jnp.concatenate(gy, axis=-1), rnn_ref, hl_ref, a_scr, b_scr, h_scr, hcar)


RNN_SLAB_BLOCKS = 2


def _proj(x, g, w_in, cos_t, sin_t, qkg, batch, seq, tm, precise, rnn_weights=None):
    n = batch * seq
    nt = seq // tm
    row = lambda b, t: (b * nt + t, 0)
    per_seq = lambda b, t: (b, 0, 0)
    act = F32 if precise else BF16
    rows_out = lambda width, dtype: (jax.ShapeDtypeStruct((n, width), dtype),
                                     pl.BlockSpec((tm, width), row))
    state_out = (jax.ShapeDtypeStruct((batch, SUBLANES, D_RNN), F32),
                 pl.BlockSpec((1, SUBLANES, D_RNN), per_seq))
    outs = [rows_out(D_Q, act), rows_out(D_KV, F32), rows_out(D_KV, F32)]
    in_specs = [
        pl.BlockSpec((tm, D_MODEL), row),
        _full((1, D_MODEL)),
        _full((D_MODEL, D_IN), single_buffer=True),
        pl.BlockSpec((tm, LANES), lambda b, t: (t, 0)),
        pl.BlockSpec((tm, LANES), lambda b, t: (t, 0)),
        _full((1, OFF_V)),
    ]
    args = [x, g, w_in, cos_t, sin_t, qkg]
    scratch = []
    if rnn_weights is None:
        outs += [rows_out(D_RNN, F32), rows_out(D_RNN, act)]
    else:
        assert not precise
        outs += [rows_out(D_RNN, act)]
        in_specs += [_full(w.shape) for w in rnn_weights]
        args += list(rnn_weights)
        groups = tm // SUBLANES
        scratch = [pltpu.VMEM((tm + SUBLANES, D_RNN), F32)]
        scratch += [pltpu.VMEM((groups, SUBLANES, D_RNN), F32)] * 3
        scratch += [pltpu.VMEM((SUBLANES, D_RNN), F32)]
    outs += [rows_out(D_MODEL, act), rows_out(D_MODEL, act)]
    if rnn_weights is not None:
        outs += [state_out, state_out]
    else:
        assert batch * nt == 1 and w_in.dtype == F32
        outs += [(jax.ShapeDtypeStruct(w_in.shape, BF16), _full(w_in.shape, single_buffer=True))]
    return pl.pallas_call(
        functools.partial(_proj_kernel, precise=precise, fuse_rnn=rnn_weights is not None),
        grid=(batch, nt),
        in_specs=in_specs,
        out_specs=[spec for _, spec in outs],
        out_shape=[shape for shape, _ in outs],
        scratch_shapes=scratch,
        compiler_params=_params("parallel", "arbitrary"),
        name="proj",
    )(*args)


def _softmax_pv(s, sink, v2, precise):
    m = jnp.maximum(jnp.max(s, axis=-1, keepdims=True), sink)
    p = jnp.exp(s - m)
    denom = jnp.sum(p, axis=-1, keepdims=True) + jnp.exp(sink - m)
    return _mm(p, v2, precise) * (1.0 / denom)


def _sink_column(sink_ref, h, rows, rows_per_head):
    r = lax.broadcasted_iota(jnp.int32, (rows, 1), 0) // rows_per_head
    col = jnp.full((rows, 1), sink_ref[h * Q_PER_KV], F32)
    for g in range(1, Q_PER_KV):
        col = jnp.where(r == g, sink_ref[h * Q_PER_KV + g], col)
    return col


ATTN_BLOCKS_PER_STEP = 1


def _attn_prompt_kernel(sink_ref, q_ref, kc_ref, kp_ref, vc_ref, vp_ref, o_ref):
    n = pl.program_id(1)
    blk = ATTN_BLOCK
    k_all = jnp.concatenate([kp_ref[...], kc_ref[...]], axis=0).astype(BF16)
    v_all = jnp.concatenate([vp_ref[...], vc_ref[...]], axis=0).astype(BF16)
    rows = Q_PER_KV * blk
    i = lax.broadcasted_iota(jnp.int32, (rows, 2 * blk), 0) % blk
    j = lax.broadcasted_iota(jnp.int32, (rows, 2 * blk), 1)
    d = j - i
    in_window = (d >= 1) & (d <= WINDOW)
    first_valid = (d >= jnp.where(n > 0, 1, jnp.maximum(1, blk - i))) & (d <= WINDOW)
    for sub in range(ATTN_BLOCKS_PER_STEP):
        q = q_ref[sub * blk:(sub + 1) * blk, :]
        k2 = k_all[sub * blk:(sub + 2) * blk]
        v2 = v_all[sub * blk:(sub + 2) * blk]
        valid = first_valid if sub == 0 else in_window
        for h in range(N_KV_HEADS):
            kh = k2[:, h * HEAD_DIM:(h + 1) * HEAD_DIM]
            vh = v2[:, h * HEAD_DIM:(h + 1) * HEAD_DIM]
            qs = jnp.concatenate(
                [q[:, (h * Q_PER_KV + g) * HEAD_DIM:(h * Q_PER_KV + g + 1) * HEAD_DIM]
                 for g in range(Q_PER_KV)], axis=0)
            s = _mm(qs, kh, False, _NT)
            s = jnp.where(valid, s, NEG_INF)
            o = _softmax_pv(s, _sink_column(sink_ref, h, rows, blk), vh, False)
            for g in range(Q_PER_KV):
                c = (h * Q_PER_KV + g) * HEAD_DIM
                o_ref[sub * blk:(sub + 1) * blk, c:c + HEAD_DIM] = (
                    o[g * blk:(g + 1) * blk].astype(o_ref.dtype))


def _attn_prompt(q, k, v, sinks, batch, seq):
    step = ATTN_BLOCKS_PER_STEP * ATTN_BLOCK
    ns = seq // step
    cur = lambda b, n: (b * ns + n, 0)
    prev = lambda b, n: (jnp.maximum((b * ns + n) * ATTN_BLOCKS_PER_STEP - 1, 0), 0)
    return pl.pallas_call(
        _attn_prompt_kernel,
        grid=(batch, ns),
        in_specs=[
            pl.BlockSpec(memory_space=pltpu.SMEM),
            pl.BlockSpec((step, D_Q), cur),
            pl.BlockSpec((step, D_KV), cur),
            pl.BlockSpec((ATTN_BLOCK, D_KV), prev),
            pl.BlockSpec((step, D_KV), cur),
            pl.BlockSpec((ATTN_BLOCK, D_KV), prev),
        ],
        out_specs=pl.BlockSpec((step, D_Q), cur),
        out_shape=jax.ShapeDtypeStruct((batch * seq, D_Q), BF16),
        compiler_params=_params("parallel", "parallel"),
        name="attn_prompt",
    )(sinks, q, k, k, v, v)


SAMPLE_BT = 8


def _attn_sample_kernel(sink_ref, q_ref, kn_ref, vn_ref, kc_ref, vc_ref, o_ref, ko_ref, vo_ref):
    bt = SAMPLE_BT
    w = lax.broadcasted_iota(jnp.int32, (bt, WINDOW, D_KV), 1)

    def shifted(cache_ref, new_ref):
        rolled = pltpu.roll(cache_ref[...], WINDOW - 1, axis=1)
        return jnp.where(w == WINDOW - 1, new_ref[...][:, None, :], rolled)

    k_win = shifted(kc_ref, kn_ref)
    v_win = shifted(vc_ref, vn_ref)
    ko_ref[...] = k_win
    vo_ref[...] = v_win
    k2 = k_win.reshape(bt * WINDOW, D_KV)
    v2 = v_win.reshape(bt * WINDOW, D_KV)
    q = q_ref[...]
    rows = Q_PER_KV * bt
    rb = lax.broadcasted_iota(jnp.int32, (rows, bt * WINDOW), 0) % bt
    cb = lax.broadcasted_iota(jnp.int32, (rows, bt * WINDOW), 1) // WINDOW
    valid = rb == cb
    for h in range(N_KV_HEADS):
        kh = k2[:, h * HEAD_DIM:(h + 1) * HEAD_DIM]
        vh = v2[:, h * HEAD_DIM:(h + 1) * HEAD_DIM]
        qs = jnp.concatenate(
            [q[:, (h * Q_PER_KV + g) * HEAD_DIM:(h * Q_PER_KV + g + 1) * HEAD_DIM]
             for g in range(Q_PER_KV)], axis=0)
        s = _mm(qs, kh, True, _NT)
        s = jnp.where(valid, s, NEG_INF)
        o = _softmax_pv(s, _sink_column(sink_ref, h, rows, bt), vh, True)
        for g in range(Q_PER_KV):
            c = (h * Q_PER_KV + g) * HEAD_DIM
            o_ref[:, c:c + HEAD_DIM] = o[g * bt:(g + 1) * bt].astype(o_ref.dtype)


def _attn_sample(q, k_new, v_new, cache_k, cache_v, sinks):
    nbatch = q.shape[0]
    bt = SAMPLE_BT
    row = lambda i: (i, 0)
    win = lambda i: (i, 0, 0)
    return pl.pallas_call(
        _attn_sample_kernel,
        grid=(nbatch // bt,),
        in_specs=[
            pl.BlockSpec(memory_space=pltpu.SMEM),
            pl.BlockSpec((bt, D_Q), row),
            pl.BlockSpec((bt, D_KV), row),
            pl.BlockSpec((bt, D_KV), row),
            pl.BlockSpec((bt, WINDOW, D_KV), win),
            pl.BlockSpec((bt, WINDOW, D_KV), win),
        ],
        out_specs=[
            pl.BlockSpec((bt, D_Q), row),
            pl.BlockSpec((bt, WINDOW, D_KV), win),
            pl.BlockSpec((bt, WINDOW, D_KV), win),
        ],
        out_shape=(
            jax.ShapeDtypeStruct((nbatch, D_Q), F32),
            jax.ShapeDtypeStruct((nbatch, WINDOW, D_KV), F32),
            jax.ShapeDtypeStruct((nbatch, WINDOW, D_KV), F32),
        ),
        compiler_params=_params("parallel"),
        name="attn_sample",
    )(sinks, q, k_new, v_new, cache_k, cache_v)


def _lru_terms(xc, wcat_ref, ba_ref, bi_ref, lam_ref, precise, block0=0):
    cols = slice(block0 * RNN_BLOCK, block0 * RNN_BLOCK + xc.shape[1])
    xcb = xc if precise else xc.astype(BF16)
    ya, yi = [], []
    for n in range(xc.shape[1] // RNN_BLOCK):
        y = _mm(xcb[:, n * RNN_BLOCK:(n + 1) * RNN_BLOCK], wcat_ref[block0 + n], precise)
        ya.append(y[:, :RNN_BLOCK])
        yi.append(y[:, RNN_BLOCK:])
    r = _sigmoid(jnp.concatenate(ya, axis=-1) + ba_ref[:, cols])
    gate_i = _sigmoid(jnp.concatenate(yi, axis=-1) + bi_ref[:, cols])
    neg_lam = -lam_ref[:, cols]
    softplus = jnp.maximum(neg_lam, 0.0) + jnp.log1p(jnp.exp(-jnp.abs(neg_lam)))
    log_a = (-LRU_C * softplus) * r
    a = jnp.exp(log_a)
    m = 1.0 - a * a
    b = jnp.where(m > 0.0, m * lax.rsqrt(m), 0.0) * (gate_i * xc)
    return a, b


def _rnn_start_sequence(xbuf, hcar):
    xbuf[0:SUBLANES, :] = jnp.zeros((SUBLANES, D_RNN), F32)
    hcar[...] = jnp.zeros((SUBLANES, D_RNN), F32)


def _rnn_scan_terms(x, block0, cw_ref, cb_ref, wcat_ref, ba_ref, bi_ref, lam_ref,
                    xbuf, a_scr, b_scr):
    tt, width = x.shape
    groups = tt // SUBLANES
    cols = slice(block0 * RNN_BLOCK, block0 * RNN_BLOCK + width)
    xbuf[SUBLANES:, cols] = x
    xc = cb_ref[:, cols] + cw_ref[CONV_W - 1:CONV_W, cols] * x
    for j in range(CONV_W - 1):
        s = CONV_W - 1 - j
        xc = xc + cw_ref[j:j + 1, cols] * xbuf[SUBLANES - s:SUBLANES - s + tt, cols]
    xbuf[0:SUBLANES, cols] = x[tt - SUBLANES:, :]
    yield
    a, b = _lru_terms(xc, wcat_ref, ba_ref, bi_ref, lam_ref, False, block0)
    yield
    a = a.reshape(groups, SUBLANES, width)
    b = b.reshape(groups, SUBLANES, width)
    step = lax.broadcasted_iota(jnp.int32, (groups, SUBLANES, width), 1)
    k = 1
    while k < SUBLANES:
        keep = step >= k
        a_sh = jnp.where(keep, pltpu.roll(a, k, axis=1), 1.0)
        b_sh = jnp.where(keep, pltpu.roll(b, k, axis=1), 0.0)
        b = a * b_sh + b
        a = a * a_sh
        k *= 2
        if k < SUBLANES:
            yield
    a_scr[:, :, cols] = a
    b_scr[:, :, cols] = b


def _rnn_scan_finish(gy, o_ref, hl_ref, a_scr, b_scr, h_scr, hcar):
    groups = a_scr.shape[0]
    tt = groups * SUBLANES

    def chain(g, h_in):
        h = a_scr[g] * h_in + b_scr[g]
        h_scr[g] = h
        return jnp.broadcast_to(h[SUBLANES - 1:SUBLANES, :], (SUBLANES, D_RNN))

    h_last = lax.fori_loop(0, groups, chain, hcar[...], unroll=True)
    hcar[...] = h_last
    hl_ref[0] = h_last
    h = h_scr[...].reshape(tt, D_RNN)
    o_ref[...] = (h * gy).astype(o_ref.dtype)


def _rnn_sample_kernel(xr_ref, gy_ref, s0_ref, s1_ref, s2_ref, h_ref, cw_ref, cb_ref,
                       wcat_ref, ba_ref, bi_ref, lam_ref, o_ref, hn_ref):
    x = xr_ref[...]
    xc = (cb_ref[...] + cw_ref[0:1, :] * s0_ref[...] + cw_ref[1:2, :] * s1_ref[...]
          + cw_ref[2:3, :] * s2_ref[...] + cw_ref[3:4, :] * x)
    a, b = _lru_terms(xc, wcat_ref, ba_ref, bi_ref, lam_ref, True)
    h = a * h_ref[...] + b
    hn_ref[...] = h
    o_ref[...] = (h * gy_ref[...].astype(F32)).astype(o_ref.dtype)


def _rnn_sample(xr, gy, s0, s1, s2, h_prev, conv_w, conv_b, wcat, b_a, b_i, lam):
    n = xr.shape[0]
    act = _full((n, D_RNN))
    return pl.pallas_call(
        _rnn_sample_kernel,
        grid=(1,),
        in_specs=[act, act, act, act, act, act,
                  _full((CONV_W, D_RNN)), _full((1, D_RNN)),
                  _full((N_RNN_BLOCKS, RNN_BLOCK, 2 * RNN_BLOCK)),
                  _full((1, D_RNN)), _full((1, D_RNN)), _full((1, D_RNN))],
        out_specs=[act, act],
        out_shape=(jax.ShapeDtypeStruct((n, D_RNN), F32),
                   jax.ShapeDtypeStruct((n, D_RNN), F32)),
        compiler_params=_params("arbitrary"),
        name="rnn_sample",
    )(xr, gy, s0, s1, s2, h_prev, conv_w, conv_b, wcat, b_a, b_i, lam)


def _merge_kernel(x_ref, at_ref, rn_ref, sga_ref, sgr_ref, wa_ref, wr_ref, wo_ref, g_ref,
                  wrt_ref, brt_ref, tri_ref, *rest, precise, tiles_per_seg, cast_experts):
    if cast_experts:
        wg_ref, wu_ref, wd_ref, *rest = rest
        x2_ref, xtm_ref, rec_ref, rect_ref, cnt_ref, wg_o, wu_o, wd_o, cnt_scr = rest
    else:
        x2_ref, xtm_ref, rec_ref, rect_ref, cnt_ref, cnt_scr = rest

    @pl.when(pl.program_id(0) % tiles_per_seg == 0)
    def _():
        cnt_scr[...] = jnp.zeros_like(cnt_scr)

    tm = x_ref.shape[0]
    parts = MERGE_PARTS if tm % (MERGE_PARTS * LANES) == 0 else 1
    part = tm // parts
    counts = cnt_scr[...]
    for i in range(parts):
        rows = slice(i * part, (i + 1) * part)
        counts = _merge_rows(rows, counts, x_ref, at_ref, rn_ref, sga_ref, sgr_ref, wa_ref, wr_ref,
                             wo_ref, g_ref, wrt_ref, brt_ref, tri_ref, x2_ref, xtm_ref, rec_ref,
                             rect_ref, precise)
    cnt_scr[...] = counts
    cnt_ref[0] = counts
    if cast_experts:
        wg_o[...] = wg_ref[...].astype(BF16)
        wu_o[...] = wu_ref[...].astype(BF16)
        wd_o[...] = wd_ref[...].astype(BF16)


MERGE_PARTS = 2


def _merge_rows(rows, counts, x_ref, at_ref, rn_ref, sga_ref, sgr_ref, wa_ref, wr_ref, wo_ref, g_ref,
                wrt_ref, brt_ref, tri_ref, x2_ref, xtm_ref, rec_ref, rect_ref, precise):
    ya = _mm(at_ref[rows, :], wa_ref[...], precise)
    yr = _mm(rn_ref[rows, :], wr_ref[...], precise)
    merged = sga_ref[rows, :].astype(F32) * ya + sgr_ref[rows, :].astype(F32) * yr
    x2 = x_ref[rows, :] + _mm(merged, wo_ref[...], precise)
    x2_ref[rows, :] = x2
    inv = lax.rsqrt(jnp.mean(x2 * x2, axis=-1, keepdims=True) + EPS)
    xn = x2 * inv * g_ref[...]

    logits = _mm(xn, wrt_ref[...], precise) + brt_ref[...]
    tm = logits.shape[0]
    lane_i = lax.broadcasted_iota(jnp.int32, (tm, LANES), 1)
    lane = lane_i.astype(F32)
    big = float(LANES)
    is_grp = (lane_i >= N_EXPERTS) & (lane_i < N_EXPERTS + N_GROUPS)
    gl = jnp.where(is_grp, logits, NEG_INF)
    gmax = jnp.max(gl, axis=-1, keepdims=True)
    g_idx = jnp.min(jnp.where(gl == gmax, lane, big), axis=-1, keepdims=True) - N_EXPERTS
    p_g = 1.0 / jnp.sum(jnp.exp(gl - gmax), axis=-1, keepdims=True)
    in_grp = (lane_i // EXPERTS_PER_GROUP).astype(F32) == g_idx
    el = jnp.where(in_grp, logits, NEG_INF)
    v1 = jnp.max(el, axis=-1, keepdims=True)
    i1 = jnp.min(jnp.where(el == v1, lane, big), axis=-1, keepdims=True)
    el2 = jnp.where(lane == i1, NEG_INF, el)
    v2 = jnp.max(el2, axis=-1, keepdims=True)
    i2 = jnp.min(jnp.where(el2 == v2, lane, big), axis=-1, keepdims=True)
    e2 = jnp.exp(v2 - v1)
    w1 = p_g / (1.0 + e2)
    w2 = p_g * e2 / (1.0 + e2)

    hit = jnp.where(lane == i1, 1.0, jnp.where(lane == i2, 1.0, 0.0))
    before = (jnp.dot(tri_ref[0:tm, 0:tm], hit.astype(BF16), preferred_element_type=F32)
              + counts[0:1, :])
    r1 = jnp.sum(jnp.where(lane == i1, before, 0.0), axis=-1, keepdims=True)
    r2 = jnp.sum(jnp.where(lane == i2, before, 0.0), axis=-1, keepdims=True)
    rec = jnp.where(lane == REC_W2, w2, 0.0)
    for field, val in ((REC_W1, w1), (REC_R2, r2), (REC_R1, r1), (REC_E2, i2), (REC_E1, i1)):
        rec = jnp.where(lane == field, val, rec)
    rec_ref[rows, :] = rec
    rect_ref[:, rows] = rec.T

    for c in range(TOKEN_ROWS):
        xtm_ref[pl.ds(rows.start * TOKEN_ROWS + c, tm, stride=TOKEN_ROWS), :] = (
            xn[:, c * LANES:(c + 1) * LANES])
    return counts + jnp.sum(hit, axis=0, keepdims=True)


REC_E1, REC_E2, REC_R1, REC_R2, REC_W1, REC_W2 = range(6)
TOKEN_ROWS = D_MODEL // LANES


def _merge(x, attn, rnn, sga, sgr, wa, wr, wo, g2, w_route, b_route, tri, tm, seg, precise,
           expert_weights=None):
    n = x.shape[0]
    steps = n // tm
    tiles_per_seg = seg // tm
    row = lambda i: (i, 0)
    in_specs = [
        pl.BlockSpec((tm, D_MODEL), row),
        pl.BlockSpec((tm, D_Q), row),
        pl.BlockSpec((tm, D_RNN), row),
        pl.BlockSpec((tm, D_MODEL), row),
        pl.BlockSpec((tm, D_MODEL), row),
        _full((D_Q, D_MODEL)),
        _full((D_RNN, D_MODEL)),
        _full((D_MODEL, D_MODEL)),
        _full((1, D_MODEL)),
        _full((D_MODEL, LANES)),
        _full((1, LANES)),
        _full((tm, tm)),
    ]
    out_specs = [
        pl.BlockSpec((tm, D_MODEL), row),
        pl.BlockSpec((tm * TOKEN_ROWS, LANES), row),
        pl.BlockSpec((tm, LANES), row),
        pl.BlockSpec((LANES, tm), lambda i: (0, i)),
        pl.BlockSpec((1, SUBLANES, LANES), lambda i: (i // tiles_per_seg, 0, 0)),
    ]
    out_shape = [
        jax.ShapeDtypeStruct((n, D_MODEL), F32),
        jax.ShapeDtypeStruct((n * TOKEN_ROWS, LANES), F32),
        jax.ShapeDtypeStruct((n, LANES), F32),
        jax.ShapeDtypeStruct((LANES, n), F32),
        jax.ShapeDtypeStruct((n // seg, SUBLANES, LANES), F32),
    ]
    args = [x, attn, rnn, sga, sgr, wa, wr, wo, g2, w_route, b_route, tri]
    if expert_weights is not None:
        assert steps == N_EXPERTS
        for w in expert_weights:
            spec = pl.BlockSpec((1,) + w.shape[1:], lambda i: (i, 0, 0))
            in_specs.append(spec)
            out_specs.append(spec)
            out_shape.append(jax.ShapeDtypeStruct(w.shape, BF16))
            args.append(w)
    return pl.pallas_call(
        functools.partial(_merge_kernel, precise=precise, tiles_per_seg=tiles_per_seg,
                          cast_experts=expert_weights is not None),
        grid=(steps,),
        in_specs=in_specs,
        out_specs=out_specs,
        out_shape=out_shape,
        scratch_shapes=[pltpu.VMEM((SUBLANES, LANES), F32)],
        compiler_params=_params("arbitrary"),
        name="merge",
    )(*args)


MOE_CHUNK = 256
MOE_TAIL = 128
MOE_EXPERTS_PER_STEP = 2
MOE_VMEM_LIMIT = 60 * 1024 * 1024


def _seg_rows(seg):
    return 2 * seg + N_EXPERTS * SUBLANES + MOE_CHUNK


def _token_rows(i):
    return pl.ds(pl.multiple_of(i * TOKEN_ROWS, TOKEN_ROWS), TOKEN_ROWS)


def _sorted_rows(first_row):
    return pl.ds(pl.multiple_of(first_row, TOKEN_ROWS), TOKEN_ROWS)


def _moe_kernel(off_ref, nfull_ref, rem_ref, slot_ref, xtm_ref, wg_ref, wu_ref, wd_ref,
                x2_ref, rec_ref, o_ref, buf, g1, g2, *, seg, td, tc):
    s = pl.program_id(0)
    p = pl.program_id(1)
    n_disp = seg // td
    n_exp = N_EXPERTS // MOE_EXPERTS_PER_STEP

    @pl.when((s == 0) & (p == 0))
    def _():
        buf[...] = jnp.zeros_like(buf)

    @pl.when(p < n_disp)
    def _():
        def dispatch(g, carry):
            for j in range(SUBLANES):
                t = g * SUBLANES + j
                row = xtm_ref[_token_rows(t), :]
                for k in range(2):
                    buf[_sorted_rows(slot_ref[0, 0, k * seg + p * td + t]), :] = row
            return carry

        lax.fori_loop(0, td // SUBLANES, dispatch, 0)

    def run_chunk(e, row0, rows, valid):
        r0 = pl.multiple_of(row0 * TOKEN_ROWS, SUBLANES * TOKEN_ROWS)
        xf = [buf[pl.ds(r0 + j, rows, stride=TOKEN_ROWS), :] for j in range(TOKEN_ROWS)]
        x = jnp.concatenate(xf, axis=-1).astype(BF16)
        hg = jnp.dot(x, wg_ref[e], preferred_element_type=F32)
        hu = jnp.dot(x, wu_ref[e], preferred_element_type=F32)
        h = (hg * _sigmoid(hg)) * hu
        y = jnp.dot(h.astype(BF16), wd_ref[e], preferred_element_type=F32)
        if valid is not None:
            mine = lax.broadcasted_iota(jnp.int32, (rows, LANES), 0) < valid
        for j in range(TOKEN_ROWS):
            yj = y[:, j * LANES:(j + 1) * LANES]
            if valid is not None:
                yj = jnp.where(mine, yj, xf[j])
            buf[pl.ds(r0 + j, rows, stride=TOKEN_ROWS), :] = yj

    def run_expert(e):
        idx = s * N_EXPERTS + (p - n_disp) * MOE_EXPERTS_PER_STEP + e
        base = off_ref[idx]
        n_full = nfull_ref[idx]
        rem = rem_ref[idx]

        def chunk(c, carry):
            run_chunk(e, base + c * MOE_CHUNK, MOE_CHUNK, None)
            return carry

        lax.fori_loop(0, n_full, chunk, 0)
        last = base + n_full * MOE_CHUNK
        for units in range(1, MOE_CHUNK // MOE_TAIL + 2):
            @pl.when((rem > (units - 1) * MOE_TAIL) & (rem <= units * MOE_TAIL))
            def _():
                run_chunk(e, last, units * MOE_TAIL, rem)

    @pl.when((p >= n_disp) & (p < n_disp + n_exp))
    def _():
        for e in range(MOE_EXPERTS_PER_STEP):
            run_expert(e)

    @pl.when(p >= n_disp + n_exp)
    def _():
        t0 = (p - n_disp - n_exp) * tc

        def gather(g, carry):
            for j in range(SUBLANES):
                t = g * SUBLANES + j
                g1[_token_rows(t), :] = buf[_sorted_rows(slot_ref[0, 0, t0 + t]), :]
                g2[_token_rows(t), :] = buf[_sorted_rows(slot_ref[0, 0, seg + t0 + t]), :]
            return carry

        lax.fori_loop(0, tc // SUBLANES, gather, 0)
        rec = rec_ref[...]
        lane = lax.broadcasted_iota(jnp.int32, rec.shape, 1)
        w1 = jnp.broadcast_to(
            jnp.sum(jnp.where(lane == REC_W1, rec, 0.0), axis=-1, keepdims=True), rec.shape)
        w2 = jnp.broadcast_to(
            jnp.sum(jnp.where(lane == REC_W2, rec, 0.0), axis=-1, keepdims=True), rec.shape)
        for j in range(TOKEN_ROWS):
            cols = slice(j * LANES, (j + 1) * LANES)
            o_ref[:, cols] = (x2_ref[:, cols] + w1 * g1[pl.ds(j, tc, stride=TOKEN_ROWS), :]
                              + w2 * g2[pl.ds(j, tc, stride=TOKEN_ROWS), :])


def _moe(off, nfull, rem, slot, xtm, wg, wu, wd, x2, rec, seg, td, tc):
    n = x2.shape[0]
    n_seg = n // seg
    n_disp, n_comb = seg // td, seg // tc
    per_step = MOE_EXPERTS_PER_STEP
    n_exp = N_EXPERTS // per_step
    rows = _seg_rows(seg) * TOKEN_ROWS
    disp_tile = lambda s, p, *_: (s * n_disp + jnp.minimum(p, n_disp - 1), 0)
    expert = lambda s, p, *_: (jnp.clip(p - n_disp, 0, n_exp - 1), 0, 0)
    comb_tile = lambda s, p, *_: (s * n_comb + jnp.clip(p - n_disp - n_exp, 0, n_comb - 1), 0)
    grid_spec = pltpu.PrefetchScalarGridSpec(
        num_scalar_prefetch=3,
        grid=(n_seg, n_disp + n_exp + n_comb),
        in_specs=[
            pl.BlockSpec((1, 1, 2 * seg), lambda s, p, *_: (s, 0, 0), memory_space=pltpu.SMEM),
            pl.BlockSpec((td * TOKEN_ROWS, LANES), disp_tile),
            pl.BlockSpec((per_step, D_MODEL, D_EXPERT), expert),
            pl.BlockSpec((per_step, D_MODEL, D_EXPERT), expert),
            pl.BlockSpec((per_step, D_EXPERT, D_MODEL), expert),
            pl.BlockSpec((tc, D_MODEL), comb_tile),
            pl.BlockSpec((tc, LANES), comb_tile),
        ],
        out_specs=pl.BlockSpec((tc, D_MODEL), comb_tile),
        scratch_shapes=[pltpu.VMEM((rows, LANES), F32),
                        pltpu.VMEM((tc * TOKEN_ROWS, LANES), F32),
                        pltpu.VMEM((tc * TOKEN_ROWS, LANES), F32)],
    )
    return pl.pallas_call(
        functools.partial(_moe_kernel, seg=seg, td=td, tc=tc),
        grid_spec=grid_spec,
        out_shape=jax.ShapeDtypeStruct((n, D_MODEL), F32),
        compiler_params=pltpu.CompilerParams(
            dimension_semantics=("arbitrary", "arbitrary"), vmem_limit_bytes=MOE_VMEM_LIMIT),
        name="moe",
    )(off, nfull, rem, slot, xtm, wg, wu, wd, x2, rec)


def _plan(rect, cnt, seg):
    n = rect.shape[1]
    expert = rect[REC_E1:REC_E2 + 1].astype(jnp.int32)
    rank = rect[REC_R1:REC_R2 + 1].astype(jnp.int32)
    counts = cnt[:, 0, :N_EXPERTS].astype(jnp.int32)
    padded = (counts + SUBLANES - 1) // SUBLANES * SUBLANES
    off = jnp.cumsum(padded, axis=1) - padded
    n_full = counts // MOE_CHUNK
    rem = counts - n_full * MOE_CHUNK
    join = (rem > 0) & (rem <= MOE_TAIL) & (n_full > 0)
    n_full = n_full - join
    rem = rem + join * MOE_CHUNK
    off_tok = jnp.repeat(off.T, seg, axis=1)
    hit = expert[:, None, :] == jnp.arange(N_EXPERTS, dtype=jnp.int32)[None, :, None]
    slot = (rank + jnp.sum(jnp.where(hit, off_tok[None], 0), axis=1)) * TOKEN_ROWS
    slot = slot.reshape(2, n // seg, seg).transpose(1, 0, 2).reshape(n // seg, 1, 2 * seg)
    return off.reshape(-1), n_full.reshape(-1), rem.reshape(-1), slot


def _rope_tables(pos):
    half = HEAD_DIM // 2
    inv_freq = ROPE_THETA ** (-jnp.arange(half, dtype=F32) / half)
    ang = pos[:, None] * inv_freq[None, :]
    cos = jnp.cos(ang)
    sin = jnp.sin(ang)
    reps = LANES // HEAD_DIM
    cos_t = jnp.tile(jnp.concatenate([cos, cos], axis=-1), (1, reps))
    sin_t = jnp.tile(jnp.concatenate([-sin, sin], axis=-1), (1, reps))
    return cos_t, sin_t


def kernel(x_prompt, x_sample, cache_k_win, cache_v_win, state_conv, state_lru_h, attn_norm_g, w_in, q_norm_g, k_norm_g, attn_sinks, conv_w, conv_b, w_lru_a, b_lru_a, w_lru_i, b_lru_i, lru_lambda, w_br_attn, w_br_rnn, w_out, ffn_norm_g, w_route_group, b_route_group, w_route_expert, b_route_expert, w_exp_gate, w_exp_up, w_exp_down):
    batch, seq, _ = x_prompt.shape
    dec_batch, dec_seq, _ = x_sample.shape
    depth = w_in.shape[0]
    assert depth == 1 and dec_seq == 1
    l = 0

    w_in_f = w_in[l]
    qkg = jnp.concatenate([jnp.tile(q_norm_g[l], N_Q_HEADS), jnp.tile(k_norm_g[l], N_KV_HEADS)])[None, :]
    wcat_f = jnp.concatenate([w_lru_a[l], w_lru_i[l]], axis=-1)
    wa_f, wr_f, wo_f = w_br_attn[l], w_br_rnn[l], w_out[l]
    w_route_f = jnp.concatenate(
        [w_route_expert[l], w_route_group[l],
         jnp.zeros((D_MODEL, LANES - N_EXPERTS - N_GROUPS), F32)], axis=-1)
    wcat = wcat_f.astype(BF16)
    wa_b, wr_b, wo_b, w_route = (w.astype(BF16) for w in (wa_f, wr_f, wo_f, w_route_f))
    b_route = jnp.concatenate(
        [b_route_expert[l], b_route_group[l], jnp.zeros((LANES - N_EXPERTS - N_GROUPS,), F32)])[None, :]
    experts_f = (w_exp_gate[l], w_exp_up[l], w_exp_down[l])
    g1 = attn_norm_g[l][None, :]
    g2 = ffn_norm_g[l][None, :]
    cw, cb = conv_w[l], conv_b[l][None, :]
    b_a, b_i, lam = b_lru_a[l][None, :], b_lru_i[l][None, :], lru_lambda[l][None, :]
    sinks = attn_sinks[l]

    def tail(x, attn, rnn, sga, sgr, tm, seg, precise, experts_b=None):
        wa, wr, wo, wrt = (wa_f, wr_f, wo_f, w_route_f) if precise else (wa_b, wr_b, wo_b, w_route)
        tri = jnp.tril(jnp.ones((tm, tm), BF16), -1)
        outs = _merge(x, attn, rnn, sga, sgr, wa, wr, wo, g2, wrt, b_route, tri, tm, seg, precise,
                      expert_weights=None if experts_b else experts_f)
        x2, xtm, rec, rect, cnt = outs[:5]
        wg, wu, wd = experts_b or outs[5:]
        off, nfull, rem, slot = _plan(rect, cnt, seg)
        y = _moe(off, nfull, rem, slot, xtm, wg, wu, wd, x2, rec, seg, tm, tm)
        return y, (wg, wu, wd)

    xs = x_sample.reshape(dec_batch, D_MODEL)
    cos_s, sin_s = _rope_tables(jnp.full((dec_batch,), PAST_LEN, F32))
    qs, ks, vs, xrs, gys, sgas, sgrs, w_in_b = _proj(xs, g1, w_in_f, cos_s, sin_s, qkg, 1, dec_batch,
                                                     dec_batch, True)

    xp = x_prompt.reshape(batch * seq, D_MODEL)
    cos_p, sin_p = _rope_tables(jnp.arange(seq, dtype=F32))
    q, k, v, rnn, sga, sgr, h_last, conv_tail = _proj(
        xp, g1, w_in_b, cos_p, sin_p, qkg, batch, seq, 512, False,
        rnn_weights=(cw, cb, wcat, b_a, b_i, lam))
    attn = _attn_prompt(q, k, v, sinks, batch, seq)
    y_prompt, experts_b = tail(xp, attn, rnn, sga, sgr, 512, 4096, False)
    y_prompt = y_prompt.reshape(batch, seq, D_MODEL)

    def last_rows(a, rows):
        return a.reshape(batch, seq, a.shape[-1])[:, seq - rows:]

    k_win_p = last_rows(k, WINDOW).reshape(1, batch, WINDOW, N_KV_HEADS, HEAD_DIM)
    v_win_p = last_rows(v, WINDOW).reshape(1, batch, WINDOW, N_KV_HEADS, HEAD_DIM)
    conv_p = conv_tail[None, :, SUBLANES - (CONV_W - 1):, :]
    h_p = h_last[None, :, 0, :]

    ck = cache_k_win[l].reshape(dec_batch, WINDOW, D_KV)
    cv = cache_v_win[l].reshape(dec_batch, WINDOW, D_KV)
    attn_s, k_win_s, v_win_s = _attn_sample(qs, ks, vs, ck, cv, sinks)
    sc = state_conv[l]
    rnn_s, h_s = _rnn_sample(xrs, gys, sc[:, 0], sc[:, 1], sc[:, 2], state_lru_h[l],
                             cw, cb, wcat_f, b_a, b_i, lam)
    y_sample, _ = tail(xs, attn_s, rnn_s, sgas, sgrs, dec_batch, dec_batch, True, experts_b)
    y_sample = y_sample.reshape(dec_batch, 1, D_MODEL)
    conv_s = jnp.stack([sc[:, 1], sc[:, 2], xrs], axis=1)[None]

    return (y_prompt, y_sample, k_win_p, v_win_p, conv_p, h_p,
            k_win_s.reshape(1, dec_batch, WINDOW, N_KV_HEADS, HEAD_DIM),
            v_win_s.reshape(1, dec_batch, WINDOW, N_KV_HEADS, HEAD_DIM),
            conv_s, h_s[None])
```

```python
import functools

import jax
import jax.numpy as jnp
from jax import lax
from jax.experimental import pallas as pl
from jax.experimental.pallas import tpu as pltpu

D_MODEL = 1024
HEAD_DIM = 64
N_Q_HEADS = 8
N_KV_HEADS = 2
Q_PER_KV = N_Q_HEADS // N_KV_HEADS
WINDOW = 128
ATTN_BLOCK = 128
ROPE_THETA = 10000.0
SCALE = HEAD_DIM ** -0.5
NEG_INF = -1e30
D_RNN = 1280
N_RNN_BLOCKS = 10
RNN_BLOCK = D_RNN // N_RNN_BLOCKS
CONV_W = 4
LRU_C = 8.0
N_GROUPS = 4
EXPERTS_PER_GROUP = 8
N_EXPERTS = N_GROUPS * EXPERTS_PER_GROUP
D_EXPERT = 256
PAST_LEN = 16384
EPS = 1e-6
D_Q = N_Q_HEADS * HEAD_DIM
D_KV = N_KV_HEADS * HEAD_DIM
D_IN = D_Q + 2 * D_KV + 2 * D_RNN + 2 * D_MODEL
OFF_K = D_Q
OFF_V = OFF_K + D_KV
OFF_XR = OFF_V + D_KV
OFF_YR = OFF_XR + D_RNN
OFF_GA = OFF_YR + D_RNN
OFF_GR = OFF_GA + D_MODEL

LANES = 128
SUBLANES = 8
VMEM_LIMIT = 56 * 1024 * 1024

F32 = jnp.float32
BF16 = jnp.bfloat16


def _params(*sem):
    return pltpu.CompilerParams(dimension_semantics=sem, vmem_limit_bytes=VMEM_LIMIT)


def _sigmoid(x):
    return 1.0 / (1.0 + jnp.exp(-x))


def _gelu_tanh(x):
    c = 0.7978845608028654
    return 0.5 * x * (1.0 + jnp.tanh(c * (x + 0.044715 * (x * x * x))))


def _full(shape, single_buffer=False):
    index_map = lambda *_: (0,) * len(shape)
    if single_buffer:
        return pl.BlockSpec(shape, index_map, pipeline_mode=pl.Buffered(1))
    return pl.BlockSpec(shape, index_map)


def _mm(a, b, precise, dims=None):
    if precise:
        a, b, prec = a.astype(F32), b.astype(F32), lax.Precision.HIGHEST
    else:
        a, b, prec = a.astype(BF16), b.astype(BF16), None
    if dims is None:
        return jnp.dot(a, b, preferred_element_type=F32, precision=prec)
    return lax.dot_general(a, b, dims, preferred_element_type=F32, precision=prec)


_NT = (((1,), (1,)), ((), ()))


def _proj_kernel(x_ref, g_ref, w_ref, cos_ref, sin_ref, qkg_ref, *rest, precise, fuse_rnn):
    if fuse_rnn:
        rnn_w = rest[:6]
        q_ref, k_ref, v_ref, rnn_ref, sga_ref, sgr_ref, hl_ref, ct_ref = rest[6:14]
        xbuf, a_scr, b_scr, h_scr, hcar = rest[14:]
        pl.when(pl.program_id(1) == 0)(functools.partial(_rnn_start_sequence, xbuf, hcar))
    else:
        q_ref, k_ref, v_ref, xr_ref, gy_ref, sga_ref, sgr_ref, wb_ref = rest
        wb_ref[...] = w_ref[...].astype(BF16)
    x = x_ref[...]
    inv = lax.rsqrt(jnp.mean(x * x, axis=-1, keepdims=True) + EPS)
    xn = x * inv * g_ref[...]
    if not precise:
        xn = xn.astype(BF16)

    def proj(lo, hi):
        return _mm(xn, w_ref[:, lo:hi], precise)

    def qk_heads():
        qk = proj(0, OFF_V)
        tm = qk.shape[0]
        lane = lax.broadcasted_iota(jnp.int32, (tm, LANES), 1)
        lo_head = lane < HEAD_DIM
        first_half = (lane % HEAD_DIM) < (HEAD_DIM // 2)
        cos = cos_ref[...]
        sin = sin_ref[...]
        for g in range(OFF_V // LANES):
            seg = qk[:, g * LANES:(g + 1) * LANES]
            sq = seg * seg
            s_lo = jnp.sum(jnp.where(lo_head, sq, 0.0), axis=-1, keepdims=True)
            s_hi = jnp.sum(jnp.where(lo_head, 0.0, sq), axis=-1, keepdims=True)
            ms = jnp.where(lo_head, s_lo, s_hi) * (1.0 / HEAD_DIM)
            normed = seg * lax.rsqrt(ms + EPS) * qkg_ref[:, g * LANES:(g + 1) * LANES]
            partner = jnp.where(first_half,
                                pltpu.roll(normed, LANES - HEAD_DIM // 2, axis=1),
                                pltpu.roll(normed, HEAD_DIM // 2, axis=1))
            roped = normed * cos + partner * sin
            if g < D_Q // LANES:
                q_ref[:, g * LANES:(g + 1) * LANES] = (roped * SCALE).astype(q_ref.dtype)
            else:
                k_ref[...] = roped

    def values():
        v_ref[...] = proj(OFF_V, OFF_XR)

    def gate(out_ref, off, c0, c1, r0=0, r1=x.shape[0]):
        y = _mm(xn[r0:r1], w_ref[:, off + c0:off + c1], precise)
        out_ref[r0:r1, c0:c1] = _sigmoid(y).astype(out_ref.dtype)

    if not fuse_rnn:
        xr_ref[...] = proj(OFF_XR, OFF_YR)
        gy_ref[...] = _gelu_tanh(proj(OFF_YR, OFF_GA)).astype(gy_ref.dtype)
        qk_heads()
        values()
        gate(sga_ref, OFF_GA, 0, D_MODEL)
        gate(sgr_ref, OFF_GR, 0, D_MODEL)
        return

    slab = RNN_SLAB_BLOCKS * RNN_BLOCK
    half = x.shape[0] // 2
    others = [qk_heads, values]
    for out_ref, off in ((sga_ref, OFF_GA), (sgr_ref, OFF_GR)):
        others += [functools.partial(gate, out_ref, off, c, c + slab, r, r + half)
                   for c in range(0, D_MODEL, slab) for r in (0, half)]
    others = iter(others)
    gy = []
    for i in range(D_RNN // slab):
        c0 = i * slab
        xr = proj(OFF_XR + c0, OFF_XR + c0 + slab)
        gy.append(_gelu_tanh(proj(OFF_YR + c0, OFF_YR + c0 + slab)))
        ct_ref[0, :, c0:c0 + slab] = xr[xr.shape[0] - SUBLANES:, :]
        for _ in _rnn_scan_terms(xr, i * RNN_SLAB_BLOCKS, *rnn_w, xbuf, a_scr, b_scr):
            next(others, lambda: None)()
    for other in others:
        other()
    _rnn_scan_finish(jnp.concatenate(gy, axis=-1), rnn_ref, hl_ref, a_scr, b_scr, h_scr, hcar)


RNN_SLAB_BLOCKS = 2


def _proj(x, g, w_in, cos_t, sin_t, qkg, batch, seq, tm, precise, rnn_weights=None):
    n = batch * seq
    nt = seq // tm
    row = lambda b, t: (b * nt + t, 0)
    per_seq = lambda b, t: (b, 0, 0)
    act = F32 if precise else BF16
    rows_out = lambda width, dtype: (jax.ShapeDtypeStruct((n, width), dtype),
                                     pl.BlockSpec((tm, width), row))
    state_out = (jax.ShapeDtypeStruct((batch, SUBLANES, D_RNN), F32),
                 pl.BlockSpec((1, SUBLANES, D_RNN), per_seq))
    outs = [rows_out(D_Q, act), rows_out(D_KV, F32), rows_out(D_KV, F32)]
    in_specs = [
        pl.BlockSpec((tm, D_MODEL), row),
        _full((1, D_MODEL)),
        _full((D_MODEL, D_IN), single_buffer=True),
        pl.BlockSpec((tm, LANES), lambda b, t: (t, 0)),
        pl.BlockSpec((tm, LANES), lambda b, t: (t, 0)),
        _full((1, OFF_V)),
    ]
    args = [x, g, w_in, cos_t, sin_t, qkg]
    scratch = []
    if rnn_weights is None:
        outs += [rows_out(D_RNN, F32), rows_out(D_RNN, act)]
    else:
        assert not precise
        outs += [rows_out(D_RNN, act)]
        in_specs += [_full(w.shape) for w in rnn_weights]
        args += list(rnn_weights)
        groups = tm // SUBLANES
        scratch = [pltpu.VMEM((tm + SUBLANES, D_RNN), F32)]
        scratch += [pltpu.VMEM((groups, SUBLANES, D_RNN), F32)] * 3
        scratch += [pltpu.VMEM((SUBLANES, D_RNN), F32)]
    outs += [rows_out(D_MODEL, act), rows_out(D_MODEL, act)]
    if rnn_weights is not None:
        outs += [state_out, state_out]
    else:
        assert batch * nt == 1 and w_in.dtype == F32
        outs += [(jax.ShapeDtypeStruct(w_in.shape, BF16), _full(w_in.shape, single_buffer=True))]
    return pl.pallas_call(
        functools.partial(_proj_kernel, precise=precise, fuse_rnn=rnn_weights is not None),
        grid=(batch, nt),
        in_specs=in_specs,
        out_specs=[spec for _, spec in outs],
        out_shape=[shape for shape, _ in outs],
        scratch_shapes=scratch,
        compiler_params=_params("parallel", "arbitrary"),
        name="proj",
    )(*args)


def _softmax_pv(s, sink, v2, precise):
    m = jnp.maximum(jnp.max(s, axis=-1, keepdims=True), sink)
    p = jnp.exp(s - m)
    denom = jnp.sum(p, axis=-1, keepdims=True) + jnp.exp(sink - m)
    return _mm(p, v2, precise) * (1.0 / denom)


def _sink_column(sink_ref, h, rows, rows_per_head):
    r = lax.broadcasted_iota(jnp.int32, (rows, 1), 0) // rows_per_head
    col = jnp.full((rows, 1), sink_ref[h * Q_PER_KV], F32)
    for g in range(1, Q_PER_KV):
        col = jnp.where(r == g, sink_ref[h * Q_PER_KV + g], col)
    return col


ATTN_BLOCKS_PER_STEP = 4


def _attn_prompt_kernel(sink_ref, q_ref, kc_ref, kp_ref, vc_ref, vp_ref, o_ref):
    n = pl.program_id(1)
    blk = ATTN_BLOCK
    k_all = jnp.concatenate([kp_ref[...], kc_ref[...]], axis=0).astype(BF16)
    v_all = jnp.concatenate([vp_ref[...], vc_ref[...]], axis=0).astype(BF16)
    rows = Q_PER_KV * blk
    i = lax.broadcasted_iota(jnp.int32, (rows, 2 * blk), 0) % blk
    j = lax.broadcasted_iota(jnp.int32, (rows, 2 * blk), 1)
    d = j - i
    in_window = (d >= 1) & (d <= WINDOW)
    first_valid = (d >= jnp.where(n > 0, 1, jnp.maximum(1, blk - i))) & (d <= WINDOW)
    def kv_head(sub, h):
        q = q_ref[sub * blk:(sub + 1) * blk, :]
        kh = k_all[sub * blk:(sub + 2) * blk, h * HEAD_DIM:(h + 1) * HEAD_DIM]
        vh = v_all[sub * blk:(sub + 2) * blk, h * HEAD_DIM:(h + 1) * HEAD_DIM]
        valid = first_valid if sub == 0 else in_window
        qs = jnp.concatenate(
            [q[:, (h * Q_PER_KV + g) * HEAD_DIM:(h * Q_PER_KV + g + 1) * HEAD_DIM]
             for g in range(Q_PER_KV)], axis=0)
        s = _mm(qs, kh, False, _NT)
        yield
        s = jnp.where(valid, s, NEG_INF)
        sink = _sink_column(sink_ref, h, rows, blk)
        m = jnp.maximum(jnp.max(s, axis=-1, keepdims=True), sink)
        p = jnp.exp(s - m)
        denom = jnp.sum(p, axis=-1, keepdims=True) + jnp.exp(sink - m)
        yield
        o = _mm(p, vh, False) * (1.0 / denom)
        for g in range(Q_PER_KV):
            c = (h * Q_PER_KV + g) * HEAD_DIM
            o_ref[sub * blk:(sub + 1) * blk, c:c + HEAD_DIM] = (
                o[g * blk:(g + 1) * blk].astype(o_ref.dtype))

    heads = [kv_head(sub, h) for sub in range(ATTN_BLOCKS_PER_STEP) for h in range(N_KV_HEADS)]
    while heads:
        heads = [head for head in heads if next(head, "done") != "done"]


def _attn_prompt(q, k, v, sinks, batch, seq):
    step = ATTN_BLOCKS_PER_STEP * ATTN_BLOCK
    ns = seq // step
    cur = lambda b, n: (b * ns + n, 0)
    prev = lambda b, n: (jnp.maximum((b * ns + n) * ATTN_BLOCKS_PER_STEP - 1, 0), 0)
    return pl.pallas_call(
        _attn_prompt_kernel,
        grid=(batch, ns),
        in_specs=[
            pl.BlockSpec(memory_space=pltpu.SMEM),
            pl.BlockSpec((step, D_Q), cur),
            pl.BlockSpec((step, D_KV), cur),
            pl.BlockSpec((ATTN_BLOCK, D_KV), prev),
            pl.BlockSpec((step, D_KV), cur),
            pl.BlockSpec((ATTN_BLOCK, D_KV), prev),
        ],
        out_specs=pl.BlockSpec((step, D_Q), cur),
        out_shape=jax.ShapeDtypeStruct((batch * seq, D_Q), BF16),
        compiler_params=_params("parallel", "parallel"),
        name="attn_prompt",
    )(sinks, q, k, k, v, v)


SAMPLE_BT = 8


def _attn_sample_kernel(sink_ref, q_ref, kn_ref, vn_ref, kc_ref, vc_ref, o_ref, ko_ref, vo_ref):
    bt = SAMPLE_BT
    w = lax.broadcasted_iota(jnp.int32, (bt, WINDOW, D_KV), 1)

    def shifted(cache_ref, new_ref):
        rolled = pltpu.roll(cache_ref[...], WINDOW - 1, axis=1)
        return jnp.where(w == WINDOW - 1, new_ref[...][:, None, :], rolled)

    k_win = shifted(kc_ref, kn_ref)
    v_win = shifted(vc_ref, vn_ref)
    ko_ref[...] = k_win
    vo_ref[...] = v_win
    k2 = k_win.reshape(bt * WINDOW, D_KV)
    v2 = v_win.reshape(bt * WINDOW, D_KV)
    q = q_ref[...]
    rows = Q_PER_KV * bt
    rb = lax.broadcasted_iota(jnp.int32, (rows, bt * WINDOW), 0) % bt
    cb = lax.broadcasted_iota(jnp.int32, (rows, bt * WINDOW), 1) // WINDOW
    valid = rb == cb
    for h in range(N_KV_HEADS):
        kh = k2[:, h * HEAD_DIM:(h + 1) * HEAD_DIM]
        vh = v2[:, h * HEAD_DIM:(h + 1) * HEAD_DIM]
        qs = jnp.concatenate(
            [q[:, (h * Q_PER_KV + g) * HEAD_DIM:(h * Q_PER_KV + g + 1) * HEAD_DIM]
             for g in range(Q_PER_KV)], axis=0)
        s = _mm(qs, kh, True, _NT)
        s = jnp.where(valid, s, NEG_INF)
        o = _softmax_pv(s, _sink_column(sink_ref, h, rows, bt), vh, True)
        for g in range(Q_PER_KV):
            c = (h * Q_PER_KV + g) * HEAD_DIM
            o_ref[:, c:c + HEAD_DIM] = o[g * bt:(g + 1) * bt].astype(o_ref.dtype)


def _attn_sample(q, k_new, v_new, cache_k, cache_v, sinks):
    nbatch = q.shape[0]
    bt = SAMPLE_BT
    row = lambda i: (i, 0)
    win = lambda i: (i, 0, 0)
    return pl.pallas_call(
        _attn_sample_kernel,
        grid=(nbatch // bt,),
        in_specs=[
            pl.BlockSpec(memory_space=pltpu.SMEM),
            pl.BlockSpec((bt, D_Q), row),
            pl.BlockSpec((bt, D_KV), row),
            pl.BlockSpec((bt, D_KV), row),
            pl.BlockSpec((bt, WINDOW, D_KV), win),
            pl.BlockSpec((bt, WINDOW, D_KV), win),
        ],
        out_specs=[
            pl.BlockSpec((bt, D_Q), row),
            pl.BlockSpec((bt, WINDOW, D_KV), win),
            pl.BlockSpec((bt, WINDOW, D_KV), win),
        ],
        out_shape=(
            jax.ShapeDtypeStruct((nbatch, D_Q), F32),
            jax.ShapeDtypeStruct((nbatch, WINDOW, D_KV), F32),
            jax.ShapeDtypeStruct((nbatch, WINDOW, D_KV), F32),
        ),
        compiler_params=_params("parallel"),
        name="attn_sample",
    )(sinks, q, k_new, v_new, cache_k, cache_v)


def _lru_terms(xc, wcat_ref, ba_ref, bi_ref, lam_ref, precise, block0=0):
    cols = slice(block0 * RNN_BLOCK, block0 * RNN_BLOCK + xc.shape[1])
    xcb = xc if precise else xc.astype(BF16)
    ya, yi = [], []
    for n in range(xc.shape[1] // RNN_BLOCK):
        y = _mm(xcb[:, n * RNN_BLOCK:(n + 1) * RNN_BLOCK], wcat_ref[block0 + n], precise)
        ya.append(y[:, :RNN_BLOCK])
        yi.append(y[:, RNN_BLOCK:])
    r = _sigmoid(jnp.concatenate(ya, axis=-1) + ba_ref[:, cols])
    gate_i = _sigmoid(jnp.concatenate(yi, axis=-1) + bi_ref[:, cols])
    neg_lam = -lam_ref[:, cols]
    softplus = jnp.maximum(neg_lam, 0.0) + jnp.log1p(jnp.exp(-jnp.abs(neg_lam)))
    log_a = (-LRU_C * softplus) * r
    a = jnp.exp(log_a)
    m = 1.0 - a * a
    b = jnp.where(m > 0.0, m * lax.rsqrt(m), 0.0) * (gate_i * xc)
    return a, b


def _rnn_start_sequence(xbuf, hcar):
    xbuf[0:SUBLANES, :] = jnp.zeros((SUBLANES, D_RNN), F32)
    hcar[...] = jnp.zeros((SUBLANES, D_RNN), F32)


def _rnn_scan_terms(x, block0, cw_ref, cb_ref, wcat_ref, ba_ref, bi_ref, lam_ref,
                    xbuf, a_scr, b_scr):
    tt, width = x.shape
    groups = tt // SUBLANES
    cols = slice(block0 * RNN_BLOCK, block0 * RNN_BLOCK + width)
    xbuf[SUBLANES:, cols] = x
    xc = cb_ref[:, cols] + cw_ref[CONV_W - 1:CONV_W, cols] * x
    for j in range(CONV_W - 1):
        s = CONV_W - 1 - j
        xc = xc + cw_ref[j:j + 1, cols] * xbuf[SUBLANES - s:SUBLANES - s + tt, cols]
    xbuf[0:SUBLANES, cols] = x[tt - SUBLANES:, :]
    yield
    a, b = _lru_terms(xc, wcat_ref, ba_ref, bi_ref, lam_ref, False, block0)
    yield
    a = a.reshape(groups, SUBLANES, width)
    b = b.reshape(groups, SUBLANES, width)
    step = lax.broadcasted_iota(jnp.int32, (groups, SUBLANES, width), 1)
    k = 1
    while k < SUBLANES:
        keep = step >= k
        a_sh = jnp.where(keep, pltpu.roll(a, k, axis=1), 1.0)
        b_sh = jnp.where(keep, pltpu.roll(b, k, axis=1), 0.0)
        b = a * b_sh + b
        a = a * a_sh
        k *= 2
        if k < SUBLANES:
            yield
    a_scr[:, :, cols] = a
    b_scr[:, :, cols] = b


def _rnn_scan_finish(gy, o_ref, hl_ref, a_scr, b_scr, h_scr, hcar):
    groups = a_scr.shape[0]
    tt = groups * SUBLANES

    def chain(g, h_in):
        h = a_scr[g] * h_in + b_scr[g]
        h_scr[g] = h
        return jnp.broadcast_to(h[SUBLANES - 1:SUBLANES, :], (SUBLANES, D_RNN))

    h_last = lax.fori_loop(0, groups, chain, hcar[...], unroll=True)
    hcar[...] = h_last
    hl_ref[0] = h_last
    h = h_scr[...].reshape(tt, D_RNN)
    o_ref[...] = (h * gy).astype(o_ref.dtype)


def _rnn_sample_kernel(xr_ref, gy_ref, s0_ref, s1_ref, s2_ref, h_ref, cw_ref, cb_ref,
                       wcat_ref, ba_ref, bi_ref, lam_ref, o_ref, hn_ref):
    x = xr_ref[...]
    xc = (cb_ref[...] + cw_ref[0:1, :] * s0_ref[...] + cw_ref[1:2, :] * s1_ref[...]
          + cw_ref[2:3, :] * s2_ref[...] + cw_ref[3:4, :] * x)
    a, b = _lru_terms(xc, wcat_ref, ba_ref, bi_ref, lam_ref, True)
    h = a * h_ref[...] + b
    hn_ref[...] = h
    o_ref[...] = (h * gy_ref[...].astype(F32)).astype(o_ref.dtype)


def _rnn_sample(xr, gy, s0, s1, s2, h_prev, conv_w, conv_b, wcat, b_a, b_i, lam):
    n = xr.shape[0]
    act = _full((n, D_RNN))
    return pl.pallas_call(
        _rnn_sample_kernel,
        grid=(1,),
        in_specs=[act, act, act, act, act, act,
                  _full((CONV_W, D_RNN)), _full((1, D_RNN)),
                  _full((N_RNN_BLOCKS, RNN_BLOCK, 2 * RNN_BLOCK)),
                  _full((1, D_RNN)), _full((1, D_RNN)), _full((1, D_RNN))],
        out_specs=[act, act],
        out_shape=(jax.ShapeDtypeStruct((n, D_RNN), F32),
                   jax.ShapeDtypeStruct((n, D_RNN), F32)),
        compiler_params=_params("arbitrary"),
        name="rnn_sample",
    )(xr, gy, s0, s1, s2, h_prev, conv_w, conv_b, wcat, b_a, b_i, lam)


def _merge_kernel(x_ref, at_ref, rn_ref, sga_ref, sgr_ref, wa_ref, wr_ref, wo_ref, g_ref,
                  wrt_ref, brt_ref, tri_ref, *rest, precise, tiles_per_seg, cast_experts):
    if cast_experts:
        wg_ref, wu_ref, wd_ref, *rest = rest
        x2_ref, xtm_ref, rec_ref, rect_ref, cnt_ref, wg_o, wu_o, wd_o, cnt_scr = rest
    else:
        x2_ref, xtm_ref, rec_ref, rect_ref, cnt_ref, cnt_scr = rest

    @pl.when(pl.program_id(0) % tiles_per_seg == 0)
    def _():
        cnt_scr[...] = jnp.zeros_like(cnt_scr)

    tm = x_ref.shape[0]
    parts = MERGE_PARTS if tm % (MERGE_PARTS * LANES) == 0 else 1
    part = tm // parts
    state = {"counts": cnt_scr[...]}
    pending = [
        _merge_rows(slice(i * part, (i + 1) * part), state, x_ref, at_ref, rn_ref, sga_ref, sgr_ref,
                    wa_ref, wr_ref, wo_ref, g_ref, wrt_ref, brt_ref, tri_ref, x2_ref, xtm_ref,
                    rec_ref, rect_ref, precise)
        for i in range(parts)]
    active, step = [], 0
    while pending or active:
        if pending and step % MERGE_STAGGER == 0:
            active.append(pending.pop(0))
        for part_pieces in list(active):
            if next(part_pieces, "done") == "done":
                active.remove(part_pieces)
        step += 1
    cnt_scr[...] = state["counts"]
    cnt_ref[0] = state["counts"]
    if cast_experts:
        wg_o[...] = wg_ref[...].astype(BF16)
        wu_o[...] = wu_ref[...].astype(BF16)
        wd_o[...] = wd_ref[...].astype(BF16)


MERGE_PARTS = 2
MERGE_STAGGER = 3


def _merge_rows(rows, state, x_ref, at_ref, rn_ref, sga_ref, sgr_ref, wa_ref, wr_ref, wo_ref, g_ref,
                wrt_ref, brt_ref, tri_ref, x2_ref, xtm_ref, rec_ref, rect_ref, precise):
    ya = _mm(at_ref[rows, :], wa_ref[...], precise)
    yield
    yr = _mm(rn_ref[rows, :], wr_ref[...], precise)
    yield
    merged = sga_ref[rows, :].astype(F32) * ya + sgr_ref[rows, :].astype(F32) * yr
    yield
    x2 = x_ref[rows, :] + _mm(merged, wo_ref[...], precise)
    x2_ref[rows, :] = x2
    yield
    inv = lax.rsqrt(jnp.mean(x2 * x2, axis=-1, keepdims=True) + EPS)
    xn = x2 * inv * g_ref[...]
    for c in range(TOKEN_ROWS):
        xtm_ref[pl.ds(rows.start * TOKEN_ROWS + c, rows.stop - rows.start, stride=TOKEN_ROWS), :] = (
            xn[:, c * LANES:(c + 1) * LANES])
    yield

    logits = _mm(xn, wrt_ref[...], precise) + brt_ref[...]
    yield
    tm = logits.shape[0]
    lane_i = lax.broadcasted_iota(jnp.int32, (tm, LANES), 1)
    lane = lane_i.astype(F32)
    big = float(LANES)
    is_grp = (lane_i >= N_EXPERTS) & (lane_i < N_EXPERTS + N_GROUPS)
    gl = jnp.where(is_grp, logits, NEG_INF)
    gmax = jnp.max(gl, axis=-1, keepdims=True)
    g_idx = jnp.min(jnp.where(gl == gmax, lane, big), axis=-1, keepdims=True) - N_EXPERTS
    p_g = 1.0 / jnp.sum(jnp.exp(gl - gmax), axis=-1, keepdims=True)
    in_grp = (lane_i // EXPERTS_PER_GROUP).astype(F32) == g_idx
    el = jnp.where(in_grp, logits, NEG_INF)
    v1 = jnp.max(el, axis=-1, keepdims=True)
    i1 = jnp.min(jnp.where(el == v1, lane, big), axis=-1, keepdims=True)
    el2 = jnp.where(lane == i1, NEG_INF, el)
    v2 = jnp.max(el2, axis=-1, keepdims=True)
    i2 = jnp.min(jnp.where(el2 == v2, lane, big), axis=-1, keepdims=True)
    e2 = jnp.exp(v2 - v1)
    w1 = p_g / (1.0 + e2)
    w2 = p_g * e2 / (1.0 + e2)
    yield

    hit = jnp.where(lane == i1, 1.0, jnp.where(lane == i2, 1.0, 0.0))
    counts = state["counts"]
    state["counts"] = counts + jnp.sum(hit, axis=0, keepdims=True)
    before = (jnp.dot(tri_ref[0:tm, 0:tm], hit.astype(BF16), preferred_element_type=F32)
              + counts[0:1, :])
    yield
    r1 = jnp.sum(jnp.where(lane == i1, before, 0.0), axis=-1, keepdims=True)
    r2 = jnp.sum(jnp.where(lane == i2, before, 0.0), axis=-1, keepdims=True)
    rec = jnp.where(lane == REC_W2, w2, 0.0)
    for field, val in ((REC_W1, w1), (REC_R2, r2), (REC_R1, r1), (REC_E2, i2), (REC_E1, i1)):
        rec = jnp.where(lane == field, val, rec)
    rec_ref[rows, :] = rec
    rect_ref[:, rows] = rec.T


REC_E1, REC_E2, REC_R1, REC_R2, REC_W1, REC_W2 = range(6)
TOKEN_ROWS = D_MODEL // LANES


def _merge(x, attn, rnn, sga, sgr, wa, wr, wo, g2, w_route, b_route, tri, tm, seg, precise,
           expert_weights=None):
    n = x.shape[0]
    steps = n // tm
    tiles_per_seg = seg // tm
    row = lambda i: (i, 0)
    in_specs = [
        pl.BlockSpec((tm, D_MODEL), row),
        pl.BlockSpec((tm, D_Q), row),
        pl.BlockSpec((tm, D_RNN), row),
        pl.BlockSpec((tm, D_MODEL), row),
        pl.BlockSpec((tm, D_MODEL), row),
        _full((D_Q, D_MODEL)),
        _full((D_RNN, D_MODEL)),
        _full((D_MODEL, D_MODEL)),
        _full((1, D_MODEL)),
        _full((D_MODEL, LANES)),
        _full((1, LANES)),
        _full((tm, tm)),
    ]
    out_specs = [
        pl.BlockSpec((tm, D_MODEL), row),
        pl.BlockSpec((tm * TOKEN_ROWS, LANES), row),
        pl.BlockSpec((tm, LANES), row),
        pl.BlockSpec((LANES, tm), lambda i: (0, i)),
        pl.BlockSpec((1, SUBLANES, LANES), lambda i: (i // tiles_per_seg, 0, 0)),
    ]
    out_shape = [
        jax.ShapeDtypeStruct((n, D_MODEL), F32),
        jax.ShapeDtypeStruct((n * TOKEN_ROWS, LANES), F32),
        jax.ShapeDtypeStruct((n, LANES), F32),
        jax.ShapeDtypeStruct((LANES, n), F32),
        jax.ShapeDtypeStruct((n // seg, SUBLANES, LANES), F32),
    ]
    args = [x, attn, rnn, sga, sgr, wa, wr, wo, g2, w_route, b_route, tri]
    if expert_weights is not None:
        assert steps == N_EXPERTS
        for w in expert_weights:
            spec = pl.BlockSpec((1,) + w.shape[1:], lambda i: (i, 0, 0))
            in_specs.append(spec)
            out_specs.append(spec)
            out_shape.append(jax.ShapeDtypeStruct(w.shape, BF16))
            args.append(w)
    return pl.pallas_call(
        functools.partial(_merge_kernel, precise=precise, tiles_per_seg=tiles_per_seg,
                          cast_experts=expert_weights is not None),
        grid=(steps,),
        in_specs=in_specs,
        out_specs=out_specs,
        out_shape=out_shape,
        scratch_shapes=[pltpu.VMEM((SUBLANES, LANES), F32)],
        compiler_params=_params("arbitrary"),
        name="merge",
    )(*args)


MOE_CHUNK = 256
MOE_TAIL = 128
MOE_EXPERTS_PER_STEP = 2
MOE_VMEM_LIMIT = 60 * 1024 * 1024


def _seg_rows(seg):
    return 2 * seg + N_EXPERTS * SUBLANES + MOE_CHUNK


def _token_rows(i):
    return pl.ds(pl.multiple_of(i * TOKEN_ROWS, TOKEN_ROWS), TOKEN_ROWS)


def _sorted_rows(first_row):
    return pl.ds(pl.multiple_of(first_row, TOKEN_ROWS), TOKEN_ROWS)


def _moe_kernel(off_ref, nfull_ref, rem_ref, slot_ref, xtm_ref, wg_ref, wu_ref, wd_ref,
                x2_ref, rec_ref, o_ref, buf, g1, g2, *, seg, td, tc):
    s = pl.program_id(0)
    p = pl.program_id(1)
    n_disp = seg // td
    n_exp = N_EXPERTS // MOE_EXPERTS_PER_STEP

    @pl.when((s == 0) & (p == 0))
    def _():
        buf[...] = jnp.zeros_like(buf)

    @pl.when(p < n_disp)
    def _():
        def dispatch(g, carry):
            for j in range(SUBLANES):
                t = g * SUBLANES + j
                row = xtm_ref[_token_rows(t), :]
                for k in range(2):
                    buf[_sorted_rows(slot_ref[0, 0, k * seg + p * td + t]), :] = row
            return carry

        lax.fori_loop(0, td // SUBLANES, dispatch, 0)

    def run_chunk(e, row0, rows, valid):
        r0 = pl.multiple_of(row0 * TOKEN_ROWS, SUBLANES * TOKEN_ROWS)
        xf = [buf[pl.ds(r0 + j, rows, stride=TOKEN_ROWS), :] for j in range(TOKEN_ROWS)]
        x = jnp.concatenate(xf, axis=-1).astype(BF16)
        hg = jnp.dot(x, wg_ref[e], preferred_element_type=F32)
        hu = jnp.dot(x, wu_ref[e], preferred_element_type=F32)
        h = (hg * _sigmoid(hg)) * hu
        y = jnp.dot(h.astype(BF16), wd_ref[e], preferred_element_type=F32)
        if valid is not None:
            mine = lax.broadcasted_iota(jnp.int32, (rows, LANES), 0) < valid
        for j in range(TOKEN_ROWS):
            yj = y[:, j * LANES:(j + 1) * LANES]
            if valid is not None:
                yj = jnp.where(mine, yj, xf[j])
            buf[pl.ds(r0 + j, rows, stride=TOKEN_ROWS), :] = yj

    def run_expert(e):
        idx = s * N_EXPERTS + (p - n_disp) * MOE_EXPERTS_PER_STEP + e
        base = off_ref[idx]
        n_full = nfull_ref[idx]
        rem = rem_ref[idx]

        def chunk(c, carry):
            run_chunk(e, base + c * MOE_CHUNK, MOE_CHUNK, None)
            return carry

        lax.fori_loop(0, n_full, chunk, 0)
        last = base + n_full * MOE_CHUNK
        for units in range(1, MOE_CHUNK // MOE_TAIL + 2):
            @pl.when((rem > (units - 1) * MOE_TAIL) & (rem <= units * MOE_TAIL))
            def _():
                run_chunk(e, last, units * MOE_TAIL, rem)

    @pl.when((p >= n_disp) & (p < n_disp + n_exp))
    def _():
        for e in range(MOE_EXPERTS_PER_STEP):
            run_expert(e)

    @pl.when(p >= n_disp + n_exp)
    def _():
        t0 = (p - n_disp - n_exp) * tc

        def gather(g, carry):
            for j in range(SUBLANES):
                t = g * SUBLANES + j
                g1[_token_rows(t), :] = buf[_sorted_rows(slot_ref[0, 0, t0 + t]), :]
                g2[_token_rows(t), :] = buf[_sorted_rows(slot_ref[0, 0, seg + t0 + t]), :]
            return carry

        lax.fori_loop(0, tc // SUBLANES, gather, 0)
        rec = rec_ref[...]
        lane = lax.broadcasted_iota(jnp.int32, rec.shape, 1)
        w1 = jnp.broadcast_to(
            jnp.sum(jnp.where(lane == REC_W1, rec, 0.0), axis=-1, keepdims=True), rec.shape)
        w2 = jnp.broadcast_to(
            jnp.sum(jnp.where(lane == REC_W2, rec, 0.0), axis=-1, keepdims=True), rec.shape)
        for j in range(TOKEN_ROWS):
            cols = slice(j * LANES, (j + 1) * LANES)
            o_ref[:, cols] = (x2_ref[:, cols] + w1 * g1[pl.ds(j, tc, stride=TOKEN_ROWS), :]
                              + w2 * g2[pl.ds(j, tc, stride=TOKEN_ROWS), :])


def _moe(off, nfull, rem, slot, xtm, wg, wu, wd, x2, rec, seg, td, tc):
    n = x2.shape[0]
    n_seg = n // seg
    n_disp, n_comb = seg // td, seg // tc
    per_step = MOE_EXPERTS_PER_STEP
    n_exp = N_EXPERTS // per_step
    rows = _seg_rows(seg) * TOKEN_ROWS
    disp_tile = lambda s, p, *_: (s * n_disp + jnp.minimum(p, n_disp - 1), 0)
    expert = lambda s, p, *_: (jnp.clip(p - n_disp, 0, n_exp - 1), 0, 0)
    comb_tile = lambda s, p, *_: (s * n_comb + jnp.clip(p - n_disp - n_exp, 0, n_comb - 1), 0)
    grid_spec = pltpu.PrefetchScalarGridSpec(
        num_scalar_prefetch=3,
        grid=(n_seg, n_disp + n_exp + n_comb),
        in_specs=[
            pl.BlockSpec((1, 1, 2 * seg), lambda s, p, *_: (s, 0, 0), memory_space=pltpu.SMEM),
            pl.BlockSpec((td * TOKEN_ROWS, LANES), disp_tile),
            pl.BlockSpec((per_step, D_MODEL, D_EXPERT), expert),
            pl.BlockSpec((per_step, D_MODEL, D_EXPERT), expert),
            pl.BlockSpec((per_step, D_EXPERT, D_MODEL), expert),
            pl.BlockSpec((tc, D_MODEL), comb_tile),
            pl.BlockSpec((tc, LANES), comb_tile),
        ],
        out_specs=pl.BlockSpec((tc, D_MODEL), comb_tile),
        scratch_shapes=[pltpu.VMEM((rows, LANES), F32),
                        pltpu.VMEM((tc * TOKEN_ROWS, LANES), F32),
                        pltpu.VMEM((tc * TOKEN_ROWS, LANES), F32)],
    )
    return pl.pallas_call(
        functools.partial(_moe_kernel, seg=seg, td=td, tc=tc),
        grid_spec=grid_spec,
        out_shape=jax.ShapeDtypeStruct((n, D_MODEL), F32),
        compiler_params=pltpu.CompilerParams(
            dimension_semantics=("arbitrary", "arbitrary"), vmem_limit_bytes=MOE_VMEM_LIMIT),
        name="moe",
    )(off, nfull, rem, slot, xtm, wg, wu, wd, x2, rec)


def _plan(rect, cnt, seg):
    n = rect.shape[1]
    expert = rect[REC_E1:REC_E2 + 1].astype(jnp.int32)
    rank = rect[REC_R1:REC_R2 + 1].astype(jnp.int32)
    counts = cnt[:, 0, :N_EXPERTS].astype(jnp.int32)
    padded = (counts + SUBLANES - 1) // SUBLANES * SUBLANES
    off = jnp.cumsum(padded, axis=1) - padded
    n_full = counts // MOE_CHUNK
    rem = counts - n_full * MOE_CHUNK
    join = (rem > 0) & (rem <= MOE_TAIL) & (n_full > 0)
    n_full = n_full - join
    rem = rem + join * MOE_CHUNK
    off_tok = jnp.repeat(off.T, seg, axis=1)
    hit = expert[:, None, :] == jnp.arange(N_EXPERTS, dtype=jnp.int32)[None, :, None]
    slot = (rank + jnp.sum(jnp.where(hit, off_tok[None], 0), axis=1)) * TOKEN_ROWS
    slot = slot.reshape(2, n // seg, seg).transpose(1, 0, 2).reshape(n // seg, 1, 2 * seg)
    return off.reshape(-1), n_full.reshape(-1), rem.reshape(-1), slot


def _rope_tables(pos):
    half = HEAD_DIM // 2
    inv_freq = ROPE_THETA ** (-jnp.arange(half, dtype=F32) / half)
    ang = pos[:, None] * inv_freq[None, :]
    cos = jnp.cos(ang)
    sin = jnp.sin(ang)
    reps = LANES // HEAD_DIM
    cos_t = jnp.tile(jnp.concatenate([cos, cos], axis=-1), (1, reps))
    sin_t = jnp.tile(jnp.concatenate([-sin, sin], axis=-1), (1, reps))
    return cos_t, sin_t


def kernel(x_prompt, x_sample, cache_k_win, cache_v_win, state_conv, state_lru_h, attn_norm_g, w_in, q_norm_g, k_norm_g, attn_sinks, conv_w, conv_b, w_lru_a, b_lru_a, w_lru_i, b_lru_i, lru_lambda, w_br_attn, w_br_rnn, w_out, ffn_norm_g, w_route_group, b_route_group, w_route_expert, b_route_expert, w_exp_gate, w_exp_up, w_exp_down):
    batch, seq, _ = x_prompt.shape
    dec_batch, dec_seq, _ = x_sample.shape
    depth = w_in.shape[0]
    assert depth == 1 and dec_seq == 1
    l = 0

    w_in_f = w_in[l]
    qkg = jnp.concatenate([jnp.tile(q_norm_g[l], N_Q_HEADS), jnp.tile(k_norm_g[l], N_KV_HEADS)])[None, :]
    wcat_f = jnp.concatenate([w_lru_a[l], w_lru_i[l]], axis=-1)
    wa_f, wr_f, wo_f = w_br_attn[l], w_br_rnn[l], w_out[l]
    w_route_f = jnp.concatenate(
        [w_route_expert[l], w_route_group[l],
         jnp.zeros((D_MODEL, LANES - N_EXPERTS - N_GROUPS), F32)], axis=-1)
    wcat = wcat_f.astype(BF16)
    wa_b, wr_b, wo_b, w_route = (w.astype(BF16) for w in (wa_f, wr_f, wo_f, w_route_f))
    b_route = jnp.concatenate(
        [b_route_expert[l], b_route_group[l], jnp.zeros((LANES - N_EXPERTS - N_GROUPS,), F32)])[None, :]
    experts_f = (w_exp_gate[l], w_exp_up[l], w_exp_down[l])
    g1 = attn_norm_g[l][None, :]
    g2 = ffn_norm_g[l][None, :]
    cw, cb = conv_w[l], conv_b[l][None, :]
    b_a, b_i, lam = b_lru_a[l][None, :], b_lru_i[l][None, :], lru_lambda[l][None, :]
    sinks = attn_sinks[l]

    def tail(x, attn, rnn, sga, sgr, tm, seg, precise, experts_b=None):
        wa, wr, wo, wrt = (wa_f, wr_f, wo_f, w_route_f) if precise else (wa_b, wr_b, wo_b, w_route)
        tri = jnp.tril(jnp.ones((tm, tm), BF16), -1)
        outs = _merge(x, attn, rnn, sga, sgr, wa, wr, wo, g2, wrt, b_route, tri, tm, seg, precise,
                      expert_weights=None if experts_b else experts_f)
        x2, xtm, rec, rect, cnt = outs[:5]
        wg, wu, wd = experts_b or outs[5:]
        off, nfull, rem, slot = _plan(rect, cnt, seg)
        y = _moe(off, nfull, rem, slot, xtm, wg, wu, wd, x2, rec, seg, tm, tm)
        return y, (wg, wu, wd)

    xs = x_sample.reshape(dec_batch, D_MODEL)
    cos_s, sin_s = _rope_tables(jnp.full((dec_batch,), PAST_LEN, F32))
    qs, ks, vs, xrs, gys, sgas, sgrs, w_in_b = _proj(xs, g1, w_in_f, cos_s, sin_s, qkg, 1, dec_batch,
                                                     dec_batch, True)

    xp = x_prompt.reshape(batch * seq, D_MODEL)
    cos_p, sin_p = _rope_tables(jnp.arange(seq, dtype=F32))
    q, k, v, rnn, sga, sgr, h_last, conv_tail = _proj(
        xp, g1, w_in_b, cos_p, sin_p, qkg, batch, seq, 512, False,
        rnn_weights=(cw, cb, wcat, b_a, b_i, lam))
    attn = _attn_prompt(q, k, v, sinks, batch, seq)
    y_prompt, experts_b = tail(xp, attn, rnn, sga, sgr, 512, 4096, False)
    y_prompt = y_prompt.reshape(batch, seq, D_MODEL)

    def last_rows(a, rows):
        return a.reshape(batch, seq, a.shape[-1])[:, seq - rows:]

    k_win_p = last_rows(k, WINDOW).reshape(1, batch, WINDOW, N_KV_HEADS, HEAD_DIM)
    v_win_p = last_rows(v, WINDOW).reshape(1, batch, WINDOW, N_KV_HEADS, HEAD_DIM)
    conv_p = conv_tail[None, :, SUBLANES - (CONV_W - 1):, :]
    h_p = h_last[None, :, 0, :]

    ck = cache_k_win[l].reshape(dec_batch, WINDOW, D_KV)
    cv = cache_v_win[l].reshape(dec_batch, WINDOW, D_KV)
    attn_s, k_win_s, v_win_s = _attn_sample(qs, ks, vs, ck, cv, sinks)
    sc = state_conv[l]
    rnn_s, h_s = _rnn_sample(xrs, gys, sc[:, 0], sc[:, 1], sc[:, 2], state_lru_h[l],
                             cw, cb, wcat_f, b_a, b_i, lam)
    y_sample, _ = tail(xs, attn_s, rnn_s, sgas, sgrs, dec_batch, dec_batch, True, experts_b)
    y_sample = y_sample.reshape(dec_batch, 1, D_MODEL)
    conv_s = jnp.stack([sc[:, 1], sc[:, 2], xrs], axis=1)[None]

    return (y_prompt, y_sample, k_win_p, v_win_p, conv_p, h_p,
            k_win_s.reshape(1, dec_batch, WINDOW, N_KV_HEADS, HEAD_DIM),
            v_win_s.reshape(1, dec_batch, WINDOW, N_KV_HEADS, HEAD_DIM),
            conv_s, h_s[None])
```

```python
import functools

import jax
import jax.numpy as jnp
from jax import lax
from jax.experimental import pallas as pl
from jax.experimental.pallas import tpu as pltpu

D_MODEL = 1024
HEAD_DIM = 64
N_Q_HEADS = 8
N_KV_HEADS = 2
Q_PER_KV = N_Q_HEADS // N_KV_HEADS
WINDOW = 128
ATTN_BLOCK = 128
ROPE_THETA = 10000.0
SCALE = HEAD_DIM ** -0.5
NEG_INF = -1e30
D_RNN = 1280
N_RNN_BLOCKS = 10
RNN_BLOCK = D_RNN // N_RNN_BLOCKS
CONV_W = 4
LRU_C = 8.0
N_GROUPS = 4
EXPERTS_PER_GROUP = 8
N_EXPERTS = N_GROUPS * EXPERTS_PER_GROUP
D_EXPERT = 256
PAST_LEN = 16384
EPS = 1e-6
D_Q = N_Q_HEADS * HEAD_DIM
D_KV = N_KV_HEADS * HEAD_DIM
D_IN = D_Q + 2 * D_KV + 2 * D_RNN + 2 * D_MODEL
OFF_K = D_Q
OFF_V = OFF_K + D_KV
OFF_XR = OFF_V + D_KV
OFF_YR = OFF_XR + D_RNN
OFF_GA = OFF_YR + D_RNN
OFF_GR = OFF_GA + D_MODEL

LANES = 128
SUBLANES = 8
VMEM_LIMIT = 56 * 1024 * 1024

F32 = jnp.float32
BF16 = jnp.bfloat16


def _params(*sem):
    return pltpu.CompilerParams(dimension_semantics=sem, vmem_limit_bytes=VMEM_LIMIT)


def _sigmoid(x):
    return 1.0 / (1.0 + jnp.exp(-x))


def _gelu_tanh(x):
    c = 0.7978845608028654
    half_x = 0.5 * x
    return half_x + half_x * jnp.tanh(x * (c + (c * 0.044715) * (x * x)))


def _full(shape, single_buffer=False):
    index_map = lambda *_: (0,) * len(shape)
    if single_buffer:
        return pl.BlockSpec(shape, index_map, pipeline_mode=pl.Buffered(1))
    return pl.BlockSpec(shape, index_map)


def _mm(a, b, precise, dims=None):
    if precise:
        a, b, prec = a.astype(F32), b.astype(F32), lax.Precision.HIGHEST
    else:
        a, b, prec = a.astype(BF16), b.astype(BF16), None
    if dims is None:
        return jnp.dot(a, b, preferred_element_type=F32, precision=prec)
    return lax.dot_general(a, b, dims, preferred_element_type=F32, precision=prec)


_NT = (((1,), (1,)), ((), ()))


def _proj_kernel(x_ref, g_ref, w_ref, cos_ref, sin_ref, qkg_ref, *rest, precise, fuse_rnn):
    if fuse_rnn:
        rnn_w = rest[:6]
        q_ref, k_ref, v_ref, rnn_ref, sga_ref, sgr_ref, hl_ref, ct_ref = rest[6:14]
        xbuf, a_scr, b_scr, h_scr, hcar = rest[14:]
        pl.when(pl.program_id(1) == 0)(functools.partial(_rnn_start_sequence, xbuf, hcar))
    else:
        q_ref, k_ref, v_ref, xr_ref, gy_ref, sga_ref, sgr_ref, wb_ref = rest
        wb_ref[...] = w_ref[...].astype(BF16)
    x = x_ref[...]
    inv = lax.rsqrt(jnp.mean(x * x, axis=-1, keepdims=True) + EPS)
    xn = x * inv * g_ref[...]
    if not precise:
        xn = xn.astype(BF16)

    def proj(lo, hi):
        return _mm(xn, w_ref[:, lo:hi], precise)

    def qk_heads():
        qk = proj(0, OFF_V)
        tm = qk.shape[0]
        lane = lax.broadcasted_iota(jnp.int32, (tm, LANES), 1)
        lo_head = lane < HEAD_DIM
        first_half = (lane % HEAD_DIM) < (HEAD_DIM // 2)
        cos = cos_ref[...]
        sin = sin_ref[...]
        for g in range(OFF_V // LANES):
            seg = qk[:, g * LANES:(g + 1) * LANES]
            sq = seg * seg
            s_lo = jnp.sum(jnp.where(lo_head, sq, 0.0), axis=-1, keepdims=True)
            s_hi = jnp.sum(jnp.where(lo_head, 0.0, sq), axis=-1, keepdims=True)
            ms = jnp.where(lo_head, s_lo, s_hi) * (1.0 / HEAD_DIM)
            normed = seg * lax.rsqrt(ms + EPS) * qkg_ref[:, g * LANES:(g + 1) * LANES]
            partner = jnp.where(first_half,
                                pltpu.roll(normed, LANES - HEAD_DIM // 2, axis=1),
                                pltpu.roll(normed, HEAD_DIM // 2, axis=1))
            roped = normed * cos + partner * sin
            if g < D_Q // LANES:
                q_ref[:, g * LANES:(g + 1) * LANES] = (roped * SCALE).astype(q_ref.dtype)
            else:
                k_ref[...] = roped

    def values():
        v_ref[...] = proj(OFF_V, OFF_XR)

    def gate(out_ref, off, c0, c1, r0=0, r1=x.shape[0]):
        y = _mm(xn[r0:r1], w_ref[:, off + c0:off + c1], precise)
        out_ref[r0:r1, c0:c1] = _sigmoid(y).astype(out_ref.dtype)

    if not fuse_rnn:
        xr_ref[...] = proj(OFF_XR, OFF_YR)
        gy_ref[...] = _gelu_tanh(proj(OFF_YR, OFF_GA)).astype(gy_ref.dtype)
        qk_heads()
        values()
        gate(sga_ref, OFF_GA, 0, D_MODEL)
        gate(sgr_ref, OFF_GR, 0, D_MODEL)
        return

    slab = RNN_SLAB_BLOCKS * RNN_BLOCK
    half = x.shape[0] // 2
    others = [qk_heads, values]
    for out_ref, off in ((sga_ref, OFF_GA), (sgr_ref, OFF_GR)):
        others += [functools.partial(gate, out_ref, off, c, c + slab, r, r + half)
                   for c in range(0, D_MODEL, slab) for r in (0, half)]
    others = iter(others)
    n_slabs = D_RNN // slab

    def slab_inputs(i):
        c0 = i * slab
        return (proj(OFF_XR + c0, OFF_XR + c0 + slab),
                _gelu_tanh(proj(OFF_YR + c0, OFF_YR + c0 + slab)))

    gy = []
    nxt = slab_inputs(0)
    for i in range(n_slabs):
        (xr, gy_i), nxt = nxt, None
        gy.append(gy_i)
        ct_ref[0, :, i * slab:(i + 1) * slab] = xr[xr.shape[0] - SUBLANES:, :]
        for _ in _rnn_scan_terms(xr, i * RNN_SLAB_BLOCKS, *rnn_w, xbuf, a_scr, b_scr):
            if nxt is None and i + 1 < n_slabs:
                nxt = slab_inputs(i + 1)
            next(others, lambda: None)()
    for other in others:
        other()
    _rnn_scan_finish(jnp.concatenate(gy, axis=-1), rnn_ref, hl_ref, a_scr, b_scr, h_scr, hcar)


RNN_SLAB_BLOCKS = 2


def _proj(x, g, w_in, cos_t, sin_t, qkg, batch, seq, tm, precise, rnn_weights=None):
    n = batch * seq
    nt = seq // tm
    row = lambda b, t: (b * nt + t, 0)
    per_seq = lambda b, t: (b, 0, 0)
    act = F32 if precise else BF16
    rows_out = lambda width, dtype: (jax.ShapeDtypeStruct((n, width), dtype),
                                     pl.BlockSpec((tm, width), row))
    state_out = (jax.ShapeDtypeStruct((batch, SUBLANES, D_RNN), F32),
                 pl.BlockSpec((1, SUBLANES, D_RNN), per_seq))
    outs = [rows_out(D_Q, act), rows_out(D_KV, F32), rows_out(D_KV, F32)]
    in_specs = [
        pl.BlockSpec((tm, D_MODEL), row),
        _full((1, D_MODEL)),
        _full((D_MODEL, D_IN), single_buffer=True),
        pl.BlockSpec((tm, LANES), lambda b, t: (t, 0)),
        pl.BlockSpec((tm, LANES), lambda b, t: (t, 0)),
        _full((1, OFF_V)),
    ]
    args = [x, g, w_in, cos_t, sin_t, qkg]
    scratch = []
    if rnn_weights is None:
        outs += [rows_out(D_RNN, F32), rows_out(D_RNN, act)]
    else:
        assert not precise
        outs += [rows_out(D_RNN, act)]
        in_specs += [_full(w.shape) for w in rnn_weights]
        args += list(rnn_weights)
        groups = tm // SUBLANES
        scratch = [pltpu.VMEM((tm + SUBLANES, D_RNN), F32)]
        scratch += [pltpu.VMEM((groups, SUBLANES, D_RNN), F32)] * 3
        scratch += [pltpu.VMEM((SUBLANES, D_RNN), F32)]
    outs += [rows_out(D_MODEL, act), rows_out(D_MODEL, act)]
    if rnn_weights is not None:
        outs += [state_out, state_out]
    else:
        assert batch * nt == 1 and w_in.dtype == F32
        outs += [(jax.ShapeDtypeStruct(w_in.shape, BF16), _full(w_in.shape, single_buffer=True))]
    return pl.pallas_call(
        functools.partial(_proj_kernel, precise=precise, fuse_rnn=rnn_weights is not None),
        grid=(batch, nt),
        in_specs=in_specs,
        out_specs=[spec for _, spec in outs],
        out_shape=[shape for shape, _ in outs],
        scratch_shapes=scratch,
        compiler_params=_params("parallel", "arbitrary"),
        name="proj",
    )(*args)


def _softmax_pv(s, sink, v2, precise):
    m = jnp.maximum(jnp.max(s, axis=-1, keepdims=True), sink)
    p = jnp.exp(s - m)
    denom = jnp.sum(p, axis=-1, keepdims=True) + jnp.exp(sink - m)
    return _mm(p, v2, precise) * (1.0 / denom)


def _sink_column(sink_ref, h, rows, rows_per_head):
    r = lax.broadcasted_iota(jnp.int32, (rows, 1), 0) // rows_per_head
    col = jnp.full((rows, 1), sink_ref[h * Q_PER_KV], F32)
    for g in range(1, Q_PER_KV):
        col = jnp.where(r == g, sink_ref[h * Q_PER_KV + g], col)
    return col


ATTN_BLOCKS_PER_STEP = 4


def _attn_prompt_kernel(sink_ref, q_ref, kc_ref, kp_ref, vc_ref, vp_ref, o_ref):
    n = pl.program_id(1)
    blk = ATTN_BLOCK
    k_all = jnp.concatenate([kp_ref[...], kc_ref[...]], axis=0).astype(BF16)
    v_all = jnp.concatenate([vp_ref[...], vc_ref[...]], axis=0).astype(BF16)
    rows = Q_PER_KV * blk
    i = lax.broadcasted_iota(jnp.int32, (rows, 2 * blk), 0) % blk
    j = lax.broadcasted_iota(jnp.int32, (rows, 2 * blk), 1)
    d = j - i
    in_window = (d >= 1) & (d <= WINDOW)
    first_valid = (d >= jnp.where(n > 0, 1, jnp.maximum(1, blk - i))) & (d <= WINDOW)
    def kv_head(sub, h):
        q = q_ref[sub * blk:(sub + 1) * blk, :]
        kh = k_all[sub * blk:(sub + 2) * blk, h * HEAD_DIM:(h + 1) * HEAD_DIM]
        vh = v_all[sub * blk:(sub + 2) * blk, h * HEAD_DIM:(h + 1) * HEAD_DIM]
        valid = first_valid if sub == 0 else in_window
        qs = jnp.concatenate(
            [q[:, (h * Q_PER_KV + g) * HEAD_DIM:(h * Q_PER_KV + g + 1) * HEAD_DIM]
             for g in range(Q_PER_KV)], axis=0)
        s = _mm(qs, kh, False, _NT)
        yield
        s = jnp.where(valid, s, NEG_INF)
        sink = _sink_column(sink_ref, h, rows, blk)
        m = jnp.maximum(jnp.max(s, axis=-1, keepdims=True), sink)
        p = jnp.exp(s - m)
        denom = jnp.sum(p, axis=-1, keepdims=True) + jnp.exp(sink - m)
        yield
        o = _mm(p, vh, False) * (1.0 / denom)
        for g in range(Q_PER_KV):
            c = (h * Q_PER_KV + g) * HEAD_DIM
            o_ref[sub * blk:(sub + 1) * blk, c:c + HEAD_DIM] = (
                o[g * blk:(g + 1) * blk].astype(o_ref.dtype))

    heads = [kv_head(sub, h) for sub in range(ATTN_BLOCKS_PER_STEP) for h in range(N_KV_HEADS)]
    while heads:
        heads = [head for head in heads if next(head, "done") != "done"]


def _attn_prompt(q, k, v, sinks, batch, seq):
    step = ATTN_BLOCKS_PER_STEP * ATTN_BLOCK
    ns = seq // step
    cur = lambda b, n: (b * ns + n, 0)
    prev = lambda b, n: (jnp.maximum((b * ns + n) * ATTN_BLOCKS_PER_STEP - 1, 0), 0)
    return pl.pallas_call(
        _attn_prompt_kernel,
        grid=(batch, ns),
        in_specs=[
            pl.BlockSpec(memory_space=pltpu.SMEM),
            pl.BlockSpec((step, D_Q), cur),
            pl.BlockSpec((step, D_KV), cur),
            pl.BlockSpec((ATTN_BLOCK, D_KV), prev),
            pl.BlockSpec((step, D_KV), cur),
            pl.BlockSpec((ATTN_BLOCK, D_KV), prev),
        ],
        out_specs=pl.BlockSpec((step, D_Q), cur),
        out_shape=jax.ShapeDtypeStruct((batch * seq, D_Q), BF16),
        compiler_params=_params("parallel", "parallel"),
        name="attn_prompt",
    )(sinks, q, k, k, v, v)


SAMPLE_BT = 8


def _attn_sample_kernel(sink_ref, q_ref, kn_ref, vn_ref, kc_ref, vc_ref, o_ref, ko_ref, vo_ref):
    bt = SAMPLE_BT
    w = lax.broadcasted_iota(jnp.int32, (bt, WINDOW, D_KV), 1)

    def shifted(cache_ref, new_ref):
        rolled = pltpu.roll(cache_ref[...], WINDOW - 1, axis=1)
        return jnp.where(w == WINDOW - 1, new_ref[...][:, None, :], rolled)

    k_win = shifted(kc_ref, kn_ref)
    v_win = shifted(vc_ref, vn_ref)
    ko_ref[...] = k_win
    vo_ref[...] = v_win
    k2 = k_win.reshape(bt * WINDOW, D_KV)
    v2 = v_win.reshape(bt * WINDOW, D_KV)
    q = q_ref[...]
    rows = Q_PER_KV * bt
    rb = lax.broadcasted_iota(jnp.int32, (rows, bt * WINDOW), 0) % bt
    cb = lax.broadcasted_iota(jnp.int32, (rows, bt * WINDOW), 1) // WINDOW
    valid = rb == cb
    def kv_head(h):
        kh = k2[:, h * HEAD_DIM:(h + 1) * HEAD_DIM]
        vh = v2[:, h * HEAD_DIM:(h + 1) * HEAD_DIM]
        qs = jnp.concatenate(
            [q[:, (h * Q_PER_KV + g) * HEAD_DIM:(h * Q_PER_KV + g + 1) * HEAD_DIM]
             for g in range(Q_PER_KV)], axis=0)
        s = _mm(qs, kh, True, _NT)
        yield
        s = jnp.where(valid, s, NEG_INF)
        o = _softmax_pv(s, _sink_column(sink_ref, h, rows, bt), vh, True)
        for g in range(Q_PER_KV):
            c = (h * Q_PER_KV + g) * HEAD_DIM
            o_ref[:, c:c + HEAD_DIM] = o[g * bt:(g + 1) * bt].astype(o_ref.dtype)

    heads = [kv_head(h) for h in range(N_KV_HEADS)]
    while heads:
        heads = [head for head in heads if next(head, "done") != "done"]


def _attn_sample(q, k_new, v_new, cache_k, cache_v, sinks):
    nbatch = q.shape[0]
    bt = SAMPLE_BT
    row = lambda i: (i, 0)
    win = lambda i: (i, 0, 0)
    return pl.pallas_call(
        _attn_sample_kernel,
        grid=(nbatch // bt,),
        in_specs=[
            pl.BlockSpec(memory_space=pltpu.SMEM),
            pl.BlockSpec((bt, D_Q), row),
            pl.BlockSpec((bt, D_KV), row),
            pl.BlockSpec((bt, D_KV), row),
            pl.BlockSpec((bt, WINDOW, D_KV), win),
            pl.BlockSpec((bt, WINDOW, D_KV), win),
        ],
        out_specs=[
            pl.BlockSpec((bt, D_Q), row),
            pl.BlockSpec((bt, WINDOW, D_KV), win),
            pl.BlockSpec((bt, WINDOW, D_KV), win),
        ],
        out_shape=(
            jax.ShapeDtypeStruct((nbatch, D_Q), F32),
            jax.ShapeDtypeStruct((nbatch, WINDOW, D_KV), F32),
            jax.ShapeDtypeStruct((nbatch, WINDOW, D_KV), F32),
        ),
        compiler_params=_params("parallel"),
        name="attn_sample",
    )(sinks, q, k_new, v_new, cache_k, cache_v)


def _lru_terms(xc, wcat_ref, ba_ref, bi_ref, lam_ref, precise, block0=0):
    cols = slice(block0 * RNN_BLOCK, block0 * RNN_BLOCK + xc.shape[1])
    xcb = xc if precise else xc.astype(BF16)
    ya, yi = [], []
    for n in range(xc.shape[1] // RNN_BLOCK):
        y = _mm(xcb[:, n * RNN_BLOCK:(n + 1) * RNN_BLOCK], wcat_ref[block0 + n], precise)
        ya.append(y[:, :RNN_BLOCK])
        yi.append(y[:, RNN_BLOCK:])
    r = _sigmoid(jnp.concatenate(ya, axis=-1) + ba_ref[:, cols])
    gate_i = _sigmoid(jnp.concatenate(yi, axis=-1) + bi_ref[:, cols])
    neg_lam = -lam_ref[:, cols]
    softplus = jnp.maximum(neg_lam, 0.0) + jnp.log1p(jnp.exp(-jnp.abs(neg_lam)))
    log_a = (-LRU_C * softplus) * r
    a = jnp.exp(log_a)
    m = 1.0 - a * a
    b = jnp.where(m > 0.0, m * lax.rsqrt(m), 0.0) * (gate_i * xc)
    return a, b


def _rnn_start_sequence(xbuf, hcar):
    xbuf[0:SUBLANES, :] = jnp.zeros((SUBLANES, D_RNN), F32)
    hcar[...] = jnp.zeros((SUBLANES, D_RNN), F32)


def _rnn_scan_terms(x, block0, cw_ref, cb_ref, wcat_ref, ba_ref, bi_ref, lam_ref,
                    xbuf, a_scr, b_scr):
    tt, width = x.shape
    groups = tt // SUBLANES
    cols = slice(block0 * RNN_BLOCK, block0 * RNN_BLOCK + width)
    xbuf[SUBLANES:, cols] = x
    xc = cb_ref[:, cols] + cw_ref[CONV_W - 1:CONV_W, cols] * x
    for j in range(CONV_W - 1):
        s = CONV_W - 1 - j
        xc = xc + cw_ref[j:j + 1, cols] * xbuf[SUBLANES - s:SUBLANES - s + tt, cols]
    xbuf[0:SUBLANES, cols] = x[tt - SUBLANES:, :]
    yield
    a, b = _lru_terms(xc, wcat_ref, ba_ref, bi_ref, lam_ref, False, block0)
    yield
    a = a.reshape(groups, SUBLANES, width)
    b = b.reshape(groups, SUBLANES, width)
    step = lax.broadcasted_iota(jnp.int32, (groups, SUBLANES, width), 1)
    k = 1
    while k < SUBLANES:
        keep = step >= k
        a_sh = jnp.where(keep, pltpu.roll(a, k, axis=1), 1.0)
        b_sh = jnp.where(keep, pltpu.roll(b, k, axis=1), 0.0)
        b = a * b_sh + b
        a = a * a_sh
        k *= 2
        if k < SUBLANES:
            yield
    a_scr[:, :, cols] = a
    b_scr[:, :, cols] = b


def _rnn_scan_finish(gy, o_ref, hl_ref, a_scr, b_scr, h_scr, hcar):
    groups = a_scr.shape[0]
    tt = groups * SUBLANES

    def chain(g, h_in):
        h = a_scr[g] * h_in + b_scr[g]
        h_scr[g] = h
        return jnp.broadcast_to(h[SUBLANES - 1:SUBLANES, :], (SUBLANES, D_RNN))

    h_last = lax.fori_loop(0, groups, chain, hcar[...], unroll=True)
    hcar[...] = h_last
    hl_ref[0] = h_last
    h = h_scr[...].reshape(tt, D_RNN)
    o_ref[...] = (h * gy).astype(o_ref.dtype)


def _rnn_sample_kernel(xr_ref, gy_ref, s0_ref, s1_ref, s2_ref, h_ref, cw_ref, cb_ref,
                       wcat_ref, ba_ref, bi_ref, lam_ref, o_ref, hn_ref):
    x = xr_ref[...]
    xc = (cb_ref[...] + cw_ref[0:1, :] * s0_ref[...] + cw_ref[1:2, :] * s1_ref[...]
          + cw_ref[2:3, :] * s2_ref[...] + cw_ref[3:4, :] * x)
    a, b = _lru_terms(xc, wcat_ref, ba_ref, bi_ref, lam_ref, True)
    h = a * h_ref[...] + b
    hn_ref[...] = h
    o_ref[...] = (h * gy_ref[...].astype(F32)).astype(o_ref.dtype)


def _rnn_sample(xr, gy, s0, s1, s2, h_prev, conv_w, conv_b, wcat, b_a, b_i, lam):
    n = xr.shape[0]
    act = _full((n, D_RNN))
    return pl.pallas_call(
        _rnn_sample_kernel,
        grid=(1,),
        in_specs=[act, act, act, act, act, act,
                  _full((CONV_W, D_RNN)), _full((1, D_RNN)),
                  _full((N_RNN_BLOCKS, RNN_BLOCK, 2 * RNN_BLOCK)),
                  _full((1, D_RNN)), _full((1, D_RNN)), _full((1, D_RNN))],
        out_specs=[act, act],
        out_shape=(jax.ShapeDtypeStruct((n, D_RNN), F32),
                   jax.ShapeDtypeStruct((n, D_RNN), F32)),
        compiler_params=_params("arbitrary"),
        name="rnn_sample",
    )(xr, gy, s0, s1, s2, h_prev, conv_w, conv_b, wcat, b_a, b_i, lam)


def _merge_kernel(x_ref, at_ref, rn_ref, sga_ref, sgr_ref, wa_ref, wr_ref, wo_ref, g_ref,
                  wrt_ref, brt_ref, tri_ref, *rest, precise, tiles_per_seg, cast_experts):
    if cast_experts:
        wg_ref, wu_ref, wd_ref, *rest = rest
        x2_ref, xtm_ref, rec_ref, rect_ref, cnt_ref, wg_o, wu_o, wd_o, cnt_scr = rest
    else:
        x2_ref, xtm_ref, rec_ref, rect_ref, cnt_ref, cnt_scr = rest

    @pl.when(pl.program_id(0) % tiles_per_seg == 0)
    def _():
        cnt_scr[...] = jnp.zeros_like(cnt_scr)

    tm = x_ref.shape[0]
    parts = MERGE_PARTS if tm % (MERGE_PARTS * LANES) == 0 else 1
    part = tm // parts
    state = {"counts": cnt_scr[...]}
    pending = [
        _merge_rows(slice(i * part, (i + 1) * part), state, x_ref, at_ref, rn_ref, sga_ref, sgr_ref,
                    wa_ref, wr_ref, wo_ref, g_ref, wrt_ref, brt_ref, tri_ref, x2_ref, xtm_ref,
                    rec_ref, rect_ref, precise)
        for i in range(parts)]
    active, step = [], 0
    while pending or active:
        if pending and step % MERGE_STAGGER == 0:
            active.append(pending.pop(0))
        for part_pieces in list(active):
            if next(part_pieces, "done") == "done":
                active.remove(part_pieces)
        step += 1
    cnt_scr[...] = state["counts"]
    cnt_ref[0] = state["counts"]
    if cast_experts:
        wg_o[...] = wg_ref[...].astype(BF16)
        wu_o[...] = wu_ref[...].astype(BF16)
        wd_o[...] = wd_ref[...].astype(BF16)


MERGE_PARTS = 2
MERGE_STAGGER = 3


def _merge_rows(rows, state, x_ref, at_ref, rn_ref, sga_ref, sgr_ref, wa_ref, wr_ref, wo_ref, g_ref,
                wrt_ref, brt_ref, tri_ref, x2_ref, xtm_ref, rec_ref, rect_ref, precise):
    ya = _mm(at_ref[rows, :], wa_ref[...], precise)
    yield
    yr = _mm(rn_ref[rows, :], wr_ref[...], precise)
    yield
    merged = sga_ref[rows, :].astype(F32) * ya + sgr_ref[rows, :].astype(F32) * yr
    yield
    x2 = x_ref[rows, :] + _mm(merged, wo_ref[...], precise)
    x2_ref[rows, :] = x2
    yield
    inv = lax.rsqrt(jnp.mean(x2 * x2, axis=-1, keepdims=True) + EPS)
    xn = x2 * inv * g_ref[...]
    for c in range(TOKEN_ROWS):
        xtm_ref[pl.ds(rows.start * TOKEN_ROWS + c, rows.stop - rows.start, stride=TOKEN_ROWS), :] = (
            xn[:, c * LANES:(c + 1) * LANES])
    yield

    logits = _mm(xn, wrt_ref[...], precise) + brt_ref[...]
    yield
    tm = logits.shape[0]
    lane_i = lax.broadcasted_iota(jnp.int32, (tm, LANES), 1)
    lane = lane_i.astype(F32)
    big = float(LANES)
    is_grp = (lane_i >= N_EXPERTS) & (lane_i < N_EXPERTS + N_GROUPS)
    gl = jnp.where(is_grp, logits, NEG_INF)
    gmax = jnp.max(gl, axis=-1, keepdims=True)
    g_idx = jnp.min(jnp.where(gl == gmax, lane, big), axis=-1, keepdims=True) - N_EXPERTS
    p_g = 1.0 / jnp.sum(jnp.exp(gl - gmax), axis=-1, keepdims=True)
    in_grp = (lane_i // EXPERTS_PER_GROUP).astype(F32) == g_idx
    el = jnp.where(in_grp, logits, NEG_INF)
    v1 = jnp.max(el, axis=-1, keepdims=True)
    i1 = jnp.min(jnp.where(el == v1, lane, big), axis=-1, keepdims=True)
    el2 = jnp.where(lane == i1, NEG_INF, el)
    v2 = jnp.max(el2, axis=-1, keepdims=True)
    i2 = jnp.min(jnp.where(el2 == v2, lane, big), axis=-1, keepdims=True)
    e2 = jnp.exp(v2 - v1)
    w1 = p_g / (1.0 + e2)
    w2 = p_g * e2 / (1.0 + e2)
    yield

    hit = jnp.where(lane == i1, 1.0, jnp.where(lane == i2, 1.0, 0.0))
    counts = state["counts"]
    state["counts"] = counts + jnp.sum(hit, axis=0, keepdims=True)
    before = (jnp.dot(tri_ref[0:tm, 0:tm], hit.astype(BF16), preferred_element_type=F32)
              + counts[0:1, :])
    yield
    r1 = jnp.sum(jnp.where(lane == i1, before, 0.0), axis=-1, keepdims=True)
    r2 = jnp.sum(jnp.where(lane == i2, before, 0.0), axis=-1, keepdims=True)
    rec = jnp.where(lane == REC_W2, w2, 0.0)
    for field, val in ((REC_W1, w1), (REC_R2, r2), (REC_R1, r1), (REC_E2, i2), (REC_E1, i1)):
        rec = jnp.where(lane == field, val, rec)
    rec_ref[rows, :] = rec
    rect_ref[:, rows] = rec.T


REC_E1, REC_E2, REC_R1, REC_R2, REC_W1, REC_W2 = range(6)
TOKEN_ROWS = D_MODEL // LANES


def _merge(x, attn, rnn, sga, sgr, wa, wr, wo, g2, w_route, b_route, tri, tm, seg, precise,
           expert_weights=None):
    n = x.shape[0]
    steps = n // tm
    tiles_per_seg = seg // tm
    row = lambda i: (i, 0)
    in_specs = [
        pl.BlockSpec((tm, D_MODEL), row),
        pl.BlockSpec((tm, D_Q), row),
        pl.BlockSpec((tm, D_RNN), row),
        pl.BlockSpec((tm, D_MODEL), row),
        pl.BlockSpec((tm, D_MODEL), row),
        _full((D_Q, D_MODEL)),
        _full((D_RNN, D_MODEL)),
        _full((D_MODEL, D_MODEL)),
        _full((1, D_MODEL)),
        _full((D_MODEL, LANES)),
        _full((1, LANES)),
        _full((tm, tm)),
    ]
    out_specs = [
        pl.BlockSpec((tm, D_MODEL), row),
        pl.BlockSpec((tm * TOKEN_ROWS, LANES), row),
        pl.BlockSpec((tm, LANES), row),
        pl.BlockSpec((LANES, tm), lambda i: (0, i)),
        pl.BlockSpec((1, SUBLANES, LANES), lambda i: (i // tiles_per_seg, 0, 0)),
    ]
    out_shape = [
        jax.ShapeDtypeStruct((n, D_MODEL), F32),
        jax.ShapeDtypeStruct((n * TOKEN_ROWS, LANES), F32),
        jax.ShapeDtypeStruct((n, LANES), F32),
        jax.ShapeDtypeStruct((LANES, n), F32),
        jax.ShapeDtypeStruct((n // seg, SUBLANES, LANES), F32),
    ]
    args = [x, attn, rnn, sga, sgr, wa, wr, wo, g2, w_route, b_route, tri]
    if expert_weights is not None:
        assert steps == N_EXPERTS
        for w in expert_weights:
            spec = pl.BlockSpec((1,) + w.shape[1:], lambda i: (i, 0, 0))
            in_specs.append(spec)
            out_specs.append(spec)
            out_shape.append(jax.ShapeDtypeStruct(w.shape, BF16))
            args.append(w)
    return pl.pallas_call(
        functools.partial(_merge_kernel, precise=precise, tiles_per_seg=tiles_per_seg,
                          cast_experts=expert_weights is not None),
        grid=(steps,),
        in_specs=in_specs,
        out_specs=out_specs,
        out_shape=out_shape,
        scratch_shapes=[pltpu.VMEM((SUBLANES, LANES), F32)],
        compiler_params=_params("arbitrary"),
        name="merge",
    )(*args)


MOE_CHUNK = 256
MOE_TAIL = 128
MOE_EXPERTS_PER_STEP = 2
MOE_VMEM_LIMIT = 60 * 1024 * 1024


def _seg_rows(seg):
    return 2 * seg + N_EXPERTS * SUBLANES + MOE_CHUNK


def _token_rows(i):
    return pl.ds(pl.multiple_of(i * TOKEN_ROWS, TOKEN_ROWS), TOKEN_ROWS)


def _sorted_rows(first_row):
    return pl.ds(pl.multiple_of(first_row, TOKEN_ROWS), TOKEN_ROWS)


def _moe_kernel(off_ref, nfull_ref, rem_ref, slot_ref, xtm_ref, wg_ref, wu_ref, wd_ref,
                x2_ref, rec_ref, o_ref, buf, g1, g2, *, seg, td, tc):
    s = pl.program_id(0)
    p = pl.program_id(1)
    n_disp = seg // td
    n_exp = N_EXPERTS // MOE_EXPERTS_PER_STEP

    @pl.when((s == 0) & (p == 0))
    def _():
        buf[...] = jnp.zeros_like(buf)

    @pl.when(p < n_disp)
    def _():
        def dispatch(g, carry):
            for j in range(SUBLANES):
                t = g * SUBLANES + j
                row = xtm_ref[_token_rows(t), :]
                for k in range(2):
                    buf[_sorted_rows(slot_ref[0, 0, k * seg + p * td + t]), :] = row
            return carry

        lax.fori_loop(0, td // SUBLANES, dispatch, 0)

    def run_chunk(e, row0, rows, valid):
        r0 = pl.multiple_of(row0 * TOKEN_ROWS, SUBLANES * TOKEN_ROWS)
        xf = [buf[pl.ds(r0 + j, rows, stride=TOKEN_ROWS), :] for j in range(TOKEN_ROWS)]
        x = jnp.concatenate(xf, axis=-1).astype(BF16)
        hg = jnp.dot(x, wg_ref[e], preferred_element_type=F32)
        hu = jnp.dot(x, wu_ref[e], preferred_element_type=F32)
        h = (hg * _sigmoid(hg)) * hu
        y = jnp.dot(h.astype(BF16), wd_ref[e], preferred_element_type=F32)
        if valid is not None:
            mine = lax.broadcasted_iota(jnp.int32, (rows, LANES), 0) < valid
        for j in range(TOKEN_ROWS):
            yj = y[:, j * LANES:(j + 1) * LANES]
            if valid is not None:
                yj = jnp.where(mine, yj, xf[j])
            buf[pl.ds(r0 + j, rows, stride=TOKEN_ROWS), :] = yj

    def run_expert(e):
        idx = s * N_EXPERTS + (p - n_disp) * MOE_EXPERTS_PER_STEP + e
        base = off_ref[idx]
        n_full = nfull_ref[idx]
        rem = rem_ref[idx]

        def chunk(c, carry):
            run_chunk(e, base + c * MOE_CHUNK, MOE_CHUNK, None)
            return carry

        lax.fori_loop(0, n_full, chunk, 0)
        last = base + n_full * MOE_CHUNK
        for units in range(1, MOE_CHUNK // MOE_TAIL + 2):
            @pl.when((rem > (units - 1) * MOE_TAIL) & (rem <= units * MOE_TAIL))
            def _():
                run_chunk(e, last, units * MOE_TAIL, rem)

    @pl.when((p >= n_disp) & (p < n_disp + n_exp))
    def _():
        for e in range(MOE_EXPERTS_PER_STEP):
            run_expert(e)

    @pl.when(p >= n_disp + n_exp)
    def _():
        t0 = (p - n_disp - n_exp) * tc

        def gather(g, carry):
            for j in range(SUBLANES):
                t = g * SUBLANES + j
                g1[_token_rows(t), :] = buf[_sorted_rows(slot_ref[0, 0, t0 + t]), :]
                g2[_token_rows(t), :] = buf[_sorted_rows(slot_ref[0, 0, seg + t0 + t]), :]
            return carry

        lax.fori_loop(0, tc // SUBLANES, gather, 0)
        rec = rec_ref[...]
        lane = lax.broadcasted_iota(jnp.int32, rec.shape, 1)
        w1 = jnp.broadcast_to(
            jnp.sum(jnp.where(lane == REC_W1, rec, 0.0), axis=-1, keepdims=True), rec.shape)
        w2 = jnp.broadcast_to(
            jnp.sum(jnp.where(lane == REC_W2, rec, 0.0), axis=-1, keepdims=True), rec.shape)
        for j in range(TOKEN_ROWS):
            cols = slice(j * LANES, (j + 1) * LANES)
            o_ref[:, cols] = (x2_ref[:, cols] + w1 * g1[pl.ds(j, tc, stride=TOKEN_ROWS), :]
                              + w2 * g2[pl.ds(j, tc, stride=TOKEN_ROWS), :])


def _moe(off, nfull, rem, slot, xtm, wg, wu, wd, x2, rec, seg, td, tc):
    n = x2.shape[0]
    n_seg = n // seg
    n_disp, n_comb = seg // td, seg // tc
    per_step = MOE_EXPERTS_PER_STEP
    n_exp = N_EXPERTS // per_step
    rows = _seg_rows(seg) * TOKEN_ROWS
    disp_tile = lambda s, p, *_: (s * n_disp + jnp.minimum(p, n_disp - 1), 0)
    expert = lambda s, p, *_: (jnp.clip(p - n_disp, 0, n_exp - 1), 0, 0)
    comb_tile = lambda s, p, *_: (s * n_comb + jnp.clip(p - n_disp - n_exp, 0, n_comb - 1), 0)
    grid_spec = pltpu.PrefetchScalarGridSpec(
        num_scalar_prefetch=3,
        grid=(n_seg, n_disp + n_exp + n_comb),
        in_specs=[
            pl.BlockSpec((1, 1, 2 * seg), lambda s, p, *_: (s, 0, 0), memory_space=pltpu.SMEM),
            pl.BlockSpec((td * TOKEN_ROWS, LANES), disp_tile),
            pl.BlockSpec((per_step, D_MODEL, D_EXPERT), expert),
            pl.BlockSpec((per_step, D_MODEL, D_EXPERT), expert),
            pl.BlockSpec((per_step, D_EXPERT, D_MODEL), expert),
            pl.BlockSpec((tc, D_MODEL), comb_tile),
            pl.BlockSpec((tc, LANES), comb_tile),
        ],
        out_specs=pl.BlockSpec((tc, D_MODEL), comb_tile),
        scratch_shapes=[pltpu.VMEM((rows, LANES), F32),
                        pltpu.VMEM((tc * TOKEN_ROWS, LANES), F32),
                        pltpu.VMEM((tc * TOKEN_ROWS, LANES), F32)],
    )
    return pl.pallas_call(
        functools.partial(_moe_kernel, seg=seg, td=td, tc=tc),
        grid_spec=grid_spec,
        out_shape=jax.ShapeDtypeStruct((n, D_MODEL), F32),
        compiler_params=pltpu.CompilerParams(
            dimension_semantics=("arbitrary", "arbitrary"), vmem_limit_bytes=MOE_VMEM_LIMIT),
        name="moe",
    )(off, nfull, rem, slot, xtm, wg, wu, wd, x2, rec)


def _plan(rect, cnt, seg):
    n = rect.shape[1]
    expert = rect[REC_E1:REC_E2 + 1].astype(jnp.int32)
    rank = rect[REC_R1:REC_R2 + 1].astype(jnp.int32)
    counts = cnt[:, 0, :N_EXPERTS].astype(jnp.int32)
    padded = (counts + SUBLANES - 1) // SUBLANES * SUBLANES
    off = jnp.cumsum(padded, axis=1) - padded
    n_full = counts // MOE_CHUNK
    rem = counts - n_full * MOE_CHUNK
    join = (rem > 0) & (rem <= MOE_TAIL) & (n_full > 0)
    n_full = n_full - join
    rem = rem + join * MOE_CHUNK
    off_tok = jnp.repeat(off.T, seg, axis=1)
    hit = expert[:, None, :] == jnp.arange(N_EXPERTS, dtype=jnp.int32)[None, :, None]
    slot = (rank + jnp.sum(jnp.where(hit, off_tok[None], 0), axis=1)) * TOKEN_ROWS
    slot = slot.reshape(2, n // seg, seg).transpose(1, 0, 2).reshape(n // seg, 1, 2 * seg)
    return off.reshape(-1), n_full.reshape(-1), rem.reshape(-1), slot


def _rope_tables(pos):
    half = HEAD_DIM // 2
    inv_freq = ROPE_THETA ** (-jnp.arange(half, dtype=F32) / half)
    lane = jnp.arange(LANES)
    ang = pos[:, None] * jnp.tile(inv_freq, LANES // half)[None, :]
    sign = jnp.where(lane % HEAD_DIM < half, -1.0, 1.0).astype(F32)
    return jnp.cos(ang), jnp.sin(ang) * sign[None, :]


def kernel(x_prompt, x_sample, cache_k_win, cache_v_win, state_conv, state_lru_h, attn_norm_g, w_in, q_norm_g, k_norm_g, attn_sinks, conv_w, conv_b, w_lru_a, b_lru_a, w_lru_i, b_lru_i, lru_lambda, w_br_attn, w_br_rnn, w_out, ffn_norm_g, w_route_group, b_route_group, w_route_expert, b_route_expert, w_exp_gate, w_exp_up, w_exp_down):
    batch, seq, _ = x_prompt.shape
    dec_batch, dec_seq, _ = x_sample.shape
    depth = w_in.shape[0]
    assert depth == 1 and dec_seq == 1
    l = 0

    w_in_f = w_in[l]
    qkg = jnp.concatenate([jnp.tile(q_norm_g[l], N_Q_HEADS), jnp.tile(k_norm_g[l], N_KV_HEADS)])[None, :]
    wcat_f = jnp.concatenate([w_lru_a[l], w_lru_i[l]], axis=-1)
    wa_f, wr_f, wo_f = w_br_attn[l], w_br_rnn[l], w_out[l]
    w_route_f = jnp.concatenate(
        [w_route_expert[l], w_route_group[l],
         jnp.zeros((D_MODEL, LANES - N_EXPERTS - N_GROUPS), F32)], axis=-1)
    wcat = wcat_f.astype(BF16)
    wa_b, wr_b, wo_b, w_route = (w.astype(BF16) for w in (wa_f, wr_f, wo_f, w_route_f))
    b_route = jnp.concatenate(
        [b_route_expert[l], b_route_group[l], jnp.zeros((LANES - N_EXPERTS - N_GROUPS,), F32)])[None, :]
    experts_f = (w_exp_gate[l], w_exp_up[l], w_exp_down[l])
    g1 = attn_norm_g[l][None, :]
    g2 = ffn_norm_g[l][None, :]
    cw, cb = conv_w[l], conv_b[l][None, :]
    b_a, b_i, lam = b_lru_a[l][None, :], b_lru_i[l][None, :], lru_lambda[l][None, :]
    sinks = attn_sinks[l]

    def tail(x, attn, rnn, sga, sgr, tm, seg, precise, experts_b=None):
        wa, wr, wo, wrt = (wa_f, wr_f, wo_f, w_route_f) if precise else (wa_b, wr_b, wo_b, w_route)
        tri = jnp.tril(jnp.ones((tm, tm), BF16), -1)
        outs = _merge(x, attn, rnn, sga, sgr, wa, wr, wo, g2, wrt, b_route, tri, tm, seg, precise,
                      expert_weights=None if experts_b else experts_f)
        x2, xtm, rec, rect, cnt = outs[:5]
        wg, wu, wd = experts_b or outs[5:]
        off, nfull, rem, slot = _plan(rect, cnt, seg)
        y = _moe(off, nfull, rem, slot, xtm, wg, wu, wd, x2, rec, seg, tm, tm)
        return y, (wg, wu, wd)

    xs = x_sample.reshape(dec_batch, D_MODEL)
    cos_s, sin_s = _rope_tables(jnp.full((dec_batch,), PAST_LEN, F32))
    qs, ks, vs, xrs, gys, sgas, sgrs, w_in_b = _proj(xs, g1, w_in_f, cos_s, sin_s, qkg, 1, dec_batch,
                                                     dec_batch, True)

    xp = x_prompt.reshape(batch * seq, D_MODEL)
    cos_p, sin_p = _rope_tables(jnp.arange(seq, dtype=F32))
    q, k, v, rnn, sga, sgr, h_last, conv_tail = _proj(
        xp, g1, w_in_b, cos_p, sin_p, qkg, batch, seq, 512, False,
        rnn_weights=(cw, cb, wcat, b_a, b_i, lam))
    attn = _attn_prompt(q, k, v, sinks, batch, seq)
    y_prompt, experts_b = tail(xp, attn, rnn, sga, sgr, 512, 4096, False)
    y_prompt = y_prompt.reshape(batch, seq, D_MODEL)

    def last_rows(a, rows):
        return a.reshape(batch, seq, a.shape[-1])[:, seq - rows:]

    k_win_p = last_rows(k, WINDOW).reshape(1, batch, WINDOW, N_KV_HEADS, HEAD_DIM)
    v_win_p = last_rows(v, WINDOW).reshape(1, batch, WINDOW, N_KV_HEADS, HEAD_DIM)
    conv_p = conv_tail[None, :, SUBLANES - (CONV_W - 1):, :]
    h_p = h_last[None, :, 0, :]

    ck = cache_k_win[l].reshape(dec_batch, WINDOW, D_KV)
    cv = cache_v_win[l].reshape(dec_batch, WINDOW, D_KV)
    attn_s, k_win_s, v_win_s = _attn_sample(qs, ks, vs, ck, cv, sinks)
    sc = state_conv[l]
    rnn_s, h_s = _rnn_sample(xrs, gys, sc[:, 0], sc[:, 1], sc[:, 2], state_lru_h[l],
                             cw, cb, wcat_f, b_a, b_i, lam)
    y_sample, _ = tail(xs, attn_s, rnn_s, sgas, sgrs, dec_batch, dec_batch, True, experts_b)
    y_sample = y_sample.reshape(dec_batch, 1, D_MODEL)
    conv_s = jnp.stack([sc[:, 1], sc[:, 2], xrs], axis=1)[None]

    return (y_prompt, y_sample, k_win_p, v_win_p, conv_p, h_p,
            k_win_s.reshape(1, dec_batch, WINDOW, N_KV_HEADS, HEAD_DIM),
            v_win_s.reshape(1, dec_batch, WINDOW, N_KV_HEADS, HEAD_DIM),
            conv_s, h_s[None])
```

```python
import functools

import jax
import jax.numpy as jnp
from jax import lax
from jax.experimental import pallas as pl
from jax.experimental.pallas import tpu as pltpu

D_MODEL = 1024
HEAD_DIM = 64
N_Q_HEADS = 8
N_KV_HEADS = 2
Q_PER_KV = N_Q_HEADS // N_KV_HEADS
WINDOW = 128
ATTN_BLOCK = 128
ROPE_THETA = 10000.0
SCALE = HEAD_DIM ** -0.5
NEG_INF = -1e30
D_RNN = 1280
N_RNN_BLOCKS = 10
RNN_BLOCK = D_RNN // N_RNN_BLOCKS
CONV_W = 4
LRU_C = 8.0
N_GROUPS = 4
EXPERTS_PER_GROUP = 8
N_EXPERTS = N_GROUPS * EXPERTS_PER_GROUP
D_EXPERT = 256
PAST_LEN = 16384
EPS = 1e-6
D_Q = N_Q_HEADS * HEAD_DIM
D_KV = N_KV_HEADS * HEAD_DIM
D_IN = D_Q + 2 * D_KV + 2 * D_RNN + 2 * D_MODEL
OFF_K = D_Q
OFF_V = OFF_K + D_KV
OFF_XR = OFF_V + D_KV
OFF_YR = OFF_XR + D_RNN
OFF_GA = OFF_YR + D_RNN
OFF_GR = OFF_GA + D_MODEL

LANES = 128
SUBLANES = 8
VMEM_LIMIT = 56 * 1024 * 1024

F32 = jnp.float32
BF16 = jnp.bfloat16


def _params(*sem):
    return pltpu.CompilerParams(dimension_semantics=sem, vmem_limit_bytes=VMEM_LIMIT)


def _sigmoid(x):
    return 1.0 / (1.0 + jnp.exp(-x))


def _gelu_tanh(x):
    c = 0.7978845608028654
    half_x = 0.5 * x
    return half_x + half_x * jnp.tanh(x * (c + (c * 0.044715) * (x * x)))


def _full(shape, single_buffer=False):
    index_map = lambda *_: (0,) * len(shape)
    if single_buffer:
        return pl.BlockSpec(shape, index_map, pipeline_mode=pl.Buffered(1))
    return pl.BlockSpec(shape, index_map)


def _mm(a, b, precise, dims=None):
    if precise:
        a, b, prec = a.astype(F32), b.astype(F32), lax.Precision.HIGHEST
    else:
        a, b, prec = a.astype(BF16), b.astype(BF16), None
    if dims is None:
        return jnp.dot(a, b, preferred_element_type=F32, precision=prec)
    return lax.dot_general(a, b, dims, preferred_element_type=F32, precision=prec)


_NT = (((1,), (1,)), ((), ()))


def _proj_kernel(x_ref, g_ref, w_ref, cos_ref, sin_ref, qkg_ref, *rest, precise, fuse_rnn):
    if fuse_rnn:
        rnn_w = rest[:6]
        q_ref, k_ref, v_ref, rnn_ref, sga_ref, sgr_ref, hl_ref, ct_ref = rest[6:14]
        xbuf, a_scr, b_scr, h_scr, hcar = rest[14:]
        pl.when(pl.program_id(1) == 0)(functools.partial(_rnn_start_sequence, xbuf, hcar))
    else:
        q_ref, k_ref, v_ref, xr_ref, gy_ref, sga_ref, sgr_ref, wb_ref = rest
        wb_ref[...] = w_ref[...].astype(BF16)
    x = x_ref[...]
    inv = lax.rsqrt(jnp.mean(x * x, axis=-1, keepdims=True) + EPS)
    xn = x * inv * g_ref[...]
    if not precise:
        xn = xn.astype(BF16)

    def proj(lo, hi):
        return _mm(xn, w_ref[:, lo:hi], precise)

    def qk_heads():
        qk = proj(0, OFF_V)
        tm = qk.shape[0]
        lane = lax.broadcasted_iota(jnp.int32, (tm, LANES), 1)
        lo_head = lane < HEAD_DIM
        first_half = (lane % HEAD_DIM) < (HEAD_DIM // 2)
        cos_f, sin_f = cos_ref[...], sin_ref[...]
        reps = LANES // HEAD_DIM
        cos = jnp.concatenate([cos_f, cos_f] * reps, axis=-1)
        sin = jnp.concatenate([-sin_f, sin_f] * reps, axis=-1)
        for g in range(OFF_V // LANES):
            seg = qk[:, g * LANES:(g + 1) * LANES]
            sq = seg * seg
            s_lo = jnp.sum(jnp.where(lo_head, sq, 0.0), axis=-1, keepdims=True)
            s_hi = jnp.sum(jnp.where(lo_head, 0.0, sq), axis=-1, keepdims=True)
            ms = jnp.where(lo_head, s_lo, s_hi) * (1.0 / HEAD_DIM)
            normed = seg * lax.rsqrt(ms + EPS) * qkg_ref[:, g * LANES:(g + 1) * LANES]
            partner = jnp.where(first_half,
                                pltpu.roll(normed, LANES - HEAD_DIM // 2, axis=1),
                                pltpu.roll(normed, HEAD_DIM // 2, axis=1))
            roped = normed * cos + partner * sin
            if g < D_Q // LANES:
                q_ref[:, g * LANES:(g + 1) * LANES] = (roped * SCALE).astype(q_ref.dtype)
            else:
                k_ref[...] = roped

    def values():
        v_ref[...] = proj(OFF_V, OFF_XR)

    def gate(out_ref, off, c0, c1, r0=0, r1=x.shape[0]):
        y = _mm(xn[r0:r1], w_ref[:, off + c0:off + c1], precise)
        out_ref[r0:r1, c0:c1] = _sigmoid(y).astype(out_ref.dtype)

    if not fuse_rnn:
        xr_ref[...] = proj(OFF_XR, OFF_YR)
        gy_ref[...] = _gelu_tanh(proj(OFF_YR, OFF_GA)).astype(gy_ref.dtype)
        qk_heads()
        values()
        gate(sga_ref, OFF_GA, 0, D_MODEL)
        gate(sgr_ref, OFF_GR, 0, D_MODEL)
        return

    slab = RNN_SLAB_BLOCKS * RNN_BLOCK
    half = x.shape[0] // 2
    others = [qk_heads, values]
    for out_ref, off in ((sga_ref, OFF_GA), (sgr_ref, OFF_GR)):
        others += [functools.partial(gate, out_ref, off, c, c + slab, r, r + half)
                   for c in range(0, D_MODEL, slab) for r in (0, half)]
    others = iter(others)
    n_slabs = D_RNN // slab

    def slab_inputs(i):
        c0 = i * slab
        return (proj(OFF_XR + c0, OFF_XR + c0 + slab),
                _gelu_tanh(proj(OFF_YR + c0, OFF_YR + c0 + slab)))

    gy = []
    nxt = slab_inputs(0)
    for i in range(n_slabs):
        (xr, gy_i), nxt = nxt, None
        gy.append(gy_i)
        ct_ref[0, :, i * slab:(i + 1) * slab] = xr[xr.shape[0] - SUBLANES:, :]
        for _ in _rnn_scan_terms(xr, i * RNN_SLAB_BLOCKS, *rnn_w, xbuf, a_scr, b_scr):
            if nxt is None and i + 1 < n_slabs:
                nxt = slab_inputs(i + 1)
            next(others, lambda: None)()
    for other in others:
        other()
    _rnn_scan_finish(jnp.concatenate(gy, axis=-1), rnn_ref, hl_ref, a_scr, b_scr, h_scr, hcar)


RNN_SLAB_BLOCKS = 2


def _proj(x, g, w_in, cos_t, sin_t, qkg, batch, seq, tm, precise, rnn_weights=None):
    n = batch * seq
    nt = seq // tm
    row = lambda b, t: (b * nt + t, 0)
    per_seq = lambda b, t: (b, 0, 0)
    act = F32 if precise else BF16
    rows_out = lambda width, dtype: (jax.ShapeDtypeStruct((n, width), dtype),
                                     pl.BlockSpec((tm, width), row))
    state_out = (jax.ShapeDtypeStruct((batch, SUBLANES, D_RNN), F32),
                 pl.BlockSpec((1, SUBLANES, D_RNN), per_seq))
    outs = [rows_out(D_Q, act), rows_out(D_KV, F32), rows_out(D_KV, F32)]
    in_specs = [
        pl.BlockSpec((tm, D_MODEL), row),
        _full((1, D_MODEL)),
        _full((D_MODEL, D_IN), single_buffer=True),
        pl.BlockSpec((tm, HEAD_DIM // 2), lambda b, t: (t, 0)),
        pl.BlockSpec((tm, HEAD_DIM // 2), lambda b, t: (t, 0)),
        _full((1, OFF_V)),
    ]
    args = [x, g, w_in, cos_t, sin_t, qkg]
    scratch = []
    if rnn_weights is None:
        outs += [rows_out(D_RNN, F32), rows_out(D_RNN, act)]
    else:
        assert not precise
        outs += [rows_out(D_RNN, act)]
        in_specs += [_full(w.shape) for w in rnn_weights]
        args += list(rnn_weights)
        groups = tm // SUBLANES
        scratch = [pltpu.VMEM((tm + SUBLANES, D_RNN), F32)]
        scratch += [pltpu.VMEM((groups, SUBLANES, D_RNN), F32)] * 3
        scratch += [pltpu.VMEM((SUBLANES, D_RNN), F32)]
    outs += [rows_out(D_MODEL, act), rows_out(D_MODEL, act)]
    if rnn_weights is not None:
        outs += [state_out, state_out]
    else:
        assert batch * nt == 1 and w_in.dtype == F32
        outs += [(jax.ShapeDtypeStruct(w_in.shape, BF16), _full(w_in.shape, single_buffer=True))]
    return pl.pallas_call(
        functools.partial(_proj_kernel, precise=precise, fuse_rnn=rnn_weights is not None),
        grid=(batch, nt),
        in_specs=in_specs,
        out_specs=[spec for _, spec in outs],
        out_shape=[shape for shape, _ in outs],
        scratch_shapes=scratch,
        compiler_params=_params("parallel", "arbitrary"),
        name="proj",
    )(*args)


def _softmax_pv(s, sink, v2, precise):
    m = jnp.maximum(jnp.max(s, axis=-1, keepdims=True), sink)
    p = jnp.exp(s - m)
    denom = jnp.sum(p, axis=-1, keepdims=True) + jnp.exp(sink - m)
    return _mm(p, v2, precise) * (1.0 / denom)


def _sink_column(sink_ref, h, rows, rows_per_head):
    r = lax.broadcasted_iota(jnp.int32, (rows, 1), 0) // rows_per_head
    col = jnp.full((rows, 1), sink_ref[h * Q_PER_KV], F32)
    for g in range(1, Q_PER_KV):
        col = jnp.where(r == g, sink_ref[h * Q_PER_KV + g], col)
    return col


ATTN_BLOCKS_PER_STEP = 4


def _attn_prompt_kernel(sink_ref, q_ref, kc_ref, kp_ref, vc_ref, vp_ref, o_ref):
    n = pl.program_id(1)
    blk = ATTN_BLOCK
    k_all = jnp.concatenate([kp_ref[...], kc_ref[...]], axis=0).astype(BF16)
    v_all = jnp.concatenate([vp_ref[...], vc_ref[...]], axis=0).astype(BF16)
    rows = Q_PER_KV * blk
    i = lax.broadcasted_iota(jnp.int32, (rows, 2 * blk), 0) % blk
    j = lax.broadcasted_iota(jnp.int32, (rows, 2 * blk), 1)
    d = j - i
    in_window = (d >= 1) & (d <= WINDOW)
    first_valid = (d >= jnp.where(n > 0, 1, jnp.maximum(1, blk - i))) & (d <= WINDOW)
    def kv_head(sub, h):
        q = q_ref[sub * blk:(sub + 1) * blk, :]
        kh = k_all[sub * blk:(sub + 2) * blk, h * HEAD_DIM:(h + 1) * HEAD_DIM]
        vh = v_all[sub * blk:(sub + 2) * blk, h * HEAD_DIM:(h + 1) * HEAD_DIM]
        valid = first_valid if sub == 0 else in_window
        qs = jnp.concatenate(
            [q[:, (h * Q_PER_KV + g) * HEAD_DIM:(h * Q_PER_KV + g + 1) * HEAD_DIM]
             for g in range(Q_PER_KV)], axis=0)
        s = _mm(qs, kh, False, _NT)
        yield
        s = jnp.where(valid, s, NEG_INF)
        sink = _sink_column(sink_ref, h, rows, blk)
        m = jnp.maximum(jnp.max(s, axis=-1, keepdims=True), sink)
        p = jnp.exp(s - m)
        denom = jnp.sum(p, axis=-1, keepdims=True) + jnp.exp(sink - m)
        yield
        o = _mm(p, vh, False) * (1.0 / denom)
        for g in range(Q_PER_KV):
            c = (h * Q_PER_KV + g) * HEAD_DIM
            o_ref[sub * blk:(sub + 1) * blk, c:c + HEAD_DIM] = (
                o[g * blk:(g + 1) * blk].astype(o_ref.dtype))

    heads = [kv_head(sub, h) for sub in range(ATTN_BLOCKS_PER_STEP) for h in range(N_KV_HEADS)]
    while heads:
        heads = [head for head in heads if next(head, "done") != "done"]


def _attn_prompt(q, k, v, sinks, batch, seq):
    step = ATTN_BLOCKS_PER_STEP * ATTN_BLOCK
    ns = seq // step
    cur = lambda b, n: (b * ns + n, 0)
    prev = lambda b, n: (jnp.maximum((b * ns + n) * ATTN_BLOCKS_PER_STEP - 1, 0), 0)
    return pl.pallas_call(
        _attn_prompt_kernel,
        grid=(batch, ns),
        in_specs=[
            pl.BlockSpec(memory_space=pltpu.SMEM),
            pl.BlockSpec((step, D_Q), cur),
            pl.BlockSpec((step, D_KV), cur),
            pl.BlockSpec((ATTN_BLOCK, D_KV), prev),
            pl.BlockSpec((step, D_KV), cur),
            pl.BlockSpec((ATTN_BLOCK, D_KV), prev),
        ],
        out_specs=pl.BlockSpec((step, D_Q), cur),
        out_shape=jax.ShapeDtypeStruct((batch * seq, D_Q), BF16),
        compiler_params=_params("parallel", "parallel"),
        name="attn_prompt",
    )(sinks, q, k, k, v, v)


SAMPLE_BT = 8


def _attn_sample_kernel(sink_ref, q_ref, kn_ref, vn_ref, kc_ref, vc_ref, o_ref, ko_ref, vo_ref):
    bt = SAMPLE_BT
    w = lax.broadcasted_iota(jnp.int32, (bt, WINDOW, D_KV), 1)

    def shifted(cache_ref, new_ref):
        rolled = pltpu.roll(cache_ref[...], WINDOW - 1, axis=1)
        return jnp.where(w == WINDOW - 1, new_ref[...][:, None, :], rolled)

    k_win = shifted(kc_ref, kn_ref)
    v_win = shifted(vc_ref, vn_ref)
    ko_ref[...] = k_win
    vo_ref[...] = v_win
    k2 = k_win.reshape(bt * WINDOW, D_KV)
    v2 = v_win.reshape(bt * WINDOW, D_KV)
    q = q_ref[...]
    rows = Q_PER_KV * bt
    rb = lax.broadcasted_iota(jnp.int32, (rows, bt * WINDOW), 0) % bt
    cb = lax.broadcasted_iota(jnp.int32, (rows, bt * WINDOW), 1) // WINDOW
    valid = rb == cb
    def kv_head(h):
        kh = k2[:, h * HEAD_DIM:(h + 1) * HEAD_DIM]
        vh = v2[:, h * HEAD_DIM:(h + 1) * HEAD_DIM]
        qs = jnp.concatenate(
            [q[:, (h * Q_PER_KV + g) * HEAD_DIM:(h * Q_PER_KV + g + 1) * HEAD_DIM]
             for g in range(Q_PER_KV)], axis=0)
        s = _mm(qs, kh, True, _NT)
        yield
        s = jnp.where(valid, s, NEG_INF)
        o = _softmax_pv(s, _sink_column(sink_ref, h, rows, bt), vh, True)
        for g in range(Q_PER_KV):
            c = (h * Q_PER_KV + g) * HEAD_DIM
            o_ref[:, c:c + HEAD_DIM] = o[g * bt:(g + 1) * bt].astype(o_ref.dtype)

    heads = [kv_head(h) for h in range(N_KV_HEADS)]
    while heads:
        heads = [head for head in heads if next(head, "done") != "done"]


def _attn_sample(q, k_new, v_new, cache_k, cache_v, sinks):
    nbatch = q.shape[0]
    bt = SAMPLE_BT
    row = lambda i: (i, 0)
    win = lambda i: (i, 0, 0)
    return pl.pallas_call(
        _attn_sample_kernel,
        grid=(nbatch // bt,),
        in_specs=[
            pl.BlockSpec(memory_space=pltpu.SMEM),
            pl.BlockSpec((bt, D_Q), row),
            pl.BlockSpec((bt, D_KV), row),
            pl.BlockSpec((bt, D_KV), row),
            pl.BlockSpec((bt, WINDOW, D_KV), win),
            pl.BlockSpec((bt, WINDOW, D_KV), win),
        ],
        out_specs=[
            pl.BlockSpec((bt, D_Q), row),
            pl.BlockSpec((bt, WINDOW, D_KV), win),
            pl.BlockSpec((bt, WINDOW, D_KV), win),
        ],
        out_shape=(
            jax.ShapeDtypeStruct((nbatch, D_Q), F32),
            jax.ShapeDtypeStruct((nbatch, WINDOW, D_KV), F32),
            jax.ShapeDtypeStruct((nbatch, WINDOW, D_KV), F32),
        ),
        compiler_params=_params("parallel"),
        name="attn_sample",
    )(sinks, q, k_new, v_new, cache_k, cache_v)


def _lru_terms(xc, wcat_ref, ba_ref, bi_ref, lam_ref, precise, block0=0):
    cols = slice(block0 * RNN_BLOCK, block0 * RNN_BLOCK + xc.shape[1])
    xcb = xc if precise else xc.astype(BF16)
    ya, yi = [], []
    for n in range(xc.shape[1] // RNN_BLOCK):
        y = _mm(xcb[:, n * RNN_BLOCK:(n + 1) * RNN_BLOCK], wcat_ref[block0 + n], precise)
        ya.append(y[:, :RNN_BLOCK])
        yi.append(y[:, RNN_BLOCK:])
    r = _sigmoid(jnp.concatenate(ya, axis=-1) + ba_ref[:, cols])
    gate_i = _sigmoid(jnp.concatenate(yi, axis=-1) + bi_ref[:, cols])
    neg_lam = -lam_ref[:, cols]
    softplus = jnp.maximum(neg_lam, 0.0) + jnp.log1p(jnp.exp(-jnp.abs(neg_lam)))
    log_a = (-LRU_C * softplus) * r
    a = jnp.exp(log_a)
    m = 1.0 - a * a
    b = jnp.where(m > 0.0, m * lax.rsqrt(m), 0.0) * (gate_i * xc)
    return a, b


def _rnn_start_sequence(xbuf, hcar):
    xbuf[0:SUBLANES, :] = jnp.zeros((SUBLANES, D_RNN), F32)
    hcar[...] = jnp.zeros((SUBLANES, D_RNN), F32)


def _rnn_scan_terms(x, block0, cw_ref, cb_ref, wcat_ref, ba_ref, bi_ref, lam_ref,
                    xbuf, a_scr, b_scr):
    tt, width = x.shape
    groups = tt // SUBLANES
    cols = slice(block0 * RNN_BLOCK, block0 * RNN_BLOCK + width)
    xbuf[SUBLANES:, cols] = x
    xc = cb_ref[:, cols] + cw_ref[CONV_W - 1:CONV_W, cols] * x
    for j in range(CONV_W - 1):
        s = CONV_W - 1 - j
        xc = xc + cw_ref[j:j + 1, cols] * xbuf[SUBLANES - s:SUBLANES - s + tt, cols]
    xbuf[0:SUBLANES, cols] = x[tt - SUBLANES:, :]
    yield
    a, b = _lru_terms(xc, wcat_ref, ba_ref, bi_ref, lam_ref, False, block0)
    yield
    a = a.reshape(groups, SUBLANES, width)
    b = b.reshape(groups, SUBLANES, width)
    step = lax.broadcasted_iota(jnp.int32, (groups, SUBLANES, width), 1)
    k = 1
    while k < SUBLANES:
        keep = step >= k
        a_sh = jnp.where(keep, pltpu.roll(a, k, axis=1), 1.0)
        b_sh = jnp.where(keep, pltpu.roll(b, k, axis=1), 0.0)
        b = a * b_sh + b
        a = a * a_sh
        k *= 2
        if k < SUBLANES:
            yield
    a_scr[:, :, cols] = a
    b_scr[:, :, cols] = b


def _rnn_scan_finish(gy, o_ref, hl_ref, a_scr, b_scr, h_scr, hcar):
    groups = a_scr.shape[0]
    tt = groups * SUBLANES

    def chain(g, h_in):
        h = a_scr[g] * h_in + b_scr[g]
        h_scr[g] = h
        return jnp.broadcast_to(h[SUBLANES - 1:SUBLANES, :], (SUBLANES, D_RNN))

    h_last = lax.fori_loop(0, groups, chain, hcar[...], unroll=True)
    hcar[...] = h_last
    hl_ref[0] = h_last
    h = h_scr[...].reshape(tt, D_RNN)
    o_ref[...] = (h * gy).astype(o_ref.dtype)


def _rnn_sample_kernel(xr_ref, gy_ref, s0_ref, s1_ref, s2_ref, h_ref, cw_ref, cb_ref,
                       wcat_ref, ba_ref, bi_ref, lam_ref, o_ref, hn_ref):
    x = xr_ref[...]
    xc = (cb_ref[...] + cw_ref[0:1, :] * s0_ref[...] + cw_ref[1:2, :] * s1_ref[...]
          + cw_ref[2:3, :] * s2_ref[...] + cw_ref[3:4, :] * x)
    a, b = _lru_terms(xc, wcat_ref, ba_ref, bi_ref, lam_ref, True)
    h = a * h_ref[...] + b
    hn_ref[...] = h
    o_ref[...] = (h * gy_ref[...].astype(F32)).astype(o_ref.dtype)


def _rnn_sample(xr, gy, s0, s1, s2, h_prev, conv_w, conv_b, wcat, b_a, b_i, lam):
    n = xr.shape[0]
    act = _full((n, D_RNN))
    return pl.pallas_call(
        _rnn_sample_kernel,
        grid=(1,),
        in_specs=[act, act, act, act, act, act,
                  _full((CONV_W, D_RNN)), _full((1, D_RNN)),
                  _full((N_RNN_BLOCKS, RNN_BLOCK, 2 * RNN_BLOCK)),
                  _full((1, D_RNN)), _full((1, D_RNN)), _full((1, D_RNN))],
        out_specs=[act, act],
        out_shape=(jax.ShapeDtypeStruct((n, D_RNN), F32),
                   jax.ShapeDtypeStruct((n, D_RNN), F32)),
        compiler_params=_params("arbitrary"),
        name="rnn_sample",
    )(xr, gy, s0, s1, s2, h_prev, conv_w, conv_b, wcat, b_a, b_i, lam)


def _merge_kernel(x_ref, at_ref, rn_ref, sga_ref, sgr_ref, wa_ref, wr_ref, wo_ref, g_ref,
                  wrt_ref, brt_ref, tri_ref, *rest, precise, tiles_per_seg, cast_experts):
    if cast_experts:
        wg_ref, wu_ref, wd_ref, *rest = rest
        x2_ref, xtm_ref, rec_ref, rect_ref, cnt_ref, wg_o, wu_o, wd_o, cnt_scr = rest
    else:
        x2_ref, xtm_ref, rec_ref, rect_ref, cnt_ref, cnt_scr = rest

    @pl.when(pl.program_id(0) % tiles_per_seg == 0)
    def _():
        cnt_scr[...] = jnp.zeros_like(cnt_scr)

    tm = x_ref.shape[0]
    parts = MERGE_PARTS if tm % (MERGE_PARTS * LANES) == 0 else 1
    part = tm // parts
    state = {"counts": cnt_scr[...]}
    pending = [
        _merge_rows(slice(i * part, (i + 1) * part), state, x_ref, at_ref, rn_ref, sga_ref, sgr_ref,
                    wa_ref, wr_ref, wo_ref, g_ref, wrt_ref, brt_ref, tri_ref, x2_ref, xtm_ref,
                    rec_ref, rect_ref, precise)
        for i in range(parts)]
    active, step = [], 0
    while pending or active:
        if pending and step % MERGE_STAGGER == 0:
            active.append(pending.pop(0))
        for part_pieces in list(active):
            if next(part_pieces, "done") == "done":
                active.remove(part_pieces)
        step += 1
    cnt_scr[...] = state["counts"]
    cnt_ref[0] = state["counts"]
    if cast_experts:
        wg_o[...] = wg_ref[...].astype(BF16)
        wu_o[...] = wu_ref[...].astype(BF16)
        wd_o[...] = wd_ref[...].astype(BF16)


MERGE_PARTS = 2
MERGE_STAGGER = 3


def _merge_rows(rows, state, x_ref, at_ref, rn_ref, sga_ref, sgr_ref, wa_ref, wr_ref, wo_ref, g_ref,
                wrt_ref, brt_ref, tri_ref, x2_ref, xtm_ref, rec_ref, rect_ref, precise):
    ya = _mm(at_ref[rows, :], wa_ref[...], precise)
    yield
    yr = _mm(rn_ref[rows, :], wr_ref[...], precise)
    yield
    merged = sga_ref[rows, :].astype(F32) * ya + sgr_ref[rows, :].astype(F32) * yr
    yield
    x2 = x_ref[rows, :] + _mm(merged, wo_ref[...], precise)
    x2_ref[rows, :] = x2
    yield
    inv = lax.rsqrt(jnp.mean(x2 * x2, axis=-1, keepdims=True) + EPS)
    xn = x2 * inv * g_ref[...]
    for c in range(TOKEN_ROWS):
        xtm_ref[pl.ds(rows.start * TOKEN_ROWS + c, rows.stop - rows.start, stride=TOKEN_ROWS), :] = (
            xn[:, c * LANES:(c + 1) * LANES])
    yield

    logits = _mm(xn, wrt_ref[...], precise) + brt_ref[...]
    yield
    tm = logits.shape[0]
    lane_i = lax.broadcasted_iota(jnp.int32, (tm, LANES), 1)
    lane = lane_i.astype(F32)
    big = float(LANES)
    is_grp = (lane_i >= N_EXPERTS) & (lane_i < N_EXPERTS + N_GROUPS)
    gl = jnp.where(is_grp, logits, NEG_INF)
    gmax = jnp.max(gl, axis=-1, keepdims=True)
    g_idx = jnp.min(jnp.where(gl == gmax, lane, big), axis=-1, keepdims=True) - N_EXPERTS
    p_g = 1.0 / jnp.sum(jnp.exp(gl - gmax), axis=-1, keepdims=True)
    in_grp = (lane_i // EXPERTS_PER_GROUP).astype(F32) == g_idx
    el = jnp.where(in_grp, logits, NEG_INF)
    v1 = jnp.max(el, axis=-1, keepdims=True)
    i1 = jnp.min(jnp.where(el == v1, lane, big), axis=-1, keepdims=True)
    el2 = jnp.where(lane == i1, NEG_INF, el)
    v2 = jnp.max(el2, axis=-1, keepdims=True)
    i2 = jnp.min(jnp.where(el2 == v2, lane, big), axis=-1, keepdims=True)
    e2 = jnp.exp(v2 - v1)
    w1 = p_g / (1.0 + e2)
    w2 = p_g * e2 / (1.0 + e2)
    yield

    hit = jnp.where(lane == i1, 1.0, jnp.where(lane == i2, 1.0, 0.0))
    counts = state["counts"]
    state["counts"] = counts + jnp.sum(hit, axis=0, keepdims=True)
    before = (jnp.dot(tri_ref[0:tm, 0:tm], hit.astype(BF16), preferred_element_type=F32)
              + counts[0:1, :])
    yield
    r1 = jnp.sum(jnp.where(lane == i1, before, 0.0), axis=-1, keepdims=True)
    r2 = jnp.sum(jnp.where(lane == i2, before, 0.0), axis=-1, keepdims=True)
    rec = jnp.where(lane == REC_W2, w2, 0.0)
    for field, val in ((REC_W1, w1), (REC_R2, r2), (REC_R1, r1), (REC_E2, i2), (REC_E1, i1)):
        rec = jnp.where(lane == field, val, rec)
    rec_ref[rows, :] = rec
    rect_ref[:, rows] = rec.T


REC_E1, REC_E2, REC_R1, REC_R2, REC_W1, REC_W2 = range(6)
TOKEN_ROWS = D_MODEL // LANES


def _merge(x, attn, rnn, sga, sgr, wa, wr, wo, g2, w_route, b_route, tri, tm, seg, precise,
           expert_weights=None):
    n = x.shape[0]
    steps = n // tm
    tiles_per_seg = seg // tm
    row = lambda i: (i, 0)
    in_specs = [
        pl.BlockSpec((tm, D_MODEL), row),
        pl.BlockSpec((tm, D_Q), row),
        pl.BlockSpec((tm, D_RNN), row),
        pl.BlockSpec((tm, D_MODEL), row),
        pl.BlockSpec((tm, D_MODEL), row),
        _full((D_Q, D_MODEL)),
        _full((D_RNN, D_MODEL)),
        _full((D_MODEL, D_MODEL)),
        _full((1, D_MODEL)),
        _full((D_MODEL, LANES)),
        _full((1, LANES)),
        _full((tm, tm)),
    ]
    out_specs = [
        pl.BlockSpec((tm, D_MODEL), row),
        pl.BlockSpec((tm * TOKEN_ROWS, LANES), row),
        pl.BlockSpec((tm, LANES), row),
        pl.BlockSpec((LANES, tm), lambda i: (0, i)),
        pl.BlockSpec((1, SUBLANES, LANES), lambda i: (i // tiles_per_seg, 0, 0)),
    ]
    out_shape = [
        jax.ShapeDtypeStruct((n, D_MODEL), F32),
        jax.ShapeDtypeStruct((n * TOKEN_ROWS, LANES), F32),
        jax.ShapeDtypeStruct((n, LANES), F32),
        jax.ShapeDtypeStruct((LANES, n), F32),
        jax.ShapeDtypeStruct((n // seg, SUBLANES, LANES), F32),
    ]
    args = [x, attn, rnn, sga, sgr, wa, wr, wo, g2, w_route, b_route, tri]
    if expert_weights is not None:
        assert steps == N_EXPERTS
        for w in expert_weights:
            spec = pl.BlockSpec((1,) + w.shape[1:], lambda i: (i, 0, 0))
            in_specs.append(spec)
            out_specs.append(spec)
            out_shape.append(jax.ShapeDtypeStruct(w.shape, BF16))
            args.append(w)
    return pl.pallas_call(
        functools.partial(_merge_kernel, precise=precise, tiles_per_seg=tiles_per_seg,
                          cast_experts=expert_weights is not None),
        grid=(steps,),
        in_specs=in_specs,
        out_specs=out_specs,
        out_shape=out_shape,
        scratch_shapes=[pltpu.VMEM((SUBLANES, LANES), F32)],
        compiler_params=_params("arbitrary"),
        name="merge",
    )(*args)


MOE_CHUNK = 256
MOE_TAIL = 128
MOE_EXPERTS_PER_STEP = 2
MOE_VMEM_LIMIT = 60 * 1024 * 1024


def _seg_rows(seg):
    return 2 * seg + N_EXPERTS * SUBLANES + MOE_CHUNK


def _token_rows(i):
    return pl.ds(pl.multiple_of(i * TOKEN_ROWS, TOKEN_ROWS), TOKEN_ROWS)


def _sorted_rows(first_row):
    return pl.ds(pl.multiple_of(first_row, TOKEN_ROWS), TOKEN_ROWS)


def _moe_kernel(off_ref, nfull_ref, rem_ref, slot_ref, xtm_ref, wg_ref, wu_ref, wd_ref,
                x2_ref, rec_ref, o_ref, buf, g1, g2, *, seg, td, tc):
    s = pl.program_id(0)
    p = pl.program_id(1)
    n_disp = seg // td
    n_exp = N_EXPERTS // MOE_EXPERTS_PER_STEP

    @pl.when((s == 0) & (p == 0))
    def _():
        buf[...] = jnp.zeros_like(buf)

    @pl.when(p < n_disp)
    def _():
        def dispatch(g, carry):
            for j in range(SUBLANES):
                t = g * SUBLANES + j
                row = xtm_ref[_token_rows(t), :]
                for k in range(2):
                    buf[_sorted_rows(slot_ref[0, 0, k * seg + p * td + t]), :] = row
            return carry

        lax.fori_loop(0, td // SUBLANES, dispatch, 0)

    def run_chunk(e, row0, rows, valid):
        r0 = pl.multiple_of(row0 * TOKEN_ROWS, SUBLANES * TOKEN_ROWS)
        xf = [buf[pl.ds(r0 + j, rows, stride=TOKEN_ROWS), :] for j in range(TOKEN_ROWS)]
        x = jnp.concatenate(xf, axis=-1).astype(BF16)
        hg = jnp.dot(x, wg_ref[e], preferred_element_type=F32)
        hu = jnp.dot(x, wu_ref[e], preferred_element_type=F32)
        h = (hg * _sigmoid(hg)) * hu
        y = jnp.dot(h.astype(BF16), wd_ref[e], preferred_element_type=F32)
        if valid is not None:
            mine = lax.broadcasted_iota(jnp.int32, (rows, LANES), 0) < valid
        for j in range(TOKEN_ROWS):
            yj = y[:, j * LANES:(j + 1) * LANES]
            if valid is not None:
                yj = jnp.where(mine, yj, xf[j])
            buf[pl.ds(r0 + j, rows, stride=TOKEN_ROWS), :] = yj

    def run_expert(e):
        idx = s * N_EXPERTS + (p - n_disp) * MOE_EXPERTS_PER_STEP + e
        base = off_ref[idx]
        n_full = nfull_ref[idx]
        rem = rem_ref[idx]

        def chunk(c, carry):
            run_chunk(e, base + c * MOE_CHUNK, MOE_CHUNK, None)
            return carry

        lax.fori_loop(0, n_full, chunk, 0)
        last = base + n_full * MOE_CHUNK
        for units in range(1, MOE_CHUNK // MOE_TAIL + 2):
            @pl.when((rem > (units - 1) * MOE_TAIL) & (rem <= units * MOE_TAIL))
            def _():
                run_chunk(e, last, units * MOE_TAIL, rem)

    @pl.when((p >= n_disp) & (p < n_disp + n_exp))
    def _():
        for e in range(MOE_EXPERTS_PER_STEP):
            run_expert(e)

    @pl.when(p >= n_disp + n_exp)
    def _():
        t0 = (p - n_disp - n_exp) * tc

        def gather(g, carry):
            for j in range(SUBLANES):
                t = g * SUBLANES + j
                g1[_token_rows(t), :] = buf[_sorted_rows(slot_ref[0, 0, t0 + t]), :]
                g2[_token_rows(t), :] = buf[_sorted_rows(slot_ref[0, 0, seg + t0 + t]), :]
            return carry

        lax.fori_loop(0, tc // SUBLANES, gather, 0)
        rec = rec_ref[...]
        lane = lax.broadcasted_iota(jnp.int32, rec.shape, 1)
        w1 = jnp.broadcast_to(
            jnp.sum(jnp.where(lane == REC_W1, rec, 0.0), axis=-1, keepdims=True), rec.shape)
        w2 = jnp.broadcast_to(
            jnp.sum(jnp.where(lane == REC_W2, rec, 0.0), axis=-1, keepdims=True), rec.shape)
        for j in range(TOKEN_ROWS):
            cols = slice(j * LANES, (j + 1) * LANES)
            o_ref[:, cols] = (x2_ref[:, cols] + w1 * g1[pl.ds(j, tc, stride=TOKEN_ROWS), :]
                              + w2 * g2[pl.ds(j, tc, stride=TOKEN_ROWS), :])


def _moe(off, nfull, rem, slot, xtm, wg, wu, wd, x2, rec, seg, td, tc):
    n = x2.shape[0]
    n_seg = n // seg
    n_disp, n_comb = seg // td, seg // tc
    per_step = MOE_EXPERTS_PER_STEP
    n_exp = N_EXPERTS // per_step
    rows = _seg_rows(seg) * TOKEN_ROWS
    disp_tile = lambda s, p, *_: (s * n_disp + jnp.minimum(p, n_disp - 1), 0)
    expert = lambda s, p, *_: (jnp.clip(p - n_disp, 0, n_exp - 1), 0, 0)
    comb_tile = lambda s, p, *_: (s * n_comb + jnp.clip(p - n_disp - n_exp, 0, n_comb - 1), 0)
    grid_spec = pltpu.PrefetchScalarGridSpec(
        num_scalar_prefetch=3,
        grid=(n_seg, n_disp + n_exp + n_comb),
        in_specs=[
            pl.BlockSpec((1, 1, 2 * seg), lambda s, p, *_: (s, 0, 0), memory_space=pltpu.SMEM),
            pl.BlockSpec((td * TOKEN_ROWS, LANES), disp_tile),
            pl.BlockSpec((per_step, D_MODEL, D_EXPERT), expert),
            pl.BlockSpec((per_step, D_MODEL, D_EXPERT), expert),
            pl.BlockSpec((per_step, D_EXPERT, D_MODEL), expert),
            pl.BlockSpec((tc, D_MODEL), comb_tile),
            pl.BlockSpec((tc, LANES), comb_tile),
        ],
        out_specs=pl.BlockSpec((tc, D_MODEL), comb_tile),
        scratch_shapes=[pltpu.VMEM((rows, LANES), F32),
                        pltpu.VMEM((tc * TOKEN_ROWS, LANES), F32),
                        pltpu.VMEM((tc * TOKEN_ROWS, LANES), F32)],
    )
    return pl.pallas_call(
        functools.partial(_moe_kernel, seg=seg, td=td, tc=tc),
        grid_spec=grid_spec,
        out_shape=jax.ShapeDtypeStruct((n, D_MODEL), F32),
        compiler_params=pltpu.CompilerParams(
            dimension_semantics=("arbitrary", "arbitrary"), vmem_limit_bytes=MOE_VMEM_LIMIT),
        name="moe",
    )(off, nfull, rem, slot, xtm, wg, wu, wd, x2, rec)


def _plan(rect, cnt, seg):
    n = rect.shape[1]
    expert = rect[REC_E1:REC_E2 + 1].astype(jnp.int32)
    rank = rect[REC_R1:REC_R2 + 1].astype(jnp.int32)
    counts = cnt[:, 0, :N_EXPERTS].astype(jnp.int32)
    padded = (counts + SUBLANES - 1) // SUBLANES * SUBLANES
    off = jnp.cumsum(padded, axis=1) - padded
    n_full = counts // MOE_CHUNK
    rem = counts - n_full * MOE_CHUNK
    join = (rem > 0) & (rem <= MOE_TAIL) & (n_full > 0)
    n_full = n_full - join
    rem = rem + join * MOE_CHUNK
    off_tok = jnp.repeat(off.T, seg, axis=1)
    hit = expert[:, None, :] == jnp.arange(N_EXPERTS, dtype=jnp.int32)[None, :, None]
    slot = (rank + jnp.sum(jnp.where(hit, off_tok[None], 0), axis=1)) * TOKEN_ROWS
    slot = slot.reshape(2, n // seg, seg).transpose(1, 0, 2).reshape(n // seg, 1, 2 * seg)
    return off.reshape(-1), n_full.reshape(-1), rem.reshape(-1), slot


def _rope_tables(pos):
    half = HEAD_DIM // 2
    inv_freq = ROPE_THETA ** (-jnp.arange(half, dtype=F32) / half)
    ang = pos[:, None] * inv_freq[None, :]
    return jnp.cos(ang), jnp.sin(ang)


def kernel(x_prompt, x_sample, cache_k_win, cache_v_win, state_conv, state_lru_h, attn_norm_g, w_in, q_norm_g, k_norm_g, attn_sinks, conv_w, conv_b, w_lru_a, b_lru_a, w_lru_i, b_lru_i, lru_lambda, w_br_attn, w_br_rnn, w_out, ffn_norm_g, w_route_group, b_route_group, w_route_expert, b_route_expert, w_exp_gate, w_exp_up, w_exp_down):
    batch, seq, _ = x_prompt.shape
    dec_batch, dec_seq, _ = x_sample.shape
    depth = w_in.shape[0]
    assert depth == 1 and dec_seq == 1
    l = 0

    w_in_f = w_in[l]
    qkg = jnp.concatenate([jnp.tile(q_norm_g[l], N_Q_HEADS), jnp.tile(k_norm_g[l], N_KV_HEADS)])[None, :]
    wcat_f = jnp.concatenate([w_lru_a[l], w_lru_i[l]], axis=-1)
    wa_f, wr_f, wo_f = w_br_attn[l], w_br_rnn[l], w_out[l]
    w_route_f = jnp.concatenate(
        [w_route_expert[l], w_route_group[l],
         jnp.zeros((D_MODEL, LANES - N_EXPERTS - N_GROUPS), F32)], axis=-1)
    wcat = wcat_f.astype(BF16)
    wa_b, wr_b, wo_b, w_route = (w.astype(BF16) for w in (wa_f, wr_f, wo_f, w_route_f))
    b_route = jnp.concatenate(
        [b_route_expert[l], b_route_group[l], jnp.zeros((LANES - N_EXPERTS - N_GROUPS,), F32)])[None, :]
    experts_f = (w_exp_gate[l], w_exp_up[l], w_exp_down[l])
    g1 = attn_norm_g[l][None, :]
    g2 = ffn_norm_g[l][None, :]
    cw, cb = conv_w[l], conv_b[l][None, :]
    b_a, b_i, lam = b_lru_a[l][None, :], b_lru_i[l][None, :], lru_lambda[l][None, :]
    sinks = attn_sinks[l]

    def tail(x, attn, rnn, sga, sgr, tm, seg, precise, experts_b=None):
        wa, wr, wo, wrt = (wa_f, wr_f, wo_f, w_route_f) if precise else (wa_b, wr_b, wo_b, w_route)
        tri = jnp.tril(jnp.ones((tm, tm), BF16), -1)
        outs = _merge(x, attn, rnn, sga, sgr, wa, wr, wo, g2, wrt, b_route, tri, tm, seg, precise,
                      expert_weights=None if experts_b else experts_f)
        x2, xtm, rec, rect, cnt = outs[:5]
        wg, wu, wd = experts_b or outs[5:]
        off, nfull, rem, slot = _plan(rect, cnt, seg)
        y = _moe(off, nfull, rem, slot, xtm, wg, wu, wd, x2, rec, seg, tm, tm)
        return y, (wg, wu, wd)

    xs = x_sample.reshape(dec_batch, D_MODEL)
    cos_s, sin_s = _rope_tables(jnp.full((dec_batch,), PAST_LEN, F32))
    qs, ks, vs, xrs, gys, sgas, sgrs, w_in_b = _proj(xs, g1, w_in_f, cos_s, sin_s, qkg, 1, dec_batch,
                                                     dec_batch, True)

    xp = x_prompt.reshape(batch * seq, D_MODEL)
    cos_p, sin_p = _rope_tables(jnp.arange(seq, dtype=F32))
    q, k, v, rnn, sga, sgr, h_last, conv_tail = _proj(
        xp, g1, w_in_b, cos_p, sin_p, qkg, batch, seq, 512, False,
        rnn_weights=(cw, cb, wcat, b_a, b_i, lam))
    attn = _attn_prompt(q, k, v, sinks, batch, seq)
    y_prompt, experts_b = tail(xp, attn, rnn, sga, sgr, 512, 4096, False)
    y_prompt = y_prompt.reshape(batch, seq, D_MODEL)

    def last_rows(a, rows):
        return a.reshape(batch, seq, a.shape[-1])[:, seq - rows:]

    k_win_p = last_rows(k, WINDOW).reshape(1, batch, WINDOW, N_KV_HEADS, HEAD_DIM)
    v_win_p = last_rows(v, WINDOW).reshape(1, batch, WINDOW, N_KV_HEADS, HEAD_DIM)
    conv_p = conv_tail[None, :, SUBLANES - (CONV_W - 1):, :]
    h_p = h_last[None, :, 0, :]

    ck = cache_k_win[l].reshape(dec_batch, WINDOW, D_KV)
    cv = cache_v_win[l].reshape(dec_batch, WINDOW, D_KV)
    attn_s, k_win_s, v_win_s = _attn_sample(qs, ks, vs, ck, cv, sinks)
    sc = state_conv[l]
    rnn_s, h_s = _rnn_sample(xrs, gys, sc[:, 0], sc[:, 1], sc[:, 2], state_lru_h[l],
                             cw, cb, wcat_f, b_a, b_i, lam)
    y_sample, _ = tail(xs, attn_s, rnn_s, sgas, sgrs, dec_batch, dec_batch, True, experts_b)
    y_sample = y_sample.reshape(dec_batch, 1, D_MODEL)
    conv_s = jnp.stack([sc[:, 1], sc[:, 2], xrs], axis=1)[None]

    return (y_prompt, y_sample, k_win_p, v_win_p, conv_p, h_p,
            k_win_s.reshape(1, dec_batch, WINDOW, N_KV_HEADS, HEAD_DIM),
            v_win_s.reshape(1, dec_batch, WINDOW, N_KV_HEADS, HEAD_DIM),
            conv_s, h_s[None])
```

```python
import functools

import jax
import jax.numpy as jnp
from jax import lax
from jax.experimental import pallas as pl
from jax.experimental.pallas import tpu as pltpu

D_MODEL = 1024
HEAD_DIM = 64
N_Q_HEADS = 8
N_KV_HEADS = 2
Q_PER_KV = N_Q_HEADS // N_KV_HEADS
WINDOW = 128
ATTN_BLOCK = 128
ROPE_THETA = 10000.0
SCALE = HEAD_DIM ** -0.5
NEG_INF = -1e30
D_RNN = 1280
N_RNN_BLOCKS = 10
RNN_BLOCK = D_RNN // N_RNN_BLOCKS
CONV_W = 4
LRU_C = 8.0
N_GROUPS = 4
EXPERTS_PER_GROUP = 8
N_EXPERTS = N_GROUPS * EXPERTS_PER_GROUP
D_EXPERT = 256
PAST_LEN = 16384
EPS = 1e-6
D_Q = N_Q_HEADS * HEAD_DIM
D_KV = N_KV_HEADS * HEAD_DIM
D_IN = D_Q + 2 * D_KV + 2 * D_RNN + 2 * D_MODEL
OFF_K = D_Q
OFF_V = OFF_K + D_KV
OFF_XR = OFF_V + D_KV
OFF_YR = OFF_XR + D_RNN
OFF_GA = OFF_YR + D_RNN
OFF_GR = OFF_GA + D_MODEL

LANES = 128
SUBLANES = 8
VMEM_LIMIT = 56 * 1024 * 1024

F32 = jnp.float32
BF16 = jnp.bfloat16


def _params(*sem):
    return pltpu.CompilerParams(dimension_semantics=sem, vmem_limit_bytes=VMEM_LIMIT)


def _sigmoid(x):
    return 1.0 / (1.0 + jnp.exp(-x))


def _gelu_tanh(x):
    c = 0.7978845608028654
    half_x = 0.5 * x
    return half_x + half_x * jnp.tanh(x * (c + (c * 0.044715) * (x * x)))


def _full(shape, single_buffer=False):
    index_map = lambda *_: (0,) * len(shape)
    if single_buffer:
        return pl.BlockSpec(shape, index_map, pipeline_mode=pl.Buffered(1))
    return pl.BlockSpec(shape, index_map)


def _mm(a, b, precise, dims=None):
    if precise:
        a, b, prec = a.astype(F32), b.astype(F32), lax.Precision.HIGHEST
    else:
        a, b, prec = a.astype(BF16), b.astype(BF16), None
    if dims is None:
        return jnp.dot(a, b, preferred_element_type=F32, precision=prec)
    return lax.dot_general(a, b, dims, preferred_element_type=F32, precision=prec)


_NT = (((1,), (1,)), ((), ()))


def _proj_kernel(x_ref, g_ref, w_ref, cos_ref, sin_ref, qkg_ref, *rest, precise, fuse_rnn):
    if fuse_rnn:
        rnn_w = rest[:6]
        q_ref, k_ref, v_ref, rnn_ref, sga_ref, sgr_ref, hl_ref, ct_ref = rest[6:14]
        xbuf, a_scr, b_scr, h_scr, hcar = rest[14:]
        pl.when(pl.program_id(1) == 0)(functools.partial(_rnn_start_sequence, xbuf, hcar))
    else:
        q_ref, k_ref, v_ref, xr_ref, gy_ref, sga_ref, sgr_ref, wb_ref = rest
        wb_ref[...] = w_ref[...].astype(BF16)
    x = x_ref[...]
    inv = lax.rsqrt(jnp.mean(x * x, axis=-1, keepdims=True) + EPS)
    xn = x * inv * g_ref[...]
    if not precise:
        xn = xn.astype(BF16)

    def proj(lo, hi):
        return _mm(xn, w_ref[:, lo:hi], precise)

    def qk_heads():
        qk = proj(0, OFF_V)
        tm = qk.shape[0]
        lane = lax.broadcasted_iota(jnp.int32, (tm, LANES), 1)
        lo_head = lane < HEAD_DIM
        first_half = (lane % HEAD_DIM) < (HEAD_DIM // 2)
        cos_f, sin_f = cos_ref[...], sin_ref[...]
        reps = LANES // HEAD_DIM
        cos = jnp.concatenate([cos_f, cos_f] * reps, axis=-1)
        sin = jnp.concatenate([-sin_f, sin_f] * reps, axis=-1)
        for g in range(OFF_V // LANES):
            seg = qk[:, g * LANES:(g + 1) * LANES]
            sq = seg * seg
            s_lo = jnp.sum(jnp.where(lo_head, sq, 0.0), axis=-1, keepdims=True)
            s_hi = jnp.sum(jnp.where(lo_head, 0.0, sq), axis=-1, keepdims=True)
            ms = jnp.where(lo_head, s_lo, s_hi) * (1.0 / HEAD_DIM)
            normed = seg * lax.rsqrt(ms + EPS) * qkg_ref[:, g * LANES:(g + 1) * LANES]
            partner = jnp.where(first_half,
                                pltpu.roll(normed, LANES - HEAD_DIM // 2, axis=1),
                                pltpu.roll(normed, HEAD_DIM // 2, axis=1))
            roped = normed * cos + partner * sin
            if g < D_Q // LANES:
                q_ref[:, g * LANES:(g + 1) * LANES] = (roped * SCALE).astype(q_ref.dtype)
            else:
                k_ref[...] = roped

    def values():
        v_ref[...] = proj(OFF_V, OFF_XR)

    def gate(out_ref, off, c0, c1, r0=0, r1=x.shape[0]):
        y = _mm(xn[r0:r1], w_ref[:, off + c0:off + c1], precise)
        out_ref[r0:r1, c0:c1] = _sigmoid(y).astype(out_ref.dtype)

    if not fuse_rnn:
        xr_ref[...] = proj(OFF_XR, OFF_YR)
        gy_ref[...] = _gelu_tanh(proj(OFF_YR, OFF_GA)).astype(gy_ref.dtype)
        qk_heads()
        values()
        gate(sga_ref, OFF_GA, 0, D_MODEL)
        gate(sgr_ref, OFF_GR, 0, D_MODEL)
        return

    slab = RNN_SLAB_BLOCKS * RNN_BLOCK
    half = x.shape[0] // 2
    others = [qk_heads, values]
    for out_ref, off in ((sga_ref, OFF_GA), (sgr_ref, OFF_GR)):
        others += [functools.partial(gate, out_ref, off, c, c + slab, r, r + half)
                   for c in range(0, D_MODEL, slab) for r in (0, half)]
    others = iter(others)
    n_slabs = D_RNN // slab

    def slab_inputs(i):
        c0 = i * slab
        return (proj(OFF_XR + c0, OFF_XR + c0 + slab),
                _gelu_tanh(proj(OFF_YR + c0, OFF_YR + c0 + slab)))

    gy = []
    nxt = slab_inputs(0)
    for i in range(n_slabs):
        (xr, gy_i), nxt = nxt, None
        gy.append(gy_i)
        ct_ref[0, :, i * slab:(i + 1) * slab] = xr[xr.shape[0] - SUBLANES:, :]
        for _ in _rnn_scan_terms(xr, i * RNN_SLAB_BLOCKS, *rnn_w, xbuf, a_scr, b_scr):
            if nxt is None and i + 1 < n_slabs:
                nxt = slab_inputs(i + 1)
            next(others, lambda: None)()
    for other in others:
        other()
    _rnn_scan_finish(jnp.concatenate(gy, axis=-1), rnn_ref, hl_ref, a_scr, b_scr, h_scr, hcar)


RNN_SLAB_BLOCKS = 2


def _proj(x, g, w_in, cos_t, sin_t, qkg, batch, seq, tm, precise, rnn_weights=None):
    n = batch * seq
    nt = seq // tm
    row = lambda b, t: (b * nt + t, 0)
    per_seq = lambda b, t: (b, 0, 0)
    act = F32 if precise else BF16
    rows_out = lambda width, dtype: (jax.ShapeDtypeStruct((n, width), dtype),
                                     pl.BlockSpec((tm, width), row))
    state_out = (jax.ShapeDtypeStruct((batch, SUBLANES, D_RNN), F32),
                 pl.BlockSpec((1, SUBLANES, D_RNN), per_seq))
    outs = [rows_out(D_Q, act), rows_out(D_KV, F32), rows_out(D_KV, F32)]
    in_specs = [
        pl.BlockSpec((tm, D_MODEL), row),
        _full((1, D_MODEL)),
        _full((D_MODEL, D_IN), single_buffer=True),
        pl.BlockSpec((tm, HEAD_DIM // 2), lambda b, t: (t, 0)),
        pl.BlockSpec((tm, HEAD_DIM // 2), lambda b, t: (t, 0)),
        _full((1, OFF_V)),
    ]
    args = [x, g, w_in, cos_t, sin_t, qkg]
    scratch = []
    if rnn_weights is None:
        outs += [rows_out(D_RNN, F32), rows_out(D_RNN, act)]
    else:
        assert not precise
        outs += [rows_out(D_RNN, act)]
        in_specs += [_full(w.shape) for w in rnn_weights]
        args += list(rnn_weights)
        groups = tm // SUBLANES
        scratch = [pltpu.VMEM((tm + SUBLANES, D_RNN), F32)]
        scratch += [pltpu.VMEM((groups, SUBLANES, D_RNN), F32)] * 3
        scratch += [pltpu.VMEM((SUBLANES, D_RNN), F32)]
    outs += [rows_out(D_MODEL, act), rows_out(D_MODEL, act)]
    if rnn_weights is not None:
        outs += [state_out, state_out]
    else:
        assert batch * nt == 1 and w_in.dtype == F32
        outs += [(jax.ShapeDtypeStruct(w_in.shape, BF16), _full(w_in.shape, single_buffer=True))]
    return pl.pallas_call(
        functools.partial(_proj_kernel, precise=precise, fuse_rnn=rnn_weights is not None),
        grid=(batch, nt),
        in_specs=in_specs,
        out_specs=[spec for _, spec in outs],
        out_shape=[shape for shape, _ in outs],
        scratch_shapes=scratch,
        compiler_params=_params("parallel", "arbitrary"),
        name="proj",
    )(*args)


def _softmax_pv(s, sink, v2, precise):
    m = jnp.maximum(jnp.max(s, axis=-1, keepdims=True), sink)
    p = jnp.exp(s - m)
    denom = jnp.sum(p, axis=-1, keepdims=True) + jnp.exp(sink - m)
    return _mm(p, v2, precise) * (1.0 / denom)


def _sink_column(sink_ref, h, rows, rows_per_head):
    r = lax.broadcasted_iota(jnp.int32, (rows, 1), 0) // rows_per_head
    col = jnp.full((rows, 1), sink_ref[h * Q_PER_KV], F32)
    for g in range(1, Q_PER_KV):
        col = jnp.where(r == g, sink_ref[h * Q_PER_KV + g], col)
    return col


def _window_masks(first_tile):
    blk = ATTN_BLOCK
    rows = Q_PER_KV * blk
    i = lax.broadcasted_iota(jnp.int32, (rows, 2 * blk), 0) % blk
    j = lax.broadcasted_iota(jnp.int32, (rows, 2 * blk), 1)
    d = j - i
    in_window = (d >= 1) & (d <= WINDOW)
    first = (d >= jnp.where(first_tile, jnp.maximum(1, blk - i), 1)) & (d <= WINDOW)
    return in_window, first


def _attn_kv_head(sub, h, valid, sink_ref, q_ref, k_all, v_all, o_ref):
    blk = ATTN_BLOCK
    rows = Q_PER_KV * blk
    q = q_ref[sub * blk:(sub + 1) * blk, :]
    kh = k_all[sub * blk:(sub + 2) * blk, h * HEAD_DIM:(h + 1) * HEAD_DIM]
    vh = v_all[sub * blk:(sub + 2) * blk, h * HEAD_DIM:(h + 1) * HEAD_DIM]
    qs = jnp.concatenate(
        [q[:, (h * Q_PER_KV + g) * HEAD_DIM:(h * Q_PER_KV + g + 1) * HEAD_DIM]
         for g in range(Q_PER_KV)], axis=0)
    s = _mm(qs, kh, False, _NT)
    yield
    s = jnp.where(valid, s, NEG_INF)
    sink = _sink_column(sink_ref, h, rows, blk)
    m = jnp.maximum(jnp.max(s, axis=-1, keepdims=True), sink)
    p = jnp.exp(s - m)
    denom = jnp.sum(p, axis=-1, keepdims=True) + jnp.exp(sink - m)
    yield
    o = _mm(p, vh, False) * (1.0 / denom)
    for g in range(Q_PER_KV):
        c = (h * Q_PER_KV + g) * HEAD_DIM
        o_ref[sub * blk:(sub + 1) * blk, c:c + HEAD_DIM] = o[g * blk:(g + 1) * blk].astype(o_ref.dtype)


def _lockstep(generators):
    while generators:
        generators = [g for g in generators if next(g, "done") != "done"]
        yield


SAMPLE_BT = 8


def _attn_sample_kernel(sink_ref, q_ref, kn_ref, vn_ref, kc_ref, vc_ref, o_ref, ko_ref, vo_ref):
    bt = SAMPLE_BT
    w = lax.broadcasted_iota(jnp.int32, (bt, WINDOW, D_KV), 1)

    def shifted(cache_ref, new_ref):
        rolled = pltpu.roll(cache_ref[...], WINDOW - 1, axis=1)
        return jnp.where(w == WINDOW - 1, new_ref[...][:, None, :], rolled)

    k_win = shifted(kc_ref, kn_ref)
    v_win = shifted(vc_ref, vn_ref)
    ko_ref[...] = k_win
    vo_ref[...] = v_win
    k2 = k_win.reshape(bt * WINDOW, D_KV)
    v2 = v_win.reshape(bt * WINDOW, D_KV)
    q = q_ref[...]
    rows = Q_PER_KV * bt
    rb = lax.broadcasted_iota(jnp.int32, (rows, bt * WINDOW), 0) % bt
    cb = lax.broadcasted_iota(jnp.int32, (rows, bt * WINDOW), 1) // WINDOW
    valid = rb == cb
    def kv_head(h):
        kh = k2[:, h * HEAD_DIM:(h + 1) * HEAD_DIM]
        vh = v2[:, h * HEAD_DIM:(h + 1) * HEAD_DIM]
        qs = jnp.concatenate(
            [q[:, (h * Q_PER_KV + g) * HEAD_DIM:(h * Q_PER_KV + g + 1) * HEAD_DIM]
             for g in range(Q_PER_KV)], axis=0)
        s = _mm(qs, kh, True, _NT)
        yield
        s = jnp.where(valid, s, NEG_INF)
        o = _softmax_pv(s, _sink_column(sink_ref, h, rows, bt), vh, True)
        for g in range(Q_PER_KV):
            c = (h * Q_PER_KV + g) * HEAD_DIM
            o_ref[:, c:c + HEAD_DIM] = o[g * bt:(g + 1) * bt].astype(o_ref.dtype)

    heads = [kv_head(h) for h in range(N_KV_HEADS)]
    while heads:
        heads = [head for head in heads if next(head, "done") != "done"]


def _attn_sample(q, k_new, v_new, cache_k, cache_v, sinks):
    nbatch = q.shape[0]
    bt = SAMPLE_BT
    row = lambda i: (i, 0)
    win = lambda i: (i, 0, 0)
    return pl.pallas_call(
        _attn_sample_kernel,
        grid=(nbatch // bt,),
        in_specs=[
            pl.BlockSpec(memory_space=pltpu.SMEM),
            pl.BlockSpec((bt, D_Q), row),
            pl.BlockSpec((bt, D_KV), row),
            pl.BlockSpec((bt, D_KV), row),
            pl.BlockSpec((bt, WINDOW, D_KV), win),
            pl.BlockSpec((bt, WINDOW, D_KV), win),
        ],
        out_specs=[
            pl.BlockSpec((bt, D_Q), row),
            pl.BlockSpec((bt, WINDOW, D_KV), win),
            pl.BlockSpec((bt, WINDOW, D_KV), win),
        ],
        out_shape=(
            jax.ShapeDtypeStruct((nbatch, D_Q), F32),
            jax.ShapeDtypeStruct((nbatch, WINDOW, D_KV), F32),
            jax.ShapeDtypeStruct((nbatch, WINDOW, D_KV), F32),
        ),
        compiler_params=_params("parallel"),
        name="attn_sample",
    )(sinks, q, k_new, v_new, cache_k, cache_v)


def _lru_terms(xc, wcat_ref, ba_ref, bi_ref, lam_ref, precise, block0=0):
    cols = slice(block0 * RNN_BLOCK, block0 * RNN_BLOCK + xc.shape[1])
    xcb = xc if precise else xc.astype(BF16)
    ya, yi = [], []
    for n in range(xc.shape[1] // RNN_BLOCK):
        y = _mm(xcb[:, n * RNN_BLOCK:(n + 1) * RNN_BLOCK], wcat_ref[block0 + n], precise)
        ya.append(y[:, :RNN_BLOCK])
        yi.append(y[:, RNN_BLOCK:])
    r = _sigmoid(jnp.concatenate(ya, axis=-1) + ba_ref[:, cols])
    gate_i = _sigmoid(jnp.concatenate(yi, axis=-1) + bi_ref[:, cols])
    neg_lam = -lam_ref[:, cols]
    softplus = jnp.maximum(neg_lam, 0.0) + jnp.log1p(jnp.exp(-jnp.abs(neg_lam)))
    log_a = (-LRU_C * softplus) * r
    a = jnp.exp(log_a)
    m = 1.0 - a * a
    b = jnp.where(m > 0.0, m * lax.rsqrt(m), 0.0) * (gate_i * xc)
    return a, b


def _rnn_start_sequence(xbuf, hcar):
    xbuf[0:SUBLANES, :] = jnp.zeros((SUBLANES, D_RNN), F32)
    hcar[...] = jnp.zeros((SUBLANES, D_RNN), F32)


def _rnn_scan_terms(x, block0, cw_ref, cb_ref, wcat_ref, ba_ref, bi_ref, lam_ref,
                    xbuf, a_scr, b_scr):
    tt, width = x.shape
    groups = tt // SUBLANES
    cols = slice(block0 * RNN_BLOCK, block0 * RNN_BLOCK + width)
    xbuf[SUBLANES:, cols] = x
    xc = cb_ref[:, cols] + cw_ref[CONV_W - 1:CONV_W, cols] * x
    for j in range(CONV_W - 1):
        s = CONV_W - 1 - j
        xc = xc + cw_ref[j:j + 1, cols] * xbuf[SUBLANES - s:SUBLANES - s + tt, cols]
    xbuf[0:SUBLANES, cols] = x[tt - SUBLANES:, :]
    yield
    a, b = _lru_terms(xc, wcat_ref, ba_ref, bi_ref, lam_ref, False, block0)
    yield
    a = a.reshape(groups, SUBLANES, width)
    b = b.reshape(groups, SUBLANES, width)
    step = lax.broadcasted_iota(jnp.int32, (groups, SUBLANES, width), 1)
    k = 1
    while k < SUBLANES:
        keep = step >= k
        a_sh = jnp.where(keep, pltpu.roll(a, k, axis=1), 1.0)
        b_sh = jnp.where(keep, pltpu.roll(b, k, axis=1), 0.0)
        b = a * b_sh + b
        a = a * a_sh
        k *= 2
        if k < SUBLANES:
            yield
    a_scr[:, :, cols] = a
    b_scr[:, :, cols] = b


def _rnn_scan_finish(gy, o_ref, hl_ref, a_scr, b_scr, h_scr, hcar):
    groups = a_scr.shape[0]
    tt = groups * SUBLANES

    def chain(g, h_in):
        h = a_scr[g] * h_in + b_scr[g]
        h_scr[g] = h
        return jnp.broadcast_to(h[SUBLANES - 1:SUBLANES, :], (SUBLANES, D_RNN))

    h_last = lax.fori_loop(0, groups, chain, hcar[...], unroll=True)
    hcar[...] = h_last
    hl_ref[0] = h_last
    h = h_scr[...].reshape(tt, D_RNN)
    o_ref[...] = (h * gy).astype(o_ref.dtype)


def _rnn_sample_kernel(xr_ref, gy_ref, s0_ref, s1_ref, s2_ref, h_ref, cw_ref, cb_ref,
                       wcat_ref, ba_ref, bi_ref, lam_ref, o_ref, hn_ref):
    x = xr_ref[...]
    xc = (cb_ref[...] + cw_ref[0:1, :] * s0_ref[...] + cw_ref[1:2, :] * s1_ref[...]
          + cw_ref[2:3, :] * s2_ref[...] + cw_ref[3:4, :] * x)
    a, b = _lru_terms(xc, wcat_ref, ba_ref, bi_ref, lam_ref, True)
    h = a * h_ref[...] + b
    hn_ref[...] = h
    o_ref[...] = (h * gy_ref[...].astype(F32)).astype(o_ref.dtype)


def _rnn_sample(xr, gy, s0, s1, s2, h_prev, conv_w, conv_b, wcat, b_a, b_i, lam):
    n = xr.shape[0]
    act = _full((n, D_RNN))
    return pl.pallas_call(
        _rnn_sample_kernel,
        grid=(1,),
        in_specs=[act, act, act, act, act, act,
                  _full((CONV_W, D_RNN)), _full((1, D_RNN)),
                  _full((N_RNN_BLOCKS, RNN_BLOCK, 2 * RNN_BLOCK)),
                  _full((1, D_RNN)), _full((1, D_RNN)), _full((1, D_RNN))],
        out_specs=[act, act],
        out_shape=(jax.ShapeDtypeStruct((n, D_RNN), F32),
                   jax.ShapeDtypeStruct((n, D_RNN), F32)),
        compiler_params=_params("arbitrary"),
        name="rnn_sample",
    )(xr, gy, s0, s1, s2, h_prev, conv_w, conv_b, wcat, b_a, b_i, lam)


def _merge_kernel(*refs, precise, tiles_per_seg, tiles_per_seq):
    fused = tiles_per_seq is not None
    if fused:
        (sink_ref, q_ref, kc_ref, kp_ref, vc_ref, vp_ref, x_ref, rn_ref, sga_ref, sgr_ref,
         wa_ref, wr_ref, wo_ref, g_ref, wrt_ref, brt_ref, tri_ref, wg_ref, wu_ref, wd_ref,
         x2_ref, xtm_ref, rec_ref, rect_ref, cnt_ref, wg_o, wu_o, wd_o, cnt_scr, at_ref) = refs
    else:
        (x_ref, at_ref, rn_ref, sga_ref, sgr_ref, wa_ref, wr_ref, wo_ref, g_ref, wrt_ref, brt_ref,
         tri_ref, x2_ref, xtm_ref, rec_ref, rect_ref, cnt_ref, cnt_scr) = refs

    @pl.when(pl.program_id(0) % tiles_per_seg == 0)
    def _():
        cnt_scr[...] = jnp.zeros_like(cnt_scr)

    tm = x_ref.shape[0]
    parts = MERGE_PARTS if tm % (MERGE_PARTS * LANES) == 0 else 1
    part = tm // parts
    state = {"counts": cnt_scr[...]}
    pending = [
        _merge_rows(slice(i * part, (i + 1) * part), state, x_ref, at_ref, rn_ref, sga_ref, sgr_ref,
                    wa_ref, wr_ref, wo_ref, g_ref, wrt_ref, brt_ref, tri_ref, x2_ref, xtm_ref,
                    rec_ref, rect_ref, precise)
        for i in range(parts)]
    active = []
    if fused:
        k_all = jnp.concatenate([kp_ref[...], kc_ref[...]], axis=0).astype(BF16)
        v_all = jnp.concatenate([vp_ref[...], vc_ref[...]], axis=0).astype(BF16)
        in_window, first = _window_masks(pl.program_id(0) % tiles_per_seq == 0)
        blocks = part // ATTN_BLOCK
        side = list(pending)
        for i in range(parts):
            heads = [_attn_kv_head(sub, h, first if sub == 0 else in_window, sink_ref, q_ref,
                                   k_all, v_all, at_ref)
                     for sub in range(i * blocks, (i + 1) * blocks) for h in range(N_KV_HEADS)]
            for _ in _lockstep(heads):
                if side:
                    next(side.pop(0))
                for part_pieces in list(active):
                    if next(part_pieces, "done") == "done":
                        active.remove(part_pieces)
            active.append(pending.pop(0))
    step = 0
    while pending or active:
        if pending and step % MERGE_STAGGER == 0:
            active.append(pending.pop(0))
        for part_pieces in list(active):
            if next(part_pieces, "done") == "done":
                active.remove(part_pieces)
        step += 1
    cnt_scr[...] = state["counts"]
    cnt_ref[0] = state["counts"]
    if fused:
        wg_o[...] = wg_ref[...].astype(BF16)
        wu_o[...] = wu_ref[...].astype(BF16)
        wd_o[...] = wd_ref[...].astype(BF16)


MERGE_PARTS = 2
MERGE_STAGGER = 3


def _merge_rows(rows, state, x_ref, at_ref, rn_ref, sga_ref, sgr_ref, wa_ref, wr_ref, wo_ref, g_ref,
                wrt_ref, brt_ref, tri_ref, x2_ref, xtm_ref, rec_ref, rect_ref, precise):
    yr = _mm(rn_ref[rows, :], wr_ref[...], precise)
    yield
    ya = _mm(at_ref[rows, :], wa_ref[...], precise)
    yield
    merged =sga_ref[rows, :].astype(F32) * ya + sgr_ref[rows, :].astype(F32) * yr
    yield
    x2 = x_ref[rows, :] + _mm(merged, wo_ref[...], precise)
    x2_ref[rows, :] = x2
    yield
    inv = lax.rsqrt(jnp.mean(x2 * x2, axis=-1, keepdims=True) + EPS)
    xn = x2 * inv * g_ref[...]
    for c in range(TOKEN_ROWS):
        xtm_ref[pl.ds(rows.start * TOKEN_ROWS + c, rows.stop - rows.start, stride=TOKEN_ROWS), :] = (
            xn[:, c * LANES:(c + 1) * LANES])
    yield

    logits = _mm(xn, wrt_ref[...], precise) + brt_ref[...]
    yield
    tm = logits.shape[0]
    lane_i = lax.broadcasted_iota(jnp.int32, (tm, LANES), 1)
    lane = lane_i.astype(F32)
    big = float(LANES)
    is_grp = (lane_i >= N_EXPERTS) & (lane_i < N_EXPERTS + N_GROUPS)
    gl = jnp.where(is_grp, logits, NEG_INF)
    gmax = jnp.max(gl, axis=-1, keepdims=True)
    g_idx = jnp.min(jnp.where(gl == gmax, lane, big), axis=-1, keepdims=True) - N_EXPERTS
    p_g = 1.0 / jnp.sum(jnp.exp(gl - gmax), axis=-1, keepdims=True)
    in_grp = (lane_i // EXPERTS_PER_GROUP).astype(F32) == g_idx
    el = jnp.where(in_grp, logits, NEG_INF)
    v1 = jnp.max(el, axis=-1, keepdims=True)
    i1 = jnp.min(jnp.where(el == v1, lane, big), axis=-1, keepdims=True)
    el2 = jnp.where(lane == i1, NEG_INF, el)
    v2 = jnp.max(el2, axis=-1, keepdims=True)
    i2 = jnp.min(jnp.where(el2 == v2, lane, big), axis=-1, keepdims=True)
    e2 = jnp.exp(v2 - v1)
    w1 = p_g / (1.0 + e2)
    w2 = p_g * e2 / (1.0 + e2)
    yield

    hit = jnp.where(lane == i1, 1.0, jnp.where(lane == i2, 1.0, 0.0))
    counts = state["counts"]
    state["counts"] = counts + jnp.sum(hit, axis=0, keepdims=True)
    before = (jnp.dot(tri_ref[0:tm, 0:tm], hit.astype(BF16), preferred_element_type=F32)
              + counts[0:1, :])
    yield
    r1 = jnp.sum(jnp.where(lane == i1, before, 0.0), axis=-1, keepdims=True)
    r2 = jnp.sum(jnp.where(lane == i2, before, 0.0), axis=-1, keepdims=True)
    rec = jnp.where(lane == REC_W2, w2, 0.0)
    for field, val in ((REC_W1, w1), (REC_R2, r2), (REC_R1, r1), (REC_E2, i2), (REC_E1, i1)):
        rec = jnp.where(lane == field, val, rec)
    rec_ref[rows, :] = rec
    rect_ref[:, rows] = rec.T


REC_E1, REC_E2, REC_R1, REC_R2, REC_W1, REC_W2 = range(6)
TOKEN_ROWS = D_MODEL // LANES


def _merge(x, attn, rnn, sga, sgr, wa, wr, wo, g2, w_route, b_route, tri, tm, seg, precise,
           new_sequences=None):
    n = x.shape[0]
    steps = n // tm
    tiles_per_seg = seg // tm
    row = lambda i: (i, 0)
    scratch = [pltpu.VMEM((SUBLANES, LANES), F32)]
    if new_sequences is None:
        head_specs = [pl.BlockSpec((tm, D_MODEL), row), pl.BlockSpec((tm, D_Q), row)]
        head_args = [x, attn]
        tiles_per_seq = None
    else:
        sinks, q, k, v, seq_len, expert_weights = new_sequences
        assert steps == N_EXPERTS and attn is None
        tiles_per_seq = seq_len // tm
        prev = lambda i: (jnp.maximum(i * (tm // ATTN_BLOCK) - 1, 0), 0)
        head_specs = [
            pl.BlockSpec(memory_space=pltpu.SMEM),
            pl.BlockSpec((tm, D_Q), row),
            pl.BlockSpec((tm, D_KV), row),
            pl.BlockSpec((ATTN_BLOCK, D_KV), prev),
            pl.BlockSpec((tm, D_KV), row),
            pl.BlockSpec((ATTN_BLOCK, D_KV), prev),
            pl.BlockSpec((tm, D_MODEL), row),
        ]
        head_args = [sinks, q, k, k, v, v, x]
        scratch.append(pltpu.VMEM((tm, D_Q), BF16))
    in_specs = head_specs + [
        pl.BlockSpec((tm, D_RNN), row),
        pl.BlockSpec((tm, D_MODEL), row),
        pl.BlockSpec((tm, D_MODEL), row),
        _full((D_Q, D_MODEL)),
        _full((D_RNN, D_MODEL)),
        _full((D_MODEL, D_MODEL)),
        _full((1, D_MODEL)),
        _full((D_MODEL, LANES)),
        _full((1, LANES)),
        _full((tm, tm)),
    ]
    out_specs = [
        pl.BlockSpec((tm, D_MODEL), row),
        pl.BlockSpec((tm * TOKEN_ROWS, LANES), row),
        pl.BlockSpec((tm, LANES), row),
        pl.BlockSpec((LANES, tm), lambda i: (0, i)),
        pl.BlockSpec((1, SUBLANES, LANES), lambda i: (i // tiles_per_seg, 0, 0)),
    ]
    out_shape = [
        jax.ShapeDtypeStruct((n, D_MODEL), F32),
        jax.ShapeDtypeStruct((n * TOKEN_ROWS, LANES), F32),
        jax.ShapeDtypeStruct((n, LANES), F32),
        jax.ShapeDtypeStruct((LANES, n), F32),
        jax.ShapeDtypeStruct((n // seg, SUBLANES, LANES), F32),
    ]
    args = head_args + [rnn, sga, sgr, wa, wr, wo, g2, w_route, b_route, tri]
    if new_sequences is not None:
        for w in expert_weights:
            spec = pl.BlockSpec((1,) + w.shape[1:], lambda i: (i, 0, 0))
            in_specs.append(spec)
            out_specs.append(spec)
            out_shape.append(jax.ShapeDtypeStruct(w.shape, BF16))
            args.append(w)
    return pl.pallas_call(
        functools.partial(_merge_kernel, precise=precise, tiles_per_seg=tiles_per_seg,
                          tiles_per_seq=tiles_per_seq),
        grid=(steps,),
        in_specs=in_specs,
        out_specs=out_specs,
        out_shape=out_shape,
        scratch_shapes=scratch,
        compiler_params=_params("arbitrary"),
        name="merge",
    )(*args)


MOE_CHUNK = 256
MOE_TAIL = 128
MOE_EXPERTS_PER_STEP = 2
MOE_VMEM_LIMIT = 60 * 1024 * 1024


def _seg_rows(seg):
    return 2 * seg + N_EXPERTS * SUBLANES + MOE_CHUNK


def _token_rows(i):
    return pl.ds(pl.multiple_of(i * TOKEN_ROWS, TOKEN_ROWS), TOKEN_ROWS)


def _sorted_rows(first_row):
    return pl.ds(pl.multiple_of(first_row, TOKEN_ROWS), TOKEN_ROWS)


def _moe_kernel(off_ref, nfull_ref, rem_ref, slot_ref, xtm_ref, wg_ref, wu_ref, wd_ref,
                x2_ref, rec_ref, o_ref, buf, g1, g2, *, seg, td, tc):
    s = pl.program_id(0)
    p = pl.program_id(1)
    n_disp = seg // td
    n_exp = N_EXPERTS // MOE_EXPERTS_PER_STEP

    @pl.when((s == 0) & (p == 0))
    def _():
        buf[...] = jnp.zeros_like(buf)

    @pl.when(p < n_disp)
    def _():
        def dispatch(g, carry):
            for j in range(SUBLANES):
                t = g * SUBLANES + j
                row = xtm_ref[_token_rows(t), :]
                for k in range(2):
                    buf[_sorted_rows(slot_ref[0, 0, k * seg + p * td + t]), :] = row
            return carry

        lax.fori_loop(0, td // SUBLANES, dispatch, 0)

    def run_chunk(e, row0, rows, valid):
        r0 = pl.multiple_of(row0 * TOKEN_ROWS, SUBLANES * TOKEN_ROWS)
        xf = [buf[pl.ds(r0 + j, rows, stride=TOKEN_ROWS), :] for j in range(TOKEN_ROWS)]
        x = jnp.concatenate(xf, axis=-1).astype(BF16)
        hg = jnp.dot(x, wg_ref[e], preferred_element_type=F32)
        hu = jnp.dot(x, wu_ref[e], preferred_element_type=F32)
        h = (hg * _sigmoid(hg)) * hu
        y = jnp.dot(h.astype(BF16), wd_ref[e], preferred_element_type=F32)
        if valid is not None:
            mine = lax.broadcasted_iota(jnp.int32, (rows, LANES), 0) < valid
        for j in range(TOKEN_ROWS):
            yj = y[:, j * LANES:(j + 1) * LANES]
            if valid is not None:
                yj = jnp.where(mine, yj, xf[j])
            buf[pl.ds(r0 + j, rows, stride=TOKEN_ROWS), :] = yj

    def run_expert(e):
        idx = s * N_EXPERTS + (p - n_disp) * MOE_EXPERTS_PER_STEP + e
        base = off_ref[idx]
        n_full = nfull_ref[idx]
        rem = rem_ref[idx]

        def chunk(c, carry):
            run_chunk(e, base + c * MOE_CHUNK, MOE_CHUNK, None)
            return carry

        lax.fori_loop(0, n_full, chunk, 0)
        last = base + n_full * MOE_CHUNK
        for units in range(1, MOE_CHUNK // MOE_TAIL + 2):
            @pl.when((rem > (units - 1) * MOE_TAIL) & (rem <= units * MOE_TAIL))
            def _():
                run_chunk(e, last, units * MOE_TAIL, rem)

    @pl.when((p >= n_disp) & (p < n_disp + n_exp))
    def _():
        for e in range(MOE_EXPERTS_PER_STEP):
            run_expert(e)

    @pl.when(p >= n_disp + n_exp)
    def _():
        t0 = (p - n_disp - n_exp) * tc

        def gather(g, carry):
            for j in range(SUBLANES):
                t = g * SUBLANES + j
                g1[_token_rows(t), :] = buf[_sorted_rows(slot_ref[0, 0, t0 + t]), :]
                g2[_token_rows(t), :] = buf[_sorted_rows(slot_ref[0, 0, seg + t0 + t]), :]
            return carry

        lax.fori_loop(0, tc // SUBLANES, gather, 0)
        rec = rec_ref[...]
        lane = lax.broadcasted_iota(jnp.int32, rec.shape, 1)
        w1 = jnp.broadcast_to(
            jnp.sum(jnp.where(lane == REC_W1, rec, 0.0), axis=-1, keepdims=True), rec.shape)
        w2 = jnp.broadcast_to(
            jnp.sum(jnp.where(lane == REC_W2, rec, 0.0), axis=-1, keepdims=True), rec.shape)
        for j in range(TOKEN_ROWS):
            cols = slice(j * LANES, (j + 1) * LANES)
            o_ref[:, cols] = (x2_ref[:, cols] + w1 * g1[pl.ds(j, tc, stride=TOKEN_ROWS), :]
                              + w2 * g2[pl.ds(j, tc, stride=TOKEN_ROWS), :])


def _moe(off, nfull, rem, slot, xtm, wg, wu, wd, x2, rec, seg, td, tc):
    n = x2.shape[0]
    n_seg = n // seg
    n_disp, n_comb = seg // td, seg // tc
    per_step = MOE_EXPERTS_PER_STEP
    n_exp = N_EXPERTS // per_step
    rows = _seg_rows(seg) * TOKEN_ROWS
    disp_tile = lambda s, p, *_: (s * n_disp + jnp.minimum(p, n_disp - 1), 0)
    expert = lambda s, p, *_: (jnp.clip(p - n_disp, 0, n_exp - 1), 0, 0)
    comb_tile = lambda s, p, *_: (s * n_comb + jnp.clip(p - n_disp - n_exp, 0, n_comb - 1), 0)
    grid_spec = pltpu.PrefetchScalarGridSpec(
        num_scalar_prefetch=3,
        grid=(n_seg, n_disp + n_exp + n_comb),
        in_specs=[
            pl.BlockSpec((1, 1, 2 * seg), lambda s, p, *_: (s, 0, 0), memory_space=pltpu.SMEM),
            pl.BlockSpec((td * TOKEN_ROWS, LANES), disp_tile),
            pl.BlockSpec((per_step, D_MODEL, D_EXPERT), expert),
            pl.BlockSpec((per_step, D_MODEL, D_EXPERT), expert),
            pl.BlockSpec((per_step, D_EXPERT, D_MODEL), expert),
            pl.BlockSpec((tc, D_MODEL), comb_tile),
            pl.BlockSpec((tc, LANES), comb_tile),
        ],
        out_specs=pl.BlockSpec((tc, D_MODEL), comb_tile),
        scratch_shapes=[pltpu.VMEM((rows, LANES), F32),
                        pltpu.VMEM((tc * TOKEN_ROWS, LANES), F32),
                        pltpu.VMEM((tc * TOKEN_ROWS, LANES), F32)],
    )
    return pl.pallas_call(
        functools.partial(_moe_kernel, seg=seg, td=td, tc=tc),
        grid_spec=grid_spec,
        out_shape=jax.ShapeDtypeStruct((n, D_MODEL), F32),
        compiler_params=pltpu.CompilerParams(
            dimension_semantics=("arbitrary", "arbitrary"), vmem_limit_bytes=MOE_VMEM_LIMIT),
        name="moe",
    )(off, nfull, rem, slot, xtm, wg, wu, wd, x2, rec)


def _plan(rect, cnt, seg):
    n = rect.shape[1]
    expert = rect[REC_E1:REC_E2 + 1].astype(jnp.int32)
    rank = rect[REC_R1:REC_R2 + 1].astype(jnp.int32)
    counts = cnt[:, 0, :N_EXPERTS].astype(jnp.int32)
    padded = (counts + SUBLANES - 1) // SUBLANES * SUBLANES
    off = jnp.cumsum(padded, axis=1) - padded
    n_full = counts // MOE_CHUNK
    rem = counts - n_full * MOE_CHUNK
    join = (rem > 0) & (rem <= MOE_TAIL) & (n_full > 0)
    n_full = n_full - join
    rem = rem + join * MOE_CHUNK
    off_tok = jnp.repeat(off.T, seg, axis=1)
    hit = expert[:, None, :] == jnp.arange(N_EXPERTS, dtype=jnp.int32)[None, :, None]
    slot = (rank + jnp.sum(jnp.where(hit, off_tok[None], 0), axis=1)) * TOKEN_ROWS
    slot = slot.reshape(2, n // seg, seg).transpose(1, 0, 2).reshape(n // seg, 1, 2 * seg)
    return off.reshape(-1), n_full.reshape(-1), rem.reshape(-1), slot


def _rope_tables(pos):
    half = HEAD_DIM // 2
    inv_freq = ROPE_THETA ** (-jnp.arange(half, dtype=F32) / half)
    ang = pos[:, None] * inv_freq[None, :]
    return jnp.cos(ang), jnp.sin(ang)


def _rope_tables_range(n):
    hi = jnp.arange(n // ATTN_BLOCK, dtype=F32) * ATTN_BLOCK
    lo = jnp.arange(ATTN_BLOCK, dtype=F32)
    (cos_hi, sin_hi), (cos_lo, sin_lo) = _rope_tables(hi), _rope_tables(lo)
    cos = cos_hi[:, None, :] * cos_lo[None] - sin_hi[:, None, :] * sin_lo[None]
    sin = sin_hi[:, None, :] * cos_lo[None] + cos_hi[:, None, :] * sin_lo[None]
    return cos.reshape(n, -1), sin.reshape(n, -1)


def kernel(x_prompt, x_sample, cache_k_win, cache_v_win, state_conv, state_lru_h, attn_norm_g, w_in, q_norm_g, k_norm_g, attn_sinks, conv_w, conv_b, w_lru_a, b_lru_a, w_lru_i, b_lru_i, lru_lambda, w_br_attn, w_br_rnn, w_out, ffn_norm_g, w_route_group, b_route_group, w_route_expert, b_route_expert, w_exp_gate, w_exp_up, w_exp_down):
    batch, seq, _ = x_prompt.shape
    dec_batch, dec_seq, _ = x_sample.shape
    depth = w_in.shape[0]
    assert depth == 1 and dec_seq == 1
    l = 0

    w_in_f = w_in[l]
    qkg = jnp.concatenate([jnp.tile(q_norm_g[l], N_Q_HEADS), jnp.tile(k_norm_g[l], N_KV_HEADS)])[None, :]
    wcat_f = jnp.concatenate([w_lru_a[l], w_lru_i[l]], axis=-1)
    wa_f, wr_f, wo_f = w_br_attn[l], w_br_rnn[l], w_out[l]
    w_route_f = jnp.concatenate(
        [w_route_expert[l], w_route_group[l],
         jnp.zeros((D_MODEL, LANES - N_EXPERTS - N_GROUPS), F32)], axis=-1)
    wcat = wcat_f.astype(BF16)
    wa_b, wr_b, wo_b, w_route = (w.astype(BF16) for w in (wa_f, wr_f, wo_f, w_route_f))
    b_route = jnp.concatenate(
        [b_route_expert[l], b_route_group[l], jnp.zeros((LANES - N_EXPERTS - N_GROUPS,), F32)])[None, :]
    experts_f = (w_exp_gate[l], w_exp_up[l], w_exp_down[l])
    g1 = attn_norm_g[l][None, :]
    g2 = ffn_norm_g[l][None, :]
    cw, cb = conv_w[l], conv_b[l][None, :]
    b_a, b_i, lam = b_lru_a[l][None, :], b_lru_i[l][None, :], lru_lambda[l][None, :]
    sinks = attn_sinks[l]

    def tail(x, attn, rnn, sga, sgr, tm, seg, precise, experts_b=None, qkv=None):
        wa, wr, wo, wrt = (wa_f, wr_f, wo_f, w_route_f) if precise else (wa_b, wr_b, wo_b, w_route)
        tri = jnp.tril(jnp.ones((tm, tm), BF16), -1)
        new_sequences = None if qkv is None else (sinks, *qkv, seq, experts_f)
        outs = _merge(x, attn, rnn, sga, sgr, wa, wr, wo, g2, wrt, b_route, tri, tm, seg, precise,
                      new_sequences=new_sequences)
        x2, xtm, rec, rect, cnt = outs[:5]
        wg, wu, wd = experts_b or outs[5:]
        off, nfull, rem, slot = _plan(rect, cnt, seg)
        y = _moe(off, nfull, rem, slot, xtm, wg, wu, wd, x2, rec, seg, tm, tm)
        return y, (wg, wu, wd)

    xs = x_sample.reshape(dec_batch, D_MODEL)
    cos_s, sin_s = _rope_tables(jnp.full((dec_batch,), PAST_LEN, F32))
    qs, ks, vs, xrs, gys, sgas, sgrs, w_in_b = _proj(xs, g1, w_in_f, cos_s, sin_s, qkg, 1, dec_batch,
                                                     dec_batch, True)

    xp = x_prompt.reshape(batch * seq, D_MODEL)
    cos_p, sin_p = _rope_tables_range(seq)
    q, k, v, rnn, sga, sgr, h_last, conv_tail = _proj(
        xp, g1, w_in_b, cos_p, sin_p, qkg, batch, seq, 512, False,
        rnn_weights=(cw, cb, wcat, b_a, b_i, lam))
    y_prompt, experts_b = tail(xp, None, rnn, sga, sgr, 512, 4096, False, qkv=(q, k, v))
    y_prompt = y_prompt.reshape(batch, seq, D_MODEL)

    def last_rows(a, rows):
        return a.reshape(batch, seq, a.shape[-1])[:, seq - rows:]

    k_win_p = last_rows(k, WINDOW).reshape(1, batch, WINDOW, N_KV_HEADS, HEAD_DIM)
    v_win_p = last_rows(v, WINDOW).reshape(1, batch, WINDOW, N_KV_HEADS, HEAD_DIM)
    conv_p = conv_tail[None, :, SUBLANES - (CONV_W - 1):, :]
    h_p = h_last[None, :, 0, :]

    ck = cache_k_win[l].reshape(dec_batch, WINDOW, D_KV)
    cv = cache_v_win[l].reshape(dec_batch, WINDOW, D_KV)
    attn_s, k_win_s, v_win_s = _attn_sample(qs, ks, vs, ck, cv, sinks)
    sc = state_conv[l]
    rnn_s, h_s = _rnn_sample(xrs, gys, sc[:, 0], sc[:, 1], sc[:, 2], state_lru_h[l],
                             cw, cb, wcat_f, b_a, b_i, lam)
    y_sample, _ = tail(xs, attn_s, rnn_s, sgas, sgrs, dec_batch, dec_batch, True, experts_b)
    y_sample = y_sample.reshape(dec_batch, 1, D_MODEL)
    conv_s = jnp.stack([sc[:, 1], sc[:, 2], xrs], axis=1)[None]

    return (y_prompt, y_sample, k_win_p, v_win_p, conv_p, h_p,
            k_win_s.reshape(1, dec_batch, WINDOW, N_KV_HEADS, HEAD_DIM),
            v_win_s.reshape(1, dec_batch, WINDOW, N_KV_HEADS, HEAD_DIM),
            conv_s, h_s[None])
```

```python
import functools

import jax
import jax.numpy as jnp
from jax import lax
from jax.experimental import pallas as pl
from jax.experimental.pallas import tpu as pltpu

D_MODEL = 1024
HEAD_DIM = 64
N_Q_HEADS = 8
N_KV_HEADS = 2
Q_PER_KV = N_Q_HEADS // N_KV_HEADS
WINDOW = 128
ATTN_BLOCK = 128
ROPE_THETA = 10000.0
SCALE = HEAD_DIM ** -0.5
NEG_INF = -1e30
D_RNN = 1280
N_RNN_BLOCKS = 10
RNN_BLOCK = D_RNN // N_RNN_BLOCKS
CONV_W = 4
LRU_C = 8.0
N_GROUPS = 4
EXPERTS_PER_GROUP = 8
N_EXPERTS = N_GROUPS * EXPERTS_PER_GROUP
D_EXPERT = 256
PAST_LEN = 16384
EPS = 1e-6
D_Q = N_Q_HEADS * HEAD_DIM
D_KV = N_KV_HEADS * HEAD_DIM
D_IN = D_Q + 2 * D_KV + 2 * D_RNN + 2 * D_MODEL
OFF_K = D_Q
OFF_V = OFF_K + D_KV
OFF_XR = OFF_V + D_KV
OFF_YR = OFF_XR + D_RNN
OFF_GA = OFF_YR + D_RNN
OFF_GR = OFF_GA + D_MODEL

LANES = 128
SUBLANES = 8
VMEM_LIMIT = 56 * 1024 * 1024
PROMPT_TILE = 512
MOE_SEGMENT = 4096

F32 = jnp.float32
BF16 = jnp.bfloat16


def _params(*sem):
    return pltpu.CompilerParams(dimension_semantics=sem, vmem_limit_bytes=VMEM_LIMIT)


def _sigmoid(x):
    return 1.0 / (1.0 + jnp.exp(-x))


def _gelu_tanh(x):
    c = 0.7978845608028654
    half_x = 0.5 * x
    return half_x + half_x * jnp.tanh(x * (c + (c * 0.044715) * (x * x)))


def _full(shape, single_buffer=False):
    index_map = lambda *_: (0,) * len(shape)
    if single_buffer:
        return pl.BlockSpec(shape, index_map, pipeline_mode=pl.Buffered(1))
    return pl.BlockSpec(shape, index_map)


def _mm(a, b, precise, dims=None):
    if precise:
        a, b, prec = a.astype(F32), b.astype(F32), lax.Precision.HIGHEST
    else:
        a, b, prec = a.astype(BF16), b.astype(BF16), None
    if dims is None:
        return jnp.dot(a, b, preferred_element_type=F32, precision=prec)
    return lax.dot_general(a, b, dims, preferred_element_type=F32, precision=prec)


_NT = (((1,), (1,)), ((), ()))


def _proj_kernel(x_ref, g_ref, w_ref, cos_ref, sin_ref, qkg_ref, *rest, precise, fuse_rnn):
    if fuse_rnn:
        rnn_w = rest[:6]
        q_ref, k_ref, v_ref, rnn_ref, sga_ref, sgr_ref, hl_ref, ct_ref = rest[6:14]
        xbuf, a_scr, b_scr, h_scr, hcar = rest[14:]
        pl.when(pl.program_id(1) == 0)(functools.partial(_rnn_start_sequence, xbuf, hcar))
    else:
        q_ref, k_ref, v_ref, xr_ref, gy_ref, sga_ref, sgr_ref, wb_ref = rest
        wb_ref[...] = w_ref[...].astype(BF16)
    x = x_ref[...]
    inv = lax.rsqrt(jnp.mean(x * x, axis=-1, keepdims=True) + EPS)
    xn = x * inv * g_ref[...]
    if not precise:
        xn = xn.astype(BF16)

    def proj(lo, hi):
        return _mm(xn, w_ref[:, lo:hi], precise)

    def qk_heads():
        qk = proj(0, OFF_V)
        tm = qk.shape[0]
        lane = lax.broadcasted_iota(jnp.int32, (tm, LANES), 1)
        lo_head = lane < HEAD_DIM
        first_half = (lane % HEAD_DIM) < (HEAD_DIM // 2)
        cos_f, sin_f = cos_ref[...], sin_ref[...]
        reps = LANES // HEAD_DIM
        cos = jnp.concatenate([cos_f, cos_f] * reps, axis=-1)
        sin = jnp.concatenate([-sin_f, sin_f] * reps, axis=-1)
        for g in range(OFF_V // LANES):
            seg = qk[:, g * LANES:(g + 1) * LANES]
            sq = seg * seg
            s_lo = jnp.sum(jnp.where(lo_head, sq, 0.0), axis=-1, keepdims=True)
            s_hi = jnp.sum(jnp.where(lo_head, 0.0, sq), axis=-1, keepdims=True)
            ms = jnp.where(lo_head, s_lo, s_hi) * (1.0 / HEAD_DIM)
            normed = seg * lax.rsqrt(ms + EPS) * qkg_ref[:, g * LANES:(g + 1) * LANES]
            partner = jnp.where(first_half,
                                pltpu.roll(normed, LANES - HEAD_DIM // 2, axis=1),
                                pltpu.roll(normed, HEAD_DIM // 2, axis=1))
            roped = normed * cos + partner * sin
            if g < D_Q // LANES:
                q_ref[:, g * LANES:(g + 1) * LANES] = (roped * SCALE).astype(q_ref.dtype)
            else:
                k_ref[...] = roped

    def values():
        v_ref[...] = proj(OFF_V, OFF_XR)

    def gate(out_ref, off, c0, c1, r0=0, r1=x.shape[0]):
        y = _mm(xn[r0:r1], w_ref[:, off + c0:off + c1], precise)
        out_ref[r0:r1, c0:c1] = _sigmoid(y).astype(out_ref.dtype)

    if not fuse_rnn:
        xr_ref[...] = proj(OFF_XR, OFF_YR)
        gy_ref[...] = _gelu_tanh(proj(OFF_YR, OFF_GA)).astype(gy_ref.dtype)
        qk_heads()
        values()
        gate(sga_ref, OFF_GA, 0, D_MODEL)
        gate(sgr_ref, OFF_GR, 0, D_MODEL)
        return

    slab = RNN_SLAB_BLOCKS * RNN_BLOCK
    half = x.shape[0] // 2
    others = [qk_heads, values]
    for out_ref, off in ((sga_ref, OFF_GA), (sgr_ref, OFF_GR)):
        others += [functools.partial(gate, out_ref, off, c, c + slab, r, r + half)
                   for c in range(0, D_MODEL, slab) for r in (0, half)]
    others = iter(others)
    n_slabs = D_RNN // slab

    def slab_inputs(i):
        c0 = i * slab
        return (proj(OFF_XR + c0, OFF_XR + c0 + slab),
                _gelu_tanh(proj(OFF_YR + c0, OFF_YR + c0 + slab)))

    gy = []
    nxt = slab_inputs(0)
    for i in range(n_slabs):
        (xr, gy_i), nxt = nxt, None
        gy.append(gy_i)
        ct_ref[0, :, i * slab:(i + 1) * slab] = xr[xr.shape[0] - SUBLANES:, :]
        for _ in _rnn_scan_terms(xr, i * RNN_SLAB_BLOCKS, *rnn_w, xbuf, a_scr, b_scr):
            if nxt is None and i + 1 < n_slabs:
                nxt = slab_inputs(i + 1)
            next(others, lambda: None)()
    for other in others:
        other()
    _rnn_scan_finish(jnp.concatenate(gy, axis=-1), rnn_ref, hl_ref, a_scr, b_scr, h_scr, hcar)


RNN_SLAB_BLOCKS = 2


def _proj(x, g, w_in, cos_t, sin_t, qkg, batch, seq, tm, precise, rnn_weights=None):
    n = batch * seq
    nt = seq // tm
    row = lambda b, t: (b * nt + t, 0)
    per_seq = lambda b, t: (b, 0, 0)
    act = F32 if precise else BF16
    rows_out = lambda width, dtype: (jax.ShapeDtypeStruct((n, width), dtype),
                                     pl.BlockSpec((tm, width), row))
    state_out = (jax.ShapeDtypeStruct((batch, SUBLANES, D_RNN), F32),
                 pl.BlockSpec((1, SUBLANES, D_RNN), per_seq))
    outs = [rows_out(D_Q, act), rows_out(D_KV, F32), rows_out(D_KV, F32)]
    in_specs = [
        pl.BlockSpec((tm, D_MODEL), row),
        _full((1, D_MODEL)),
        _full((D_MODEL, D_IN), single_buffer=True),
        pl.BlockSpec((tm, HEAD_DIM // 2), lambda b, t: (t, 0)),
        pl.BlockSpec((tm, HEAD_DIM // 2), lambda b, t: (t, 0)),
        _full((1, OFF_V)),
    ]
    args = [x, g, w_in, cos_t, sin_t, qkg]
    scratch = []
    if rnn_weights is None:
        outs += [rows_out(D_RNN, F32), rows_out(D_RNN, act)]
    else:
        assert not precise
        outs += [rows_out(D_RNN, act)]
        in_specs += [_full(w.shape) for w in rnn_weights]
        args += list(rnn_weights)
        groups = tm // SUBLANES
        scratch = [pltpu.VMEM((tm + SUBLANES, D_RNN), F32)]
        scratch += [pltpu.VMEM((groups, SUBLANES, D_RNN), F32)] * 3
        scratch += [pltpu.VMEM((SUBLANES, D_RNN), F32)]
    outs += [rows_out(D_MODEL, act), rows_out(D_MODEL, act)]
    if rnn_weights is not None:
        outs += [state_out, state_out]
    else:
        assert batch * nt == 1 and w_in.dtype == F32
        outs += [(jax.ShapeDtypeStruct(w_in.shape, BF16), _full(w_in.shape, single_buffer=True))]
    return pl.pallas_call(
        functools.partial(_proj_kernel, precise=precise, fuse_rnn=rnn_weights is not None),
        grid=(batch, nt),
        in_specs=in_specs,
        out_specs=[spec for _, spec in outs],
        out_shape=[shape for shape, _ in outs],
        scratch_shapes=scratch,
        compiler_params=_params("parallel", "arbitrary"),
        name="proj",
    )(*args)


def _softmax_pv(s, sink, v2, precise):
    m = jnp.maximum(jnp.max(s, axis=-1, keepdims=True), sink)
    p = jnp.exp(s - m)
    denom = jnp.sum(p, axis=-1, keepdims=True) + jnp.exp(sink - m)
    return _mm(p, v2, precise) * (1.0 / denom)


def _sink_column(sink_ref, h, rows, rows_per_head):
    r = lax.broadcasted_iota(jnp.int32, (rows, 1), 0) // rows_per_head
    col = jnp.full((rows, 1), sink_ref[h * Q_PER_KV], F32)
    for g in range(1, Q_PER_KV):
        col = jnp.where(r == g, sink_ref[h * Q_PER_KV + g], col)
    return col


def _window_masks(first_tile):
    blk = ATTN_BLOCK
    rows = Q_PER_KV * blk
    i = lax.broadcasted_iota(jnp.int32, (rows, 2 * blk), 0) % blk
    j = lax.broadcasted_iota(jnp.int32, (rows, 2 * blk), 1)
    d = j - i
    in_window = (d >= 1) & (d <= WINDOW)
    first = (d >= jnp.where(first_tile, jnp.maximum(1, blk - i), 1)) & (d <= WINDOW)
    return in_window, first


def _attn_kv_head(sub, h, valid, sink_ref, q_ref, k_all, v_all, o_ref):
    blk = ATTN_BLOCK
    rows = Q_PER_KV * blk
    q = q_ref[sub * blk:(sub + 1) * blk, :]
    kh = k_all[sub * blk:(sub + 2) * blk, h * HEAD_DIM:(h + 1) * HEAD_DIM]
    vh = v_all[sub * blk:(sub + 2) * blk, h * HEAD_DIM:(h + 1) * HEAD_DIM]
    qs = jnp.concatenate(
        [q[:, (h * Q_PER_KV + g) * HEAD_DIM:(h * Q_PER_KV + g + 1) * HEAD_DIM]
         for g in range(Q_PER_KV)], axis=0)
    s = _mm(qs, kh, False, _NT)
    yield
    s = jnp.where(valid, s, NEG_INF)
    sink = _sink_column(sink_ref, h, rows, blk)
    m = jnp.maximum(jnp.max(s, axis=-1, keepdims=True), sink)
    p = jnp.exp(s - m)
    denom = jnp.sum(p, axis=-1, keepdims=True) + jnp.exp(sink - m)
    yield
    o = _mm(p, vh, False) * (1.0 / denom)
    for g in range(Q_PER_KV):
        c = (h * Q_PER_KV + g) * HEAD_DIM
        o_ref[sub * blk:(sub + 1) * blk, c:c + HEAD_DIM] = o[g * blk:(g + 1) * blk].astype(o_ref.dtype)


def _lockstep(generators):
    while generators:
        generators = [g for g in generators if next(g, "done") != "done"]
        yield


SAMPLE_BT = 8


def _attn_sample_kernel(sink_ref, q_ref, kn_ref, vn_ref, kc_ref, vc_ref, o_ref, ko_ref, vo_ref):
    bt = SAMPLE_BT
    w = lax.broadcasted_iota(jnp.int32, (bt, WINDOW, D_KV), 1)

    def shifted(cache_ref, new_ref):
        rolled = pltpu.roll(cache_ref[...], WINDOW - 1, axis=1)
        return jnp.where(w == WINDOW - 1, new_ref[...][:, None, :], rolled)

    k_win = shifted(kc_ref, kn_ref)
    v_win = shifted(vc_ref, vn_ref)
    ko_ref[...] = k_win
    vo_ref[...] = v_win
    k2 = k_win.reshape(bt * WINDOW, D_KV)
    v2 = v_win.reshape(bt * WINDOW, D_KV)
    q = q_ref[...]
    rows = Q_PER_KV * bt
    rb = lax.broadcasted_iota(jnp.int32, (rows, bt * WINDOW), 0) % bt
    cb = lax.broadcasted_iota(jnp.int32, (rows, bt * WINDOW), 1) // WINDOW
    valid = rb == cb
    def kv_head(h):
        kh = k2[:, h * HEAD_DIM:(h + 1) * HEAD_DIM]
        vh = v2[:, h * HEAD_DIM:(h + 1) * HEAD_DIM]
        qs = jnp.concatenate(
            [q[:, (h * Q_PER_KV + g) * HEAD_DIM:(h * Q_PER_KV + g + 1) * HEAD_DIM]
             for g in range(Q_PER_KV)], axis=0)
        s = _mm(qs, kh, True, _NT)
        yield
        s = jnp.where(valid, s, NEG_INF)
        o = _softmax_pv(s, _sink_column(sink_ref, h, rows, bt), vh, True)
        for g in range(Q_PER_KV):
            c = (h * Q_PER_KV + g) * HEAD_DIM
            o_ref[:, c:c + HEAD_DIM] = o[g * bt:(g + 1) * bt].astype(o_ref.dtype)

    heads = [kv_head(h) for h in range(N_KV_HEADS)]
    while heads:
        heads = [head for head in heads if next(head, "done") != "done"]


def _attn_sample(q, k_new, v_new, cache_k, cache_v, sinks):
    nbatch = q.shape[0]
    bt = SAMPLE_BT
    row = lambda i: (i, 0)
    win = lambda i: (i, 0, 0)
    return pl.pallas_call(
        _attn_sample_kernel,
        grid=(nbatch // bt,),
        in_specs=[
            pl.BlockSpec(memory_space=pltpu.SMEM),
            pl.BlockSpec((bt, D_Q), row),
            pl.BlockSpec((bt, D_KV), row),
            pl.BlockSpec((bt, D_KV), row),
            pl.BlockSpec((bt, WINDOW, D_KV), win),
            pl.BlockSpec((bt, WINDOW, D_KV), win),
        ],
        out_specs=[
            pl.BlockSpec((bt, D_Q), row),
            pl.BlockSpec((bt, WINDOW, D_KV), win),
            pl.BlockSpec((bt, WINDOW, D_KV), win),
        ],
        out_shape=(
            jax.ShapeDtypeStruct((nbatch, D_Q), F32),
            jax.ShapeDtypeStruct((nbatch, WINDOW, D_KV), F32),
            jax.ShapeDtypeStruct((nbatch, WINDOW, D_KV), F32),
        ),
        compiler_params=_params("parallel"),
        name="attn_sample",
    )(sinks, q, k_new, v_new, cache_k, cache_v)


def _lru_terms(xc, wcat_ref, ba_ref, bi_ref, lam_ref, precise, block0=0):
    cols = slice(block0 * RNN_BLOCK, block0 * RNN_BLOCK + xc.shape[1])
    xcb = xc if precise else xc.astype(BF16)
    ya, yi = [], []
    for n in range(xc.shape[1] // RNN_BLOCK):
        y = _mm(xcb[:, n * RNN_BLOCK:(n + 1) * RNN_BLOCK], wcat_ref[block0 + n], precise)
        ya.append(y[:, :RNN_BLOCK])
        yi.append(y[:, RNN_BLOCK:])
    r = _sigmoid(jnp.concatenate(ya, axis=-1) + ba_ref[:, cols])
    gate_i = _sigmoid(jnp.concatenate(yi, axis=-1) + bi_ref[:, cols])
    neg_lam = -lam_ref[:, cols]
    softplus = jnp.maximum(neg_lam, 0.0) + jnp.log1p(jnp.exp(-jnp.abs(neg_lam)))
    log_a = (-LRU_C * softplus) * r
    a = jnp.exp(log_a)
    m = 1.0 - a * a
    b = jnp.where(m > 0.0, m * lax.rsqrt(m), 0.0) * (gate_i * xc)
    return a, b


def _rnn_start_sequence(xbuf, hcar):
    xbuf[0:SUBLANES, :] = jnp.zeros((SUBLANES, D_RNN), F32)
    hcar[...] = jnp.zeros((SUBLANES, D_RNN), F32)


def _rnn_scan_terms(x, block0, cw_ref, cb_ref, wcat_ref, ba_ref, bi_ref, lam_ref,
                    xbuf, a_scr, b_scr):
    tt, width = x.shape
    groups = tt // SUBLANES
    cols = slice(block0 * RNN_BLOCK, block0 * RNN_BLOCK + width)
    xbuf[SUBLANES:, cols] = x
    xc = cb_ref[:, cols] + cw_ref[CONV_W - 1:CONV_W, cols] * x
    for j in range(CONV_W - 1):
        s = CONV_W - 1 - j
        xc = xc + cw_ref[j:j + 1, cols] * xbuf[SUBLANES - s:SUBLANES - s + tt, cols]
    xbuf[0:SUBLANES, cols] = x[tt - SUBLANES:, :]
    yield
    a, b = _lru_terms(xc, wcat_ref, ba_ref, bi_ref, lam_ref, False, block0)
    yield
    a = a.reshape(groups, SUBLANES, width)
    b = b.reshape(groups, SUBLANES, width)
    step = lax.broadcasted_iota(jnp.int32, (groups, SUBLANES, width), 1)
    k = 1
    while k < SUBLANES:
        keep = step >= k
        a_sh = jnp.where(keep, pltpu.roll(a, k, axis=1), 1.0)
        b_sh = jnp.where(keep, pltpu.roll(b, k, axis=1), 0.0)
        b = a * b_sh + b
        a = a * a_sh
        k *= 2
        if k < SUBLANES:
            yield
    a_scr[:, :, cols] = a
    b_scr[:, :, cols] = b


def _rnn_scan_finish(gy, o_ref, hl_ref, a_scr, b_scr, h_scr, hcar):
    groups = a_scr.shape[0]
    tt = groups * SUBLANES

    def chain(g, h_in):
        h = a_scr[g] * h_in + b_scr[g]
        h_scr[g] = h
        return jnp.broadcast_to(h[SUBLANES - 1:SUBLANES, :], (SUBLANES, D_RNN))

    h_last = lax.fori_loop(0, groups, chain, hcar[...], unroll=True)
    hcar[...] = h_last
    hl_ref[0] = h_last
    h = h_scr[...].reshape(tt, D_RNN)
    o_ref[...] = (h * gy).astype(o_ref.dtype)


def _rnn_sample_kernel(xr_ref, gy_ref, s0_ref, s1_ref, s2_ref, h_ref, cw_ref, cb_ref,
                       wcat_ref, ba_ref, bi_ref, lam_ref, o_ref, hn_ref):
    x = xr_ref[...]
    xc = (cb_ref[...] + cw_ref[0:1, :] * s0_ref[...] + cw_ref[1:2, :] * s1_ref[...]
          + cw_ref[2:3, :] * s2_ref[...] + cw_ref[3:4, :] * x)
    a, b = _lru_terms(xc, wcat_ref, ba_ref, bi_ref, lam_ref, True)
    h = a * h_ref[...] + b
    hn_ref[...] = h
    o_ref[...] = (h * gy_ref[...].astype(F32)).astype(o_ref.dtype)


def _rnn_sample(xr, gy, s0, s1, s2, h_prev, conv_w, conv_b, wcat, b_a, b_i, lam):
    n = xr.shape[0]
    act = _full((n, D_RNN))
    return pl.pallas_call(
        _rnn_sample_kernel,
        grid=(1,),
        in_specs=[act, act, act, act, act, act,
                  _full((CONV_W, D_RNN)), _full((1, D_RNN)),
                  _full((N_RNN_BLOCKS, RNN_BLOCK, 2 * RNN_BLOCK)),
                  _full((1, D_RNN)), _full((1, D_RNN)), _full((1, D_RNN))],
        out_specs=[act, act],
        out_shape=(jax.ShapeDtypeStruct((n, D_RNN), F32),
                   jax.ShapeDtypeStruct((n, D_RNN), F32)),
        compiler_params=_params("arbitrary"),
        name="rnn_sample",
    )(xr, gy, s0, s1, s2, h_prev, conv_w, conv_b, wcat, b_a, b_i, lam)


def _merge_kernel(*refs, precise, tiles_per_seg, tiles_per_seq):
    fused = tiles_per_seq is not None
    if fused:
        (sink_ref, q_ref, kc_ref, kp_ref, vc_ref, vp_ref, x_ref, rn_ref, sga_ref, sgr_ref,
         wa_ref, wr_ref, wo_ref, g_ref, wrt_ref, brt_ref, tri_ref, wg_ref, wu_ref, wd_ref,
         x2_ref, xtm_ref, rec_ref, rect_ref, cnt_ref, wg_o, wu_o, wd_o, cnt_scr, at_ref) = refs
    else:
        (x_ref, at_ref, rn_ref, sga_ref, sgr_ref, wa_ref, wr_ref, wo_ref, g_ref, wrt_ref, brt_ref,
         tri_ref, x2_ref, xtm_ref, rec_ref, rect_ref, cnt_ref, cnt_scr) = refs

    @pl.when(pl.program_id(0) % tiles_per_seg == 0)
    def _():
        cnt_scr[...] = jnp.zeros_like(cnt_scr)

    tm = x_ref.shape[0]
    parts = MERGE_PARTS if tm % (MERGE_PARTS * LANES) == 0 else 1
    part = tm // parts
    state = {"counts": cnt_scr[...]}
    pending = [
        _merge_rows(slice(i * part, (i + 1) * part), state, x_ref, at_ref, rn_ref, sga_ref, sgr_ref,
                    wa_ref, wr_ref, wo_ref, g_ref, wrt_ref, brt_ref, tri_ref, x2_ref, xtm_ref,
                    rec_ref, rect_ref, precise)
        for i in range(parts)]
    active = []
    if fused:
        k_all = jnp.concatenate([kp_ref[...], kc_ref[...]], axis=0).astype(BF16)
        v_all = jnp.concatenate([vp_ref[...], vc_ref[...]], axis=0).astype(BF16)
        in_window, first = _window_masks(pl.program_id(0) % tiles_per_seq == 0)
        blocks = part // ATTN_BLOCK
        side = list(pending)
        for i in range(parts):
            heads = [_attn_kv_head(sub, h, first if sub == 0 else in_window, sink_ref, q_ref,
                                   k_all, v_all, at_ref)
                     for sub in range(i * blocks, (i + 1) * blocks) for h in range(N_KV_HEADS)]
            for _ in _lockstep(heads):
                if side:
                    next(side.pop(0))
                for part_pieces in list(active):
                    if next(part_pieces, "done") == "done":
                        active.remove(part_pieces)
            active.append(pending.pop(0))
    step = 0
    while pending or active:
        if pending and step % MERGE_STAGGER == 0:
            active.append(pending.pop(0))
        for part_pieces in list(active):
            if next(part_pieces, "done") == "done":
                active.remove(part_pieces)
        step += 1
    cnt_scr[...] = state["counts"]
    cnt_ref[0] = state["counts"]
    if fused:
        wg_o[...] = wg_ref[...].astype(BF16)
        wu_o[...] = wu_ref[...].astype(BF16)
        wd_o[...] = wd_ref[...].astype(BF16)


MERGE_PARTS = 2
MERGE_STAGGER = 3


def _merge_rows(rows, state, x_ref, at_ref, rn_ref, sga_ref, sgr_ref, wa_ref, wr_ref, wo_ref, g_ref,
                wrt_ref, brt_ref, tri_ref, x2_ref, xtm_ref, rec_ref, rect_ref, precise):
    yr = _mm(rn_ref[rows, :], wr_ref[...], precise)
    yield
    ya = _mm(at_ref[rows, :], wa_ref[...], precise)
    yield
    merged =sga_ref[rows, :].astype(F32) * ya + sgr_ref[rows, :].astype(F32) * yr
    yield
    x2 = x_ref[rows, :] + _mm(merged, wo_ref[...], precise)
    x2_ref[rows, :] = x2
    yield
    inv = lax.rsqrt(jnp.mean(x2 * x2, axis=-1, keepdims=True) + EPS)
    xn = x2 * inv * g_ref[...]
    for c in range(TOKEN_ROWS):
        xtm_ref[pl.ds(rows.start * TOKEN_ROWS + c, rows.stop - rows.start, stride=TOKEN_ROWS), :] = (
            xn[:, c * LANES:(c + 1) * LANES])
    yield

    logits = _mm(xn, wrt_ref[...], precise) + brt_ref[...]
    yield
    tm = logits.shape[0]
    lane_i = lax.broadcasted_iota(jnp.int32, (tm, LANES), 1)
    lane = lane_i.astype(F32)
    big = float(LANES)
    is_grp = (lane_i >= N_EXPERTS) & (lane_i < N_EXPERTS + N_GROUPS)
    gl = jnp.where(is_grp, logits, NEG_INF)
    gmax = jnp.max(gl, axis=-1, keepdims=True)
    g_idx = jnp.min(jnp.where(gl == gmax, lane, big), axis=-1, keepdims=True) - N_EXPERTS
    p_g = 1.0 / jnp.sum(jnp.exp(gl - gmax), axis=-1, keepdims=True)
    in_grp = (lane_i // EXPERTS_PER_GROUP).astype(F32) == g_idx
    el = jnp.where(in_grp, logits, NEG_INF)
    v1 = jnp.max(el, axis=-1, keepdims=True)
    i1 = jnp.min(jnp.where(el == v1, lane, big), axis=-1, keepdims=True)
    el2 = jnp.where(lane == i1, NEG_INF, el)
    v2 = jnp.max(el2, axis=-1, keepdims=True)
    i2 = jnp.min(jnp.where(el2 == v2, lane, big), axis=-1, keepdims=True)
    e2 = jnp.exp(v2 - v1)
    w1 = p_g / (1.0 + e2)
    w2 = p_g * e2 / (1.0 + e2)
    yield

    hit = jnp.where(lane == i1, 1.0, jnp.where(lane == i2, 1.0, 0.0))
    counts = state["counts"]
    state["counts"] = counts + jnp.sum(hit, axis=0, keepdims=True)
    before = (jnp.dot(tri_ref[0:tm, 0:tm], hit.astype(BF16), preferred_element_type=F32)
              + counts[0:1, :])
    yield
    r1 = jnp.sum(jnp.where(lane == i1, before, 0.0), axis=-1, keepdims=True)
    r2 = jnp.sum(jnp.where(lane == i2, before, 0.0), axis=-1, keepdims=True)
    rec = jnp.where(lane == REC_W2, w2, 0.0)
    for field, val in ((REC_W1, w1), (REC_R2, r2), (REC_R1, r1), (REC_E2, i2), (REC_E1, i1)):
        rec = jnp.where(lane == field, val, rec)
    rec_ref[rows, :] = rec
    rect_ref[:, rows] = rec.T


REC_E1, REC_E2, REC_R1, REC_R2, REC_W1, REC_W2 = range(6)
TOKEN_ROWS = D_MODEL // LANES


def _merge(x, attn, rnn, sga, sgr, wa, wr, wo, g2, w_route, b_route, tri, tm, seg, precise,
           new_sequences=None):
    n = x.shape[0]
    steps = n // tm
    tiles_per_seg = seg // tm
    row = lambda i: (i, 0)
    scratch = [pltpu.VMEM((SUBLANES, LANES), F32)]
    if new_sequences is None:
        head_specs = [pl.BlockSpec((tm, D_MODEL), row), pl.BlockSpec((tm, D_Q), row)]
        head_args = [x, attn]
        tiles_per_seq = None
    else:
        sinks, q, k, v, seq_len, expert_weights = new_sequences
        assert steps == N_EXPERTS and attn is None
        tiles_per_seq = seq_len // tm
        prev = lambda i: (jnp.maximum(i * (tm // ATTN_BLOCK) - 1, 0), 0)
        head_specs = [
            pl.BlockSpec(memory_space=pltpu.SMEM),
            pl.BlockSpec((tm, D_Q), row),
            pl.BlockSpec((tm, D_KV), row),
            pl.BlockSpec((ATTN_BLOCK, D_KV), prev),
            pl.BlockSpec((tm, D_KV), row),
            pl.BlockSpec((ATTN_BLOCK, D_KV), prev),
            pl.BlockSpec((tm, D_MODEL), row),
        ]
        head_args = [sinks, q, k, k, v, v, x]
        scratch.append(pltpu.VMEM((tm, D_Q), BF16))
    in_specs = head_specs + [
        pl.BlockSpec((tm, D_RNN), row),
        pl.BlockSpec((tm, D_MODEL), row),
        pl.BlockSpec((tm, D_MODEL), row),
        _full((D_Q, D_MODEL)),
        _full((D_RNN, D_MODEL)),
        _full((D_MODEL, D_MODEL)),
        _full((1, D_MODEL)),
        _full((D_MODEL, LANES)),
        _full((1, LANES)),
        _full((tm, tm)),
    ]
    out_specs = [
        pl.BlockSpec((tm, D_MODEL), row),
        pl.BlockSpec((tm * TOKEN_ROWS, LANES), row),
        pl.BlockSpec((tm, LANES), row),
        pl.BlockSpec((LANES, tm), lambda i: (0, i)),
        pl.BlockSpec((1, SUBLANES, LANES), lambda i: (i // tiles_per_seg, 0, 0)),
    ]
    out_shape = [
        jax.ShapeDtypeStruct((n, D_MODEL), F32),
        jax.ShapeDtypeStruct((n * TOKEN_ROWS, LANES), F32),
        jax.ShapeDtypeStruct((n, LANES), F32),
        jax.ShapeDtypeStruct((LANES, n), F32),
        jax.ShapeDtypeStruct((n // seg, SUBLANES, LANES), F32),
    ]
    args = head_args + [rnn, sga, sgr, wa, wr, wo, g2, w_route, b_route, tri]
    if new_sequences is not None:
        for w in expert_weights:
            spec = pl.BlockSpec((1,) + w.shape[1:], lambda i: (i, 0, 0))
            in_specs.append(spec)
            out_specs.append(spec)
            out_shape.append(jax.ShapeDtypeStruct(w.shape, BF16))
            args.append(w)
    return pl.pallas_call(
        functools.partial(_merge_kernel, precise=precise, tiles_per_seg=tiles_per_seg,
                          tiles_per_seq=tiles_per_seq),
        grid=(steps,),
        in_specs=in_specs,
        out_specs=out_specs,
        out_shape=out_shape,
        scratch_shapes=scratch,
        compiler_params=_params("arbitrary"),
        name="merge",
    )(*args)


MOE_CHUNK = 256
MOE_TAIL = 128
MOE_EXPERTS_PER_STEP = 2
MOE_LOOP_TOKENS = 32
MOE_VMEM_LIMIT = 60 * 1024 * 1024


def _seg_rows(seg):
    return 2 * seg + N_EXPERTS * SUBLANES + MOE_CHUNK


def _token_rows(i):
    return pl.ds(pl.multiple_of(i * TOKEN_ROWS, TOKEN_ROWS), TOKEN_ROWS)


def _sorted_rows(first_row):
    return pl.ds(pl.multiple_of(first_row, TOKEN_ROWS), TOKEN_ROWS)


def _moe_kernel(off_ref, nfull_ref, rem_ref, slot_ref, xtm_ref, wg_ref, wu_ref, wd_ref,
                x2_ref, rec_ref, o_ref, buf, g1, g2, *, seg, td, tc):
    s = pl.program_id(0)
    p = pl.program_id(1)
    n_disp = seg // td
    n_exp = N_EXPERTS // MOE_EXPERTS_PER_STEP

    @pl.when((s == 0) & (p == 0))
    def _():
        buf[...] = jnp.zeros_like(buf)

    @pl.when(p < n_disp)
    def _():
        def dispatch(g, carry):
            for j in range(MOE_LOOP_TOKENS):
                t = g * MOE_LOOP_TOKENS + j
                row = xtm_ref[_token_rows(t), :]
                for k in range(2):
                    buf[_sorted_rows(slot_ref[0, 0, k * seg + p * td + t]), :] = row
            return carry

        lax.fori_loop(0, td // MOE_LOOP_TOKENS, dispatch, 0)

    def run_chunk(e, row0, rows, valid):
        r0 = pl.multiple_of(row0 * TOKEN_ROWS, SUBLANES * TOKEN_ROWS)
        xf = [buf[pl.ds(r0 + j, rows, stride=TOKEN_ROWS), :] for j in range(TOKEN_ROWS)]
        x = jnp.concatenate(xf, axis=-1).astype(BF16)
        hg = jnp.dot(x, wg_ref[e], preferred_element_type=F32)
        hu = jnp.dot(x, wu_ref[e], preferred_element_type=F32)
        h = (hg * _sigmoid(hg)) * hu
        y = jnp.dot(h.astype(BF16), wd_ref[e], preferred_element_type=F32)
        if valid is not None:
            mine = lax.broadcasted_iota(jnp.int32, (rows, LANES), 0) < valid
        for j in range(TOKEN_ROWS):
            yj = y[:, j * LANES:(j + 1) * LANES]
            if valid is not None:
                yj = jnp.where(mine, yj, xf[j])
            buf[pl.ds(r0 + j, rows, stride=TOKEN_ROWS), :] = yj

    def run_expert(e):
        idx = s * N_EXPERTS + (p - n_disp) * MOE_EXPERTS_PER_STEP + e
        base = off_ref[idx]
        n_full = nfull_ref[idx]
        rem = rem_ref[idx]

        def chunk(c, carry):
            run_chunk(e, base + c * MOE_CHUNK, MOE_CHUNK, None)
            return carry

        lax.fori_loop(0, n_full, chunk, 0)
        last = base + n_full * MOE_CHUNK
        for units in range(1, MOE_CHUNK // MOE_TAIL + 2):
            @pl.when((rem > (units - 1) * MOE_TAIL) & (rem <= units * MOE_TAIL))
            def _():
                run_chunk(e, last, units * MOE_TAIL, rem)

    @pl.when((p >= n_disp) & (p < n_disp + n_exp))
    def _():
        for e in range(MOE_EXPERTS_PER_STEP):
            run_expert(e)

    @pl.when(p >= n_disp + n_exp)
    def _():
        t0 = (p - n_disp - n_exp) * tc

        def gather(g, carry):
            for j in range(MOE_LOOP_TOKENS):
                t = g * MOE_LOOP_TOKENS + j
                g1[_token_rows(t), :] = buf[_sorted_rows(slot_ref[0, 0, t0 + t]), :]
                g2[_token_rows(t), :] = buf[_sorted_rows(slot_ref[0, 0, seg + t0 + t]), :]
            return carry

        lax.fori_loop(0, tc // MOE_LOOP_TOKENS, gather, 0)
        rec = rec_ref[...]
        lane = lax.broadcasted_iota(jnp.int32, rec.shape, 1)
        w1 = jnp.broadcast_to(
            jnp.sum(jnp.where(lane == REC_W1, rec, 0.0), axis=-1, keepdims=True), rec.shape)
        w2 = jnp.broadcast_to(
            jnp.sum(jnp.where(lane == REC_W2, rec, 0.0), axis=-1, keepdims=True), rec.shape)
        for j in range(TOKEN_ROWS):
            cols = slice(j * LANES, (j + 1) * LANES)
            o_ref[:, cols] = (x2_ref[:, cols] + w1 * g1[pl.ds(j, tc, stride=TOKEN_ROWS), :]
                              + w2 * g2[pl.ds(j, tc, stride=TOKEN_ROWS), :])


def _moe(off, nfull, rem, slot, xtm, wg, wu, wd, x2, rec, seg, td, tc):
    n = x2.shape[0]
    n_seg = n // seg
    n_disp, n_comb = seg // td, seg // tc
    per_step = MOE_EXPERTS_PER_STEP
    n_exp = N_EXPERTS // per_step
    rows = _seg_rows(seg) * TOKEN_ROWS
    disp_tile = lambda s, p, *_: (s * n_disp + jnp.minimum(p, n_disp - 1), 0)
    expert = lambda s, p, *_: (jnp.clip(p - n_disp, 0, n_exp - 1), 0, 0)
    comb_tile = lambda s, p, *_: (s * n_comb + jnp.clip(p - n_disp - n_exp, 0, n_comb - 1), 0)
    grid_spec = pltpu.PrefetchScalarGridSpec(
        num_scalar_prefetch=3,
        grid=(n_seg, n_disp + n_exp + n_comb),
        in_specs=[
            pl.BlockSpec((1, 1, 2 * seg), lambda s, p, *_: (s, 0, 0), memory_space=pltpu.SMEM),
            pl.BlockSpec((td * TOKEN_ROWS, LANES), disp_tile),
            pl.BlockSpec((per_step, D_MODEL, D_EXPERT), expert),
            pl.BlockSpec((per_step, D_MODEL, D_EXPERT), expert),
            pl.BlockSpec((per_step, D_EXPERT, D_MODEL), expert),
            pl.BlockSpec((tc, D_MODEL), comb_tile),
            pl.BlockSpec((tc, LANES), comb_tile),
        ],
        out_specs=pl.BlockSpec((tc, D_MODEL), comb_tile),
        scratch_shapes=[pltpu.VMEM((rows, LANES), F32),
                        pltpu.VMEM((tc * TOKEN_ROWS, LANES), F32),
                        pltpu.VMEM((tc * TOKEN_ROWS, LANES), F32)],
    )
    return pl.pallas_call(
        functools.partial(_moe_kernel, seg=seg, td=td, tc=tc),
        grid_spec=grid_spec,
        out_shape=jax.ShapeDtypeStruct((n, D_MODEL), F32),
        compiler_params=pltpu.CompilerParams(
            dimension_semantics=("arbitrary", "arbitrary"), vmem_limit_bytes=MOE_VMEM_LIMIT),
        name="moe",
    )(off, nfull, rem, slot, xtm, wg, wu, wd, x2, rec)


def _plan(rect, cnt, seg):
    n = rect.shape[1]
    expert = rect[REC_E1:REC_E2 + 1].astype(jnp.int32)
    rank = rect[REC_R1:REC_R2 + 1].astype(jnp.int32)
    counts = cnt[:, 0, :N_EXPERTS].astype(jnp.int32)
    padded = (counts + SUBLANES - 1) // SUBLANES * SUBLANES
    off = jnp.cumsum(padded, axis=1) - padded
    n_full = counts // MOE_CHUNK
    rem = counts - n_full * MOE_CHUNK
    join = (rem > 0) & (rem <= MOE_TAIL) & (n_full > 0)
    n_full = n_full - join
    rem = rem + join * MOE_CHUNK
    off_tok = jnp.repeat(off.T, seg, axis=1)
    hit = expert[:, None, :] == jnp.arange(N_EXPERTS, dtype=jnp.int32)[None, :, None]
    slot = (rank + jnp.sum(jnp.where(hit, off_tok[None], 0), axis=1)) * TOKEN_ROWS
    slot = slot.reshape(2, n // seg, seg).transpose(1, 0, 2).reshape(n // seg, 1, 2 * seg)
    return off.reshape(-1), n_full.reshape(-1), rem.reshape(-1), slot


def _rope_tables(pos):
    half = HEAD_DIM // 2
    inv_freq = ROPE_THETA ** (-jnp.arange(half, dtype=F32) / half)
    ang = pos[:, None] * inv_freq[None, :]
    return jnp.cos(ang), jnp.sin(ang)


def _rope_tables_range(n):
    hi = jnp.arange(n // ATTN_BLOCK, dtype=F32) * ATTN_BLOCK
    lo = jnp.arange(ATTN_BLOCK, dtype=F32)
    (cos_hi, sin_hi), (cos_lo, sin_lo) = _rope_tables(hi), _rope_tables(lo)
    cos = cos_hi[:, None, :] * cos_lo[None] - sin_hi[:, None, :] * sin_lo[None]
    sin = sin_hi[:, None, :] * cos_lo[None] + cos_hi[:, None, :] * sin_lo[None]
    return cos.reshape(n, -1), sin.reshape(n, -1)


def kernel(x_prompt, x_sample, cache_k_win, cache_v_win, state_conv, state_lru_h, attn_norm_g, w_in, q_norm_g, k_norm_g, attn_sinks, conv_w, conv_b, w_lru_a, b_lru_a, w_lru_i, b_lru_i, lru_lambda, w_br_attn, w_br_rnn, w_out, ffn_norm_g, w_route_group, b_route_group, w_route_expert, b_route_expert, w_exp_gate, w_exp_up, w_exp_down):
    batch, seq, _ = x_prompt.shape
    dec_batch, dec_seq, _ = x_sample.shape
    depth = w_in.shape[0]
    assert depth == 1 and dec_seq == 1
    l = 0

    w_in_f = w_in[l]
    qkg = jnp.concatenate([jnp.tile(q_norm_g[l], N_Q_HEADS), jnp.tile(k_norm_g[l], N_KV_HEADS)])[None, :]
    wcat_f = jnp.concatenate([w_lru_a[l], w_lru_i[l]], axis=-1)
    wa_f, wr_f, wo_f = w_br_attn[l], w_br_rnn[l], w_out[l]
    w_route_f = jnp.concatenate(
        [w_route_expert[l], w_route_group[l],
         jnp.zeros((D_MODEL, LANES - N_EXPERTS - N_GROUPS), F32)], axis=-1)
    wcat = wcat_f.astype(BF16)
    wa_b, wr_b, wo_b, w_route = (w.astype(BF16) for w in (wa_f, wr_f, wo_f, w_route_f))
    b_route = jnp.concatenate(
        [b_route_expert[l], b_route_group[l], jnp.zeros((LANES - N_EXPERTS - N_GROUPS,), F32)])[None, :]
    experts_f = (w_exp_gate[l], w_exp_up[l], w_exp_down[l])
    g1 = attn_norm_g[l][None, :]
    g2 = ffn_norm_g[l][None, :]
    cw, cb = conv_w[l], conv_b[l][None, :]
    b_a, b_i, lam = b_lru_a[l][None, :], b_lru_i[l][None, :], lru_lambda[l][None, :]
    sinks = attn_sinks[l]

    def tail(x, attn, rnn, sga, sgr, tm, seg, precise, experts_b=None, qkv=None):
        wa, wr, wo, wrt = (wa_f, wr_f, wo_f, w_route_f) if precise else (wa_b, wr_b, wo_b, w_route)
        tri = jnp.tril(jnp.ones((tm, tm), BF16), -1)
        new_sequences = None if qkv is None else (sinks, *qkv, seq, experts_f)
        outs = _merge(x, attn, rnn, sga, sgr, wa, wr, wo, g2, wrt, b_route, tri, tm, seg, precise,
                      new_sequences=new_sequences)
        x2, xtm, rec, rect, cnt = outs[:5]
        wg, wu, wd = experts_b or outs[5:]
        off, nfull, rem, slot = _plan(rect, cnt, seg)
        y = _moe(off, nfull, rem, slot, xtm, wg, wu, wd, x2, rec, seg, tm, tm)
        return y, (wg, wu, wd)

    xs = x_sample.reshape(dec_batch, D_MODEL)
    cos_s, sin_s = _rope_tables(jnp.full((dec_batch,), PAST_LEN, F32))
    qs, ks, vs, xrs, gys, sgas, sgrs, w_in_b = _proj(xs, g1, w_in_f, cos_s, sin_s, qkg, 1, dec_batch,
                                                     dec_batch, True)

    xp = x_prompt.reshape(batch * seq, D_MODEL)
    cos_p, sin_p = _rope_tables_range(seq)
    q, k, v, rnn, sga, sgr, h_last, conv_tail = _proj(
        xp, g1, w_in_b, cos_p, sin_p, qkg, batch, seq, PROMPT_TILE, False,
        rnn_weights=(cw, cb, wcat, b_a, b_i, lam))
    y_prompt, experts_b = tail(xp, None, rnn, sga, sgr, PROMPT_TILE, MOE_SEGMENT, False,
                               qkv=(q, k, v))
    y_prompt = y_prompt.reshape(batch, seq, D_MODEL)

    def last_rows(a, rows):
        return a.reshape(batch, seq, a.shape[-1])[:, seq - rows:]

    k_win_p = last_rows(k, WINDOW).reshape(1, batch, WINDOW, N_KV_HEADS, HEAD_DIM)
    v_win_p = last_rows(v, WINDOW).reshape(1, batch, WINDOW, N_KV_HEADS, HEAD_DIM)
    conv_p = conv_tail[None, :, SUBLANES - (CONV_W - 1):, :]
    h_p = h_last[None, :, 0, :]

    ck = cache_k_win[l].reshape(dec_batch, WINDOW, D_KV)
    cv = cache_v_win[l].reshape(dec_batch, WINDOW, D_KV)
    attn_s, k_win_s, v_win_s = _attn_sample(qs, ks, vs, ck, cv, sinks)
    sc = state_conv[l]
    rnn_s, h_s = _rnn_sample(xrs, gys, sc[:, 0], sc[:, 1], sc[:, 2], state_lru_h[l],
                             cw, cb, wcat_f, b_a, b_i, lam)
    y_sample, _ = tail(xs, attn_s, rnn_s, sgas, sgrs, dec_batch, dec_batch, True, experts_b)
    y_sample = y_sample.reshape(dec_batch, 1, D_MODEL)
    conv_s = jnp.stack([sc[:, 1], sc[:, 2], xrs], axis=1)[None]

    return (y_prompt, y_sample, k_win_p, v_win_p, conv_p, h_p,
            k_win_s.reshape(1, dec_batch, WINDOW, N_KV_HEADS, HEAD_DIM),
            v_win_s.reshape(1, dec_batch, WINDOW, N_KV_HEADS, HEAD_DIM),
            conv_s, h_s[None])
```

```python
import functools

import jax
import jax.numpy as jnp
from jax import lax
from jax.experimental import pallas as pl
from jax.experimental.pallas import tpu as pltpu

D_MODEL = 1024
HEAD_DIM = 64
N_Q_HEADS = 8
N_KV_HEADS = 2
Q_PER_KV = N_Q_HEADS // N_KV_HEADS
WINDOW = 128
ATTN_BLOCK = 128
ROPE_THETA = 10000.0
SCALE = HEAD_DIM ** -0.5
NEG_INF = -1e30
D_RNN = 1280
N_RNN_BLOCKS = 10
RNN_BLOCK = D_RNN // N_RNN_BLOCKS
CONV_W = 4
LRU_C = 8.0
N_GROUPS = 4
EXPERTS_PER_GROUP = 8
N_EXPERTS = N_GROUPS * EXPERTS_PER_GROUP
D_EXPERT = 256
PAST_LEN = 16384
EPS = 1e-6
D_Q = N_Q_HEADS * HEAD_DIM
D_KV = N_KV_HEADS * HEAD_DIM
D_IN = D_Q + 2 * D_KV + 2 * D_RNN + 2 * D_MODEL
OFF_K = D_Q
OFF_V = OFF_K + D_KV
OFF_XR = OFF_V + D_KV
OFF_YR = OFF_XR + D_RNN
OFF_GA = OFF_YR + D_RNN
OFF_GR = OFF_GA + D_MODEL

LANES = 128
SUBLANES = 8
VMEM_LIMIT = 56 * 1024 * 1024
PROMPT_TILE = 512
MOE_SEGMENT = 4096

F32 = jnp.float32
BF16 = jnp.bfloat16


def _params(*sem):
    return pltpu.CompilerParams(dimension_semantics=sem, vmem_limit_bytes=VMEM_LIMIT)


def _sigmoid(x):
    return 1.0 / (1.0 + jnp.exp(-x))


def _gelu_tanh(x):
    c = 0.7978845608028654
    half_x = 0.5 * x
    return half_x + half_x * jnp.tanh(x * (c + (c * 0.044715) * (x * x)))


def _full(shape, single_buffer=False):
    index_map = lambda *_: (0,) * len(shape)
    if single_buffer:
        return pl.BlockSpec(shape, index_map, pipeline_mode=pl.Buffered(1))
    return pl.BlockSpec(shape, index_map)


def _mm(a, b, precise, dims=None):
    if precise:
        a, b, prec = a.astype(F32), b.astype(F32), lax.Precision.HIGHEST
    else:
        a, b, prec = a.astype(BF16), b.astype(BF16), None
    if dims is None:
        return jnp.dot(a, b, preferred_element_type=F32, precision=prec)
    return lax.dot_general(a, b, dims, preferred_element_type=F32, precision=prec)


_NT = (((1,), (1,)), ((), ()))


def _proj_kernel(x_ref, g_ref, w_ref, cos_ref, sin_ref, qkg_ref, *rest, precise, fuse_rnn):
    if fuse_rnn:
        rnn_w = rest[:6]
        q_ref, k_ref, v_ref, rnn_ref, sga_ref, sgr_ref, hl_ref, ct_ref = rest[6:14]
        xbuf, a_scr, b_scr, h_scr, hcar = rest[14:]
        pl.when(pl.program_id(1) == 0)(functools.partial(_rnn_start_sequence, xbuf, hcar))
    else:
        q_ref, k_ref, v_ref, xr_ref, gy_ref, sga_ref, sgr_ref, wb_ref = rest
        wb_ref[...] = w_ref[...].astype(BF16)
    x = x_ref[...]
    inv = lax.rsqrt(jnp.mean(x * x, axis=-1, keepdims=True) + EPS)
    xn = x * inv * g_ref[...]
    if not precise:
        xn = xn.astype(BF16)

    def proj(lo, hi):
        return _mm(xn, w_ref[:, lo:hi], precise)

    def qk_heads():
        qk = proj(0, OFF_V)
        tm = qk.shape[0]
        lane = lax.broadcasted_iota(jnp.int32, (tm, LANES), 1)
        lo_head = lane < HEAD_DIM
        first_half = (lane % HEAD_DIM) < (HEAD_DIM // 2)
        cos_f, sin_f = cos_ref[...], sin_ref[...]
        reps = LANES // HEAD_DIM
        cos = jnp.concatenate([cos_f, cos_f] * reps, axis=-1)
        sin = jnp.concatenate([-sin_f, sin_f] * reps, axis=-1)
        for g in range(OFF_V // LANES):
            seg = qk[:, g * LANES:(g + 1) * LANES]
            sq = seg * seg
            s_lo = jnp.sum(jnp.where(lo_head, sq, 0.0), axis=-1, keepdims=True)
            s_hi = jnp.sum(jnp.where(lo_head, 0.0, sq), axis=-1, keepdims=True)
            ms = jnp.where(lo_head, s_lo, s_hi) * (1.0 / HEAD_DIM)
            normed = seg * lax.rsqrt(ms + EPS) * qkg_ref[:, g * LANES:(g + 1) * LANES]
            partner = jnp.where(first_half,
                                pltpu.roll(normed, LANES - HEAD_DIM // 2, axis=1),
                                pltpu.roll(normed, HEAD_DIM // 2, axis=1))
            roped = normed * cos + partner * sin
            if g < D_Q // LANES:
                q_ref[:, g * LANES:(g + 1) * LANES] = (roped * SCALE).astype(q_ref.dtype)
            else:
                k_ref[...] = roped

    def values():
        v_ref[...] = proj(OFF_V, OFF_XR)

    def gate(out_ref, off, c0, c1, r0=0, r1=x.shape[0]):
        y = _mm(xn[r0:r1], w_ref[:, off + c0:off + c1], precise)
        out_ref[r0:r1, c0:c1] = _sigmoid(y).astype(out_ref.dtype)

    if not fuse_rnn:
        xr_ref[...] = proj(OFF_XR, OFF_YR)
        gy_ref[...] = _gelu_tanh(proj(OFF_YR, OFF_GA)).astype(gy_ref.dtype)
        qk_heads()
        values()
        gate(sga_ref, OFF_GA, 0, D_MODEL)
        gate(sgr_ref, OFF_GR, 0, D_MODEL)
        return

    slab = RNN_SLAB_BLOCKS * RNN_BLOCK
    half = x.shape[0] // 2
    others = [qk_heads, values]
    for out_ref, off in ((sga_ref, OFF_GA), (sgr_ref, OFF_GR)):
        others += [functools.partial(gate, out_ref, off, c, c + slab, r, r + half)
                   for c in range(0, D_MODEL, slab) for r in (0, half)]
    others = iter(others)
    n_slabs = D_RNN // slab

    def slab_inputs(i):
        c0 = i * slab
        return (proj(OFF_XR + c0, OFF_XR + c0 + slab),
                _gelu_tanh(proj(OFF_YR + c0, OFF_YR + c0 + slab)))

    gy = []
    nxt = slab_inputs(0)
    for i in range(n_slabs):
        (xr, gy_i), nxt = nxt, None
        gy.append(gy_i)
        ct_ref[0, :, i * slab:(i + 1) * slab] = xr[xr.shape[0] - SUBLANES:, :]
        for _ in _rnn_scan_terms(xr, i * RNN_SLAB_BLOCKS, *rnn_w, xbuf, a_scr, b_scr):
            if nxt is None and i + 1 < n_slabs:
                nxt = slab_inputs(i + 1)
            next(others, lambda: None)()
    for other in others:
        other()
    _rnn_scan_finish(jnp.concatenate(gy, axis=-1), rnn_ref, hl_ref, a_scr, b_scr, h_scr, hcar)


RNN_SLAB_BLOCKS = 2


def _proj(x, g, w_in, cos_t, sin_t, qkg, batch, seq, tm, precise, rnn_weights=None):
    n = batch * seq
    nt = seq // tm
    row = lambda b, t: (b * nt + t, 0)
    per_seq = lambda b, t: (b, 0, 0)
    act = F32 if precise else BF16
    rows_out = lambda width, dtype: (jax.ShapeDtypeStruct((n, width), dtype),
                                     pl.BlockSpec((tm, width), row))
    state_out = (jax.ShapeDtypeStruct((batch, SUBLANES, D_RNN), F32),
                 pl.BlockSpec((1, SUBLANES, D_RNN), per_seq))
    outs = [rows_out(D_Q, act), rows_out(D_KV, F32), rows_out(D_KV, F32)]
    in_specs = [
        pl.BlockSpec((tm, D_MODEL), row),
        _full((1, D_MODEL)),
        _full((D_MODEL, D_IN), single_buffer=True),
        pl.BlockSpec((tm, HEAD_DIM // 2), lambda b, t: (t, 0)),
        pl.BlockSpec((tm, HEAD_DIM // 2), lambda b, t: (t, 0)),
        _full((1, OFF_V)),
    ]
    args = [x, g, w_in, cos_t, sin_t, qkg]
    scratch = []
    if rnn_weights is None:
        outs += [rows_out(D_RNN, F32), rows_out(D_RNN, act)]
    else:
        assert not precise
        outs += [rows_out(D_RNN, act)]
        in_specs += [_full(w.shape) for w in rnn_weights]
        args += list(rnn_weights)
        groups = tm // SUBLANES
        scratch = [pltpu.VMEM((tm + SUBLANES, D_RNN), F32)]
        scratch += [pltpu.VMEM((groups, SUBLANES, D_RNN), F32)] * 3
        scratch += [pltpu.VMEM((SUBLANES, D_RNN), F32)]
    outs += [rows_out(D_MODEL, act), rows_out(D_MODEL, act)]
    if rnn_weights is not None:
        outs += [state_out, state_out]
    else:
        assert batch * nt == 1 and w_in.dtype == F32
        outs += [(jax.ShapeDtypeStruct(w_in.shape, BF16), _full(w_in.shape, single_buffer=True))]
    return pl.pallas_call(
        functools.partial(_proj_kernel, precise=precise, fuse_rnn=rnn_weights is not None),
        grid=(batch, nt),
        in_specs=in_specs,
        out_specs=[spec for _, spec in outs],
        out_shape=[shape for shape, _ in outs],
        scratch_shapes=scratch,
        compiler_params=_params("parallel", "arbitrary"),
        name="proj",
    )(*args)


def _sink_column(sink_ref, h, rows, rows_per_head):
    r = lax.broadcasted_iota(jnp.int32, (rows, 1), 0) // rows_per_head
    col = jnp.full((rows, 1), sink_ref[h * Q_PER_KV], F32)
    for g in range(1, Q_PER_KV):
        col = jnp.where(r == g, sink_ref[h * Q_PER_KV + g], col)
    return col


def _window_masks(first_tile):
    blk = ATTN_BLOCK
    rows = Q_PER_KV * blk
    i = lax.broadcasted_iota(jnp.int32, (rows, 2 * blk), 0) % blk
    j = lax.broadcasted_iota(jnp.int32, (rows, 2 * blk), 1)
    d = j - i
    in_window = (d >= 1) & (d <= WINDOW)
    first = (d >= jnp.where(first_tile, jnp.maximum(1, blk - i), 1)) & (d <= WINDOW)
    return in_window, first


def _attn_kv_head(sub, h, valid, sink_ref, q_ref, k_all, v_all, o_ref):
    blk = ATTN_BLOCK
    rows = Q_PER_KV * blk
    q = q_ref[sub * blk:(sub + 1) * blk, :]
    kh = k_all[sub * blk:(sub + 2) * blk, h * HEAD_DIM:(h + 1) * HEAD_DIM]
    vh = v_all[sub * blk:(sub + 2) * blk, h * HEAD_DIM:(h + 1) * HEAD_DIM]
    qs = jnp.concatenate(
        [q[:, (h * Q_PER_KV + g) * HEAD_DIM:(h * Q_PER_KV + g + 1) * HEAD_DIM]
         for g in range(Q_PER_KV)], axis=0)
    s = _mm(qs, kh, False, _NT)
    yield
    s = jnp.where(valid, s, NEG_INF)
    sink = _sink_column(sink_ref, h, rows, blk)
    m = jnp.maximum(jnp.max(s, axis=-1, keepdims=True), sink)
    p = jnp.exp(s - m)
    denom = jnp.sum(p, axis=-1, keepdims=True) + jnp.exp(sink - m)
    yield
    o = _mm(p, vh, False) * (1.0 / denom)
    for g in range(Q_PER_KV):
        c = (h * Q_PER_KV + g) * HEAD_DIM
        o_ref[sub * blk:(sub + 1) * blk, c:c + HEAD_DIM] = o[g * blk:(g + 1) * blk].astype(o_ref.dtype)


def _lockstep(generators):
    while generators:
        generators = [g for g in generators if next(g, "done") != "done"]
        yield


SAMPLE_BT = 8


def _attn_sample_kernel(sink_ref, q_ref, kn_ref, vn_ref, kc_ref, vc_ref, o_ref, ko_ref, vo_ref):
    bt = SAMPLE_BT
    pos = lax.broadcasted_iota(jnp.int32, (HEAD_DIM, WINDOW), 1)
    kn_t = kn_ref[...].T
    vn_t = vn_ref[...].T

    def shifted(cache_ref, new_t, out_ref, h):
        windows = []
        for b in range(bt):
            rolled = pltpu.roll(cache_ref[b, h], WINDOW - 1, axis=1)
            new = new_t[h * HEAD_DIM:(h + 1) * HEAD_DIM, b:b + 1]
            windows.append(jnp.where(pos == WINDOW - 1, new, rolled))
            out_ref[b, h] = windows[-1]
        return jnp.concatenate(windows, axis=1)

    q = q_ref[...]
    rows = Q_PER_KV * bt
    rb = lax.broadcasted_iota(jnp.int32, (rows, bt * WINDOW), 0) % bt
    cb = lax.broadcasted_iota(jnp.int32, (rows, bt * WINDOW), 1) // WINDOW
    valid = rb == cb

    def kv_head(h):
        kh_t = shifted(kc_ref, kn_t, ko_ref, h)
        vh_t = shifted(vc_ref, vn_t, vo_ref, h)
        qs = jnp.concatenate(
            [q[:, (h * Q_PER_KV + g) * HEAD_DIM:(h * Q_PER_KV + g + 1) * HEAD_DIM]
             for g in range(Q_PER_KV)], axis=0)
        s = _mm(qs, kh_t, True)
        yield
        s = jnp.where(valid, s, NEG_INF)
        sink = _sink_column(sink_ref, h, rows, bt)
        m = jnp.maximum(jnp.max(s, axis=-1, keepdims=True), sink)
        p = jnp.exp(s - m)
        denom = jnp.sum(p, axis=-1, keepdims=True) + jnp.exp(sink - m)
        o = _mm(p, vh_t, True, _NT) * (1.0 / denom)
        for g in range(Q_PER_KV):
            c = (h * Q_PER_KV + g) * HEAD_DIM
            o_ref[:, c:c + HEAD_DIM] = o[g * bt:(g + 1) * bt].astype(o_ref.dtype)

    heads = [kv_head(h) for h in range(N_KV_HEADS)]
    while heads:
        heads = [head for head in heads if next(head, "done") != "done"]


def _attn_sample(q, k_new, v_new, cache_k, cache_v, sinks):
    nbatch = q.shape[0]
    bt = SAMPLE_BT
    row = lambda i: (i, 0)
    win = pl.BlockSpec((bt, N_KV_HEADS, HEAD_DIM, WINDOW), lambda i: (i, 0, 0, 0))
    return pl.pallas_call(
        _attn_sample_kernel,
        grid=(nbatch // bt,),
        in_specs=[
            pl.BlockSpec(memory_space=pltpu.SMEM),
            pl.BlockSpec((bt, D_Q), row),
            pl.BlockSpec((bt, D_KV), row),
            pl.BlockSpec((bt, D_KV), row),
            win,
            win,
        ],
        out_specs=[pl.BlockSpec((bt, D_Q), row), win, win],
        out_shape=(
            jax.ShapeDtypeStruct((nbatch, D_Q), F32),
            jax.ShapeDtypeStruct(cache_k.shape, F32),
            jax.ShapeDtypeStruct(cache_v.shape, F32),
        ),
        compiler_params=_params("parallel"),
        name="attn_sample",
    )(sinks, q, k_new, v_new, cache_k, cache_v)


def _lru_terms(xc, wcat_ref, ba_ref, bi_ref, lam_ref, precise, block0=0):
    cols = slice(block0 * RNN_BLOCK, block0 * RNN_BLOCK + xc.shape[1])
    xcb = xc if precise else xc.astype(BF16)
    ya, yi = [], []
    for n in range(xc.shape[1] // RNN_BLOCK):
        y = _mm(xcb[:, n * RNN_BLOCK:(n + 1) * RNN_BLOCK], wcat_ref[block0 + n], precise)
        ya.append(y[:, :RNN_BLOCK])
        yi.append(y[:, RNN_BLOCK:])
    r = _sigmoid(jnp.concatenate(ya, axis=-1) + ba_ref[:, cols])
    gate_i = _sigmoid(jnp.concatenate(yi, axis=-1) + bi_ref[:, cols])
    neg_lam = -lam_ref[:, cols]
    softplus = jnp.maximum(neg_lam, 0.0) + jnp.log1p(jnp.exp(-jnp.abs(neg_lam)))
    log_a = (-LRU_C * softplus) * r
    a = jnp.exp(log_a)
    m = 1.0 - a * a
    b = jnp.where(m > 0.0, m * lax.rsqrt(m), 0.0) * (gate_i * xc)
    return a, b


def _rnn_start_sequence(xbuf, hcar):
    xbuf[0:SUBLANES, :] = jnp.zeros((SUBLANES, D_RNN), F32)
    hcar[...] = jnp.zeros((SUBLANES, D_RNN), F32)


def _rnn_scan_terms(x, block0, cw_ref, cb_ref, wcat_ref, ba_ref, bi_ref, lam_ref,
                    xbuf, a_scr, b_scr):
    tt, width = x.shape
    groups = tt // SUBLANES
    cols = slice(block0 * RNN_BLOCK, block0 * RNN_BLOCK + width)
    xbuf[SUBLANES:, cols] = x
    xc = cb_ref[:, cols] + cw_ref[CONV_W - 1:CONV_W, cols] * x
    for j in range(CONV_W - 1):
        s = CONV_W - 1 - j
        xc = xc + cw_ref[j:j + 1, cols] * xbuf[SUBLANES - s:SUBLANES - s + tt, cols]
    xbuf[0:SUBLANES, cols] = x[tt - SUBLANES:, :]
    yield
    a, b = _lru_terms(xc, wcat_ref, ba_ref, bi_ref, lam_ref, False, block0)
    yield
    a = a.reshape(groups, SUBLANES, width)
    b = b.reshape(groups, SUBLANES, width)
    step = lax.broadcasted_iota(jnp.int32, (groups, SUBLANES, width), 1)
    k = 1
    while k < SUBLANES:
        keep = step >= k
        a_sh = jnp.where(keep, pltpu.roll(a, k, axis=1), 1.0)
        b_sh = jnp.where(keep, pltpu.roll(b, k, axis=1), 0.0)
        b = a * b_sh + b
        a = a * a_sh
        k *= 2
        if k < SUBLANES:
            yield
    a_scr[:, :, cols] = a
    b_scr[:, :, cols] = b


def _rnn_scan_finish(gy, o_ref, hl_ref, a_scr, b_scr, h_scr, hcar):
    groups = a_scr.shape[0]
    tt = groups * SUBLANES

    def chain(g, h_in):
        h = a_scr[g] * h_in + b_scr[g]
        h_scr[g] = h
        return jnp.broadcast_to(h[SUBLANES - 1:SUBLANES, :], (SUBLANES, D_RNN))

    h_last = lax.fori_loop(0, groups, chain, hcar[...], unroll=True)
    hcar[...] = h_last
    hl_ref[0] = h_last
    h = h_scr[...].reshape(tt, D_RNN)
    o_ref[...] = (h * gy).astype(o_ref.dtype)


def _rnn_sample_kernel(xr_ref, gy_ref, s0_ref, s1_ref, s2_ref, h_ref, cw_ref, cb_ref,
                       wcat_ref, ba_ref, bi_ref, lam_ref, o_ref, hn_ref):
    x = xr_ref[...]
    xc = (cb_ref[...] + cw_ref[0:1, :] * s0_ref[...] + cw_ref[1:2, :] * s1_ref[...]
          + cw_ref[2:3, :] * s2_ref[...] + cw_ref[3:4, :] * x)
    a, b = _lru_terms(xc, wcat_ref, ba_ref, bi_ref, lam_ref, True)
    h = a * h_ref[...] + b
    hn_ref[...] = h
    o_ref[...] = (h * gy_ref[...].astype(F32)).astype(o_ref.dtype)


def _rnn_sample(xr, gy, s0, s1, s2, h_prev, conv_w, conv_b, wcat, b_a, b_i, lam):
    n = xr.shape[0]
    act = _full((n, D_RNN))
    return pl.pallas_call(
        _rnn_sample_kernel,
        grid=(1,),
        in_specs=[act, act, act, act, act, act,
                  _full((CONV_W, D_RNN)), _full((1, D_RNN)),
                  _full((N_RNN_BLOCKS, RNN_BLOCK, 2 * RNN_BLOCK)),
                  _full((1, D_RNN)), _full((1, D_RNN)), _full((1, D_RNN))],
        out_specs=[act, act],
        out_shape=(jax.ShapeDtypeStruct((n, D_RNN), F32),
                   jax.ShapeDtypeStruct((n, D_RNN), F32)),
        compiler_params=_params("arbitrary"),
        name="rnn_sample",
    )(xr, gy, s0, s1, s2, h_prev, conv_w, conv_b, wcat, b_a, b_i, lam)


def _merge_kernel(*refs, precise, tiles_per_seg, tiles_per_seq):
    fused = tiles_per_seq is not None
    if fused:
        (sink_ref, q_ref, kc_ref, kp_ref, vc_ref, vp_ref, x_ref, rn_ref, sga_ref, sgr_ref,
         wa_ref, wr_ref, wo_ref, g_ref, wrt_ref, brt_ref, tri_ref, wg_ref, wu_ref, wd_ref,
         x2_ref, xtm_ref, rec_ref, rect_ref, cnt_ref, wg_o, wu_o, wd_o, cnt_scr, at_ref) = refs
    else:
        (x_ref, at_ref, rn_ref, sga_ref, sgr_ref, wa_ref, wr_ref, wo_ref, g_ref, wrt_ref, brt_ref,
         tri_ref, x2_ref, xtm_ref, rec_ref, rect_ref, cnt_ref, cnt_scr) = refs

    @pl.when(pl.program_id(0) % tiles_per_seg == 0)
    def _():
        cnt_scr[...] = jnp.zeros_like(cnt_scr)

    tm = x_ref.shape[0]
    parts = MERGE_PARTS if tm % (MERGE_PARTS * LANES) == 0 else 1
    part = tm // parts
    state = {"counts": cnt_scr[...]}
    pending = [
        _merge_rows(slice(i * part, (i + 1) * part), state, x_ref, at_ref, rn_ref, sga_ref, sgr_ref,
                    wa_ref, wr_ref, wo_ref, g_ref, wrt_ref, brt_ref, tri_ref, x2_ref, xtm_ref,
                    rec_ref, rect_ref, precise)
        for i in range(parts)]
    active = []
    if fused:
        k_all = jnp.concatenate([kp_ref[...], kc_ref[...]], axis=0).astype(BF16)
        v_all = jnp.concatenate([vp_ref[...], vc_ref[...]], axis=0).astype(BF16)
        in_window, first = _window_masks(pl.program_id(0) % tiles_per_seq == 0)
        blocks = part // ATTN_BLOCK
        side = list(pending)
        for i in range(parts):
            heads = [_attn_kv_head(sub, h, first if sub == 0 else in_window, sink_ref, q_ref,
                                   k_all, v_all, at_ref)
                     for sub in range(i * blocks, (i + 1) * blocks) for h in range(N_KV_HEADS)]
            for _ in _lockstep(heads):
                if side:
                    next(side.pop(0))
                for part_pieces in list(active) * MERGE_PIECES_PER_ATTN_ROUND:
                    if part_pieces in active and next(part_pieces, "done") == "done":
                        active.remove(part_pieces)
            active.append(pending.pop(0))
    step = 0
    while pending or active:
        if pending and step % MERGE_STAGGER == 0:
            active.append(pending.pop(0))
        for part_pieces in list(active):
            if next(part_pieces, "done") == "done":
                active.remove(part_pieces)
        step += 1
    cnt_scr[...] = state["counts"]
    cnt_ref[0] = state["counts"]
    if fused:
        wg_o[...] = wg_ref[...].astype(BF16)
        wu_o[...] = wu_ref[...].astype(BF16)
        wd_o[...] = wd_ref[...].astype(BF16)


MERGE_PARTS = 2
MERGE_STAGGER = 3
MERGE_PIECES_PER_ATTN_ROUND = 1


def _merge_rows(rows, state, x_ref, at_ref, rn_ref, sga_ref, sgr_ref, wa_ref, wr_ref, wo_ref, g_ref,
                wrt_ref, brt_ref, tri_ref, x2_ref, xtm_ref, rec_ref, rect_ref, precise):
    yr = _mm(rn_ref[rows, :], wr_ref[...], precise)
    yield
    ya = _mm(at_ref[rows, :], wa_ref[...], precise)
    yield
    merged =sga_ref[rows, :].astype(F32) * ya + sgr_ref[rows, :].astype(F32) * yr
    yield
    x2 = x_ref[rows, :] + _mm(merged, wo_ref[...], precise)
    x2_ref[rows, :] = x2
    yield
    inv = lax.rsqrt(jnp.mean(x2 * x2, axis=-1, keepdims=True) + EPS)
    xn = x2 * inv * g_ref[...]
    for c in range(TOKEN_ROWS):
        xtm_ref[pl.ds(rows.start * TOKEN_ROWS + c, rows.stop - rows.start, stride=TOKEN_ROWS), :] = (
            xn[:, c * LANES:(c + 1) * LANES])
    yield

    logits = _mm(xn, wrt_ref[...], precise) + brt_ref[...]
    yield
    tm = logits.shape[0]
    lane_i = lax.broadcasted_iota(jnp.int32, (tm, LANES), 1)
    lane = lane_i.astype(F32)
    big = float(LANES)
    is_grp = (lane_i >= N_EXPERTS) & (lane_i < N_EXPERTS + N_GROUPS)
    gl = jnp.where(is_grp, logits, NEG_INF)
    gmax = jnp.max(gl, axis=-1, keepdims=True)
    g_idx = jnp.min(jnp.where(gl == gmax, lane, big), axis=-1, keepdims=True) - N_EXPERTS
    p_g = 1.0 / jnp.sum(jnp.exp(gl - gmax), axis=-1, keepdims=True)
    in_grp = (lane_i // EXPERTS_PER_GROUP).astype(F32) == g_idx
    el = jnp.where(in_grp, logits, NEG_INF)
    v1 = jnp.max(el, axis=-1, keepdims=True)
    i1 = jnp.min(jnp.where(el == v1, lane, big), axis=-1, keepdims=True)
    el2 = jnp.where(lane == i1, NEG_INF, el)
    v2 = jnp.max(el2, axis=-1, keepdims=True)
    i2 = jnp.min(jnp.where(el2 == v2, lane, big), axis=-1, keepdims=True)
    e2 = jnp.exp(v2 - v1)
    w1 = p_g / (1.0 + e2)
    w2 = p_g * e2 / (1.0 + e2)
    yield

    hit = jnp.where(lane == i1, 1.0, jnp.where(lane == i2, 1.0, 0.0))
    counts = state["counts"]
    state["counts"] = counts + jnp.sum(hit, axis=0, keepdims=True)
    before = (jnp.dot(tri_ref[0:tm, 0:tm], hit.astype(BF16), preferred_element_type=F32)
              + counts[0:1, :])
    yield
    r1 = jnp.sum(jnp.where(lane == i1, before, 0.0), axis=-1, keepdims=True)
    r2 = jnp.sum(jnp.where(lane == i2, before, 0.0), axis=-1, keepdims=True)
    rec = jnp.where(lane == REC_W2, w2, 0.0)
    for field, val in ((REC_W1, w1), (REC_R2, r2), (REC_R1, r1), (REC_E2, i2), (REC_E1, i1)):
        rec = jnp.where(lane == field, val, rec)
    rec_ref[rows, :] = rec
    rect_ref[:, rows] = rec.T


REC_E1, REC_E2, REC_R1, REC_R2, REC_W1, REC_W2 = range(6)
TOKEN_ROWS = D_MODEL // LANES


def _merge(x, attn, rnn, sga, sgr, wa, wr, wo, g2, w_route, b_route, tri, tm, seg, precise,
           new_sequences=None):
    n = x.shape[0]
    steps = n // tm
    tiles_per_seg = seg // tm
    row = lambda i: (i, 0)
    scratch = [pltpu.VMEM((SUBLANES, LANES), F32)]
    if new_sequences is None:
        head_specs = [pl.BlockSpec((tm, D_MODEL), row), pl.BlockSpec((tm, D_Q), row)]
        head_args = [x, attn]
        tiles_per_seq = None
    else:
        sinks, q, k, v, seq_len, expert_weights = new_sequences
        assert steps == N_EXPERTS and attn is None
        tiles_per_seq = seq_len // tm
        prev = lambda i: (jnp.maximum(i * (tm // ATTN_BLOCK) - 1, 0), 0)
        head_specs = [
            pl.BlockSpec(memory_space=pltpu.SMEM),
            pl.BlockSpec((tm, D_Q), row),
            pl.BlockSpec((tm, D_KV), row),
            pl.BlockSpec((ATTN_BLOCK, D_KV), prev),
            pl.BlockSpec((tm, D_KV), row),
            pl.BlockSpec((ATTN_BLOCK, D_KV), prev),
            pl.BlockSpec((tm, D_MODEL), row),
        ]
        head_args = [sinks, q, k, k, v, v, x]
        scratch.append(pltpu.VMEM((tm, D_Q), BF16))
    in_specs = head_specs + [
        pl.BlockSpec((tm, D_RNN), row),
        pl.BlockSpec((tm, D_MODEL), row),
        pl.BlockSpec((tm, D_MODEL), row),
        _full((D_Q, D_MODEL)),
        _full((D_RNN, D_MODEL)),
        _full((D_MODEL, D_MODEL)),
        _full((1, D_MODEL)),
        _full((D_MODEL, LANES)),
        _full((1, LANES)),
        _full((tm, tm)),
    ]
    out_specs = [
        pl.BlockSpec((tm, D_MODEL), row),
        pl.BlockSpec((tm * TOKEN_ROWS, LANES), row),
        pl.BlockSpec((tm, LANES), row),
        pl.BlockSpec((LANES, tm), lambda i: (0, i)),
        pl.BlockSpec((1, SUBLANES, LANES), lambda i: (i // tiles_per_seg, 0, 0)),
    ]
    out_shape = [
        jax.ShapeDtypeStruct((n, D_MODEL), F32),
        jax.ShapeDtypeStruct((n * TOKEN_ROWS, LANES), F32),
        jax.ShapeDtypeStruct((n, LANES), F32),
        jax.ShapeDtypeStruct((LANES, n), F32),
        jax.ShapeDtypeStruct((n // seg, SUBLANES, LANES), F32),
    ]
    args = head_args + [rnn, sga, sgr, wa, wr, wo, g2, w_route, b_route, tri]
    if new_sequences is not None:
        for w in expert_weights:
            spec = pl.BlockSpec((1,) + w.shape[1:], lambda i: (i, 0, 0))
            in_specs.append(spec)
            out_specs.append(spec)
            out_shape.append(jax.ShapeDtypeStruct(w.shape, BF16))
            args.append(w)
    return pl.pallas_call(
        functools.partial(_merge_kernel, precise=precise, tiles_per_seg=tiles_per_seg,
                          tiles_per_seq=tiles_per_seq),
        grid=(steps,),
        in_specs=in_specs,
        out_specs=out_specs,
        out_shape=out_shape,
        scratch_shapes=scratch,
        compiler_params=_params("arbitrary"),
        name="merge",
    )(*args)


MOE_CHUNK = 256
MOE_TAIL = 128
MOE_EXPERTS_PER_STEP = 2
MOE_LOOP_TOKENS = 32
MOE_VMEM_LIMIT = 60 * 1024 * 1024


def _seg_rows(seg):
    return 2 * seg + N_EXPERTS * SUBLANES + MOE_CHUNK


def _token_rows(i):
    return pl.ds(pl.multiple_of(i * TOKEN_ROWS, TOKEN_ROWS), TOKEN_ROWS)


def _sorted_rows(first_row):
    return pl.ds(pl.multiple_of(first_row, TOKEN_ROWS), TOKEN_ROWS)


def _moe_kernel(off_ref, nfull_ref, rem_ref, slot_ref, xtm_ref, wg_ref, wu_ref, wd_ref,
                x2_ref, rec_ref, o_ref, buf, g1, g2, *, seg, td, tc):
    s = pl.program_id(0)
    p = pl.program_id(1)
    n_disp = seg // td
    n_exp = N_EXPERTS // MOE_EXPERTS_PER_STEP

    @pl.when((s == 0) & (p == 0))
    def _():
        buf[...] = jnp.zeros_like(buf)

    @pl.when(p < n_disp)
    def _():
        def dispatch(g, carry):
            for j in range(MOE_LOOP_TOKENS):
                t = g * MOE_LOOP_TOKENS + j
                row = xtm_ref[_token_rows(t), :]
                for k in range(2):
                    buf[_sorted_rows(slot_ref[0, 0, k * seg + p * td + t]), :] = row
            return carry

        lax.fori_loop(0, td // MOE_LOOP_TOKENS, dispatch, 0)

    def run_chunk(e, row0, rows, valid):
        r0 = pl.multiple_of(row0 * TOKEN_ROWS, SUBLANES * TOKEN_ROWS)
        xf = [buf[pl.ds(r0 + j, rows, stride=TOKEN_ROWS), :] for j in range(TOKEN_ROWS)]
        x = jnp.concatenate(xf, axis=-1).astype(BF16)
        hg = jnp.dot(x, wg_ref[e], preferred_element_type=F32)
        hu = jnp.dot(x, wu_ref[e], preferred_element_type=F32)
        h = (hg * _sigmoid(hg)) * hu
        y = jnp.dot(h.astype(BF16), wd_ref[e], preferred_element_type=F32)
        if valid is not None:
            mine = lax.broadcasted_iota(jnp.int32, (rows, LANES), 0) < valid
        for j in range(TOKEN_ROWS):
            yj = y[:, j * LANES:(j + 1) * LANES]
            if valid is not None:
                yj = jnp.where(mine, yj, xf[j])
            buf[pl.ds(r0 + j, rows, stride=TOKEN_ROWS), :] = yj

    def run_expert(e):
        idx = s * N_EXPERTS + (p - n_disp) * MOE_EXPERTS_PER_STEP + e
        base = off_ref[idx]
        n_full = nfull_ref[idx]
        rem = rem_ref[idx]

        def chunk(c, carry):
            run_chunk(e, base + c * MOE_CHUNK, MOE_CHUNK, None)
            return carry

        lax.fori_loop(0, n_full, chunk, 0)
        last = base + n_full * MOE_CHUNK
        for units in range(1, MOE_CHUNK // MOE_TAIL + 2):
            @pl.when((rem > (units - 1) * MOE_TAIL) & (rem <= units * MOE_TAIL))
            def _():
                run_chunk(e, last, units * MOE_TAIL, rem)

    @pl.when((p >= n_disp) & (p < n_disp + n_exp))
    def _():
        for e in range(MOE_EXPERTS_PER_STEP):
            run_expert(e)

    @pl.when(p >= n_disp + n_exp)
    def _():
        t0 = (p - n_disp - n_exp) * tc

        def gather(g, carry):
            for j in range(MOE_LOOP_TOKENS):
                t = g * MOE_LOOP_TOKENS + j
                g1[_token_rows(t), :] = buf[_sorted_rows(slot_ref[0, 0, t0 + t]), :]
                g2[_token_rows(t), :] = buf[_sorted_rows(slot_ref[0, 0, seg + t0 + t]), :]
            return carry

        lax.fori_loop(0, tc // MOE_LOOP_TOKENS, gather, 0)
        rec = rec_ref[...]
        lane = lax.broadcasted_iota(jnp.int32, rec.shape, 1)
        w1 = jnp.broadcast_to(
            jnp.sum(jnp.where(lane == REC_W1, rec, 0.0), axis=-1, keepdims=True), rec.shape)
        w2 = jnp.broadcast_to(
            jnp.sum(jnp.where(lane == REC_W2, rec, 0.0), axis=-1, keepdims=True), rec.shape)
        for j in range(TOKEN_ROWS):
            cols = slice(j * LANES, (j + 1) * LANES)
            o_ref[:, cols] = (x2_ref[:, cols] + w1 * g1[pl.ds(j, tc, stride=TOKEN_ROWS), :]
                              + w2 * g2[pl.ds(j, tc, stride=TOKEN_ROWS), :])


def _moe(off, nfull, rem, slot, xtm, wg, wu, wd, x2, rec, seg, td, tc):
    n = x2.shape[0]
    n_seg = n // seg
    n_disp, n_comb = seg // td, seg // tc
    per_step = MOE_EXPERTS_PER_STEP
    n_exp = N_EXPERTS // per_step
    rows = _seg_rows(seg) * TOKEN_ROWS
    disp_tile = lambda s, p, *_: (s * n_disp + jnp.minimum(p, n_disp - 1), 0)
    expert = lambda s, p, *_: (jnp.clip(p - n_disp, 0, n_exp - 1), 0, 0)
    comb_tile = lambda s, p, *_: (s * n_comb + jnp.clip(p - n_disp - n_exp, 0, n_comb - 1), 0)
    grid_spec = pltpu.PrefetchScalarGridSpec(
        num_scalar_prefetch=3,
        grid=(n_seg, n_disp + n_exp + n_comb),
        in_specs=[
            pl.BlockSpec((1, 1, 2 * seg), lambda s, p, *_: (s, 0, 0), memory_space=pltpu.SMEM),
            pl.BlockSpec((td * TOKEN_ROWS, LANES), disp_tile),
            pl.BlockSpec((per_step, D_MODEL, D_EXPERT), expert),
            pl.BlockSpec((per_step, D_MODEL, D_EXPERT), expert),
            pl.BlockSpec((per_step, D_EXPERT, D_MODEL), expert),
            pl.BlockSpec((tc, D_MODEL), comb_tile),
            pl.BlockSpec((tc, LANES), comb_tile),
        ],
        out_specs=pl.BlockSpec((tc, D_MODEL), comb_tile),
        scratch_shapes=[pltpu.VMEM((rows, LANES), F32),
                        pltpu.VMEM((tc * TOKEN_ROWS, LANES), F32),
                        pltpu.VMEM((tc * TOKEN_ROWS, LANES), F32)],
    )
    return pl.pallas_call(
        functools.partial(_moe_kernel, seg=seg, td=td, tc=tc),
        grid_spec=grid_spec,
        out_shape=jax.ShapeDtypeStruct((n, D_MODEL), F32),
        compiler_params=pltpu.CompilerParams(
            dimension_semantics=("arbitrary", "arbitrary"), vmem_limit_bytes=MOE_VMEM_LIMIT),
        name="moe",
    )(off, nfull, rem, slot, xtm, wg, wu, wd, x2, rec)


def _plan(rect, cnt, seg):
    n = rect.shape[1]
    expert = rect[REC_E1:REC_E2 + 1].astype(jnp.int32)
    rank = rect[REC_R1:REC_R2 + 1].astype(jnp.int32)
    counts = cnt[:, 0, :N_EXPERTS].astype(jnp.int32)
    padded = (counts + SUBLANES - 1) // SUBLANES * SUBLANES
    off = jnp.cumsum(padded, axis=1) - padded
    n_full = counts // MOE_CHUNK
    rem = counts - n_full * MOE_CHUNK
    join = (rem > 0) & (rem <= MOE_TAIL) & (n_full > 0)
    n_full = n_full - join
    rem = rem + join * MOE_CHUNK
    off_tok = jnp.repeat(off.T, seg, axis=1)
    hit = expert[:, None, :] == jnp.arange(N_EXPERTS, dtype=jnp.int32)[None, :, None]
    slot = (rank + jnp.sum(jnp.where(hit, off_tok[None], 0), axis=1)) * TOKEN_ROWS
    slot = slot.reshape(2, n // seg, seg).transpose(1, 0, 2).reshape(n // seg, 1, 2 * seg)
    return off.reshape(-1), n_full.reshape(-1), rem.reshape(-1), slot


def _rope_tables(pos):
    half = HEAD_DIM // 2
    inv_freq = ROPE_THETA ** (-jnp.arange(half, dtype=F32) / half)
    ang = pos[:, None] * inv_freq[None, :]
    return jnp.cos(ang), jnp.sin(ang)


def _rope_tables_range(n):
    hi = jnp.arange(n // ATTN_BLOCK, dtype=F32) * ATTN_BLOCK
    lo = jnp.arange(ATTN_BLOCK, dtype=F32)
    (cos_hi, sin_hi), (cos_lo, sin_lo) = _rope_tables(hi), _rope_tables(lo)
    cos = cos_hi[:, None, :] * cos_lo[None] - sin_hi[:, None, :] * sin_lo[None]
    sin = sin_hi[:, None, :] * cos_lo[None] + cos_hi[:, None, :] * sin_lo[None]
    return cos.reshape(n, -1), sin.reshape(n, -1)


def kernel(x_prompt, x_sample, cache_k_win, cache_v_win, state_conv, state_lru_h, attn_norm_g, w_in, q_norm_g, k_norm_g, attn_sinks, conv_w, conv_b, w_lru_a, b_lru_a, w_lru_i, b_lru_i, lru_lambda, w_br_attn, w_br_rnn, w_out, ffn_norm_g, w_route_group, b_route_group, w_route_expert, b_route_expert, w_exp_gate, w_exp_up, w_exp_down):
    batch, seq, _ = x_prompt.shape
    dec_batch, dec_seq, _ = x_sample.shape
    depth = w_in.shape[0]
    assert depth == 1 and dec_seq == 1
    l = 0

    w_in_f = w_in[l]
    qkg = jnp.concatenate([jnp.tile(q_norm_g[l], N_Q_HEADS), jnp.tile(k_norm_g[l], N_KV_HEADS)])[None, :]
    wcat_f = jnp.concatenate([w_lru_a[l], w_lru_i[l]], axis=-1)
    wa_f, wr_f, wo_f = w_br_attn[l], w_br_rnn[l], w_out[l]
    w_route_f = jnp.concatenate(
        [w_route_expert[l], w_route_group[l],
         jnp.zeros((D_MODEL, LANES - N_EXPERTS - N_GROUPS), F32)], axis=-1)
    wcat = wcat_f.astype(BF16)
    wa_b, wr_b, wo_b, w_route = (w.astype(BF16) for w in (wa_f, wr_f, wo_f, w_route_f))
    b_route = jnp.concatenate(
        [b_route_expert[l], b_route_group[l], jnp.zeros((LANES - N_EXPERTS - N_GROUPS,), F32)])[None, :]
    experts_f = (w_exp_gate[l], w_exp_up[l], w_exp_down[l])
    g1 = attn_norm_g[l][None, :]
    g2 = ffn_norm_g[l][None, :]
    cw, cb = conv_w[l], conv_b[l][None, :]
    b_a, b_i, lam = b_lru_a[l][None, :], b_lru_i[l][None, :], lru_lambda[l][None, :]
    sinks = attn_sinks[l]

    def tail(x, attn, rnn, sga, sgr, tm, seg, precise, experts_b=None, qkv=None):
        wa, wr, wo, wrt = (wa_f, wr_f, wo_f, w_route_f) if precise else (wa_b, wr_b, wo_b, w_route)
        tri = jnp.tril(jnp.ones((tm, tm), BF16), -1)
        new_sequences = None if qkv is None else (sinks, *qkv, seq, experts_f)
        outs = _merge(x, attn, rnn, sga, sgr, wa, wr, wo, g2, wrt, b_route, tri, tm, seg, precise,
                      new_sequences=new_sequences)
        x2, xtm, rec, rect, cnt = outs[:5]
        wg, wu, wd = experts_b or outs[5:]
        off, nfull, rem, slot = _plan(rect, cnt, seg)
        y = _moe(off, nfull, rem, slot, xtm, wg, wu, wd, x2, rec, seg, tm, tm)
        return y, (wg, wu, wd)

    xs = x_sample.reshape(dec_batch, D_MODEL)
    cos_s, sin_s = _rope_tables(jnp.full((dec_batch,), PAST_LEN, F32))
    qs, ks, vs, xrs, gys, sgas, sgrs, w_in_b = _proj(xs, g1, w_in_f, cos_s, sin_s, qkg, 1, dec_batch,
                                                     dec_batch, True)

    xp = x_prompt.reshape(batch * seq, D_MODEL)
    cos_p, sin_p = _rope_tables_range(seq)
    q, k, v, rnn, sga, sgr, h_last, conv_tail = _proj(
        xp, g1, w_in_b, cos_p, sin_p, qkg, batch, seq, PROMPT_TILE, False,
        rnn_weights=(cw, cb, wcat, b_a, b_i, lam))
    y_prompt, experts_b = tail(xp, None, rnn, sga, sgr, PROMPT_TILE, MOE_SEGMENT, False,
                               qkv=(q, k, v))
    y_prompt = y_prompt.reshape(batch, seq, D_MODEL)

    def last_rows(a, rows):
        return a.reshape(batch, seq, a.shape[-1])[:, seq - rows:]

    k_win_p = last_rows(k, WINDOW).reshape(1, batch, WINDOW, N_KV_HEADS, HEAD_DIM)
    v_win_p = last_rows(v, WINDOW).reshape(1, batch, WINDOW, N_KV_HEADS, HEAD_DIM)
    conv_p = conv_tail[None, :, SUBLANES - (CONV_W - 1):, :]
    h_p = h_last[None, :, 0, :]

    ck = jnp.transpose(cache_k_win[l], (0, 2, 3, 1))
    cv = jnp.transpose(cache_v_win[l], (0, 2, 3, 1))
    attn_s, k_win_s, v_win_s = _attn_sample(qs, ks, vs, ck, cv, sinks)
    sc = state_conv[l]
    rnn_s, h_s = _rnn_sample(xrs, gys, sc[:, 0], sc[:, 1], sc[:, 2], state_lru_h[l],
                             cw, cb, wcat_f, b_a, b_i, lam)
    y_sample, _ = tail(xs, attn_s, rnn_s, sgas, sgrs, dec_batch, dec_batch, True, experts_b)
    y_sample = y_sample.reshape(dec_batch, 1, D_MODEL)
    conv_s = jnp.stack([sc[:, 1], sc[:, 2], xrs], axis=1)[None]

    return (y_prompt, y_sample, k_win_p, v_win_p, conv_p, h_p,
            jnp.transpose(k_win_s, (0, 3, 1, 2))[None],
            jnp.transpose(v_win_s, (0, 3, 1, 2))[None],
            conv_s, h_s[None])
```

```python
import functools

import jax
import jax.numpy as jnp
from jax import lax
from jax.experimental import pallas as pl
from jax.experimental.pallas import tpu as pltpu

D_MODEL = 1024
HEAD_DIM = 64
N_Q_HEADS = 8
N_KV_HEADS = 2
Q_PER_KV = N_Q_HEADS // N_KV_HEADS
WINDOW = 128
ATTN_BLOCK = 128
ROPE_THETA = 10000.0
SCALE = HEAD_DIM ** -0.5
NEG_INF = -1e30
D_RNN = 1280
N_RNN_BLOCKS = 10
RNN_BLOCK = D_RNN // N_RNN_BLOCKS
CONV_W = 4
LRU_C = 8.0
N_GROUPS = 4
EXPERTS_PER_GROUP = 8
N_EXPERTS = N_GROUPS * EXPERTS_PER_GROUP
D_EXPERT = 256
PAST_LEN = 16384
EPS = 1e-6
D_Q = N_Q_HEADS * HEAD_DIM
D_KV = N_KV_HEADS * HEAD_DIM
D_IN = D_Q + 2 * D_KV + 2 * D_RNN + 2 * D_MODEL
OFF_K = D_Q
OFF_V = OFF_K + D_KV
OFF_XR = OFF_V + D_KV
OFF_YR = OFF_XR + D_RNN
OFF_GA = OFF_YR + D_RNN
OFF_GR = OFF_GA + D_MODEL

LANES = 128
SUBLANES = 8
VMEM_LIMIT = 56 * 1024 * 1024
PROMPT_TILE = 512
MOE_SEGMENT = 4096

F32 = jnp.float32
BF16 = jnp.bfloat16


def _params(*sem):
    return pltpu.CompilerParams(dimension_semantics=sem, vmem_limit_bytes=VMEM_LIMIT)


def _sigmoid(x):
    return 0.5 * jnp.tanh(0.5 * x) + 0.5


def _gelu_tanh(x):
    c = 0.7978845608028654
    half_x = 0.5 * x
    return half_x + half_x * jnp.tanh(x * (c + (c * 0.044715) * (x * x)))


def _full(shape, single_buffer=False):
    index_map = lambda *_: (0,) * len(shape)
    if single_buffer:
        return pl.BlockSpec(shape, index_map, pipeline_mode=pl.Buffered(1))
    return pl.BlockSpec(shape, index_map)


def _mm(a, b, precise, dims=None):
    if precise:
        a, b, prec = a.astype(F32), b.astype(F32), lax.Precision.HIGHEST
    else:
        a, b, prec = a.astype(BF16), b.astype(BF16), None
    if dims is None:
        return jnp.dot(a, b, preferred_element_type=F32, precision=prec)
    return lax.dot_general(a, b, dims, preferred_element_type=F32, precision=prec)


_NT = (((1,), (1,)), ((), ()))


def _proj_kernel(x_ref, g_ref, w_ref, cos_ref, sin_ref, qkg_ref, *rest, precise, fuse_rnn):
    if fuse_rnn:
        rnn_w = rest[:6]
        q_ref, k_ref, v_ref, rnn_ref, sga_ref, sgr_ref, hl_ref, ct_ref = rest[6:14]
        xbuf, a_scr, b_scr, h_scr, hcar = rest[14:]
        pl.when(pl.program_id(1) == 0)(functools.partial(_rnn_start_sequence, xbuf, hcar))
    else:
        q_ref, k_ref, v_ref, xr_ref, gy_ref, sga_ref, sgr_ref, wb_ref = rest
        wb_ref[...] = w_ref[...].astype(BF16)
    x = x_ref[...]
    inv = lax.rsqrt(jnp.mean(x * x, axis=-1, keepdims=True) + EPS)
    xn = x * inv * g_ref[...]
    if not precise:
        xn = xn.astype(BF16)

    def proj(lo, hi):
        return _mm(xn, w_ref[:, lo:hi], precise)

    def qk_heads():
        qk = proj(0, OFF_V)
        tm = qk.shape[0]
        lane = lax.broadcasted_iota(jnp.int32, (tm, LANES), 1)
        lo_head = lane < HEAD_DIM
        first_half = (lane % HEAD_DIM) < (HEAD_DIM // 2)
        cos_f, sin_f = cos_ref[...], sin_ref[...]
        reps = LANES // HEAD_DIM
        cos = jnp.concatenate([cos_f, cos_f] * reps, axis=-1)
        sin = jnp.concatenate([-sin_f, sin_f] * reps, axis=-1)
        for g in range(OFF_V // LANES):
            seg = qk[:, g * LANES:(g + 1) * LANES]
            sq = seg * seg
            s_lo = jnp.sum(jnp.where(lo_head, sq, 0.0), axis=-1, keepdims=True)
            s_hi = jnp.sum(jnp.where(lo_head, 0.0, sq), axis=-1, keepdims=True)
            ms = jnp.where(lo_head, s_lo, s_hi) * (1.0 / HEAD_DIM)
            normed = seg * lax.rsqrt(ms + EPS) * qkg_ref[:, g * LANES:(g + 1) * LANES]
            partner = jnp.where(first_half,
                                pltpu.roll(normed, LANES - HEAD_DIM // 2, axis=1),
                                pltpu.roll(normed, HEAD_DIM // 2, axis=1))
            roped = normed * cos + partner * sin
            if g < D_Q // LANES:
                q_ref[:, g * LANES:(g + 1) * LANES] = (roped * SCALE).astype(q_ref.dtype)
            else:
                k_ref[...] = roped

    def values():
        v_ref[...] = proj(OFF_V, OFF_XR)

    def gate(out_ref, off, c0, c1, r0=0, r1=x.shape[0]):
        y = _mm(xn[r0:r1], w_ref[:, off + c0:off + c1], precise)
        out_ref[r0:r1, c0:c1] = _sigmoid(y).astype(out_ref.dtype)

    if not fuse_rnn:
        xr_ref[...] = proj(OFF_XR, OFF_YR)
        gy_ref[...] = _gelu_tanh(proj(OFF_YR, OFF_GA)).astype(gy_ref.dtype)
        qk_heads()
        values()
        gate(sga_ref, OFF_GA, 0, D_MODEL)
        gate(sgr_ref, OFF_GR, 0, D_MODEL)
        return

    slab = RNN_SLAB_BLOCKS * RNN_BLOCK
    half = x.shape[0] // 2
    others = [qk_heads, values]
    for out_ref, off in ((sga_ref, OFF_GA), (sgr_ref, OFF_GR)):
        others += [functools.partial(gate, out_ref, off, c, c + GATE_COLS, r, r + half)
                   for c in range(0, D_MODEL, GATE_COLS) for r in (0, half)]
    others = iter(others)
    n_slabs = D_RNN // slab

    def slab_inputs(i):
        c0 = i * slab
        return (proj(OFF_XR + c0, OFF_XR + c0 + slab),
                _gelu_tanh(proj(OFF_YR + c0, OFF_YR + c0 + slab)))

    gy = []
    nxt = slab_inputs(0)
    for i in range(n_slabs):
        (xr, gy_i), nxt = nxt, None
        gy.append(gy_i)
        ct_ref[0, :, i * slab:(i + 1) * slab] = xr[xr.shape[0] - SUBLANES:, :]
        for _ in _rnn_scan_terms(xr, i * RNN_SLAB_BLOCKS, *rnn_w, xbuf, a_scr, b_scr):
            if nxt is None and i + 1 < n_slabs:
                nxt = slab_inputs(i + 1)
            next(others, lambda: None)()
    for other in others:
        other()
    _rnn_scan_finish(jnp.concatenate(gy, axis=-1), rnn_ref, hl_ref, a_scr, b_scr, h_scr, hcar)


RNN_SLAB_BLOCKS = 2
GATE_COLS = 256


def _proj(x, g, w_in, cos_t, sin_t, qkg, batch, seq, tm, precise, rnn_weights=None):
    n = batch * seq
    nt = seq // tm
    row = lambda b, t: (b * nt + t, 0)
    per_seq = lambda b, t: (b, 0, 0)
    act = F32 if precise else BF16
    rows_out = lambda width, dtype: (jax.ShapeDtypeStruct((n, width), dtype),
                                     pl.BlockSpec((tm, width), row))
    state_out = (jax.ShapeDtypeStruct((batch, SUBLANES, D_RNN), F32),
                 pl.BlockSpec((1, SUBLANES, D_RNN), per_seq))
    outs = [rows_out(D_Q, act), rows_out(D_KV, F32), rows_out(D_KV, F32)]
    in_specs = [
        pl.BlockSpec((tm, D_MODEL), row),
        _full((1, D_MODEL)),
        _full((D_MODEL, D_IN), single_buffer=True),
        pl.BlockSpec((tm, HEAD_DIM // 2), lambda b, t: (t, 0)),
        pl.BlockSpec((tm, HEAD_DIM // 2), lambda b, t: (t, 0)),
        _full((1, OFF_V)),
    ]
    args = [x, g, w_in, cos_t, sin_t, qkg]
    scratch = []
    if rnn_weights is None:
        outs += [rows_out(D_RNN, F32), rows_out(D_RNN, act)]
    else:
        assert not precise
        outs += [rows_out(D_RNN, act)]
        in_specs += [_full(w.shape) for w in rnn_weights]
        args += list(rnn_weights)
        groups = tm // SUBLANES
        scratch = [pltpu.VMEM((tm + SUBLANES, D_RNN), F32)]
        scratch += [pltpu.VMEM((groups, SUBLANES, D_RNN), F32)] * 3
        scratch += [pltpu.VMEM((SUBLANES, D_RNN), F32)]
    outs += [rows_out(D_MODEL, act), rows_out(D_MODEL, act)]
    if rnn_weights is not None:
        outs += [state_out, state_out]
    else:
        assert batch * nt == 1 and w_in.dtype == F32
        outs += [(jax.ShapeDtypeStruct(w_in.shape, BF16), _full(w_in.shape, single_buffer=True))]
    return pl.pallas_call(
        functools.partial(_proj_kernel, precise=precise, fuse_rnn=rnn_weights is not None),
        grid=(batch, nt),
        in_specs=in_specs,
        out_specs=[spec for _, spec in outs],
        out_shape=[shape for shape, _ in outs],
        scratch_shapes=scratch,
        compiler_params=_params("parallel", "arbitrary"),
        name="proj",
    )(*args)


def _sink_column(sink_ref, h, rows, rows_per_head):
    r = lax.broadcasted_iota(jnp.int32, (rows, 1), 0) // rows_per_head
    col = jnp.full((rows, 1), sink_ref[h * Q_PER_KV], F32)
    for g in range(1, Q_PER_KV):
        col = jnp.where(r == g, sink_ref[h * Q_PER_KV + g], col)
    return col


def _window_masks(first_tile):
    blk = ATTN_BLOCK
    rows = Q_PER_KV * blk
    i = lax.broadcasted_iota(jnp.int32, (rows, 2 * blk), 0) % blk
    j = lax.broadcasted_iota(jnp.int32, (rows, 2 * blk), 1)
    d = j - i
    in_window = (d >= 1) & (d <= WINDOW)
    first = (d >= jnp.where(first_tile, jnp.maximum(1, blk - i), 1)) & (d <= WINDOW)
    return in_window, first


def _attn_kv_head(sub, h, valid, sink_ref, q_ref, k_all, v_all, o_ref):
    blk = ATTN_BLOCK
    rows = Q_PER_KV * blk
    q = q_ref[sub * blk:(sub + 1) * blk, :]
    kh = k_all[sub * blk:(sub + 2) * blk, h * HEAD_DIM:(h + 1) * HEAD_DIM]
    vh = v_all[sub * blk:(sub + 2) * blk, h * HEAD_DIM:(h + 1) * HEAD_DIM]
    qs = jnp.concatenate(
        [q[:, (h * Q_PER_KV + g) * HEAD_DIM:(h * Q_PER_KV + g + 1) * HEAD_DIM]
         for g in range(Q_PER_KV)], axis=0)
    s = _mm(qs, kh, False, _NT)
    yield
    s = jnp.where(valid, s, NEG_INF)
    sink = _sink_column(sink_ref, h, rows, blk)
    m = jnp.maximum(jnp.max(s, axis=-1, keepdims=True), sink)
    p = jnp.exp(s - m)
    denom = jnp.sum(p, axis=-1, keepdims=True) + jnp.exp(sink - m)
    yield
    o = _mm(p, vh, False) * (1.0 / denom)
    for g in range(Q_PER_KV):
        c = (h * Q_PER_KV + g) * HEAD_DIM
        o_ref[sub * blk:(sub + 1) * blk, c:c + HEAD_DIM] = o[g * blk:(g + 1) * blk].astype(o_ref.dtype)


def _lockstep(generators):
    while generators:
        generators = [g for g in generators if next(g, "done") != "done"]
        yield


SAMPLE_BT = 8


def _attn_sample_kernel(sink_ref, q_ref, kn_ref, vn_ref, kc_ref, vc_ref, o_ref, ko_ref, vo_ref):
    bt = SAMPLE_BT
    pos = lax.broadcasted_iota(jnp.int32, (HEAD_DIM, WINDOW), 1)
    kn_t = kn_ref[...].T
    vn_t = vn_ref[...].T

    def shifted(cache_ref, new_t, out_ref, h):
        windows = []
        for b in range(bt):
            rolled = pltpu.roll(cache_ref[b, h], WINDOW - 1, axis=1)
            new = new_t[h * HEAD_DIM:(h + 1) * HEAD_DIM, b:b + 1]
            windows.append(jnp.where(pos == WINDOW - 1, new, rolled))
            out_ref[b, h] = windows[-1]
        return jnp.concatenate(windows, axis=1)

    q = q_ref[...]
    rows = Q_PER_KV * bt
    rb = lax.broadcasted_iota(jnp.int32, (rows, bt * WINDOW), 0) % bt
    cb = lax.broadcasted_iota(jnp.int32, (rows, bt * WINDOW), 1) // WINDOW
    valid = rb == cb

    def kv_head(h):
        kh_t = shifted(kc_ref, kn_t, ko_ref, h)
        vh_t = shifted(vc_ref, vn_t, vo_ref, h)
        qs = jnp.concatenate(
            [q[:, (h * Q_PER_KV + g) * HEAD_DIM:(h * Q_PER_KV + g + 1) * HEAD_DIM]
             for g in range(Q_PER_KV)], axis=0)
        s = _mm(qs, kh_t, True)
        yield
        s = jnp.where(valid, s, NEG_INF)
        sink = _sink_column(sink_ref, h, rows, bt)
        m = jnp.maximum(jnp.max(s, axis=-1, keepdims=True), sink)
        p = jnp.exp(s - m)
        denom = jnp.sum(p, axis=-1, keepdims=True) + jnp.exp(sink - m)
        o = _mm(p, vh_t, True, _NT) * (1.0 / denom)
        for g in range(Q_PER_KV):
            c = (h * Q_PER_KV + g) * HEAD_DIM
            o_ref[:, c:c + HEAD_DIM] = o[g * bt:(g + 1) * bt].astype(o_ref.dtype)

    heads = [kv_head(h) for h in range(N_KV_HEADS)]
    while heads:
        heads = [head for head in heads if next(head, "done") != "done"]


def _attn_sample(q, k_new, v_new, cache_k, cache_v, sinks):
    nbatch = q.shape[0]
    bt = SAMPLE_BT
    row = lambda i: (i, 0)
    win = pl.BlockSpec((bt, N_KV_HEADS, HEAD_DIM, WINDOW), lambda i: (i, 0, 0, 0))
    return pl.pallas_call(
        _attn_sample_kernel,
        grid=(nbatch // bt,),
        in_specs=[
            pl.BlockSpec(memory_space=pltpu.SMEM),
            pl.BlockSpec((bt, D_Q), row),
            pl.BlockSpec((bt, D_KV), row),
            pl.BlockSpec((bt, D_KV), row),
            win,
            win,
        ],
        out_specs=[pl.BlockSpec((bt, D_Q), row), win, win],
        out_shape=(
            jax.ShapeDtypeStruct((nbatch, D_Q), F32),
            jax.ShapeDtypeStruct(cache_k.shape, F32),
            jax.ShapeDtypeStruct(cache_v.shape, F32),
        ),
        compiler_params=_params("parallel"),
        name="attn_sample",
    )(sinks, q, k_new, v_new, cache_k, cache_v)


def _lru_terms(xc, wcat_ref, ba_ref, bi_ref, lam_ref, precise, block0=0):
    cols = slice(block0 * RNN_BLOCK, block0 * RNN_BLOCK + xc.shape[1])
    xcb = xc if precise else xc.astype(BF16)
    ya, yi = [], []
    for n in range(xc.shape[1] // RNN_BLOCK):
        y = _mm(xcb[:, n * RNN_BLOCK:(n + 1) * RNN_BLOCK], wcat_ref[block0 + n], precise)
        ya.append(y[:, :RNN_BLOCK])
        yi.append(y[:, RNN_BLOCK:])
    r = _sigmoid(jnp.concatenate(ya, axis=-1) + ba_ref[:, cols])
    gate_i = _sigmoid(jnp.concatenate(yi, axis=-1) + bi_ref[:, cols])
    neg_lam = -lam_ref[:, cols]
    softplus = jnp.maximum(neg_lam, 0.0) + jnp.log1p(jnp.exp(-jnp.abs(neg_lam)))
    log_a = (-LRU_C * softplus) * r
    a = jnp.exp(log_a)
    m = 1.0 - a * a
    b = jnp.where(m > 0.0, m * lax.rsqrt(m), 0.0) * (gate_i * xc)
    return a, b


def _rnn_start_sequence(xbuf, hcar):
    xbuf[0:SUBLANES, :] = jnp.zeros((SUBLANES, D_RNN), F32)
    hcar[...] = jnp.zeros((SUBLANES, D_RNN), F32)


def _rnn_scan_terms(x, block0, cw_ref, cb_ref, wcat_ref, ba_ref, bi_ref, lam_ref,
                    xbuf, a_scr, b_scr):
    tt, width = x.shape
    groups = tt // SUBLANES
    cols = slice(block0 * RNN_BLOCK, block0 * RNN_BLOCK + width)
    xbuf[SUBLANES:, cols] = x
    xc = cb_ref[:, cols] + cw_ref[CONV_W - 1:CONV_W, cols] * x
    for j in range(CONV_W - 1):
        s = CONV_W - 1 - j
        xc = xc + cw_ref[j:j + 1, cols] * xbuf[SUBLANES - s:SUBLANES - s + tt, cols]
    xbuf[0:SUBLANES, cols] = x[tt - SUBLANES:, :]
    yield
    a, b = _lru_terms(xc, wcat_ref, ba_ref, bi_ref, lam_ref, False, block0)
    yield
    a = a.reshape(groups, SUBLANES, width)
    b = b.reshape(groups, SUBLANES, width)
    step = lax.broadcasted_iota(jnp.int32, (groups, SUBLANES, width), 1)
    k = 1
    while k < SUBLANES:
        keep = step >= k
        a_sh = jnp.where(keep, pltpu.roll(a, k, axis=1), 1.0)
        b_sh = jnp.where(keep, pltpu.roll(b, k, axis=1), 0.0)
        b = a * b_sh + b
        a = a * a_sh
        k *= 2
        if k < SUBLANES:
            yield
    a_scr[:, :, cols] = a
    b_scr[:, :, cols] = b


def _rnn_scan_finish(gy, o_ref, hl_ref, a_scr, b_scr, h_scr, hcar):
    groups = a_scr.shape[0]
    tt = groups * SUBLANES

    def chain(g, h_in):
        h = a_scr[g] * h_in + b_scr[g]
        h_scr[g] = h
        return jnp.broadcast_to(h[SUBLANES - 1:SUBLANES, :], (SUBLANES, D_RNN))

    h_last = lax.fori_loop(0, groups, chain, hcar[...], unroll=True)
    hcar[...] = h_last
    hl_ref[0] = h_last
    h = h_scr[...].reshape(tt, D_RNN)
    o_ref[...] = (h * gy).astype(o_ref.dtype)


def _rnn_sample_kernel(xr_ref, gy_ref, s0_ref, s1_ref, s2_ref, h_ref, cw_ref, cb_ref,
                       wcat_ref, ba_ref, bi_ref, lam_ref, o_ref, hn_ref):
    x = xr_ref[...]
    xc = (cb_ref[...] + cw_ref[0:1, :] * s0_ref[...] + cw_ref[1:2, :] * s1_ref[...]
          + cw_ref[2:3, :] * s2_ref[...] + cw_ref[3:4, :] * x)
    a, b = _lru_terms(xc, wcat_ref, ba_ref, bi_ref, lam_ref, True)
    h = a * h_ref[...] + b
    hn_ref[...] = h
    o_ref[...] = (h * gy_ref[...].astype(F32)).astype(o_ref.dtype)


def _rnn_sample(xr, gy, s0, s1, s2, h_prev, conv_w, conv_b, wcat, b_a, b_i, lam):
    n = xr.shape[0]
    act = _full((n, D_RNN))
    return pl.pallas_call(
        _rnn_sample_kernel,
        grid=(1,),
        in_specs=[act, act, act, act, act, act,
                  _full((CONV_W, D_RNN)), _full((1, D_RNN)),
                  _full((N_RNN_BLOCKS, RNN_BLOCK, 2 * RNN_BLOCK)),
                  _full((1, D_RNN)), _full((1, D_RNN)), _full((1, D_RNN))],
        out_specs=[act, act],
        out_shape=(jax.ShapeDtypeStruct((n, D_RNN), F32),
                   jax.ShapeDtypeStruct((n, D_RNN), F32)),
        compiler_params=_params("arbitrary"),
        name="rnn_sample",
    )(xr, gy, s0, s1, s2, h_prev, conv_w, conv_b, wcat, b_a, b_i, lam)


def _merge_kernel(*refs, precise, tiles_per_seg, tiles_per_seq):
    fused = tiles_per_seq is not None
    if fused:
        (sink_ref, q_ref, kc_ref, kp_ref, vc_ref, vp_ref, x_ref, rn_ref, sga_ref, sgr_ref,
         wa_ref, wr_ref, wo_ref, g_ref, wrt_ref, brt_ref, tri_ref, wg_ref, wu_ref, wd_ref,
         x2_ref, xtm_ref, rec_ref, rect_ref, cnt_ref, wg_o, wu_o, wd_o, cnt_scr, at_ref) = refs
    else:
        (x_ref, at_ref, rn_ref, sga_ref, sgr_ref, wa_ref, wr_ref, wo_ref, g_ref, wrt_ref, brt_ref,
         tri_ref, x2_ref, xtm_ref, rec_ref, rect_ref, cnt_ref, cnt_scr) = refs

    @pl.when(pl.program_id(0) % tiles_per_seg == 0)
    def _():
        cnt_scr[...] = jnp.zeros_like(cnt_scr)

    tm = x_ref.shape[0]
    parts = MERGE_PARTS if tm % (MERGE_PARTS * LANES) == 0 else 1
    part = tm // parts
    state = {"counts": cnt_scr[...]}
    pending = [
        _merge_rows(slice(i * part, (i + 1) * part), state, x_ref, at_ref, rn_ref, sga_ref, sgr_ref,
                    wa_ref, wr_ref, wo_ref, g_ref, wrt_ref, brt_ref, tri_ref, x2_ref, xtm_ref,
                    rec_ref, rect_ref, precise)
        for i in range(parts)]
    active = []
    if fused:
        k_all = jnp.concatenate([kp_ref[...], kc_ref[...]], axis=0).astype(BF16)
        v_all = jnp.concatenate([vp_ref[...], vc_ref[...]], axis=0).astype(BF16)
        in_window, first = _window_masks(pl.program_id(0) % tiles_per_seq == 0)
        blocks = part // ATTN_BLOCK
        side = list(pending)
        for i in range(parts):
            heads = [_attn_kv_head(sub, h, first if sub == 0 else in_window, sink_ref, q_ref,
                                   k_all, v_all, at_ref)
                     for sub in range(i * blocks, (i + 1) * blocks) for h in range(N_KV_HEADS)]
            for _ in _lockstep(heads):
                if side:
                    next(side.pop(0))
                for part_pieces in list(active) * MERGE_PIECES_PER_ATTN_ROUND:
                    if part_pieces in active and next(part_pieces, "done") == "done":
                        active.remove(part_pieces)
            active.append(pending.pop(0))
    step = 0
    while pending or active:
        if pending and step % MERGE_STAGGER == 0:
            active.append(pending.pop(0))
        for part_pieces in list(active):
            if next(part_pieces, "done") == "done":
                active.remove(part_pieces)
        step += 1
    cnt_scr[...] = state["counts"]
    cnt_ref[0] = state["counts"]
    if fused:
        wg_o[...] = wg_ref[...].astype(BF16)
        wu_o[...] = wu_ref[...].astype(BF16)
        wd_o[...] = wd_ref[...].astype(BF16)


MERGE_PARTS = 2
MERGE_STAGGER = 3
MERGE_PIECES_PER_ATTN_ROUND = 1


def _merge_rows(rows, state, x_ref, at_ref, rn_ref, sga_ref, sgr_ref, wa_ref, wr_ref, wo_ref, g_ref,
                wrt_ref, brt_ref, tri_ref, x2_ref, xtm_ref, rec_ref, rect_ref, precise):
    yr = _mm(rn_ref[rows, :], wr_ref[...], precise)
    yield
    ya = _mm(at_ref[rows, :], wa_ref[...], precise)
    yield
    merged =sga_ref[rows, :].astype(F32) * ya + sgr_ref[rows, :].astype(F32) * yr
    yield
    x2 = x_ref[rows, :] + _mm(merged, wo_ref[...], precise)
    x2_ref[rows, :] = x2
    yield
    inv = lax.rsqrt(jnp.mean(x2 * x2, axis=-1, keepdims=True) + EPS)
    xn = x2 * inv * g_ref[...]
    for c in range(TOKEN_ROWS):
        xtm_ref[pl.ds(rows.start * TOKEN_ROWS + c, rows.stop - rows.start, stride=TOKEN_ROWS), :] = (
            xn[:, c * LANES:(c + 1) * LANES])
    yield

    logits = _mm(xn, wrt_ref[...], precise) + brt_ref[...]
    yield
    tm = logits.shape[0]
    lane_i = lax.broadcasted_iota(jnp.int32, (tm, LANES), 1)
    lane = lane_i.astype(F32)
    big = float(LANES)
    is_grp = (lane_i >= N_EXPERTS) & (lane_i < N_EXPERTS + N_GROUPS)
    gl = jnp.where(is_grp, logits, NEG_INF)
    gmax = jnp.max(gl, axis=-1, keepdims=True)
    g_idx = jnp.min(jnp.where(gl == gmax, lane, big), axis=-1, keepdims=True) - N_EXPERTS
    p_g = 1.0 / jnp.sum(jnp.exp(gl - gmax), axis=-1, keepdims=True)
    in_grp = (lane_i // EXPERTS_PER_GROUP).astype(F32) == g_idx
    el = jnp.where(in_grp, logits, NEG_INF)
    v1 = jnp.max(el, axis=-1, keepdims=True)
    i1 = jnp.min(jnp.where(el == v1, lane, big), axis=-1, keepdims=True)
    el2 = jnp.where(lane == i1, NEG_INF, el)
    v2 = jnp.max(el2, axis=-1, keepdims=True)
    i2 = jnp.min(jnp.where(el2 == v2, lane, big), axis=-1, keepdims=True)
    e2 = jnp.exp(v2 - v1)
    w1 = p_g / (1.0 + e2)
    w2 = p_g * e2 / (1.0 + e2)
    yield

    hit = jnp.where(lane == i1, 1.0, jnp.where(lane == i2, 1.0, 0.0))
    counts = state["counts"]
    state["counts"] = counts + jnp.sum(hit, axis=0, keepdims=True)
    before = (jnp.dot(tri_ref[0:tm, 0:tm], hit.astype(BF16), preferred_element_type=F32)
              + counts[0:1, :])
    yield
    r1 = jnp.sum(jnp.where(lane == i1, before, 0.0), axis=-1, keepdims=True)
    r2 = jnp.sum(jnp.where(lane == i2, before, 0.0), axis=-1, keepdims=True)
    rec = jnp.where(lane == REC_W2, w2, 0.0)
    for field, val in ((REC_W1, w1), (REC_R2, r2), (REC_R1, r1), (REC_E2, i2), (REC_E1, i1)):
        rec = jnp.where(lane == field, val, rec)
    rec_ref[rows, :] = rec
    rect_ref[:, rows] = rec.T


REC_E1, REC_E2, REC_R1, REC_R2, REC_W1, REC_W2 = range(6)
TOKEN_ROWS = D_MODEL // LANES


def _merge(x, attn, rnn, sga, sgr, wa, wr, wo, g2, w_route, b_route, tri, tm, seg, precise,
           new_sequences=None):
    n = x.shape[0]
    steps = n // tm
    tiles_per_seg = seg // tm
    row = lambda i: (i, 0)
    scratch = [pltpu.VMEM((SUBLANES, LANES), F32)]
    if new_sequences is None:
        head_specs = [pl.BlockSpec((tm, D_MODEL), row), pl.BlockSpec((tm, D_Q), row)]
        head_args = [x, attn]
        tiles_per_seq = None
    else:
        sinks, q, k, v, seq_len, expert_weights = new_sequences
        assert steps == N_EXPERTS and attn is None
        tiles_per_seq = seq_len // tm
        prev = lambda i: (jnp.maximum(i * (tm // ATTN_BLOCK) - 1, 0), 0)
        head_specs = [
            pl.BlockSpec(memory_space=pltpu.SMEM),
            pl.BlockSpec((tm, D_Q), row),
            pl.BlockSpec((tm, D_KV), row),
            pl.BlockSpec((ATTN_BLOCK, D_KV), prev),
            pl.BlockSpec((tm, D_KV), row),
            pl.BlockSpec((ATTN_BLOCK, D_KV), prev),
            pl.BlockSpec((tm, D_MODEL), row),
        ]
        head_args = [sinks, q, k, k, v, v, x]
        scratch.append(pltpu.VMEM((tm, D_Q), BF16))
    in_specs = head_specs + [
        pl.BlockSpec((tm, D_RNN), row),
        pl.BlockSpec((tm, D_MODEL), row),
        pl.BlockSpec((tm, D_MODEL), row),
        _full((D_Q, D_MODEL)),
        _full((D_RNN, D_MODEL)),
        _full((D_MODEL, D_MODEL)),
        _full((1, D_MODEL)),
        _full((D_MODEL, LANES)),
        _full((1, LANES)),
        _full((tm, tm)),
    ]
    out_specs = [
        pl.BlockSpec((tm, D_MODEL), row),
        pl.BlockSpec((tm * TOKEN_ROWS, LANES), row),
        pl.BlockSpec((tm, LANES), row),
        pl.BlockSpec((LANES, tm), lambda i: (0, i)),
        pl.BlockSpec((1, SUBLANES, LANES), lambda i: (i // tiles_per_seg, 0, 0)),
    ]
    out_shape = [
        jax.ShapeDtypeStruct((n, D_MODEL), F32),
        jax.ShapeDtypeStruct((n * TOKEN_ROWS, LANES), F32),
        jax.ShapeDtypeStruct((n, LANES), F32),
        jax.ShapeDtypeStruct((LANES, n), F32),
        jax.ShapeDtypeStruct((n // seg, SUBLANES, LANES), F32),
    ]
    args = head_args + [rnn, sga, sgr, wa, wr, wo, g2, w_route, b_route, tri]
    if new_sequences is not None:
        for w in expert_weights:
            spec = pl.BlockSpec((1,) + w.shape[1:], lambda i: (i, 0, 0))
            in_specs.append(spec)
            out_specs.append(spec)
            out_shape.append(jax.ShapeDtypeStruct(w.shape, BF16))
            args.append(w)
    return pl.pallas_call(
        functools.partial(_merge_kernel, precise=precise, tiles_per_seg=tiles_per_seg,
                          tiles_per_seq=tiles_per_seq),
        grid=(steps,),
        in_specs=in_specs,
        out_specs=out_specs,
        out_shape=out_shape,
        scratch_shapes=scratch,
        compiler_params=_params("arbitrary"),
        name="merge",
    )(*args)


MOE_CHUNK = 256
MOE_TAIL = 128
MOE_EXPERTS_PER_STEP = 2
MOE_LOOP_TOKENS = 32
MOE_VMEM_LIMIT = 60 * 1024 * 1024


def _seg_rows(seg):
    return 2 * seg + N_EXPERTS * SUBLANES + MOE_CHUNK


def _token_rows(i):
    return pl.ds(pl.multiple_of(i * TOKEN_ROWS, TOKEN_ROWS), TOKEN_ROWS)


def _sorted_rows(first_row):
    return pl.ds(pl.multiple_of(first_row, TOKEN_ROWS), TOKEN_ROWS)


def _moe_kernel(off_ref, nfull_ref, rem_ref, slot_ref, xtm_ref, wg_ref, wu_ref, wd_ref,
                x2_ref, rec_ref, o_ref, buf, g1, g2, *, seg, td, tc):
    s = pl.program_id(0)
    p = pl.program_id(1)
    n_disp = seg // td
    n_exp = N_EXPERTS // MOE_EXPERTS_PER_STEP

    @pl.when((s == 0) & (p == 0))
    def _():
        buf[...] = jnp.zeros_like(buf)

    @pl.when(p < n_disp)
    def _():
        def dispatch(g, carry):
            for j in range(MOE_LOOP_TOKENS):
                t = g * MOE_LOOP_TOKENS + j
                row = xtm_ref[_token_rows(t), :]
                for k in range(2):
                    buf[_sorted_rows(slot_ref[0, 0, k * seg + p * td + t]), :] = row
            return carry

        lax.fori_loop(0, td // MOE_LOOP_TOKENS, dispatch, 0)

    def run_chunk(e, row0, rows, valid):
        r0 = pl.multiple_of(row0 * TOKEN_ROWS, SUBLANES * TOKEN_ROWS)
        xf = [buf[pl.ds(r0 + j, rows, stride=TOKEN_ROWS), :] for j in range(TOKEN_ROWS)]
        x = jnp.concatenate(xf, axis=-1).astype(BF16)
        hg = jnp.dot(x, wg_ref[e], preferred_element_type=F32)
        hu = jnp.dot(x, wu_ref[e], preferred_element_type=F32)
        h = (hg * _sigmoid(hg)) * hu
        y = jnp.dot(h.astype(BF16), wd_ref[e], preferred_element_type=F32)
        if valid is not None:
            mine = lax.broadcasted_iota(jnp.int32, (rows, LANES), 0) < valid
        for j in range(TOKEN_ROWS):
            yj = y[:, j * LANES:(j + 1) * LANES]
            if valid is not None:
                yj = jnp.where(mine, yj, xf[j])
            buf[pl.ds(r0 + j, rows, stride=TOKEN_ROWS), :] = yj

    def run_expert(e):
        idx = s * N_EXPERTS + (p - n_disp) * MOE_EXPERTS_PER_STEP + e
        base = off_ref[idx]
        n_full = nfull_ref[idx]
        rem = rem_ref[idx]

        def chunk(c, carry):
            run_chunk(e, base + c * MOE_CHUNK, MOE_CHUNK, None)
            return carry

        lax.fori_loop(0, n_full, chunk, 0)
        last = base + n_full * MOE_CHUNK
        for units in range(1, MOE_CHUNK // MOE_TAIL + 2):
            @pl.when((rem > (units - 1) * MOE_TAIL) & (rem <= units * MOE_TAIL))
            def _():
                run_chunk(e, last, units * MOE_TAIL, rem)

    @pl.when((p >= n_disp) & (p < n_disp + n_exp))
    def _():
        for e in range(MOE_EXPERTS_PER_STEP):
            run_expert(e)

    @pl.when(p >= n_disp + n_exp)
    def _():
        t0 = (p - n_disp - n_exp) * tc

        def gather(g, carry):
            for j in range(MOE_LOOP_TOKENS):
                t = g * MOE_LOOP_TOKENS + j
                g1[_token_rows(t), :] = buf[_sorted_rows(slot_ref[0, 0, t0 + t]), :]
                g2[_token_rows(t), :] = buf[_sorted_rows(slot_ref[0, 0, seg + t0 + t]), :]
            return carry

        lax.fori_loop(0, tc // MOE_LOOP_TOKENS, gather, 0)
        rec = rec_ref[...]
        lane = lax.broadcasted_iota(jnp.int32, rec.shape, 1)
        w1 = jnp.broadcast_to(
            jnp.sum(jnp.where(lane == REC_W1, rec, 0.0), axis=-1, keepdims=True), rec.shape)
        w2 = jnp.broadcast_to(
            jnp.sum(jnp.where(lane == REC_W2, rec, 0.0), axis=-1, keepdims=True), rec.shape)
        for j in range(TOKEN_ROWS):
            cols = slice(j * LANES, (j + 1) * LANES)
            o_ref[:, cols] = (x2_ref[:, cols] + w1 * g1[pl.ds(j, tc, stride=TOKEN_ROWS), :]
                              + w2 * g2[pl.ds(j, tc, stride=TOKEN_ROWS), :])


def _moe(off, nfull, rem, slot, xtm, wg, wu, wd, x2, rec, seg, td, tc):
    n = x2.shape[0]
    n_seg = n // seg
    n_disp, n_comb = seg // td, seg // tc
    per_step = MOE_EXPERTS_PER_STEP
    n_exp = N_EXPERTS // per_step
    rows = _seg_rows(seg) * TOKEN_ROWS
    disp_tile = lambda s, p, *_: (s * n_disp + jnp.minimum(p, n_disp - 1), 0)
    expert = lambda s, p, *_: (jnp.clip(p - n_disp, 0, n_exp - 1), 0, 0)
    comb_tile = lambda s, p, *_: (s * n_comb + jnp.clip(p - n_disp - n_exp, 0, n_comb - 1), 0)
    grid_spec = pltpu.PrefetchScalarGridSpec(
        num_scalar_prefetch=3,
        grid=(n_seg, n_disp + n_exp + n_comb),
        in_specs=[
            pl.BlockSpec((1, 1, 2 * seg), lambda s, p, *_: (s, 0, 0), memory_space=pltpu.SMEM),
            pl.BlockSpec((td * TOKEN_ROWS, LANES), disp_tile),
            pl.BlockSpec((per_step, D_MODEL, D_EXPERT), expert),
            pl.BlockSpec((per_step, D_MODEL, D_EXPERT), expert),
            pl.BlockSpec((per_step, D_EXPERT, D_MODEL), expert),
            pl.BlockSpec((tc, D_MODEL), comb_tile),
            pl.BlockSpec((tc, LANES), comb_tile),
        ],
        out_specs=pl.BlockSpec((tc, D_MODEL), comb_tile),
        scratch_shapes=[pltpu.VMEM((rows, LANES), F32),
                        pltpu.VMEM((tc * TOKEN_ROWS, LANES), F32),
                        pltpu.VMEM((tc * TOKEN_ROWS, LANES), F32)],
    )
    return pl.pallas_call(
        functools.partial(_moe_kernel, seg=seg, td=td, tc=tc),
        grid_spec=grid_spec,
        out_shape=jax.ShapeDtypeStruct((n, D_MODEL), F32),
        compiler_params=pltpu.CompilerParams(
            dimension_semantics=("arbitrary", "arbitrary"), vmem_limit_bytes=MOE_VMEM_LIMIT),
        name="moe",
    )(off, nfull, rem, slot, xtm, wg, wu, wd, x2, rec)


def _plan(rect, cnt, seg):
    n = rect.shape[1]
    expert = rect[REC_E1:REC_E2 + 1].astype(jnp.int32)
    rank = rect[REC_R1:REC_R2 + 1].astype(jnp.int32)
    counts = cnt[:, 0, :N_EXPERTS].astype(jnp.int32)
    padded = (counts + SUBLANES - 1) // SUBLANES * SUBLANES
    off = jnp.cumsum(padded, axis=1) - padded
    n_full = counts // MOE_CHUNK
    rem = counts - n_full * MOE_CHUNK
    join = (rem > 0) & (rem <= MOE_TAIL) & (n_full > 0)
    n_full = n_full - join
    rem = rem + join * MOE_CHUNK
    off_tok = jnp.repeat(off.T, seg, axis=1)
    hit = expert[:, None, :] == jnp.arange(N_EXPERTS, dtype=jnp.int32)[None, :, None]
    slot = (rank + jnp.sum(jnp.where(hit, off_tok[None], 0), axis=1)) * TOKEN_ROWS
    slot = slot.reshape(2, n // seg, seg).transpose(1, 0, 2).reshape(n // seg, 1, 2 * seg)
    return off.reshape(-1), n_full.reshape(-1), rem.reshape(-1), slot


def _rope_tables(pos):
    half = HEAD_DIM // 2
    inv_freq = ROPE_THETA ** (-jnp.arange(half, dtype=F32) / half)
    ang = pos[:, None] * inv_freq[None, :]
    return jnp.cos(ang), jnp.sin(ang)


def _rope_tables_range(n):
    hi = jnp.arange(n // ATTN_BLOCK, dtype=F32) * ATTN_BLOCK
    lo = jnp.arange(ATTN_BLOCK, dtype=F32)
    (cos_hi, sin_hi), (cos_lo, sin_lo) = _rope_tables(hi), _rope_tables(lo)
    cos = cos_hi[:, None, :] * cos_lo[None] - sin_hi[:, None, :] * sin_lo[None]
    sin = sin_hi[:, None, :] * cos_lo[None] + cos_hi[:, None, :] * sin_lo[None]
    return cos.reshape(n, -1), sin.reshape(n, -1)


def kernel(x_prompt, x_sample, cache_k_win, cache_v_win, state_conv, state_lru_h, attn_norm_g, w_in, q_norm_g, k_norm_g, attn_sinks, conv_w, conv_b, w_lru_a, b_lru_a, w_lru_i, b_lru_i, lru_lambda, w_br_attn, w_br_rnn, w_out, ffn_norm_g, w_route_group, b_route_group, w_route_expert, b_route_expert, w_exp_gate, w_exp_up, w_exp_down):
    batch, seq, _ = x_prompt.shape
    dec_batch, dec_seq, _ = x_sample.shape
    depth = w_in.shape[0]
    assert depth == 1 and dec_seq == 1
    l = 0

    w_in_f = w_in[l]
    qkg = jnp.concatenate([jnp.tile(q_norm_g[l], N_Q_HEADS), jnp.tile(k_norm_g[l], N_KV_HEADS)])[None, :]
    wcat_f = jnp.concatenate([w_lru_a[l], w_lru_i[l]], axis=-1)
    wa_f, wr_f, wo_f = w_br_attn[l], w_br_rnn[l], w_out[l]
    w_route_f = jnp.concatenate(
        [w_route_expert[l], w_route_group[l],
         jnp.zeros((D_MODEL, LANES - N_EXPERTS - N_GROUPS), F32)], axis=-1)
    wcat = wcat_f.astype(BF16)
    wa_b, wr_b, wo_b, w_route = (w.astype(BF16) for w in (wa_f, wr_f, wo_f, w_route_f))
    b_route = jnp.concatenate(
        [b_route_expert[l], b_route_group[l], jnp.zeros((LANES - N_EXPERTS - N_GROUPS,), F32)])[None, :]
    experts_f = (w_exp_gate[l], w_exp_up[l], w_exp_down[l])
    g1 = attn_norm_g[l][None, :]
    g2 = ffn_norm_g[l][None, :]
    cw, cb = conv_w[l], conv_b[l][None, :]
    b_a, b_i, lam = b_lru_a[l][None, :], b_lru_i[l][None, :], lru_lambda[l][None, :]
    sinks = attn_sinks[l]

    def tail(x, attn, rnn, sga, sgr, tm, seg, precise, experts_b=None, qkv=None):
        wa, wr, wo, wrt = (wa_f, wr_f, wo_f, w_route_f) if precise else (wa_b, wr_b, wo_b, w_route)
        tri = jnp.tril(jnp.ones((tm, tm), BF16), -1)
        new_sequences = None if qkv is None else (sinks, *qkv, seq, experts_f)
        outs = _merge(x, attn, rnn, sga, sgr, wa, wr, wo, g2, wrt, b_route, tri, tm, seg, precise,
                      new_sequences=new_sequences)
        x2, xtm, rec, rect, cnt = outs[:5]
        wg, wu, wd = experts_b or outs[5:]
        off, nfull, rem, slot = _plan(rect, cnt, seg)
        y = _moe(off, nfull, rem, slot, xtm, wg, wu, wd, x2, rec, seg, tm, tm)
        return y, (wg, wu, wd)

    xs = x_sample.reshape(dec_batch, D_MODEL)
    cos_s, sin_s = _rope_tables(jnp.full((dec_batch,), PAST_LEN, F32))
    qs, ks, vs, xrs, gys, sgas, sgrs, w_in_b = _proj(xs, g1, w_in_f, cos_s, sin_s, qkg, 1, dec_batch,
                                                     dec_batch, True)

    xp = x_prompt.reshape(batch * seq, D_MODEL)
    cos_p, sin_p = _rope_tables_range(seq)
    q, k, v, rnn, sga, sgr, h_last, conv_tail = _proj(
        xp, g1, w_in_b, cos_p, sin_p, qkg, batch, seq, PROMPT_TILE, False,
        rnn_weights=(cw, cb, wcat, b_a, b_i, lam))
    y_prompt, experts_b = tail(xp, None, rnn, sga, sgr, PROMPT_TILE, MOE_SEGMENT, False,
                               qkv=(q, k, v))
    y_prompt = y_prompt.reshape(batch, seq, D_MODEL)

    def last_rows(a, rows):
        return a.reshape(batch, seq, a.shape[-1])[:, seq - rows:]

    k_win_p = last_rows(k, WINDOW).reshape(1, batch, WINDOW, N_KV_HEADS, HEAD_DIM)
    v_win_p = last_rows(v, WINDOW).reshape(1, batch, WINDOW, N_KV_HEADS, HEAD_DIM)
    conv_p = conv_tail[None, :, SUBLANES - (CONV_W - 1):, :]
    h_p = h_last[None, :, 0, :]

    ck = jnp.transpose(cache_k_win[l], (0, 2, 3, 1))
    cv = jnp.transpose(cache_v_win[l], (0, 2, 3, 1))
    attn_s, k_win_s, v_win_s = _attn_sample(qs, ks, vs, ck, cv, sinks)
    sc = state_conv[l]
    rnn_s, h_s = _rnn_sample(xrs, gys, sc[:, 0], sc[:, 1], sc[:, 2], state_lru_h[l],
                             cw, cb, wcat_f, b_a, b_i, lam)
    y_sample, _ = tail(xs, attn_s, rnn_s, sgas, sgrs, dec_batch, dec_batch, True, experts_b)
    y_sample = y_sample.reshape(dec_batch, 1, D_MODEL)
    conv_s = jnp.stack([sc[:, 1], sc[:, 2], xrs], axis=1)[None]

    return (y_prompt, y_sample, k_win_p, v_win_p, conv_p, h_p,
            jnp.transpose(k_win_s, (0, 3, 1, 2))[None],
            jnp.transpose(v_win_s, (0, 3, 1, 2))[None],
            conv_s, h_s[None])
```

```python
import functools

import jax
import jax.numpy as jnp
from jax import lax
from jax.experimental import pallas as pl
from jax.experimental.pallas import tpu as pltpu

D_MODEL = 1024
HEAD_DIM = 64
N_Q_HEADS = 8
N_KV_HEADS = 2
Q_PER_KV = N_Q_HEADS // N_KV_HEADS
WINDOW = 128
ATTN_BLOCK = 128
ROPE_THETA = 10000.0
SCALE = HEAD_DIM ** -0.5
NEG_INF = -1e30
D_RNN = 1280
N_RNN_BLOCKS = 10
RNN_BLOCK = D_RNN // N_RNN_BLOCKS
CONV_W = 4
LRU_C = 8.0
N_GROUPS = 4
EXPERTS_PER_GROUP = 8
N_EXPERTS = N_GROUPS * EXPERTS_PER_GROUP
D_EXPERT = 256
PAST_LEN = 16384
EPS = 1e-6
D_Q = N_Q_HEADS * HEAD_DIM
D_KV = N_KV_HEADS * HEAD_DIM
D_IN = D_Q + 2 * D_KV + 2 * D_RNN + 2 * D_MODEL
OFF_K = D_Q
OFF_V = OFF_K + D_KV
OFF_XR = OFF_V + D_KV
OFF_YR = OFF_XR + D_RNN
OFF_GA = OFF_YR + D_RNN
OFF_GR = OFF_GA + D_MODEL

LANES = 128
SUBLANES = 8
VMEM_LIMIT = 56 * 1024 * 1024
PROMPT_TILE = 512
MOE_SEGMENT = 4096

F32 = jnp.float32
BF16 = jnp.bfloat16


def _params(*sem):
    return pltpu.CompilerParams(dimension_semantics=sem, vmem_limit_bytes=VMEM_LIMIT)


def _sigmoid(x):
    return 0.5 * jnp.tanh(0.5 * x) + 0.5


def _gelu_tanh(x):
    c = 0.7978845608028654
    half_x = 0.5 * x
    return half_x + half_x * jnp.tanh(x * (c + (c * 0.044715) * (x * x)))


def _full(shape, single_buffer=False):
    index_map = lambda *_: (0,) * len(shape)
    if single_buffer:
        return pl.BlockSpec(shape, index_map, pipeline_mode=pl.Buffered(1))
    return pl.BlockSpec(shape, index_map)


def _mm(a, b, precise, dims=None):
    if precise:
        a, b, prec = a.astype(F32), b.astype(F32), lax.Precision.HIGHEST
    else:
        a, b, prec = a.astype(BF16), b.astype(BF16), None
    if dims is None:
        return jnp.dot(a, b, preferred_element_type=F32, precision=prec)
    return lax.dot_general(a, b, dims, preferred_element_type=F32, precision=prec)


_NT = (((1,), (1,)), ((), ()))


def _proj_kernel(x_ref, g_ref, w_ref, cos_ref, sin_ref, qkg_ref, *rest, precise, fuse_rnn):
    if fuse_rnn:
        rnn_w = rest[:6]
        q_ref, k_ref, v_ref, rnn_ref, sga_ref, sgr_ref, hl_ref, ct_ref = rest[6:14]
        xbuf, a_scr, b_scr, h_scr, hcar = rest[14:]
        pl.when(pl.program_id(1) == 0)(functools.partial(_rnn_start_sequence, xbuf, hcar))
    else:
        q_ref, k_ref, v_ref, xr_ref, gy_ref, sga_ref, sgr_ref, wb_ref = rest
        wb_ref[...] = w_ref[...].astype(BF16)
    x = x_ref[...]
    inv = lax.rsqrt(jnp.mean(x * x, axis=-1, keepdims=True) + EPS)
    xn = x * inv * g_ref[...]
    if not precise:
        xn = xn.astype(BF16)

    def proj(lo, hi):
        return _mm(xn, w_ref[:, lo:hi], precise)

    def qk_heads():
        qk = proj(0, OFF_V)
        tm = qk.shape[0]
        lane = lax.broadcasted_iota(jnp.int32, (tm, LANES), 1)
        lo_head = lane < HEAD_DIM
        first_half = (lane % HEAD_DIM) < (HEAD_DIM // 2)
        cos_f, sin_f = cos_ref[...], sin_ref[...]
        reps = LANES // HEAD_DIM
        cos = jnp.concatenate([cos_f, cos_f] * reps, axis=-1)
        sin = jnp.concatenate([-sin_f, sin_f] * reps, axis=-1)
        for g in range(OFF_V // LANES):
            seg = qk[:, g * LANES:(g + 1) * LANES]
            sq = seg * seg
            s_lo = jnp.sum(jnp.where(lo_head, sq, 0.0), axis=-1, keepdims=True)
            s_hi = jnp.sum(jnp.where(lo_head, 0.0, sq), axis=-1, keepdims=True)
            ms = jnp.where(lo_head, s_lo, s_hi) * (1.0 / HEAD_DIM)
            normed = seg * lax.rsqrt(ms + EPS) * qkg_ref[:, g * LANES:(g + 1) * LANES]
            partner = jnp.where(first_half,
                                pltpu.roll(normed, LANES - HEAD_DIM // 2, axis=1),
                                pltpu.roll(normed, HEAD_DIM // 2, axis=1))
            roped = normed * cos + partner * sin
            if g < D_Q // LANES:
                q_ref[:, g * LANES:(g + 1) * LANES] = (roped * SCALE).astype(q_ref.dtype)
            else:
                k_ref[...] = roped

    def values():
        v_ref[...] = proj(OFF_V, OFF_XR)

    def gate(out_ref, off, c0, c1, r0=0, r1=x.shape[0]):
        y = _mm(xn[r0:r1], w_ref[:, off + c0:off + c1], precise)
        out_ref[r0:r1, c0:c1] = _sigmoid(y).astype(out_ref.dtype)

    if not fuse_rnn:
        xr_ref[...] = proj(OFF_XR, OFF_YR)
        gy_ref[...] = _gelu_tanh(proj(OFF_YR, OFF_GA)).astype(gy_ref.dtype)
        qk_heads()
        values()
        gate(sga_ref, OFF_GA, 0, D_MODEL)
        gate(sgr_ref, OFF_GR, 0, D_MODEL)
        return

    slab = RNN_SLAB_BLOCKS * RNN_BLOCK
    half = x.shape[0] // 2
    others = [qk_heads, values]
    for out_ref, off in ((sga_ref, OFF_GA), (sgr_ref, OFF_GR)):
        others += [functools.partial(gate, out_ref, off, c, c + GATE_COLS, r, r + half)
                   for c in range(0, D_MODEL, GATE_COLS) for r in (0, half)]
    others = iter(others)
    n_slabs = D_RNN // slab

    def slab_inputs(i):
        c0 = i * slab
        return (proj(OFF_XR + c0, OFF_XR + c0 + slab),
                _gelu_tanh(proj(OFF_YR + c0, OFF_YR + c0 + slab)))

    gy = []
    nxt = slab_inputs(0)
    for i in range(n_slabs):
        (xr, gy_i), nxt = nxt, None
        gy.append(gy_i)
        ct_ref[0, :, i * slab:(i + 1) * slab] = xr[xr.shape[0] - SUBLANES:, :]
        for _ in _rnn_scan_terms(xr, i * RNN_SLAB_BLOCKS, *rnn_w, xbuf, a_scr, b_scr):
            if nxt is None and i + 1 < n_slabs:
                nxt = slab_inputs(i + 1)
            next(others, lambda: None)()
    for other in others:
        other()
    _rnn_scan_finish(jnp.concatenate(gy, axis=-1), rnn_ref, hl_ref, a_scr, b_scr, h_scr, hcar)


RNN_SLAB_BLOCKS = 2
GATE_COLS = 256


def _proj(x, g, w_in, cos_t, sin_t, qkg, batch, seq, tm, precise, rnn_weights=None):
    n = batch * seq
    nt = seq // tm
    row = lambda b, t: (b * nt + t, 0)
    per_seq = lambda b, t: (b, 0, 0)
    act = F32 if precise else BF16
    rows_out = lambda width, dtype: (jax.ShapeDtypeStruct((n, width), dtype),
                                     pl.BlockSpec((tm, width), row))
    state_out = (jax.ShapeDtypeStruct((batch, SUBLANES, D_RNN), F32),
                 pl.BlockSpec((1, SUBLANES, D_RNN), per_seq))
    outs = [rows_out(D_Q, act), rows_out(D_KV, F32), rows_out(D_KV, F32)]
    in_specs = [
        pl.BlockSpec((tm, D_MODEL), row),
        _full((1, D_MODEL)),
        _full((D_MODEL, D_IN), single_buffer=True),
        pl.BlockSpec((tm, HEAD_DIM // 2), lambda b, t: (t, 0)),
        pl.BlockSpec((tm, HEAD_DIM // 2), lambda b, t: (t, 0)),
        _full((1, OFF_V)),
    ]
    args = [x, g, w_in, cos_t, sin_t, qkg]
    scratch = []
    if rnn_weights is None:
        outs += [rows_out(D_RNN, F32), rows_out(D_RNN, act)]
    else:
        assert not precise
        outs += [rows_out(D_RNN, act)]
        in_specs += [_full(w.shape) for w in rnn_weights]
        args += list(rnn_weights)
        groups = tm // SUBLANES
        scratch = [pltpu.VMEM((tm + SUBLANES, D_RNN), F32)]
        scratch += [pltpu.VMEM((groups, SUBLANES, D_RNN), F32)] * 3
        scratch += [pltpu.VMEM((SUBLANES, D_RNN), F32)]
    outs += [rows_out(D_MODEL, act), rows_out(D_MODEL, act)]
    if rnn_weights is not None:
        outs += [state_out, state_out]
    else:
        assert batch * nt == 1 and w_in.dtype == F32
        outs += [(jax.ShapeDtypeStruct(w_in.shape, BF16), _full(w_in.shape, single_buffer=True))]
    return pl.pallas_call(
        functools.partial(_proj_kernel, precise=precise, fuse_rnn=rnn_weights is not None),
        grid=(batch, nt),
        in_specs=in_specs,
        out_specs=[spec for _, spec in outs],
        out_shape=[shape for shape, _ in outs],
        scratch_shapes=scratch,
        compiler_params=_params("parallel", "arbitrary"),
        name="proj",
    )(*args)


def _sink_column(sink_ref, h, rows, rows_per_head):
    r = lax.broadcasted_iota(jnp.int32, (rows, 1), 0) // rows_per_head
    col = jnp.full((rows, 1), sink_ref[h * Q_PER_KV], F32)
    for g in range(1, Q_PER_KV):
        col = jnp.where(r == g, sink_ref[h * Q_PER_KV + g], col)
    return col


def _window_masks(first_tile):
    blk = ATTN_BLOCK
    rows = Q_PER_KV * blk
    i = lax.broadcasted_iota(jnp.int32, (rows, 2 * blk), 0) % blk
    j = lax.broadcasted_iota(jnp.int32, (rows, 2 * blk), 1)
    d = j - i
    in_window = (d >= 1) & (d <= WINDOW)
    first = (d >= jnp.where(first_tile, jnp.maximum(1, blk - i), 1)) & (d <= WINDOW)
    return in_window, first


def _attn_kv_head(sub, h, valid, sink_ref, q_ref, k_all, v_all, o_ref):
    blk = ATTN_BLOCK
    rows = Q_PER_KV * blk
    q = q_ref[sub * blk:(sub + 1) * blk, :]
    kh = k_all[sub * blk:(sub + 2) * blk, h * HEAD_DIM:(h + 1) * HEAD_DIM]
    vh = v_all[sub * blk:(sub + 2) * blk, h * HEAD_DIM:(h + 1) * HEAD_DIM]
    qs = jnp.concatenate(
        [q[:, (h * Q_PER_KV + g) * HEAD_DIM:(h * Q_PER_KV + g + 1) * HEAD_DIM]
         for g in range(Q_PER_KV)], axis=0)
    s = _mm(qs, kh, False, _NT)
    yield
    s = jnp.where(valid, s, NEG_INF)
    sink = _sink_column(sink_ref, h, rows, blk)
    m = jnp.maximum(jnp.max(s, axis=-1, keepdims=True), sink)
    p = jnp.exp(s - m)
    denom = jnp.sum(p, axis=-1, keepdims=True) + jnp.exp(sink - m)
    yield
    o = _mm(p, vh, False) * (1.0 / denom)
    for g in range(Q_PER_KV):
        c = (h * Q_PER_KV + g) * HEAD_DIM
        o_ref[sub * blk:(sub + 1) * blk, c:c + HEAD_DIM] = o[g * blk:(g + 1) * blk].astype(o_ref.dtype)


def _lockstep(generators):
    while generators:
        generators = [g for g in generators if next(g, "done") != "done"]
        yield


SAMPLE_BT = 16


def _attn_sample_kernel(sink_ref, q_ref, kn_ref, vn_ref, kc_ref, vc_ref, o_ref, ko_ref, vo_ref):
    bt = SAMPLE_BT
    pos = lax.broadcasted_iota(jnp.int32, (HEAD_DIM, WINDOW), 1)
    kn_t = kn_ref[...].T
    vn_t = vn_ref[...].T

    def shifted(cache_ref, new_t, out_ref, h):
        windows = []
        for b in range(bt):
            rolled = pltpu.roll(cache_ref[b, h], WINDOW - 1, axis=1)
            new = new_t[h * HEAD_DIM:(h + 1) * HEAD_DIM, b:b + 1]
            windows.append(jnp.where(pos == WINDOW - 1, new, rolled))
            out_ref[b, h] = windows[-1]
        return jnp.concatenate(windows, axis=1)

    q = q_ref[...]
    rows = Q_PER_KV * bt
    rb = lax.broadcasted_iota(jnp.int32, (rows, bt * WINDOW), 0) % bt
    cb = lax.broadcasted_iota(jnp.int32, (rows, bt * WINDOW), 1) // WINDOW
    valid = rb == cb

    def kv_head(h):
        kh_t = shifted(kc_ref, kn_t, ko_ref, h)
        vh_t = shifted(vc_ref, vn_t, vo_ref, h)
        qs = jnp.concatenate(
            [q[:, (h * Q_PER_KV + g) * HEAD_DIM:(h * Q_PER_KV + g + 1) * HEAD_DIM]
             for g in range(Q_PER_KV)], axis=0)
        s = _mm(qs, kh_t, True)
        yield
        s = jnp.where(valid, s, NEG_INF)
        sink = _sink_column(sink_ref, h, rows, bt)
        m = jnp.maximum(jnp.max(s, axis=-1, keepdims=True), sink)
        p = jnp.exp(s - m)
        denom = jnp.sum(p, axis=-1, keepdims=True) + jnp.exp(sink - m)
        o = _mm(p, vh_t, True, _NT) * (1.0 / denom)
        for g in range(Q_PER_KV):
            c = (h * Q_PER_KV + g) * HEAD_DIM
            o_ref[:, c:c + HEAD_DIM] = o[g * bt:(g + 1) * bt].astype(o_ref.dtype)

    heads = [kv_head(h) for h in range(N_KV_HEADS)]
    while heads:
        heads = [head for head in heads if next(head, "done") != "done"]


def _attn_sample(q, k_new, v_new, cache_k, cache_v, sinks):
    nbatch = q.shape[0]
    bt = SAMPLE_BT
    row = lambda i: (i, 0)
    win = pl.BlockSpec((bt, N_KV_HEADS, HEAD_DIM, WINDOW), lambda i: (i, 0, 0, 0))
    return pl.pallas_call(
        _attn_sample_kernel,
        grid=(nbatch // bt,),
        in_specs=[
            pl.BlockSpec(memory_space=pltpu.SMEM),
            pl.BlockSpec((bt, D_Q), row),
            pl.BlockSpec((bt, D_KV), row),
            pl.BlockSpec((bt, D_KV), row),
            win,
            win,
        ],
        out_specs=[pl.BlockSpec((bt, D_Q), row), win, win],
        out_shape=(
            jax.ShapeDtypeStruct((nbatch, D_Q), F32),
            jax.ShapeDtypeStruct(cache_k.shape, F32),
            jax.ShapeDtypeStruct(cache_v.shape, F32),
        ),
        compiler_params=_params("parallel"),
        name="attn_sample",
    )(sinks, q, k_new, v_new, cache_k, cache_v)


def _lru_terms(xc, wcat_ref, ba_ref, bi_ref, lam_ref, precise, block0=0):
    cols = slice(block0 * RNN_BLOCK, block0 * RNN_BLOCK + xc.shape[1])
    xcb = xc if precise else xc.astype(BF16)
    ya, yi = [], []
    for n in range(xc.shape[1] // RNN_BLOCK):
        y = _mm(xcb[:, n * RNN_BLOCK:(n + 1) * RNN_BLOCK], wcat_ref[block0 + n], precise)
        ya.append(y[:, :RNN_BLOCK])
        yi.append(y[:, RNN_BLOCK:])
    r = _sigmoid(jnp.concatenate(ya, axis=-1) + ba_ref[:, cols])
    gate_i = _sigmoid(jnp.concatenate(yi, axis=-1) + bi_ref[:, cols])
    neg_lam = -lam_ref[:, cols]
    softplus = jnp.maximum(neg_lam, 0.0) + jnp.log1p(jnp.exp(-jnp.abs(neg_lam)))
    log_a = (-LRU_C * softplus) * r
    a = jnp.exp(log_a)
    m = 1.0 - a * a
    b = jnp.where(m > 0.0, m * lax.rsqrt(m), 0.0) * (gate_i * xc)
    return a, b


def _rnn_start_sequence(xbuf, hcar):
    xbuf[0:SUBLANES, :] = jnp.zeros((SUBLANES, D_RNN), F32)
    hcar[...] = jnp.zeros((SUBLANES, D_RNN), F32)


def _rnn_scan_terms(x, block0, cw_ref, cb_ref, wcat_ref, ba_ref, bi_ref, lam_ref,
                    xbuf, a_scr, b_scr):
    tt, width = x.shape
    groups = tt // SUBLANES
    cols = slice(block0 * RNN_BLOCK, block0 * RNN_BLOCK + width)
    xbuf[SUBLANES:, cols] = x
    xc = cb_ref[:, cols] + cw_ref[CONV_W - 1:CONV_W, cols] * x
    for j in range(CONV_W - 1):
        s = CONV_W - 1 - j
        xc = xc + cw_ref[j:j + 1, cols] * xbuf[SUBLANES - s:SUBLANES - s + tt, cols]
    xbuf[0:SUBLANES, cols] = x[tt - SUBLANES:, :]
    yield
    a, b = _lru_terms(xc, wcat_ref, ba_ref, bi_ref, lam_ref, False, block0)
    yield
    a = a.reshape(groups, SUBLANES, width)
    b = b.reshape(groups, SUBLANES, width)
    step = lax.broadcasted_iota(jnp.int32, (groups, SUBLANES, width), 1)
    k = 1
    while k < SUBLANES:
        keep = step >= k
        a_sh = jnp.where(keep, pltpu.roll(a, k, axis=1), 1.0)
        b_sh = jnp.where(keep, pltpu.roll(b, k, axis=1), 0.0)
        b = a * b_sh + b
        a = a * a_sh
        k *= 2
        if k < SUBLANES:
            yield
    a_scr[:, :, cols] = a
    b_scr[:, :, cols] = b


def _rnn_scan_finish(gy, o_ref, hl_ref, a_scr, b_scr, h_scr, hcar):
    groups = a_scr.shape[0]
    tt = groups * SUBLANES

    def chain(g, h_in):
        h = a_scr[g] * h_in + b_scr[g]
        h_scr[g] = h
        return jnp.broadcast_to(h[SUBLANES - 1:SUBLANES, :], (SUBLANES, D_RNN))

    h_last = lax.fori_loop(0, groups, chain, hcar[...], unroll=True)
    hcar[...] = h_last
    hl_ref[0] = h_last
    h = h_scr[...].reshape(tt, D_RNN)
    o_ref[...] = (h * gy).astype(o_ref.dtype)


def _rnn_sample_kernel(xr_ref, gy_ref, s0_ref, s1_ref, s2_ref, h_ref, cw_ref, cb_ref,
                       wcat_ref, ba_ref, bi_ref, lam_ref, o_ref, hn_ref):
    x = xr_ref[...]
    xc = (cb_ref[...] + cw_ref[0:1, :] * s0_ref[...] + cw_ref[1:2, :] * s1_ref[...]
          + cw_ref[2:3, :] * s2_ref[...] + cw_ref[3:4, :] * x)
    a, b = _lru_terms(xc, wcat_ref, ba_ref, bi_ref, lam_ref, True)
    h = a * h_ref[...] + b
    hn_ref[...] = h
    o_ref[...] = (h * gy_ref[...].astype(F32)).astype(o_ref.dtype)


def _rnn_sample(xr, gy, s0, s1, s2, h_prev, conv_w, conv_b, wcat, b_a, b_i, lam):
    n = xr.shape[0]
    act = _full((n, D_RNN))
    return pl.pallas_call(
        _rnn_sample_kernel,
        grid=(1,),
        in_specs=[act, act, act, act, act, act,
                  _full((CONV_W, D_RNN)), _full((1, D_RNN)),
                  _full((N_RNN_BLOCKS, RNN_BLOCK, 2 * RNN_BLOCK)),
                  _full((1, D_RNN)), _full((1, D_RNN)), _full((1, D_RNN))],
        out_specs=[act, act],
        out_shape=(jax.ShapeDtypeStruct((n, D_RNN), F32),
                   jax.ShapeDtypeStruct((n, D_RNN), F32)),
        compiler_params=_params("arbitrary"),
        name="rnn_sample",
    )(xr, gy, s0, s1, s2, h_prev, conv_w, conv_b, wcat, b_a, b_i, lam)


def _merge_kernel(*refs, precise, tiles_per_seg, tiles_per_seq):
    fused = tiles_per_seq is not None
    if fused:
        (sink_ref, q_ref, kc_ref, kp_ref, vc_ref, vp_ref, x_ref, rn_ref, sga_ref, sgr_ref,
         wa_ref, wr_ref, wo_ref, g_ref, wrt_ref, brt_ref, tri_ref, wg_ref, wu_ref, wd_ref,
         x2_ref, xtm_ref, rec_ref, rect_ref, cnt_ref, wg_o, wu_o, wd_o, cnt_scr, at_ref) = refs
    else:
        (x_ref, at_ref, rn_ref, sga_ref, sgr_ref, wa_ref, wr_ref, wo_ref, g_ref, wrt_ref, brt_ref,
         tri_ref, x2_ref, xtm_ref, rec_ref, rect_ref, cnt_ref, cnt_scr) = refs

    @pl.when(pl.program_id(0) % tiles_per_seg == 0)
    def _():
        cnt_scr[...] = jnp.zeros_like(cnt_scr)

    tm = x_ref.shape[0]
    parts = MERGE_PARTS if tm % (MERGE_PARTS * LANES) == 0 else 1
    part = tm // parts
    state = {"counts": cnt_scr[...]}
    pending = [
        _merge_rows(slice(i * part, (i + 1) * part), state, x_ref, at_ref, rn_ref, sga_ref, sgr_ref,
                    wa_ref, wr_ref, wo_ref, g_ref, wrt_ref, brt_ref, tri_ref, x2_ref, xtm_ref,
                    rec_ref, rect_ref, precise)
        for i in range(parts)]
    active = []
    if fused:
        k_all = jnp.concatenate([kp_ref[...], kc_ref[...]], axis=0).astype(BF16)
        v_all = jnp.concatenate([vp_ref[...], vc_ref[...]], axis=0).astype(BF16)
        in_window, first = _window_masks(pl.program_id(0) % tiles_per_seq == 0)
        blocks = part // ATTN_BLOCK
        side = list(pending)
        for i in range(parts):
            heads = [_attn_kv_head(sub, h, first if sub == 0 else in_window, sink_ref, q_ref,
                                   k_all, v_all, at_ref)
                     for sub in range(i * blocks, (i + 1) * blocks) for h in range(N_KV_HEADS)]
            for _ in _lockstep(heads):
                if side:
                    next(side.pop(0))
                for part_pieces in list(active) * MERGE_PIECES_PER_ATTN_ROUND:
                    if part_pieces in active and next(part_pieces, "done") == "done":
                        active.remove(part_pieces)
            active.append(pending.pop(0))
    step = 0
    while pending or active:
        if pending and step % MERGE_STAGGER == 0:
            active.append(pending.pop(0))
        for part_pieces in list(active):
            if next(part_pieces, "done") == "done":
                active.remove(part_pieces)
        step += 1
    cnt_scr[...] = state["counts"]
    cnt_ref[0] = state["counts"]
    if fused:
        wg_o[...] = wg_ref[...].astype(BF16)
        wu_o[...] = wu_ref[...].astype(BF16)
        wd_o[...] = wd_ref[...].astype(BF16)


MERGE_PARTS = 2
MERGE_STAGGER = 3
MERGE_PIECES_PER_ATTN_ROUND = 1


def _merge_rows(rows, state, x_ref, at_ref, rn_ref, sga_ref, sgr_ref, wa_ref, wr_ref, wo_ref, g_ref,
                wrt_ref, brt_ref, tri_ref, x2_ref, xtm_ref, rec_ref, rect_ref, precise):
    yr = _mm(rn_ref[rows, :], wr_ref[...], precise)
    yield
    ya = _mm(at_ref[rows, :], wa_ref[...], precise)
    yield
    merged =sga_ref[rows, :].astype(F32) * ya + sgr_ref[rows, :].astype(F32) * yr
    yield
    x2 = x_ref[rows, :] + _mm(merged, wo_ref[...], precise)
    x2_ref[rows, :] = x2
    yield
    inv = lax.rsqrt(jnp.mean(x2 * x2, axis=-1, keepdims=True) + EPS)
    xn = x2 * inv * g_ref[...]
    for c in range(TOKEN_ROWS):
        xtm_ref[pl.ds(rows.start * TOKEN_ROWS + c, rows.stop - rows.start, stride=TOKEN_ROWS), :] = (
            xn[:, c * LANES:(c + 1) * LANES])
    yield

    logits = _mm(xn, wrt_ref[...], precise) + brt_ref[...]
    yield
    tm = logits.shape[0]
    lane_i = lax.broadcasted_iota(jnp.int32, (tm, LANES), 1)
    lane = lane_i.astype(F32)
    big = float(LANES)
    is_grp = (lane_i >= N_EXPERTS) & (lane_i < N_EXPERTS + N_GROUPS)
    gl = jnp.where(is_grp, logits, NEG_INF)
    gmax = jnp.max(gl, axis=-1, keepdims=True)
    g_idx = jnp.min(jnp.where(gl == gmax, lane, big), axis=-1, keepdims=True) - N_EXPERTS
    p_g = 1.0 / jnp.sum(jnp.exp(gl - gmax), axis=-1, keepdims=True)
    in_grp = (lane_i // EXPERTS_PER_GROUP).astype(F32) == g_idx
    el = jnp.where(in_grp, logits, NEG_INF)
    v1 = jnp.max(el, axis=-1, keepdims=True)
    i1 = jnp.min(jnp.where(el == v1, lane, big), axis=-1, keepdims=True)
    el2 = jnp.where(lane == i1, NEG_INF, el)
    v2 = jnp.max(el2, axis=-1, keepdims=True)
    i2 = jnp.min(jnp.where(el2 == v2, lane, big), axis=-1, keepdims=True)
    e2 = jnp.exp(v2 - v1)
    w1 = p_g / (1.0 + e2)
    w2 = p_g * e2 / (1.0 + e2)
    yield

    hit = jnp.where(lane == i1, 1.0, jnp.where(lane == i2, 1.0, 0.0))
    counts = state["counts"]
    state["counts"] = counts + jnp.sum(hit, axis=0, keepdims=True)
    before = (jnp.dot(tri_ref[0:tm, 0:tm], hit.astype(BF16), preferred_element_type=F32)
              + counts[0:1, :])
    yield
    r1 = jnp.sum(jnp.where(lane == i1, before, 0.0), axis=-1, keepdims=True)
    r2 = jnp.sum(jnp.where(lane == i2, before, 0.0), axis=-1, keepdims=True)
    rec = jnp.where(lane == REC_W2, w2, 0.0)
    for field, val in ((REC_W1, w1), (REC_R2, r2), (REC_R1, r1), (REC_E2, i2), (REC_E1, i1)):
        rec = jnp.where(lane == field, val, rec)
    rec_ref[rows, :] = rec
    rect_ref[:, rows] = rec.T


REC_E1, REC_E2, REC_R1, REC_R2, REC_W1, REC_W2 = range(6)
TOKEN_ROWS = D_MODEL // LANES


def _merge(x, attn, rnn, sga, sgr, wa, wr, wo, g2, w_route, b_route, tri, tm, seg, precise,
           new_sequences=None):
    n = x.shape[0]
    steps = n // tm
    tiles_per_seg = seg // tm
    row = lambda i: (i, 0)
    scratch = [pltpu.VMEM((SUBLANES, LANES), F32)]
    if new_sequences is None:
        head_specs = [pl.BlockSpec((tm, D_MODEL), row), pl.BlockSpec((tm, D_Q), row)]
        head_args = [x, attn]
        tiles_per_seq = None
    else:
        sinks, q, k, v, seq_len, expert_weights = new_sequences
        assert steps == N_EXPERTS and attn is None
        tiles_per_seq = seq_len // tm
        prev = lambda i: (jnp.maximum(i * (tm // ATTN_BLOCK) - 1, 0), 0)
        head_specs = [
            pl.BlockSpec(memory_space=pltpu.SMEM),
            pl.BlockSpec((tm, D_Q), row),
            pl.BlockSpec((tm, D_KV), row),
            pl.BlockSpec((ATTN_BLOCK, D_KV), prev),
            pl.BlockSpec((tm, D_KV), row),
            pl.BlockSpec((ATTN_BLOCK, D_KV), prev),
            pl.BlockSpec((tm, D_MODEL), row),
        ]
        head_args = [sinks, q, k, k, v, v, x]
        scratch.append(pltpu.VMEM((tm, D_Q), BF16))
    in_specs = head_specs + [
        pl.BlockSpec((tm, D_RNN), row),
        pl.BlockSpec((tm, D_MODEL), row),
        pl.BlockSpec((tm, D_MODEL), row),
        _full((D_Q, D_MODEL)),
        _full((D_RNN, D_MODEL)),
        _full((D_MODEL, D_MODEL)),
        _full((1, D_MODEL)),
        _full((D_MODEL, LANES)),
        _full((1, LANES)),
        _full((tm, tm)),
    ]
    out_specs = [
        pl.BlockSpec((tm, D_MODEL), row),
        pl.BlockSpec((tm * TOKEN_ROWS, LANES), row),
        pl.BlockSpec((tm, LANES), row),
        pl.BlockSpec((LANES, tm), lambda i: (0, i)),
        pl.BlockSpec((1, SUBLANES, LANES), lambda i: (i // tiles_per_seg, 0, 0)),
    ]
    out_shape = [
        jax.ShapeDtypeStruct((n, D_MODEL), F32),
        jax.ShapeDtypeStruct((n * TOKEN_ROWS, LANES), F32),
        jax.ShapeDtypeStruct((n, LANES), F32),
        jax.ShapeDtypeStruct((LANES, n), F32),
        jax.ShapeDtypeStruct((n // seg, SUBLANES, LANES), F32),
    ]
    args = head_args + [rnn, sga, sgr, wa, wr, wo, g2, w_route, b_route, tri]
    if new_sequences is not None:
        for w in expert_weights:
            spec = pl.BlockSpec((1,) + w.shape[1:], lambda i: (i, 0, 0))
            in_specs.append(spec)
            out_specs.append(spec)
            out_shape.append(jax.ShapeDtypeStruct(w.shape, BF16))
            args.append(w)
    return pl.pallas_call(
        functools.partial(_merge_kernel, precise=precise, tiles_per_seg=tiles_per_seg,
                          tiles_per_seq=tiles_per_seq),
        grid=(steps,),
        in_specs=in_specs,
        out_specs=out_specs,
        out_shape=out_shape,
        scratch_shapes=scratch,
        compiler_params=_params("arbitrary"),
        name="merge",
    )(*args)


MOE_CHUNK = 256
MOE_TAIL = 128
MOE_EXPERTS_PER_STEP = 2
MOE_LOOP_TOKENS = 32
MOE_VMEM_LIMIT = 60 * 1024 * 1024


def _seg_rows(seg):
    return 2 * seg + N_EXPERTS * SUBLANES + MOE_CHUNK


def _token_rows(i):
    return pl.ds(pl.multiple_of(i * TOKEN_ROWS, TOKEN_ROWS), TOKEN_ROWS)


def _sorted_rows(first_row):
    return pl.ds(pl.multiple_of(first_row, TOKEN_ROWS), TOKEN_ROWS)


def _moe_kernel(off_ref, nfull_ref, rem_ref, slot_ref, xtm_ref, wg_ref, wu_ref, wd_ref,
                x2_ref, rec_ref, o_ref, buf, g1, g2, *, seg, td, tc):
    s = pl.program_id(0)
    p = pl.program_id(1)
    n_disp = seg // td
    n_exp = N_EXPERTS // MOE_EXPERTS_PER_STEP

    @pl.when((s == 0) & (p == 0))
    def _():
        buf[...] = jnp.zeros_like(buf)

    @pl.when(p < n_disp)
    def _():
        def dispatch(g, carry):
            for j in range(MOE_LOOP_TOKENS):
                t = g * MOE_LOOP_TOKENS + j
                row = xtm_ref[_token_rows(t), :]
                for k in range(2):
                    buf[_sorted_rows(slot_ref[0, 0, k * seg + p * td + t]), :] = row
            return carry

        lax.fori_loop(0, td // MOE_LOOP_TOKENS, dispatch, 0)

    def run_chunk(e, row0, rows, valid):
        r0 = pl.multiple_of(row0 * TOKEN_ROWS, SUBLANES * TOKEN_ROWS)
        xf = [buf[pl.ds(r0 + j, rows, stride=TOKEN_ROWS), :] for j in range(TOKEN_ROWS)]
        x = jnp.concatenate(xf, axis=-1).astype(BF16)
        hg = jnp.dot(x, wg_ref[e], preferred_element_type=F32)
        hu = jnp.dot(x, wu_ref[e], preferred_element_type=F32)
        h = (hg * _sigmoid(hg)) * hu
        y = jnp.dot(h.astype(BF16), wd_ref[e], preferred_element_type=F32)
        if valid is not None:
            mine = lax.broadcasted_iota(jnp.int32, (rows, LANES), 0) < valid
        for j in range(TOKEN_ROWS):
            yj = y[:, j * LANES:(j + 1) * LANES]
            if valid is not None:
                yj = jnp.where(mine, yj, xf[j])
            buf[pl.ds(r0 + j, rows, stride=TOKEN_ROWS), :] = yj

    def run_expert(e):
        idx = s * N_EXPERTS + (p - n_disp) * MOE_EXPERTS_PER_STEP + e
        base = off_ref[idx]
        n_full = nfull_ref[idx]
        rem = rem_ref[idx]

        def chunk(c, carry):
            run_chunk(e, base + c * MOE_CHUNK, MOE_CHUNK, None)
            return carry

        lax.fori_loop(0, n_full, chunk, 0)
        last = base + n_full * MOE_CHUNK
        for units in range(1, MOE_CHUNK // MOE_TAIL + 2):
            @pl.when((rem > (units - 1) * MOE_TAIL) & (rem <= units * MOE_TAIL))
            def _():
                run_chunk(e, last, units * MOE_TAIL, rem)

    @pl.when((p >= n_disp) & (p < n_disp + n_exp))
    def _():
        for e in range(MOE_EXPERTS_PER_STEP):
            run_expert(e)

    @pl.when(p >= n_disp + n_exp)
    def _():
        t0 = (p - n_disp - n_exp) * tc

        def gather(g, carry):
            for j in range(MOE_LOOP_TOKENS):
                t = g * MOE_LOOP_TOKENS + j
                g1[_token_rows(t), :] = buf[_sorted_rows(slot_ref[0, 0, t0 + t]), :]
                g2[_token_rows(t), :] = buf[_sorted_rows(slot_ref[0, 0, seg + t0 + t]), :]
            return carry

        lax.fori_loop(0, tc // MOE_LOOP_TOKENS, gather, 0)
        rec = rec_ref[...]
        lane = lax.broadcasted_iota(jnp.int32, rec.shape, 1)
        w1 = jnp.broadcast_to(
            jnp.sum(jnp.where(lane == REC_W1, rec, 0.0), axis=-1, keepdims=True), rec.shape)
        w2 = jnp.broadcast_to(
            jnp.sum(jnp.where(lane == REC_W2, rec, 0.0), axis=-1, keepdims=True), rec.shape)
        for j in range(TOKEN_ROWS):
            cols = slice(j * LANES, (j + 1) * LANES)
            o_ref[:, cols] = (x2_ref[:, cols] + w1 * g1[pl.ds(j, tc, stride=TOKEN_ROWS), :]
                              + w2 * g2[pl.ds(j, tc, stride=TOKEN_ROWS), :])


def _moe(off, nfull, rem, slot, xtm, wg, wu, wd, x2, rec, seg, td, tc):
    n = x2.shape[0]
    n_seg = n // seg
    n_disp, n_comb = seg // td, seg // tc
    per_step = MOE_EXPERTS_PER_STEP
    n_exp = N_EXPERTS // per_step
    rows = _seg_rows(seg) * TOKEN_ROWS
    disp_tile = lambda s, p, *_: (s * n_disp + jnp.minimum(p, n_disp - 1), 0)
    expert = lambda s, p, *_: (jnp.clip(p - n_disp, 0, n_exp - 1), 0, 0)
    comb_tile = lambda s, p, *_: (s * n_comb + jnp.clip(p - n_disp - n_exp, 0, n_comb - 1), 0)
    grid_spec = pltpu.PrefetchScalarGridSpec(
        num_scalar_prefetch=3,
        grid=(n_seg, n_disp + n_exp + n_comb),
        in_specs=[
            pl.BlockSpec((1, 1, 2 * seg), lambda s, p, *_: (s, 0, 0), memory_space=pltpu.SMEM),
            pl.BlockSpec((td * TOKEN_ROWS, LANES), disp_tile),
            pl.BlockSpec((per_step, D_MODEL, D_EXPERT), expert),
            pl.BlockSpec((per_step, D_MODEL, D_EXPERT), expert),
            pl.BlockSpec((per_step, D_EXPERT, D_MODEL), expert),
            pl.BlockSpec((tc, D_MODEL), comb_tile),
            pl.BlockSpec((tc, LANES), comb_tile),
        ],
        out_specs=pl.BlockSpec((tc, D_MODEL), comb_tile),
        scratch_shapes=[pltpu.VMEM((rows, LANES), F32),
                        pltpu.VMEM((tc * TOKEN_ROWS, LANES), F32),
                        pltpu.VMEM((tc * TOKEN_ROWS, LANES), F32)],
    )
    return pl.pallas_call(
        functools.partial(_moe_kernel, seg=seg, td=td, tc=tc),
        grid_spec=grid_spec,
        out_shape=jax.ShapeDtypeStruct((n, D_MODEL), F32),
        compiler_params=pltpu.CompilerParams(
            dimension_semantics=("arbitrary", "arbitrary"), vmem_limit_bytes=MOE_VMEM_LIMIT),
        name="moe",
    )(off, nfull, rem, slot, xtm, wg, wu, wd, x2, rec)


def _plan(rect, cnt, seg):
    n = rect.shape[1]
    expert = rect[REC_E1:REC_E2 + 1].astype(jnp.int32)
    rank = rect[REC_R1:REC_R2 + 1].astype(jnp.int32)
    counts = cnt[:, 0, :N_EXPERTS].astype(jnp.int32)
    padded = (counts + SUBLANES - 1) // SUBLANES * SUBLANES
    off = jnp.cumsum(padded, axis=1) - padded
    n_full = counts // MOE_CHUNK
    rem = counts - n_full * MOE_CHUNK
    join = (rem > 0) & (rem <= MOE_TAIL) & (n_full > 0)
    n_full = n_full - join
    rem = rem + join * MOE_CHUNK
    off_tok = jnp.repeat(off.T, seg, axis=1)
    hit = expert[:, None, :] == jnp.arange(N_EXPERTS, dtype=jnp.int32)[None, :, None]
    slot = (rank + jnp.sum(jnp.where(hit, off_tok[None], 0), axis=1)) * TOKEN_ROWS
    slot = slot.reshape(2, n // seg, seg).transpose(1, 0, 2).reshape(n // seg, 1, 2 * seg)
    return off.reshape(-1), n_full.reshape(-1), rem.reshape(-1), slot


def _rope_tables(pos):
    half = HEAD_DIM // 2
    inv_freq = ROPE_THETA ** (-jnp.arange(half, dtype=F32) / half)
    ang = pos[:, None] * inv_freq[None, :]
    return jnp.cos(ang), jnp.sin(ang)


def _rope_tables_range(n):
    hi = jnp.arange(n // ATTN_BLOCK, dtype=F32) * ATTN_BLOCK
    lo = jnp.arange(ATTN_BLOCK, dtype=F32)
    (cos_hi, sin_hi), (cos_lo, sin_lo) = _rope_tables(hi), _rope_tables(lo)
    cos = cos_hi[:, None, :] * cos_lo[None] - sin_hi[:, None, :] * sin_lo[None]
    sin = sin_hi[:, None, :] * cos_lo[None] + cos_hi[:, None, :] * sin_lo[None]
    return cos.reshape(n, -1), sin.reshape(n, -1)


def kernel(x_prompt, x_sample, cache_k_win, cache_v_win, state_conv, state_lru_h, attn_norm_g, w_in, q_norm_g, k_norm_g, attn_sinks, conv_w, conv_b, w_lru_a, b_lru_a, w_lru_i, b_lru_i, lru_lambda, w_br_attn, w_br_rnn, w_out, ffn_norm_g, w_route_group, b_route_group, w_route_expert, b_route_expert, w_exp_gate, w_exp_up, w_exp_down):
    batch, seq, _ = x_prompt.shape
    dec_batch, dec_seq, _ = x_sample.shape
    depth = w_in.shape[0]
    assert depth == 1 and dec_seq == 1
    l = 0

    w_in_f = w_in[l]
    qkg = jnp.concatenate([jnp.tile(q_norm_g[l], N_Q_HEADS), jnp.tile(k_norm_g[l], N_KV_HEADS)])[None, :]
    wcat_f = jnp.concatenate([w_lru_a[l], w_lru_i[l]], axis=-1)
    wa_f, wr_f, wo_f = w_br_attn[l], w_br_rnn[l], w_out[l]
    w_route_f = jnp.concatenate(
        [w_route_expert[l], w_route_group[l],
         jnp.zeros((D_MODEL, LANES - N_EXPERTS - N_GROUPS), F32)], axis=-1)
    wcat = wcat_f.astype(BF16)
    wa_b, wr_b, wo_b, w_route = (w.astype(BF16) for w in (wa_f, wr_f, wo_f, w_route_f))
    b_route = jnp.concatenate(
        [b_route_expert[l], b_route_group[l], jnp.zeros((LANES - N_EXPERTS - N_GROUPS,), F32)])[None, :]
    experts_f = (w_exp_gate[l], w_exp_up[l], w_exp_down[l])
    g1 = attn_norm_g[l][None, :]
    g2 = ffn_norm_g[l][None, :]
    cw, cb = conv_w[l], conv_b[l][None, :]
    b_a, b_i, lam = b_lru_a[l][None, :], b_lru_i[l][None, :], lru_lambda[l][None, :]
    sinks = attn_sinks[l]

    def tail(x, attn, rnn, sga, sgr, tm, seg, precise, experts_b=None, qkv=None):
        wa, wr, wo, wrt = (wa_f, wr_f, wo_f, w_route_f) if precise else (wa_b, wr_b, wo_b, w_route)
        tri = jnp.tril(jnp.ones((tm, tm), BF16), -1)
        new_sequences = None if qkv is None else (sinks, *qkv, seq, experts_f)
        outs = _merge(x, attn, rnn, sga, sgr, wa, wr, wo, g2, wrt, b_route, tri, tm, seg, precise,
                      new_sequences=new_sequences)
        x2, xtm, rec, rect, cnt = outs[:5]
        wg, wu, wd = experts_b or outs[5:]
        off, nfull, rem, slot = _plan(rect, cnt, seg)
        y = _moe(off, nfull, rem, slot, xtm, wg, wu, wd, x2, rec, seg, tm, tm)
        return y, (wg, wu, wd)

    xs = x_sample.reshape(dec_batch, D_MODEL)
    cos_s, sin_s = _rope_tables(jnp.full((dec_batch,), PAST_LEN, F32))
    qs, ks, vs, xrs, gys, sgas, sgrs, w_in_b = _proj(xs, g1, w_in_f, cos_s, sin_s, qkg, 1, dec_batch,
                                                     dec_batch, True)

    xp = x_prompt.reshape(batch * seq, D_MODEL)
    cos_p, sin_p = _rope_tables_range(seq)
    q, k, v, rnn, sga, sgr, h_last, conv_tail = _proj(
        xp, g1, w_in_b, cos_p, sin_p, qkg, batch, seq, PROMPT_TILE, False,
        rnn_weights=(cw, cb, wcat, b_a, b_i, lam))
    y_prompt, experts_b = tail(xp, None, rnn, sga, sgr, PROMPT_TILE, MOE_SEGMENT, False,
                               qkv=(q, k, v))
    y_prompt = y_prompt.reshape(batch, seq, D_MODEL)

    def last_rows(a, rows):
        return a.reshape(batch, seq, a.shape[-1])[:, seq - rows:]

    k_win_p = last_rows(k, WINDOW).reshape(1, batch, WINDOW, N_KV_HEADS, HEAD_DIM)
    v_win_p = last_rows(v, WINDOW).reshape(1, batch, WINDOW, N_KV_HEADS, HEAD_DIM)
    conv_p = conv_tail[None, :, SUBLANES - (CONV_W - 1):, :]
    h_p = h_last[None, :, 0, :]

    ck = jnp.transpose(cache_k_win[l], (0, 2, 3, 1))
    cv = jnp.transpose(cache_v_win[l], (0, 2, 3, 1))
    attn_s, k_win_s, v_win_s = _attn_sample(qs, ks, vs, ck, cv, sinks)
    sc = state_conv[l]
    rnn_s, h_s = _rnn_sample(xrs, gys, sc[:, 0], sc[:, 1], sc[:, 2], state_lru_h[l],
                             cw, cb, wcat_f, b_a, b_i, lam)
    y_sample, _ = tail(xs, attn_s, rnn_s, sgas, sgrs, dec_batch, dec_batch, True, experts_b)
    y_sample = y_sample.reshape(dec_batch, 1, D_MODEL)
    conv_s = jnp.stack([sc[:, 1], sc[:, 2], xrs], axis=1)[None]

    return (y_prompt, y_sample, k_win_p, v_win_p, conv_p, h_p,
            jnp.transpose(k_win_s, (0, 3, 1, 2))[None],
            jnp.transpose(v_win_s, (0, 3, 1, 2))[None],
            conv_s, h_s[None])
```

```python
import functools

import jax
import jax.numpy as jnp
from jax import lax
from jax.experimental import pallas as pl
from jax.experimental.pallas import tpu as pltpu

D_MODEL = 1024
HEAD_DIM = 64
N_Q_HEADS = 8
N_KV_HEADS = 2
Q_PER_KV = N_Q_HEADS // N_KV_HEADS
WINDOW = 128
ATTN_BLOCK = 128
ROPE_THETA = 10000.0
SCALE = HEAD_DIM ** -0.5
NEG_INF = -1e30
D_RNN = 1280
N_RNN_BLOCKS = 10
RNN_BLOCK = D_RNN // N_RNN_BLOCKS
CONV_W = 4
LRU_C = 8.0
N_GROUPS = 4
EXPERTS_PER_GROUP = 8
N_EXPERTS = N_GROUPS * EXPERTS_PER_GROUP
D_EXPERT = 256
PAST_LEN = 16384
EPS = 1e-6
D_Q = N_Q_HEADS * HEAD_DIM
D_KV = N_KV_HEADS * HEAD_DIM
D_IN = D_Q + 2 * D_KV + 2 * D_RNN + 2 * D_MODEL
OFF_K = D_Q
OFF_V = OFF_K + D_KV
OFF_XR = OFF_V + D_KV
OFF_YR = OFF_XR + D_RNN
OFF_GA = OFF_YR + D_RNN
OFF_GR = OFF_GA + D_MODEL

LANES = 128
SUBLANES = 8
VMEM_LIMIT = 56 * 1024 * 1024
PROMPT_TILE = 512
MOE_SEGMENT = 4096

F32 = jnp.float32
BF16 = jnp.bfloat16


def _params(*sem):
    return pltpu.CompilerParams(dimension_semantics=sem, vmem_limit_bytes=VMEM_LIMIT)


def _sigmoid(x):
    return 0.5 * jnp.tanh(0.5 * x) + 0.5


def _gelu_tanh(x):
    c = 0.7978845608028654
    half_x = 0.5 * x
    return half_x + half_x * jnp.tanh(x * (c + (c * 0.044715) * (x * x)))


def _full(shape, single_buffer=False):
    index_map = lambda *_: (0,) * len(shape)
    if single_buffer:
        return pl.BlockSpec(shape, index_map, pipeline_mode=pl.Buffered(1))
    return pl.BlockSpec(shape, index_map)


def _mm(a, b, precise, dims=None):
    if precise:
        a, b, prec = a.astype(F32), b.astype(F32), lax.Precision.HIGHEST
    else:
        a, b, prec = a.astype(BF16), b.astype(BF16), None
    if dims is None:
        return jnp.dot(a, b, preferred_element_type=F32, precision=prec)
    return lax.dot_general(a, b, dims, preferred_element_type=F32, precision=prec)


_NT = (((1,), (1,)), ((), ()))


def _proj_kernel(x_ref, g_ref, w_ref, cos_ref, sin_ref, qkg_ref, *rest, precise, fuse_rnn):
    if fuse_rnn:
        rnn_w = rest[:6]
        q_ref, k_ref, v_ref, rnn_ref, sga_ref, sgr_ref, hl_ref, ct_ref = rest[6:14]
        xbuf, a_scr, b_scr, h_scr, hcar = rest[14:]
        pl.when(pl.program_id(1) == 0)(functools.partial(_rnn_start_sequence, xbuf, hcar))
    else:
        q_ref, k_ref, v_ref, xr_ref, gy_ref, sga_ref, sgr_ref, wb_ref = rest
        wb_ref[...] = w_ref[...].astype(BF16)
    x = x_ref[...]
    inv = lax.rsqrt(jnp.mean(x * x, axis=-1, keepdims=True) + EPS)
    xn = x * inv * g_ref[...]
    if not precise:
        xn = xn.astype(BF16)

    def proj(lo, hi):
        return _mm(xn, w_ref[:, lo:hi], precise)

    def qk_heads():
        qk = proj(0, OFF_V)
        tm = qk.shape[0]
        lane = lax.broadcasted_iota(jnp.int32, (tm, LANES), 1)
        lo_head = lane < HEAD_DIM
        first_half = (lane % HEAD_DIM) < (HEAD_DIM // 2)
        cos_f, sin_f = cos_ref[...], sin_ref[...]
        reps = LANES // HEAD_DIM
        cos = jnp.concatenate([cos_f, cos_f] * reps, axis=-1)
        sin = jnp.concatenate([-sin_f, sin_f] * reps, axis=-1)
        for g in range(OFF_V // LANES):
            seg = qk[:, g * LANES:(g + 1) * LANES]
            sq = seg * seg
            s_lo = jnp.sum(jnp.where(lo_head, sq, 0.0), axis=-1, keepdims=True)
            s_hi = jnp.sum(jnp.where(lo_head, 0.0, sq), axis=-1, keepdims=True)
            ms = jnp.where(lo_head, s_lo, s_hi) * (1.0 / HEAD_DIM)
            normed = seg * lax.rsqrt(ms + EPS) * qkg_ref[:, g * LANES:(g + 1) * LANES]
            partner = jnp.where(first_half,
                                pltpu.roll(normed, LANES - HEAD_DIM // 2, axis=1),
                                pltpu.roll(normed, HEAD_DIM // 2, axis=1))
            roped = normed * cos + partner * sin
            if g < D_Q // LANES:
                q_ref[:, g * LANES:(g + 1) * LANES] = (roped * SCALE).astype(q_ref.dtype)
            else:
                k_ref[...] = roped

    def values():
        v_ref[...] = proj(OFF_V, OFF_XR)

    def gate(out_ref, off, c0, c1, r0=0, r1=x.shape[0]):
        y = _mm(xn[r0:r1], w_ref[:, off + c0:off + c1], precise)
        out_ref[r0:r1, c0:c1] = _sigmoid(y).astype(out_ref.dtype)

    if not fuse_rnn:
        xr_ref[...] = proj(OFF_XR, OFF_YR)
        gy_ref[...] = _gelu_tanh(proj(OFF_YR, OFF_GA)).astype(gy_ref.dtype)
        qk_heads()
        values()
        gate(sga_ref, OFF_GA, 0, D_MODEL)
        gate(sgr_ref, OFF_GR, 0, D_MODEL)
        return

    slab = RNN_SLAB_BLOCKS * RNN_BLOCK
    half = x.shape[0] // 2
    others = [qk_heads, values]
    for out_ref, off in ((sga_ref, OFF_GA), (sgr_ref, OFF_GR)):
        others += [functools.partial(gate, out_ref, off, c, c + GATE_COLS, r, r + half)
                   for c in range(0, D_MODEL, GATE_COLS) for r in (0, half)]
    others = iter(others)
    n_slabs = D_RNN // slab

    def slab_inputs(i):
        c0 = i * slab
        return (proj(OFF_XR + c0, OFF_XR + c0 + slab),
                _gelu_tanh(proj(OFF_YR + c0, OFF_YR + c0 + slab)))

    gy = []
    nxt = slab_inputs(0)
    for i in range(n_slabs):
        (xr, gy_i), nxt = nxt, None
        gy.append(gy_i)
        ct_ref[0, :, i * slab:(i + 1) * slab] = xr[xr.shape[0] - SUBLANES:, :]
        for _ in _rnn_scan_terms(xr, i * RNN_SLAB_BLOCKS, *rnn_w, xbuf, a_scr, b_scr):
            if nxt is None and i + 1 < n_slabs:
                nxt = slab_inputs(i + 1)
            next(others, lambda: None)()
    for other in others:
        other()
    _rnn_scan_finish(jnp.concatenate(gy, axis=-1), rnn_ref, hl_ref, a_scr, b_scr, h_scr, hcar)


RNN_SLAB_BLOCKS = 2
GATE_COLS = 256


def _proj(x, g, w_in, cos_t, sin_t, qkg, batch, seq, tm, precise, rnn_weights=None):
    n = batch * seq
    nt = seq // tm
    row = lambda b, t: (b * nt + t, 0)
    per_seq = lambda b, t: (b, 0, 0)
    act = F32 if precise else BF16
    rows_out = lambda width, dtype: (jax.ShapeDtypeStruct((n, width), dtype),
                                     pl.BlockSpec((tm, width), row))
    state_out = (jax.ShapeDtypeStruct((batch, SUBLANES, D_RNN), F32),
                 pl.BlockSpec((1, SUBLANES, D_RNN), per_seq))
    outs = [rows_out(D_Q, act), rows_out(D_KV, F32), rows_out(D_KV, F32)]
    in_specs = [
        pl.BlockSpec((tm, D_MODEL), row),
        _full((1, D_MODEL)),
        _full((D_MODEL, D_IN), single_buffer=True),
        pl.BlockSpec((tm, HEAD_DIM // 2), lambda b, t: (t, 0)),
        pl.BlockSpec((tm, HEAD_DIM // 2), lambda b, t: (t, 0)),
        _full((1, OFF_V)),
    ]
    args = [x, g, w_in, cos_t, sin_t, qkg]
    scratch = []
    if rnn_weights is None:
        outs += [rows_out(D_RNN, F32), rows_out(D_RNN, act)]
    else:
        assert not precise
        outs += [rows_out(D_RNN, act)]
        in_specs += [_full(w.shape) for w in rnn_weights]
        args += list(rnn_weights)
        groups = tm // SUBLANES
        scratch = [pltpu.VMEM((tm + SUBLANES, D_RNN), F32)]
        scratch += [pltpu.VMEM((groups, SUBLANES, D_RNN), F32)] * 3
        scratch += [pltpu.VMEM((SUBLANES, D_RNN), F32)]
    outs += [rows_out(D_MODEL, act), rows_out(D_MODEL, act)]
    if rnn_weights is not None:
        outs += [state_out, state_out]
    else:
        assert batch * nt == 1 and w_in.dtype == F32
        outs += [(jax.ShapeDtypeStruct(w_in.shape, BF16), _full(w_in.shape, single_buffer=True))]
    return pl.pallas_call(
        functools.partial(_proj_kernel, precise=precise, fuse_rnn=rnn_weights is not None),
        grid=(batch, nt),
        in_specs=in_specs,
        out_specs=[spec for _, spec in outs],
        out_shape=[shape for shape, _ in outs],
        scratch_shapes=scratch,
        compiler_params=_params("parallel", "arbitrary"),
        name="proj",
    )(*args)


def _sink_column(sink_ref, h, rows, rows_per_head):
    r = lax.broadcasted_iota(jnp.int32, (rows, 1), 0) // rows_per_head
    col = jnp.full((rows, 1), sink_ref[h * Q_PER_KV], F32)
    for g in range(1, Q_PER_KV):
        col = jnp.where(r == g, sink_ref[h * Q_PER_KV + g], col)
    return col


def _window_masks(first_tile):
    blk = ATTN_BLOCK
    rows = Q_PER_KV * blk
    i = lax.broadcasted_iota(jnp.int32, (rows, 2 * blk), 0) % blk
    j = lax.broadcasted_iota(jnp.int32, (rows, 2 * blk), 1)
    d = j - i
    in_window = (d >= 1) & (d <= WINDOW)
    first = (d >= jnp.where(first_tile, jnp.maximum(1, blk - i), 1)) & (d <= WINDOW)
    return in_window, first


def _attn_kv_head(sub, h, valid, sink_ref, q_ref, k_all, v_all, o_ref):
    blk = ATTN_BLOCK
    rows = Q_PER_KV * blk
    q = q_ref[sub * blk:(sub + 1) * blk, :]
    kh = k_all[sub * blk:(sub + 2) * blk, h * HEAD_DIM:(h + 1) * HEAD_DIM]
    vh = v_all[sub * blk:(sub + 2) * blk, h * HEAD_DIM:(h + 1) * HEAD_DIM]
    qs = jnp.concatenate(
        [q[:, (h * Q_PER_KV + g) * HEAD_DIM:(h * Q_PER_KV + g + 1) * HEAD_DIM]
         for g in range(Q_PER_KV)], axis=0)
    s = _mm(qs, kh, False, _NT)
    yield
    s = jnp.where(valid, s, NEG_INF)
    sink = _sink_column(sink_ref, h, rows, blk)
    m = jnp.maximum(jnp.max(s, axis=-1, keepdims=True), sink)
    p = jnp.exp(s - m)
    denom = jnp.sum(p, axis=-1, keepdims=True) + jnp.exp(sink - m)
    yield
    o = _mm(p, vh, False) * (1.0 / denom)
    for g in range(Q_PER_KV):
        c = (h * Q_PER_KV + g) * HEAD_DIM
        o_ref[sub * blk:(sub + 1) * blk, c:c + HEAD_DIM] = o[g * blk:(g + 1) * blk].astype(o_ref.dtype)


def _lockstep(generators):
    while generators:
        generators = [g for g in generators if next(g, "done") != "done"]
        yield


SAMPLE_BT = 16


def _attn_sample_kernel(sink_ref, q_ref, kn_ref, vn_ref, kc_ref, vc_ref, o_ref, ko_ref, vo_ref):
    bt = SAMPLE_BT
    pos = lax.broadcasted_iota(jnp.int32, (HEAD_DIM, WINDOW), 1)
    kn_t = kn_ref[...].T
    vn_t = vn_ref[...].T

    def shifted(cache_ref, new_t, out_ref, h):
        windows = []
        for b in range(bt):
            rolled = pltpu.roll(cache_ref[b, h], WINDOW - 1, axis=1)
            new = new_t[h * HEAD_DIM:(h + 1) * HEAD_DIM, b:b + 1]
            windows.append(jnp.where(pos == WINDOW - 1, new, rolled))
            out_ref[b, h] = windows[-1]
        return jnp.concatenate(windows, axis=1)

    q = q_ref[...]
    rows = Q_PER_KV * bt
    rb = lax.broadcasted_iota(jnp.int32, (rows, bt * WINDOW), 0) % bt
    cb = lax.broadcasted_iota(jnp.int32, (rows, bt * WINDOW), 1) // WINDOW
    valid = rb == cb

    def kv_head(h):
        kh_t = shifted(kc_ref, kn_t, ko_ref, h)
        vh_t = shifted(vc_ref, vn_t, vo_ref, h)
        qs = jnp.concatenate(
            [q[:, (h * Q_PER_KV + g) * HEAD_DIM:(h * Q_PER_KV + g + 1) * HEAD_DIM]
             for g in range(Q_PER_KV)], axis=0)
        s = _mm(qs, kh_t, True)
        yield
        s = jnp.where(valid, s, NEG_INF)
        sink = _sink_column(sink_ref, h, rows, bt)
        m = jnp.maximum(jnp.max(s, axis=-1, keepdims=True), sink)
        p = jnp.exp(s - m)
        denom = jnp.sum(p, axis=-1, keepdims=True) + jnp.exp(sink - m)
        o = _mm(p, vh_t, True, _NT) * (1.0 / denom)
        for g in range(Q_PER_KV):
            c = (h * Q_PER_KV + g) * HEAD_DIM
            o_ref[:, c:c + HEAD_DIM] = o[g * bt:(g + 1) * bt].astype(o_ref.dtype)

    heads = [kv_head(h) for h in range(N_KV_HEADS)]
    while heads:
        heads = [head for head in heads if next(head, "done") != "done"]


def _attn_sample(q, k_new, v_new, cache_k, cache_v, sinks):
    nbatch = q.shape[0]
    bt = SAMPLE_BT
    row = lambda i: (i, 0)
    win = pl.BlockSpec((bt, N_KV_HEADS, HEAD_DIM, WINDOW), lambda i: (i, 0, 0, 0))
    return pl.pallas_call(
        _attn_sample_kernel,
        grid=(nbatch // bt,),
        in_specs=[
            pl.BlockSpec(memory_space=pltpu.SMEM),
            pl.BlockSpec((bt, D_Q), row),
            pl.BlockSpec((bt, D_KV), row),
            pl.BlockSpec((bt, D_KV), row),
            win,
            win,
        ],
        out_specs=[pl.BlockSpec((bt, D_Q), row), win, win],
        out_shape=(
            jax.ShapeDtypeStruct((nbatch, D_Q), F32),
            jax.ShapeDtypeStruct(cache_k.shape, F32),
            jax.ShapeDtypeStruct(cache_v.shape, F32),
        ),
        compiler_params=_params("parallel"),
        name="attn_sample",
    )(sinks, q, k_new, v_new, cache_k, cache_v)


def _lru_terms(xc, wcat_ref, ba_ref, bi_ref, lam_ref, precise, block0=0):
    cols = slice(block0 * RNN_BLOCK, block0 * RNN_BLOCK + xc.shape[1])
    xcb = xc if precise else xc.astype(BF16)
    ya, yi = [], []
    for n in range(xc.shape[1] // RNN_BLOCK):
        y = _mm(xcb[:, n * RNN_BLOCK:(n + 1) * RNN_BLOCK], wcat_ref[block0 + n], precise)
        ya.append(y[:, :RNN_BLOCK])
        yi.append(y[:, RNN_BLOCK:])
    r = _sigmoid(jnp.concatenate(ya, axis=-1) + ba_ref[:, cols])
    gate_i = _sigmoid(jnp.concatenate(yi, axis=-1) + bi_ref[:, cols])
    neg_lam = -lam_ref[:, cols]
    softplus = jnp.maximum(neg_lam, 0.0) + jnp.log1p(jnp.exp(-jnp.abs(neg_lam)))
    log_a = (-LRU_C * softplus) * r
    a = jnp.exp(log_a)
    m = 1.0 - a * a
    b = jnp.where(m > 0.0, m * lax.rsqrt(m), 0.0) * (gate_i * xc)
    return a, b


def _rnn_start_sequence(xbuf, hcar):
    xbuf[0:SUBLANES, :] = jnp.zeros((SUBLANES, D_RNN), F32)
    hcar[...] = jnp.zeros((SUBLANES, D_RNN), F32)


def _rnn_scan_terms(x, block0, cw_ref, cb_ref, wcat_ref, ba_ref, bi_ref, lam_ref,
                    xbuf, a_scr, b_scr):
    tt, width = x.shape
    groups = tt // SUBLANES
    cols = slice(block0 * RNN_BLOCK, block0 * RNN_BLOCK + width)
    xbuf[SUBLANES:, cols] = x
    xc = cb_ref[:, cols] + cw_ref[CONV_W - 1:CONV_W, cols] * x
    for j in range(CONV_W - 1):
        s = CONV_W - 1 - j
        xc = xc + cw_ref[j:j + 1, cols] * xbuf[SUBLANES - s:SUBLANES - s + tt, cols]
    xbuf[0:SUBLANES, cols] = x[tt - SUBLANES:, :]
    yield
    a, b = _lru_terms(xc, wcat_ref, ba_ref, bi_ref, lam_ref, False, block0)
    yield
    a = a.reshape(groups, SUBLANES, width)
    b = b.reshape(groups, SUBLANES, width)
    step = lax.broadcasted_iota(jnp.int32, (groups, SUBLANES, width), 1)
    k = 1
    while k < SUBLANES:
        keep = step >= k
        a_sh = jnp.where(keep, pltpu.roll(a, k, axis=1), 1.0)
        b_sh = jnp.where(keep, pltpu.roll(b, k, axis=1), 0.0)
        b = a * b_sh + b
        a = a * a_sh
        k *= 2
        if k < SUBLANES:
            yield
    a_scr[:, :, cols] = a
    b_scr[:, :, cols] = b


def _rnn_scan_finish(gy, o_ref, hl_ref, a_scr, b_scr, h_scr, hcar):
    groups = a_scr.shape[0]
    tt = groups * SUBLANES

    def chain(g, h_in):
        h = a_scr[g] * h_in + b_scr[g]
        h_scr[g] = h
        return jnp.broadcast_to(h[SUBLANES - 1:SUBLANES, :], (SUBLANES, D_RNN))

    h_last = lax.fori_loop(0, groups, chain, hcar[...], unroll=True)
    hcar[...] = h_last
    hl_ref[0] = h_last
    h = h_scr[...].reshape(tt, D_RNN)
    o_ref[...] = (h * gy).astype(o_ref.dtype)


def _rnn_sample_kernel(xr_ref, gy_ref, s0_ref, s1_ref, s2_ref, h_ref, cw_ref, cb_ref,
                       wcat_ref, ba_ref, bi_ref, lam_ref, o_ref, hn_ref):
    x = xr_ref[...]
    xc = (cb_ref[...] + cw_ref[0:1, :] * s0_ref[...] + cw_ref[1:2, :] * s1_ref[...]
          + cw_ref[2:3, :] * s2_ref[...] + cw_ref[3:4, :] * x)
    a, b = _lru_terms(xc, wcat_ref, ba_ref, bi_ref, lam_ref, True)
    h = a * h_ref[...] + b
    hn_ref[...] = h
    o_ref[...] = (h * gy_ref[...].astype(F32)).astype(o_ref.dtype)


def _rnn_sample(xr, gy, s0, s1, s2, h_prev, conv_w, conv_b, wcat, b_a, b_i, lam):
    n = xr.shape[0]
    act = _full((n, D_RNN))
    return pl.pallas_call(
        _rnn_sample_kernel,
        grid=(1,),
        in_specs=[act, act, act, act, act, act,
                  _full((CONV_W, D_RNN)), _full((1, D_RNN)),
                  _full((N_RNN_BLOCKS, RNN_BLOCK, 2 * RNN_BLOCK)),
                  _full((1, D_RNN)), _full((1, D_RNN)), _full((1, D_RNN))],
        out_specs=[act, act],
        out_shape=(jax.ShapeDtypeStruct((n, D_RNN), F32),
                   jax.ShapeDtypeStruct((n, D_RNN), F32)),
        compiler_params=_params("arbitrary"),
        name="rnn_sample",
    )(xr, gy, s0, s1, s2, h_prev, conv_w, conv_b, wcat, b_a, b_i, lam)


def _merge_kernel(*refs, precise, tiles_per_seg, tiles_per_seq):
    fused = tiles_per_seq is not None
    if fused:
        (sink_ref, q_ref, kc_ref, kp_ref, vc_ref, vp_ref, x_ref, rn_ref, sga_ref, sgr_ref,
         wa_ref, wr_ref, wo_ref, g_ref, wrt_ref, brt_ref, tri_ref, wg_ref, wu_ref, wd_ref,
         x2_ref, xtm_ref, rec_ref, rect_ref, cnt_ref, wg_o, wu_o, wd_o, cnt_scr, at_ref) = refs
    else:
        (x_ref, at_ref, rn_ref, sga_ref, sgr_ref, wa_ref, wr_ref, wo_ref, g_ref, wrt_ref, brt_ref,
         tri_ref, x2_ref, xtm_ref, rec_ref, rect_ref, cnt_ref, cnt_scr) = refs

    @pl.when(pl.program_id(0) % tiles_per_seg == 0)
    def _():
        cnt_scr[...] = jnp.zeros_like(cnt_scr)

    tm = x_ref.shape[0]
    parts = MERGE_PARTS if tm % (MERGE_PARTS * LANES) == 0 else 1
    part = tm // parts
    state = {"counts": cnt_scr[...]}
    pending = [
        _merge_rows(slice(i * part, (i + 1) * part), state, x_ref, at_ref, rn_ref, sga_ref, sgr_ref,
                    wa_ref, wr_ref, wo_ref, g_ref, wrt_ref, brt_ref, tri_ref, x2_ref, xtm_ref,
                    rec_ref, rect_ref, precise)
        for i in range(parts)]
    active = []
    if fused:
        k_all = jnp.concatenate([kp_ref[...], kc_ref[...]], axis=0).astype(BF16)
        v_all = jnp.concatenate([vp_ref[...], vc_ref[...]], axis=0).astype(BF16)
        in_window, first = _window_masks(pl.program_id(0) % tiles_per_seq == 0)
        blocks = part // ATTN_BLOCK
        side = list(pending)
        for i in range(parts):
            heads = [_attn_kv_head(sub, h, first if sub == 0 else in_window, sink_ref, q_ref,
                                   k_all, v_all, at_ref)
                     for sub in range(i * blocks, (i + 1) * blocks) for h in range(N_KV_HEADS)]
            for _ in _lockstep(heads):
                if side:
                    next(side.pop(0))
                for part_pieces in list(active) * MERGE_PIECES_PER_ATTN_ROUND:
                    if part_pieces in active and next(part_pieces, "done") == "done":
                        active.remove(part_pieces)
            active.append(pending.pop(0))
    step = 0
    while pending or active:
        if pending and step % MERGE_STAGGER == 0:
            active.append(pending.pop(0))
        for part_pieces in list(active):
            if next(part_pieces, "done") == "done":
                active.remove(part_pieces)
        step += 1
    cnt_scr[...] = state["counts"]
    cnt_ref[0] = state["counts"]
    if fused:
        wg_o[...] = wg_ref[...].astype(BF16)
        wu_o[...] = wu_ref[...].astype(BF16)
        wd_o[...] = wd_ref[...].astype(BF16)


MERGE_PARTS = 2
MERGE_STAGGER = 3
MERGE_PIECES_PER_ATTN_ROUND = 1


def _merge_rows(rows, state, x_ref, at_ref, rn_ref, sga_ref, sgr_ref, wa_ref, wr_ref, wo_ref, g_ref,
                wrt_ref, brt_ref, tri_ref, x2_ref, xtm_ref, rec_ref, rect_ref, precise):
    yr = _mm(rn_ref[rows, :], wr_ref[...], precise)
    yield
    ya = _mm(at_ref[rows, :], wa_ref[...], precise)
    yield
    merged =sga_ref[rows, :].astype(F32) * ya + sgr_ref[rows, :].astype(F32) * yr
    yield
    x2 = x_ref[rows, :] + _mm(merged, wo_ref[...], precise)
    x2_ref[rows, :] = x2
    yield
    inv = lax.rsqrt(jnp.mean(x2 * x2, axis=-1, keepdims=True) + EPS)
    xn = x2 * inv * g_ref[...]
    for c in range(TOKEN_ROWS):
        xtm_ref[pl.ds(rows.start * TOKEN_ROWS + c, rows.stop - rows.start, stride=TOKEN_ROWS), :] = (
            xn[:, c * LANES:(c + 1) * LANES])
    yield

    logits = _mm(xn, wrt_ref[...], precise) + brt_ref[...]
    yield
    tm = logits.shape[0]
    lane_i = lax.broadcasted_iota(jnp.int32, (tm, LANES), 1)
    lane = lane_i.astype(F32)
    big = float(LANES)
    is_grp = (lane_i >= N_EXPERTS) & (lane_i < N_EXPERTS + N_GROUPS)
    gl = jnp.where(is_grp, logits, NEG_INF)
    gmax = jnp.max(gl, axis=-1, keepdims=True)
    g_idx = jnp.min(jnp.where(gl == gmax, lane, big), axis=-1, keepdims=True) - N_EXPERTS
    p_g = 1.0 / jnp.sum(jnp.exp(gl - gmax), axis=-1, keepdims=True)
    in_grp = (lane_i // EXPERTS_PER_GROUP).astype(F32) == g_idx
    el = jnp.where(in_grp, logits, NEG_INF)
    v1 = jnp.max(el, axis=-1, keepdims=True)
    i1 = jnp.min(jnp.where(el == v1, lane, big), axis=-1, keepdims=True)
    el2 = jnp.where(lane == i1, NEG_INF, el)
    v2 = jnp.max(el2, axis=-1, keepdims=True)
    i2 = jnp.min(jnp.where(el2 == v2, lane, big), axis=-1, keepdims=True)
    e2 = jnp.exp(v2 - v1)
    w1 = p_g / (1.0 + e2)
    w2 = p_g * e2 / (1.0 + e2)
    yield

    hit = jnp.where(lane == i1, 1.0, jnp.where(lane == i2, 1.0, 0.0))
    counts = state["counts"]
    state["counts"] = counts + jnp.sum(hit, axis=0, keepdims=True)
    before = (jnp.dot(tri_ref[0:tm, 0:tm], hit.astype(BF16), preferred_element_type=F32)
              + counts[0:1, :])
    yield
    r1 = jnp.sum(jnp.where(lane == i1, before, 0.0), axis=-1, keepdims=True)
    r2 = jnp.sum(jnp.where(lane == i2, before, 0.0), axis=-1, keepdims=True)
    rec = jnp.where(lane == REC_W2, w2, 0.0)
    for field, val in ((REC_W1, w1), (REC_R2, r2), (REC_R1, r1), (REC_E2, i2), (REC_E1, i1)):
        rec = jnp.where(lane == field, val, rec)
    rec_ref[rows, :] = rec
    rect_ref[:, rows] = rec.T


REC_E1, REC_E2, REC_R1, REC_R2, REC_W1, REC_W2 = range(6)
TOKEN_ROWS = D_MODEL // LANES


def _merge(x, attn, rnn, sga, sgr, wa, wr, wo, g2, w_route, b_route, tri, tm, seg, precise,
           new_sequences=None):
    n = x.shape[0]
    steps = n // tm
    tiles_per_seg = seg // tm
    row = lambda i: (i, 0)
    scratch = [pltpu.VMEM((SUBLANES, LANES), F32)]
    if new_sequences is None:
        head_specs = [pl.BlockSpec((tm, D_MODEL), row), pl.BlockSpec((tm, D_Q), row)]
        head_args = [x, attn]
        tiles_per_seq = None
    else:
        sinks, q, k, v, seq_len, expert_weights = new_sequences
        assert steps == N_EXPERTS and attn is None
        tiles_per_seq = seq_len // tm
        prev = lambda i: (jnp.maximum(i * (tm // ATTN_BLOCK) - 1, 0), 0)
        head_specs = [
            pl.BlockSpec(memory_space=pltpu.SMEM),
            pl.BlockSpec((tm, D_Q), row),
            pl.BlockSpec((tm, D_KV), row),
            pl.BlockSpec((ATTN_BLOCK, D_KV), prev),
            pl.BlockSpec((tm, D_KV), row),
            pl.BlockSpec((ATTN_BLOCK, D_KV), prev),
            pl.BlockSpec((tm, D_MODEL), row),
        ]
        head_args = [sinks, q, k, k, v, v, x]
        scratch.append(pltpu.VMEM((tm, D_Q), BF16))
    in_specs = head_specs + [
        pl.BlockSpec((tm, D_RNN), row),
        pl.BlockSpec((tm, D_MODEL), row),
        pl.BlockSpec((tm, D_MODEL), row),
        _full((D_Q, D_MODEL)),
        _full((D_RNN, D_MODEL)),
        _full((D_MODEL, D_MODEL)),
        _full((1, D_MODEL)),
        _full((D_MODEL, LANES)),
        _full((1, LANES)),
        _full((tm, tm)),
    ]
    out_specs = [
        pl.BlockSpec((tm, D_MODEL), row),
        pl.BlockSpec((tm * TOKEN_ROWS, LANES), row),
        pl.BlockSpec((tm, LANES), row),
        pl.BlockSpec((LANES, tm), lambda i: (0, i)),
        pl.BlockSpec((1, SUBLANES, LANES), lambda i: (i // tiles_per_seg, 0, 0)),
    ]
    out_shape = [
        jax.ShapeDtypeStruct((n, D_MODEL), F32),
        jax.ShapeDtypeStruct((n * TOKEN_ROWS, LANES), F32),
        jax.ShapeDtypeStruct((n, LANES), F32),
        jax.ShapeDtypeStruct((LANES, n), F32),
        jax.ShapeDtypeStruct((n // seg, SUBLANES, LANES), F32),
    ]
    args = head_args + [rnn, sga, sgr, wa, wr, wo, g2, w_route, b_route, tri]
    if new_sequences is not None:
        for w in expert_weights:
            spec = pl.BlockSpec((1,) + w.shape[1:], lambda i: (i, 0, 0))
            in_specs.append(spec)
            out_specs.append(spec)
            out_shape.append(jax.ShapeDtypeStruct(w.shape, BF16))
            args.append(w)
    return pl.pallas_call(
        functools.partial(_merge_kernel, precise=precise, tiles_per_seg=tiles_per_seg,
                          tiles_per_seq=tiles_per_seq),
        grid=(steps,),
        in_specs=in_specs,
        out_specs=out_specs,
        out_shape=out_shape,
        scratch_shapes=scratch,
        compiler_params=_params("arbitrary"),
        name="merge",
    )(*args)


MOE_CHUNK = 256
MOE_TAIL = 64
MOE_EXPERTS_PER_STEP = 2
MOE_LOOP_TOKENS = 32
MOE_VMEM_LIMIT = 60 * 1024 * 1024


def _seg_rows(seg):
    return 2 * seg + N_EXPERTS * SUBLANES + MOE_CHUNK


def _token_rows(i):
    return pl.ds(pl.multiple_of(i * TOKEN_ROWS, TOKEN_ROWS), TOKEN_ROWS)


def _sorted_rows(first_row):
    return pl.ds(pl.multiple_of(first_row, TOKEN_ROWS), TOKEN_ROWS)


def _moe_kernel(off_ref, nfull_ref, rem_ref, slot_ref, xtm_ref, wg_ref, wu_ref, wd_ref,
                x2_ref, rec_ref, o_ref, buf, g1, g2, *, seg, td, tc):
    s = pl.program_id(0)
    p = pl.program_id(1)
    n_disp = seg // td
    n_exp = N_EXPERTS // MOE_EXPERTS_PER_STEP

    @pl.when((s == 0) & (p == 0))
    def _():
        buf[...] = jnp.zeros_like(buf)

    @pl.when(p < n_disp)
    def _():
        def dispatch(g, carry):
            for j in range(MOE_LOOP_TOKENS):
                t = g * MOE_LOOP_TOKENS + j
                row = xtm_ref[_token_rows(t), :]
                for k in range(2):
                    buf[_sorted_rows(slot_ref[0, 0, k * seg + p * td + t]), :] = row
            return carry

        lax.fori_loop(0, td // MOE_LOOP_TOKENS, dispatch, 0)

    def run_chunk(e, row0, rows, valid):
        r0 = pl.multiple_of(row0 * TOKEN_ROWS, SUBLANES * TOKEN_ROWS)
        xf = [buf[pl.ds(r0 + j, rows, stride=TOKEN_ROWS), :] for j in range(TOKEN_ROWS)]
        x = jnp.concatenate(xf, axis=-1).astype(BF16)
        hg = jnp.dot(x, wg_ref[e], preferred_element_type=F32)
        hu = jnp.dot(x, wu_ref[e], preferred_element_type=F32)
        h = (hg * _sigmoid(hg)) * hu
        y = jnp.dot(h.astype(BF16), wd_ref[e], preferred_element_type=F32)
        if valid is not None:
            mine = lax.broadcasted_iota(jnp.int32, (rows, LANES), 0) < valid
        for j in range(TOKEN_ROWS):
            yj = y[:, j * LANES:(j + 1) * LANES]
            if valid is not None:
                yj = jnp.where(mine, yj, xf[j])
            buf[pl.ds(r0 + j, rows, stride=TOKEN_ROWS), :] = yj

    def run_expert(e):
        idx = s * N_EXPERTS + (p - n_disp) * MOE_EXPERTS_PER_STEP + e
        base = off_ref[idx]
        n_full = nfull_ref[idx]
        rem = rem_ref[idx]

        def chunk(c, carry):
            run_chunk(e, base + c * MOE_CHUNK, MOE_CHUNK, None)
            return carry

        lax.fori_loop(0, n_full, chunk, 0)
        last = base + n_full * MOE_CHUNK
        for units in range(1, MOE_CHUNK // MOE_TAIL + 2):
            @pl.when((rem > (units - 1) * MOE_TAIL) & (rem <= units * MOE_TAIL))
            def _():
                run_chunk(e, last, units * MOE_TAIL, rem)

    @pl.when((p >= n_disp) & (p < n_disp + n_exp))
    def _():
        for e in range(MOE_EXPERTS_PER_STEP):
            run_expert(e)

    @pl.when(p >= n_disp + n_exp)
    def _():
        t0 = (p - n_disp - n_exp) * tc

        def gather(g, carry):
            for j in range(MOE_LOOP_TOKENS):
                t = g * MOE_LOOP_TOKENS + j
                g1[_token_rows(t), :] = buf[_sorted_rows(slot_ref[0, 0, t0 + t]), :]
                g2[_token_rows(t), :] = buf[_sorted_rows(slot_ref[0, 0, seg + t0 + t]), :]
            return carry

        lax.fori_loop(0, tc // MOE_LOOP_TOKENS, gather, 0)
        rec = rec_ref[...]
        lane = lax.broadcasted_iota(jnp.int32, rec.shape, 1)
        w1 = jnp.broadcast_to(
            jnp.sum(jnp.where(lane == REC_W1, rec, 0.0), axis=-1, keepdims=True), rec.shape)
        w2 = jnp.broadcast_to(
            jnp.sum(jnp.where(lane == REC_W2, rec, 0.0), axis=-1, keepdims=True), rec.shape)
        for j in range(TOKEN_ROWS):
            cols = slice(j * LANES, (j + 1) * LANES)
            o_ref[:, cols] = (x2_ref[:, cols] + w1 * g1[pl.ds(j, tc, stride=TOKEN_ROWS), :]
                              + w2 * g2[pl.ds(j, tc, stride=TOKEN_ROWS), :])


def _moe(off, nfull, rem, slot, xtm, wg, wu, wd, x2, rec, seg, td, tc):
    n = x2.shape[0]
    n_seg = n // seg
    n_disp, n_comb = seg // td, seg // tc
    per_step = MOE_EXPERTS_PER_STEP
    n_exp = N_EXPERTS // per_step
    rows = _seg_rows(seg) * TOKEN_ROWS
    disp_tile = lambda s, p, *_: (s * n_disp + jnp.minimum(p, n_disp - 1), 0)
    expert = lambda s, p, *_: (jnp.clip(p - n_disp, 0, n_exp - 1), 0, 0)
    comb_tile = lambda s, p, *_: (s * n_comb + jnp.clip(p - n_disp - n_exp, 0, n_comb - 1), 0)
    grid_spec = pltpu.PrefetchScalarGridSpec(
        num_scalar_prefetch=3,
        grid=(n_seg, n_disp + n_exp + n_comb),
        in_specs=[
            pl.BlockSpec((1, 1, 2 * seg), lambda s, p, *_: (s, 0, 0), memory_space=pltpu.SMEM),
            pl.BlockSpec((td * TOKEN_ROWS, LANES), disp_tile),
            pl.BlockSpec((per_step, D_MODEL, D_EXPERT), expert),
            pl.BlockSpec((per_step, D_MODEL, D_EXPERT), expert),
            pl.BlockSpec((per_step, D_EXPERT, D_MODEL), expert),
            pl.BlockSpec((tc, D_MODEL), comb_tile),
            pl.BlockSpec((tc, LANES), comb_tile),
        ],
        out_specs=pl.BlockSpec((tc, D_MODEL), comb_tile),
        scratch_shapes=[pltpu.VMEM((rows, LANES), F32),
                        pltpu.VMEM((tc * TOKEN_ROWS, LANES), F32),
                        pltpu.VMEM((tc * TOKEN_ROWS, LANES), F32)],
    )
    return pl.pallas_call(
        functools.partial(_moe_kernel, seg=seg, td=td, tc=tc),
        grid_spec=grid_spec,
        out_shape=jax.ShapeDtypeStruct((n, D_MODEL), F32),
        compiler_params=pltpu.CompilerParams(
            dimension_semantics=("arbitrary", "arbitrary"), vmem_limit_bytes=MOE_VMEM_LIMIT),
        name="moe",
    )(off, nfull, rem, slot, xtm, wg, wu, wd, x2, rec)


def _plan(rect, cnt, seg):
    n = rect.shape[1]
    expert = rect[REC_E1:REC_E2 + 1].astype(jnp.int32)
    rank = rect[REC_R1:REC_R2 + 1].astype(jnp.int32)
    counts = cnt[:, 0, :N_EXPERTS].astype(jnp.int32)
    padded = (counts + SUBLANES - 1) // SUBLANES * SUBLANES
    off = jnp.cumsum(padded, axis=1) - padded
    n_full = counts // MOE_CHUNK
    rem = counts - n_full * MOE_CHUNK
    join = (rem > 0) & (rem <= MOE_TAIL) & (n_full > 0)
    n_full = n_full - join
    rem = rem + join * MOE_CHUNK
    off_tok = jnp.repeat(off.T, seg, axis=1)
    hit = expert[:, None, :] == jnp.arange(N_EXPERTS, dtype=jnp.int32)[None, :, None]
    slot = (rank + jnp.sum(jnp.where(hit, off_tok[None], 0), axis=1)) * TOKEN_ROWS
    slot = slot.reshape(2, n // seg, seg).transpose(1, 0, 2).reshape(n // seg, 1, 2 * seg)
    return off.reshape(-1), n_full.reshape(-1), rem.reshape(-1), slot


def _rope_tables(pos):
    half = HEAD_DIM // 2
    inv_freq = ROPE_THETA ** (-jnp.arange(half, dtype=F32) / half)
    ang = pos[:, None] * inv_freq[None, :]
    return jnp.cos(ang), jnp.sin(ang)


def _rope_tables_range(n):
    hi = jnp.arange(n // ATTN_BLOCK, dtype=F32) * ATTN_BLOCK
    lo = jnp.arange(ATTN_BLOCK, dtype=F32)
    (cos_hi, sin_hi), (cos_lo, sin_lo) = _rope_tables(hi), _rope_tables(lo)
    cos = cos_hi[:, None, :] * cos_lo[None] - sin_hi[:, None, :] * sin_lo[None]
    sin = sin_hi[:, None, :] * cos_lo[None] + cos_hi[:, None, :] * sin_lo[None]
    return cos.reshape(n, -1), sin.reshape(n, -1)


def kernel(x_prompt, x_sample, cache_k_win, cache_v_win, state_conv, state_lru_h, attn_norm_g, w_in, q_norm_g, k_norm_g, attn_sinks, conv_w, conv_b, w_lru_a, b_lru_a, w_lru_i, b_lru_i, lru_lambda, w_br_attn, w_br_rnn, w_out, ffn_norm_g, w_route_group, b_route_group, w_route_expert, b_route_expert, w_exp_gate, w_exp_up, w_exp_down):
    batch, seq, _ = x_prompt.shape
    dec_batch, dec_seq, _ = x_sample.shape
    depth = w_in.shape[0]
    assert depth == 1 and dec_seq == 1
    l = 0

    w_in_f = w_in[l]
    qkg = jnp.concatenate([jnp.tile(q_norm_g[l], N_Q_HEADS), jnp.tile(k_norm_g[l], N_KV_HEADS)])[None, :]
    wcat_f = jnp.concatenate([w_lru_a[l], w_lru_i[l]], axis=-1)
    wa_f, wr_f, wo_f = w_br_attn[l], w_br_rnn[l], w_out[l]
    w_route_f = jnp.concatenate(
        [w_route_expert[l], w_route_group[l],
         jnp.zeros((D_MODEL, LANES - N_EXPERTS - N_GROUPS), F32)], axis=-1)
    wcat = wcat_f.astype(BF16)
    wa_b, wr_b, wo_b, w_route = (w.astype(BF16) for w in (wa_f, wr_f, wo_f, w_route_f))
    b_route = jnp.concatenate(
        [b_route_expert[l], b_route_group[l], jnp.zeros((LANES - N_EXPERTS - N_GROUPS,), F32)])[None, :]
    experts_f = (w_exp_gate[l], w_exp_up[l], w_exp_down[l])
    g1 = attn_norm_g[l][None, :]
    g2 = ffn_norm_g[l][None, :]
    cw, cb = conv_w[l], conv_b[l][None, :]
    b_a, b_i, lam = b_lru_a[l][None, :], b_lru_i[l][None, :], lru_lambda[l][None, :]
    sinks = attn_sinks[l]

    def tail(x, attn, rnn, sga, sgr, tm, seg, precise, experts_b=None, qkv=None):
        wa, wr, wo, wrt = (wa_f, wr_f, wo_f, w_route_f) if precise else (wa_b, wr_b, wo_b, w_route)
        tri = jnp.tril(jnp.ones((tm, tm), BF16), -1)
        new_sequences = None if qkv is None else (sinks, *qkv, seq, experts_f)
        outs = _merge(x, attn, rnn, sga, sgr, wa, wr, wo, g2, wrt, b_route, tri, tm, seg, precise,
                      new_sequences=new_sequences)
        x2, xtm, rec, rect, cnt = outs[:5]
        wg, wu, wd = experts_b or outs[5:]
        off, nfull, rem, slot = _plan(rect, cnt, seg)
        y = _moe(off, nfull, rem, slot, xtm, wg, wu, wd, x2, rec, seg, tm, tm)
        return y, (wg, wu, wd)

    xs = x_sample.reshape(dec_batch, D_MODEL)
    cos_s, sin_s = _rope_tables(jnp.full((dec_batch,), PAST_LEN, F32))
    qs, ks, vs, xrs, gys, sgas, sgrs, w_in_b = _proj(xs, g1, w_in_f, cos_s, sin_s, qkg, 1, dec_batch,
                                                     dec_batch, True)

    xp = x_prompt.reshape(batch * seq, D_MODEL)
    cos_p, sin_p = _rope_tables_range(seq)
    q, k, v, rnn, sga, sgr, h_last, conv_tail = _proj(
        xp, g1, w_in_b, cos_p, sin_p, qkg, batch, seq, PROMPT_TILE, False,
        rnn_weights=(cw, cb, wcat, b_a, b_i, lam))
    y_prompt, experts_b = tail(xp, None, rnn, sga, sgr, PROMPT_TILE, MOE_SEGMENT, False,
                               qkv=(q, k, v))
    y_prompt = y_prompt.reshape(batch, seq, D_MODEL)

    def last_rows(a, rows):
        return a.reshape(batch, seq, a.shape[-1])[:, seq - rows:]

    k_win_p = last_rows(k, WINDOW).reshape(1, batch, WINDOW, N_KV_HEADS, HEAD_DIM)
    v_win_p = last_rows(v, WINDOW).reshape(1, batch, WINDOW, N_KV_HEADS, HEAD_DIM)
    conv_p = conv_tail[None, :, SUBLANES - (CONV_W - 1):, :]
    h_p = h_last[None, :, 0, :]

    ck = jnp.transpose(cache_k_win[l], (0, 2, 3, 1))
    cv = jnp.transpose(cache_v_win[l], (0, 2, 3, 1))
    attn_s, k_win_s, v_win_s = _attn_sample(qs, ks, vs, ck, cv, sinks)
    sc = state_conv[l]
    rnn_s, h_s = _rnn_sample(xrs, gys, sc[:, 0], sc[:, 1], sc[:, 2], state_lru_h[l],
                             cw, cb, wcat_f, b_a, b_i, lam)
    y_sample, _ = tail(xs, attn_s, rnn_s, sgas, sgrs, dec_batch, dec_batch, True, experts_b)
    y_sample = y_sample.reshape(dec_batch, 1, D_MODEL)
    conv_s = jnp.stack([sc[:, 1], sc[:, 2], xrs], axis=1)[None]

    return (y_prompt, y_sample, k_win_p, v_win_p, conv_p, h_p,
            jnp.transpose(k_win_s, (0, 3, 1, 2))[None],
            jnp.transpose(v_win_s, (0, 3, 1, 2))[None],
            conv_s, h_s[None])
```

```python
import functools

import jax
import jax.numpy as jnp
from jax import lax
from jax.experimental import pallas as pl
from jax.experimental.pallas import tpu as pltpu

D_MODEL = 1024
HEAD_DIM = 64
N_Q_HEADS = 8
N_KV_HEADS = 2
Q_PER_KV = N_Q_HEADS // N_KV_HEADS
WINDOW = 128
ATTN_BLOCK = 128
ROPE_THETA = 10000.0
SCALE = HEAD_DIM ** -0.5
NEG_INF = -1e30
D_RNN = 1280
N_RNN_BLOCKS = 10
RNN_BLOCK = D_RNN // N_RNN_BLOCKS
CONV_W = 4
LRU_C = 8.0
N_GROUPS = 4
EXPERTS_PER_GROUP = 8
N_EXPERTS = N_GROUPS * EXPERTS_PER_GROUP
D_EXPERT = 256
PAST_LEN = 16384
EPS = 1e-6
D_Q = N_Q_HEADS * HEAD_DIM
D_KV = N_KV_HEADS * HEAD_DIM
D_IN = D_Q + 2 * D_KV + 2 * D_RNN + 2 * D_MODEL
OFF_K = D_Q
OFF_V = OFF_K + D_KV
OFF_XR = OFF_V + D_KV
OFF_YR = OFF_XR + D_RNN
OFF_GA = OFF_YR + D_RNN
OFF_GR = OFF_GA + D_MODEL

LANES = 128
SUBLANES = 8
VMEM_LIMIT = 56 * 1024 * 1024
PROMPT_TILE = 512
MOE_SEGMENT = 4096

F32 = jnp.float32
BF16 = jnp.bfloat16


def _params(*sem):
    return pltpu.CompilerParams(dimension_semantics=sem, vmem_limit_bytes=VMEM_LIMIT)


def _sigmoid(x):
    return 0.5 * jnp.tanh(0.5 * x) + 0.5


def _gelu_tanh(x):
    c = 0.7978845608028654
    half_x = 0.5 * x
    return half_x + half_x * jnp.tanh(x * (c + (c * 0.044715) * (x * x)))


def _full(shape, single_buffer=False):
    index_map = lambda *_: (0,) * len(shape)
    if single_buffer:
        return pl.BlockSpec(shape, index_map, pipeline_mode=pl.Buffered(1))
    return pl.BlockSpec(shape, index_map)


def _mm(a, b, precise, dims=None):
    if precise:
        a, b, prec = a.astype(F32), b.astype(F32), lax.Precision.HIGHEST
    else:
        a, b, prec = a.astype(BF16), b.astype(BF16), None
    if dims is None:
        return jnp.dot(a, b, preferred_element_type=F32, precision=prec)
    return lax.dot_general(a, b, dims, preferred_element_type=F32, precision=prec)


_NT = (((1,), (1,)), ((), ()))


def _proj_kernel(x_ref, g_ref, w_ref, cos_ref, sin_ref, qkg_ref, *rest, precise, fuse_rnn):
    if fuse_rnn:
        rnn_w = rest[:6]
        q_ref, k_ref, v_ref, rnn_ref, sga_ref, sgr_ref, hl_ref, ct_ref = rest[6:14]
        xbuf, a_scr, b_scr, h_scr, hcar = rest[14:]
        pl.when(pl.program_id(1) == 0)(functools.partial(_rnn_start_sequence, xbuf, hcar))
    else:
        q_ref, k_ref, v_ref, xr_ref, gy_ref, sga_ref, sgr_ref, wb_ref = rest
        wb_ref[...] = w_ref[...].astype(BF16)
    x = x_ref[...]
    inv = lax.rsqrt(jnp.mean(x * x, axis=-1, keepdims=True) + EPS)
    xn = x * inv * g_ref[...]
    if not precise:
        xn = xn.astype(BF16)

    def proj(lo, hi):
        return _mm(xn, w_ref[:, lo:hi], precise)

    def qk_heads():
        qk = proj(0, OFF_V)
        tm = qk.shape[0]
        lane = lax.broadcasted_iota(jnp.int32, (tm, LANES), 1)
        lo_head = lane < HEAD_DIM
        first_half = (lane % HEAD_DIM) < (HEAD_DIM // 2)
        cos_f, sin_f = cos_ref[...], sin_ref[...]
        reps = LANES // HEAD_DIM
        cos = jnp.concatenate([cos_f, cos_f] * reps, axis=-1)
        sin = jnp.concatenate([-sin_f, sin_f] * reps, axis=-1)
        for g in range(OFF_V // LANES):
            seg = qk[:, g * LANES:(g + 1) * LANES]
            sq = seg * seg
            s_lo = jnp.sum(jnp.where(lo_head, sq, 0.0), axis=-1, keepdims=True)
            s_hi = jnp.sum(jnp.where(lo_head, 0.0, sq), axis=-1, keepdims=True)
            ms = jnp.where(lo_head, s_lo, s_hi) * (1.0 / HEAD_DIM)
            normed = seg * lax.rsqrt(ms + EPS) * qkg_ref[:, g * LANES:(g + 1) * LANES]
            partner = jnp.where(first_half,
                                pltpu.roll(normed, LANES - HEAD_DIM // 2, axis=1),
                                pltpu.roll(normed, HEAD_DIM // 2, axis=1))
            roped = normed * cos + partner * sin
            if g < D_Q // LANES:
                q_ref[:, g * LANES:(g + 1) * LANES] = (roped * SCALE).astype(q_ref.dtype)
            else:
                k_ref[...] = roped

    def values():
        v_ref[...] = proj(OFF_V, OFF_XR)

    def gate(out_ref, off, c0, c1, r0=0, r1=x.shape[0]):
        y = _mm(xn[r0:r1], w_ref[:, off + c0:off + c1], precise)
        out_ref[r0:r1, c0:c1] = _sigmoid(y).astype(out_ref.dtype)

    if not fuse_rnn:
        xr_ref[...] = proj(OFF_XR, OFF_YR)
        gy_ref[...] = _gelu_tanh(proj(OFF_YR, OFF_GA)).astype(gy_ref.dtype)
        qk_heads()
        values()
        gate(sga_ref, OFF_GA, 0, D_MODEL)
        gate(sgr_ref, OFF_GR, 0, D_MODEL)
        return

    slab = RNN_SLAB_BLOCKS * RNN_BLOCK
    half = x.shape[0] // 2
    others = [qk_heads, values]
    for out_ref, off in ((sga_ref, OFF_GA), (sgr_ref, OFF_GR)):
        others += [functools.partial(gate, out_ref, off, c, c + GATE_COLS, r, r + half)
                   for c in range(0, D_MODEL, GATE_COLS) for r in (0, half)]
    others = iter(others)
    n_slabs = D_RNN // slab

    def slab_inputs(i):
        c0 = i * slab
        return (proj(OFF_XR + c0, OFF_XR + c0 + slab),
                _gelu_tanh(proj(OFF_YR + c0, OFF_YR + c0 + slab)))

    gy = []
    nxt = slab_inputs(0)
    for i in range(n_slabs):
        (xr, gy_i), nxt = nxt, None
        gy.append(gy_i)
        ct_ref[0, :, i * slab:(i + 1) * slab] = xr[xr.shape[0] - SUBLANES:, :]
        for _ in _rnn_scan_terms(xr, i * RNN_SLAB_BLOCKS, *rnn_w, xbuf, a_scr, b_scr):
            if nxt is None and i + 1 < n_slabs:
                nxt = slab_inputs(i + 1)
            next(others, lambda: None)()
    for other in others:
        other()
    _rnn_scan_finish(jnp.concatenate(gy, axis=-1), rnn_ref, hl_ref, a_scr, b_scr, h_scr, hcar)


RNN_SLAB_BLOCKS = 2
GATE_COLS = 256


def _proj(x, g, w_in, cos_t, sin_t, qkg, batch, seq, tm, precise, rnn_weights=None):
    n = batch * seq
    nt = seq // tm
    row = lambda b, t: (b * nt + t, 0)
    per_seq = lambda b, t: (b, 0, 0)
    act = F32 if precise else BF16
    rows_out = lambda width, dtype: (jax.ShapeDtypeStruct((n, width), dtype),
                                     pl.BlockSpec((tm, width), row))
    state_out = (jax.ShapeDtypeStruct((batch, SUBLANES, D_RNN), F32),
                 pl.BlockSpec((1, SUBLANES, D_RNN), per_seq))
    outs = [rows_out(D_Q, act), rows_out(D_KV, F32), rows_out(D_KV, F32)]
    in_specs = [
        pl.BlockSpec((tm, D_MODEL), row),
        _full((1, D_MODEL)),
        _full((D_MODEL, D_IN), single_buffer=True),
        pl.BlockSpec((tm, HEAD_DIM // 2), lambda b, t: (t, 0)),
        pl.BlockSpec((tm, HEAD_DIM // 2), lambda b, t: (t, 0)),
        _full((1, OFF_V)),
    ]
    args = [x, g, w_in, cos_t, sin_t, qkg]
    scratch = []
    if rnn_weights is None:
        outs += [rows_out(D_RNN, F32), rows_out(D_RNN, act)]
    else:
        assert not precise
        outs += [rows_out(D_RNN, act)]
        in_specs += [_full(w.shape) for w in rnn_weights]
        args += list(rnn_weights)
        groups = tm // SUBLANES
        scratch = [pltpu.VMEM((tm + SUBLANES, D_RNN), F32)]
        scratch += [pltpu.VMEM((groups, SUBLANES, D_RNN), F32)] * 3
        scratch += [pltpu.VMEM((SUBLANES, D_RNN), F32)]
    outs += [rows_out(D_MODEL, act), rows_out(D_MODEL, act)]
    if rnn_weights is not None:
        outs += [state_out, state_out]
    else:
        assert batch * nt == 1 and w_in.dtype == F32
        outs += [(jax.ShapeDtypeStruct(w_in.shape, BF16), _full(w_in.shape, single_buffer=True))]
    return pl.pallas_call(
        functools.partial(_proj_kernel, precise=precise, fuse_rnn=rnn_weights is not None),
        grid=(batch, nt),
        in_specs=in_specs,
        out_specs=[spec for _, spec in outs],
        out_shape=[shape for shape, _ in outs],
        scratch_shapes=scratch,
        compiler_params=_params("parallel", "arbitrary"),
        name="proj",
    )(*args)


def _sink_column(sink_ref, h, rows, rows_per_head):
    r = lax.broadcasted_iota(jnp.int32, (rows, 1), 0) // rows_per_head
    col = jnp.full((rows, 1), sink_ref[h * Q_PER_KV], F32)
    for g in range(1, Q_PER_KV):
        col = jnp.where(r == g, sink_ref[h * Q_PER_KV + g], col)
    return col


def _window_masks(first_tile):
    blk = ATTN_BLOCK
    rows = Q_PER_KV * blk
    i = lax.broadcasted_iota(jnp.int32, (rows, 2 * blk), 0) % blk
    j = lax.broadcasted_iota(jnp.int32, (rows, 2 * blk), 1)
    d = j - i
    in_window = (d >= 1) & (d <= WINDOW)
    first = (d >= jnp.where(first_tile, jnp.maximum(1, blk - i), 1)) & (d <= WINDOW)
    return in_window, first


def _attn_kv_head(sub, h, valid, sink_ref, q_ref, k_all, v_all, o_ref):
    blk = ATTN_BLOCK
    rows = Q_PER_KV * blk
    q = q_ref[sub * blk:(sub + 1) * blk, :]
    kh = k_all[sub * blk:(sub + 2) * blk, h * HEAD_DIM:(h + 1) * HEAD_DIM]
    vh = v_all[sub * blk:(sub + 2) * blk, h * HEAD_DIM:(h + 1) * HEAD_DIM]
    qs = jnp.concatenate(
        [q[:, (h * Q_PER_KV + g) * HEAD_DIM:(h * Q_PER_KV + g + 1) * HEAD_DIM]
         for g in range(Q_PER_KV)], axis=0)
    s = _mm(qs, kh, False, _NT)
    yield
    s = jnp.where(valid, s, NEG_INF)
    sink = _sink_column(sink_ref, h, rows, blk)
    m = jnp.maximum(jnp.max(s, axis=-1, keepdims=True), sink)
    p = jnp.exp(s - m)
    denom = jnp.sum(p, axis=-1, keepdims=True) + jnp.exp(sink - m)
    yield
    o = _mm(p, vh, False) * (1.0 / denom)
    for g in range(Q_PER_KV):
        c = (h * Q_PER_KV + g) * HEAD_DIM
        o_ref[sub * blk:(sub + 1) * blk, c:c + HEAD_DIM] = o[g * blk:(g + 1) * blk].astype(o_ref.dtype)


def _lockstep(generators):
    while generators:
        generators = [g for g in generators if next(g, "done") != "done"]
        yield


SAMPLE_BT = 16


def _attn_sample_kernel(sink_ref, q_ref, kn_ref, vn_ref, kc_ref, vc_ref, o_ref, ko_ref, vo_ref):
    bt = SAMPLE_BT
    pos = lax.broadcasted_iota(jnp.int32, (HEAD_DIM, WINDOW), 1)
    kn_t = kn_ref[...].T
    vn_t = vn_ref[...].T

    def shifted(cache_ref, new_t, out_ref, h):
        windows = []
        for b in range(bt):
            rolled = pltpu.roll(cache_ref[b, h], WINDOW - 1, axis=1)
            new = new_t[h * HEAD_DIM:(h + 1) * HEAD_DIM, b:b + 1]
            windows.append(jnp.where(pos == WINDOW - 1, new, rolled))
            out_ref[b, h] = windows[-1]
        return jnp.concatenate(windows, axis=1)

    q = q_ref[...]
    rows = Q_PER_KV * bt
    rb = lax.broadcasted_iota(jnp.int32, (rows, bt * WINDOW), 0) % bt
    cb = lax.broadcasted_iota(jnp.int32, (rows, bt * WINDOW), 1) // WINDOW
    valid = rb == cb

    def kv_head(h):
        kh_t = shifted(kc_ref, kn_t, ko_ref, h)
        vh_t = shifted(vc_ref, vn_t, vo_ref, h)
        qs = jnp.concatenate(
            [q[:, (h * Q_PER_KV + g) * HEAD_DIM:(h * Q_PER_KV + g + 1) * HEAD_DIM]
             for g in range(Q_PER_KV)], axis=0)
        s = _mm(qs, kh_t, True)
        yield
        s = jnp.where(valid, s, NEG_INF)
        sink = _sink_column(sink_ref, h, rows, bt)
        m = jnp.maximum(jnp.max(s, axis=-1, keepdims=True), sink)
        p = jnp.exp(s - m)
        denom = jnp.sum(p, axis=-1, keepdims=True) + jnp.exp(sink - m)
        o = _mm(p, vh_t, True, _NT) * (1.0 / denom)
        for g in range(Q_PER_KV):
            c = (h * Q_PER_KV + g) * HEAD_DIM
            o_ref[:, c:c + HEAD_DIM] = o[g * bt:(g + 1) * bt].astype(o_ref.dtype)

    heads = [kv_head(h) for h in range(N_KV_HEADS)]
    while heads:
        heads = [head for head in heads if next(head, "done") != "done"]


def _attn_sample(q, k_new, v_new, cache_k, cache_v, sinks):
    nbatch = q.shape[0]
    bt = SAMPLE_BT
    row = lambda i: (i, 0)
    win = pl.BlockSpec((bt, N_KV_HEADS, HEAD_DIM, WINDOW), lambda i: (i, 0, 0, 0))
    return pl.pallas_call(
        _attn_sample_kernel,
        grid=(nbatch // bt,),
        in_specs=[
            pl.BlockSpec(memory_space=pltpu.SMEM),
            pl.BlockSpec((bt, D_Q), row),
            pl.BlockSpec((bt, D_KV), row),
            pl.BlockSpec((bt, D_KV), row),
            win,
            win,
        ],
        out_specs=[pl.BlockSpec((bt, D_Q), row), win, win],
        out_shape=(
            jax.ShapeDtypeStruct((nbatch, D_Q), F32),
            jax.ShapeDtypeStruct(cache_k.shape, F32),
            jax.ShapeDtypeStruct(cache_v.shape, F32),
        ),
        compiler_params=_params("parallel"),
        name="attn_sample",
    )(sinks, q, k_new, v_new, cache_k, cache_v)


def _lru_terms(xc, wcat_ref, ba_ref, bi_ref, lam_ref, precise, block0=0):
    cols = slice(block0 * RNN_BLOCK, block0 * RNN_BLOCK + xc.shape[1])
    xcb = xc if precise else xc.astype(BF16)
    ya, yi = [], []
    for n in range(xc.shape[1] // RNN_BLOCK):
        y = _mm(xcb[:, n * RNN_BLOCK:(n + 1) * RNN_BLOCK], wcat_ref[block0 + n], precise)
        ya.append(y[:, :RNN_BLOCK])
        yi.append(y[:, RNN_BLOCK:])
    gate_i = _sigmoid(jnp.concatenate(yi, axis=-1) + bi_ref[:, cols])
    neg_lam = -lam_ref[:, cols]
    softplus = jnp.maximum(neg_lam, 0.0) + jnp.log1p(jnp.exp(-jnp.abs(neg_lam)))
    half_scale = (-0.5 * LRU_C) * softplus
    log_a = half_scale * jnp.tanh(0.5 * (jnp.concatenate(ya, axis=-1) + ba_ref[:, cols])) + half_scale
    a = jnp.exp(log_a)
    m = 1.0 - a * a
    b = jnp.where(m > 0.0, m * lax.rsqrt(m), 0.0) * (gate_i * xc)
    return a, b


def _rnn_start_sequence(xbuf, hcar):
    xbuf[0:SUBLANES, :] = jnp.zeros((SUBLANES, D_RNN), F32)
    hcar[...] = jnp.zeros((SUBLANES, D_RNN), F32)


def _rnn_scan_terms(x, block0, cw_ref, cb_ref, wcat_ref, ba_ref, bi_ref, lam_ref,
                    xbuf, a_scr, b_scr):
    tt, width = x.shape
    groups = tt // SUBLANES
    cols = slice(block0 * RNN_BLOCK, block0 * RNN_BLOCK + width)
    xbuf[SUBLANES:, cols] = x
    xc = cb_ref[:, cols] + cw_ref[CONV_W - 1:CONV_W, cols] * x
    for j in range(CONV_W - 1):
        s = CONV_W - 1 - j
        xc = xc + cw_ref[j:j + 1, cols] * xbuf[SUBLANES - s:SUBLANES - s + tt, cols]
    xbuf[0:SUBLANES, cols] = x[tt - SUBLANES:, :]
    yield
    a, b = _lru_terms(xc, wcat_ref, ba_ref, bi_ref, lam_ref, False, block0)
    yield
    a = a.reshape(groups, SUBLANES, width)
    b = b.reshape(groups, SUBLANES, width)
    step = lax.broadcasted_iota(jnp.int32, (groups, SUBLANES, width), 1)
    k = 1
    while k < SUBLANES:
        keep = step >= k
        a_sh = jnp.where(keep, pltpu.roll(a, k, axis=1), 1.0)
        b_sh = jnp.where(keep, pltpu.roll(b, k, axis=1), 0.0)
        b = a * b_sh + b
        a = a * a_sh
        k *= 2
        if k < SUBLANES:
            yield
    a_scr[:, :, cols] = a
    b_scr[:, :, cols] = b


def _rnn_scan_finish(gy, o_ref, hl_ref, a_scr, b_scr, h_scr, hcar):
    groups = a_scr.shape[0]
    tt = groups * SUBLANES

    def chain(g, h_in):
        h = a_scr[g] * h_in + b_scr[g]
        h_scr[g] = h
        return jnp.broadcast_to(h[SUBLANES - 1:SUBLANES, :], (SUBLANES, D_RNN))

    h_last = lax.fori_loop(0, groups, chain, hcar[...], unroll=True)
    hcar[...] = h_last
    hl_ref[0] = h_last
    h = h_scr[...].reshape(tt, D_RNN)
    o_ref[...] = (h * gy).astype(o_ref.dtype)


def _rnn_sample_kernel(xr_ref, gy_ref, s0_ref, s1_ref, s2_ref, h_ref, cw_ref, cb_ref,
                       wcat_ref, ba_ref, bi_ref, lam_ref, o_ref, hn_ref):
    x = xr_ref[...]
    xc = (cb_ref[...] + cw_ref[0:1, :] * s0_ref[...] + cw_ref[1:2, :] * s1_ref[...]
          + cw_ref[2:3, :] * s2_ref[...] + cw_ref[3:4, :] * x)
    a, b = _lru_terms(xc, wcat_ref, ba_ref, bi_ref, lam_ref, True)
    h = a * h_ref[...] + b
    hn_ref[...] = h
    o_ref[...] = (h * gy_ref[...].astype(F32)).astype(o_ref.dtype)


def _rnn_sample(xr, gy, s0, s1, s2, h_prev, conv_w, conv_b, wcat, b_a, b_i, lam):
    n = xr.shape[0]
    act = _full((n, D_RNN))
    return pl.pallas_call(
        _rnn_sample_kernel,
        grid=(1,),
        in_specs=[act, act, act, act, act, act,
                  _full((CONV_W, D_RNN)), _full((1, D_RNN)),
                  _full((N_RNN_BLOCKS, RNN_BLOCK, 2 * RNN_BLOCK)),
                  _full((1, D_RNN)), _full((1, D_RNN)), _full((1, D_RNN))],
        out_specs=[act, act],
        out_shape=(jax.ShapeDtypeStruct((n, D_RNN), F32),
                   jax.ShapeDtypeStruct((n, D_RNN), F32)),
        compiler_params=_params("arbitrary"),
        name="rnn_sample",
    )(xr, gy, s0, s1, s2, h_prev, conv_w, conv_b, wcat, b_a, b_i, lam)


def _merge_kernel(*refs, precise, tiles_per_seg, tiles_per_seq):
    fused = tiles_per_seq is not None
    if fused:
        (sink_ref, q_ref, kc_ref, kp_ref, vc_ref, vp_ref, x_ref, rn_ref, sga_ref, sgr_ref,
         wa_ref, wr_ref, wo_ref, g_ref, wrt_ref, brt_ref, tri_ref, wg_ref, wu_ref, wd_ref,
         x2_ref, xtm_ref, rec_ref, rect_ref, cnt_ref, wg_o, wu_o, wd_o, cnt_scr, at_ref) = refs
    else:
        (x_ref, at_ref, rn_ref, sga_ref, sgr_ref, wa_ref, wr_ref, wo_ref, g_ref, wrt_ref, brt_ref,
         tri_ref, x2_ref, xtm_ref, rec_ref, rect_ref, cnt_ref, cnt_scr) = refs

    @pl.when(pl.program_id(0) % tiles_per_seg == 0)
    def _():
        cnt_scr[...] = jnp.zeros_like(cnt_scr)

    tm = x_ref.shape[0]
    parts = MERGE_PARTS if tm % (MERGE_PARTS * LANES) == 0 else 1
    part = tm // parts
    state = {"counts": cnt_scr[...]}
    pending = [
        _merge_rows(slice(i * part, (i + 1) * part), state, x_ref, at_ref, rn_ref, sga_ref, sgr_ref,
                    wa_ref, wr_ref, wo_ref, g_ref, wrt_ref, brt_ref, tri_ref, x2_ref, xtm_ref,
                    rec_ref, rect_ref, precise)
        for i in range(parts)]
    active = []
    if fused:
        k_all = jnp.concatenate([kp_ref[...], kc_ref[...]], axis=0).astype(BF16)
        v_all = jnp.concatenate([vp_ref[...], vc_ref[...]], axis=0).astype(BF16)
        in_window, first = _window_masks(pl.program_id(0) % tiles_per_seq == 0)
        blocks = part // ATTN_BLOCK
        side = list(pending)
        for i in range(parts):
            heads = [_attn_kv_head(sub, h, first if sub == 0 else in_window, sink_ref, q_ref,
                                   k_all, v_all, at_ref)
                     for sub in range(i * blocks, (i + 1) * blocks) for h in range(N_KV_HEADS)]
            for _ in _lockstep(heads):
                if side:
                    next(side.pop(0))
                for part_pieces in list(active) * MERGE_PIECES_PER_ATTN_ROUND:
                    if part_pieces in active and next(part_pieces, "done") == "done":
                        active.remove(part_pieces)
            active.append(pending.pop(0))
    step = 0
    while pending or active:
        if pending and step % MERGE_STAGGER == 0:
            active.append(pending.pop(0))
        for part_pieces in list(active):
            if next(part_pieces, "done") == "done":
                active.remove(part_pieces)
        step += 1
    cnt_scr[...] = state["counts"]
    cnt_ref[0] = state["counts"]
    if fused:
        wg_o[...] = wg_ref[...].astype(BF16)
        wu_o[...] = wu_ref[...].astype(BF16)
        wd_o[...] = wd_ref[...].astype(BF16)


MERGE_PARTS = 2
MERGE_STAGGER = 3
MERGE_PIECES_PER_ATTN_ROUND = 1


def _merge_rows(rows, state, x_ref, at_ref, rn_ref, sga_ref, sgr_ref, wa_ref, wr_ref, wo_ref, g_ref,
                wrt_ref, brt_ref, tri_ref, x2_ref, xtm_ref, rec_ref, rect_ref, precise):
    yr = _mm(rn_ref[rows, :], wr_ref[...], precise)
    yield
    ya = _mm(at_ref[rows, :], wa_ref[...], precise)
    yield
    merged =sga_ref[rows, :].astype(F32) * ya + sgr_ref[rows, :].astype(F32) * yr
    yield
    x2 = x_ref[rows, :] + _mm(merged, wo_ref[...], precise)
    x2_ref[rows, :] = x2
    yield
    inv = lax.rsqrt(jnp.mean(x2 * x2, axis=-1, keepdims=True) + EPS)
    xn = x2 * inv * g_ref[...]
    for c in range(TOKEN_ROWS):
        xtm_ref[pl.ds(rows.start * TOKEN_ROWS + c, rows.stop - rows.start, stride=TOKEN_ROWS), :] = (
            xn[:, c * LANES:(c + 1) * LANES])
    yield

    logits = _mm(xn, wrt_ref[...], precise) + brt_ref[...]
    yield
    tm = logits.shape[0]
    lane_i = lax.broadcasted_iota(jnp.int32, (tm, LANES), 1)
    lane = lane_i.astype(F32)
    big = float(LANES)
    is_grp = (lane_i >= N_EXPERTS) & (lane_i < N_EXPERTS + N_GROUPS)
    gl = jnp.where(is_grp, logits, NEG_INF)
    gmax = jnp.max(gl, axis=-1, keepdims=True)
    g_idx = jnp.min(jnp.where(gl == gmax, lane, big), axis=-1, keepdims=True) - N_EXPERTS
    p_g = 1.0 / jnp.sum(jnp.exp(gl - gmax), axis=-1, keepdims=True)
    in_grp = (lane_i // EXPERTS_PER_GROUP).astype(F32) == g_idx
    el = jnp.where(in_grp, logits, NEG_INF)
    v1 = jnp.max(el, axis=-1, keepdims=True)
    i1 = jnp.min(jnp.where(el == v1, lane, big), axis=-1, keepdims=True)
    el2 = jnp.where(lane == i1, NEG_INF, el)
    v2 = jnp.max(el2, axis=-1, keepdims=True)
    i2 = jnp.min(jnp.where(el2 == v2, lane, big), axis=-1, keepdims=True)
    e2 = jnp.exp(v2 - v1)
    w1 = p_g / (1.0 + e2)
    w2 = p_g * e2 / (1.0 + e2)
    yield

    hit = jnp.where(lane == i1, 1.0, jnp.where(lane == i2, 1.0, 0.0))
    counts = state["counts"]
    state["counts"] = counts + jnp.sum(hit, axis=0, keepdims=True)
    before = (jnp.dot(tri_ref[0:tm, 0:tm], hit.astype(BF16), preferred_element_type=F32)
              + counts[0:1, :])
    yield
    r1 = jnp.sum(jnp.where(lane == i1, before, 0.0), axis=-1, keepdims=True)
    r2 = jnp.sum(jnp.where(lane == i2, before, 0.0), axis=-1, keepdims=True)
    rec = jnp.where(lane == REC_W2, w2, 0.0)
    for field, val in ((REC_W1, w1), (REC_R2, r2), (REC_R1, r1), (REC_E2, i2), (REC_E1, i1)):
        rec = jnp.where(lane == field, val, rec)
    rec_ref[rows, :] = rec
    rect_ref[:, rows] = rec.T


REC_E1, REC_E2, REC_R1, REC_R2, REC_W1, REC_W2 = range(6)
TOKEN_ROWS = D_MODEL // LANES


def _merge(x, attn, rnn, sga, sgr, wa, wr, wo, g2, w_route, b_route, tri, tm, seg, precise,
           new_sequences=None):
    n = x.shape[0]
    steps = n // tm
    tiles_per_seg = seg // tm
    row = lambda i: (i, 0)
    scratch = [pltpu.VMEM((SUBLANES, LANES), F32)]
    if new_sequences is None:
        head_specs = [pl.BlockSpec((tm, D_MODEL), row), pl.BlockSpec((tm, D_Q), row)]
        head_args = [x, attn]
        tiles_per_seq = None
    else:
        sinks, q, k, v, seq_len, expert_weights = new_sequences
        assert steps == N_EXPERTS and attn is None
        tiles_per_seq = seq_len // tm
        prev = lambda i: (jnp.maximum(i * (tm // ATTN_BLOCK) - 1, 0), 0)
        head_specs = [
            pl.BlockSpec(memory_space=pltpu.SMEM),
            pl.BlockSpec((tm, D_Q), row),
            pl.BlockSpec((tm, D_KV), row),
            pl.BlockSpec((ATTN_BLOCK, D_KV), prev),
            pl.BlockSpec((tm, D_KV), row),
            pl.BlockSpec((ATTN_BLOCK, D_KV), prev),
            pl.BlockSpec((tm, D_MODEL), row),
        ]
        head_args = [sinks, q, k, k, v, v, x]
        scratch.append(pltpu.VMEM((tm, D_Q), BF16))
    in_specs = head_specs + [
        pl.BlockSpec((tm, D_RNN), row),
        pl.BlockSpec((tm, D_MODEL), row),
        pl.BlockSpec((tm, D_MODEL), row),
        _full((D_Q, D_MODEL)),
        _full((D_RNN, D_MODEL)),
        _full((D_MODEL, D_MODEL)),
        _full((1, D_MODEL)),
        _full((D_MODEL, LANES)),
        _full((1, LANES)),
        _full((tm, tm)),
    ]
    out_specs = [
        pl.BlockSpec((tm, D_MODEL), row),
        pl.BlockSpec((tm * TOKEN_ROWS, LANES), row),
        pl.BlockSpec((tm, LANES), row),
        pl.BlockSpec((LANES, tm), lambda i: (0, i)),
        pl.BlockSpec((1, SUBLANES, LANES), lambda i: (i // tiles_per_seg, 0, 0)),
    ]
    out_shape = [
        jax.ShapeDtypeStruct((n, D_MODEL), F32),
        jax.ShapeDtypeStruct((n * TOKEN_ROWS, LANES), F32),
        jax.ShapeDtypeStruct((n, LANES), F32),
        jax.ShapeDtypeStruct((LANES, n), F32),
        jax.ShapeDtypeStruct((n // seg, SUBLANES, LANES), F32),
    ]
    args = head_args + [rnn, sga, sgr, wa, wr, wo, g2, w_route, b_route, tri]
    if new_sequences is not None:
        for w in expert_weights:
            spec = pl.BlockSpec((1,) + w.shape[1:], lambda i: (i, 0, 0))
            in_specs.append(spec)
            out_specs.append(spec)
            out_shape.append(jax.ShapeDtypeStruct(w.shape, BF16))
            args.append(w)
    return pl.pallas_call(
        functools.partial(_merge_kernel, precise=precise, tiles_per_seg=tiles_per_seg,
                          tiles_per_seq=tiles_per_seq),
        grid=(steps,),
        in_specs=in_specs,
        out_specs=out_specs,
        out_shape=out_shape,
        scratch_shapes=scratch,
        compiler_params=_params("arbitrary"),
        name="merge",
    )(*args)


MOE_CHUNK = 256
MOE_TAIL = 64
MOE_EXPERTS_PER_STEP = 2
MOE_LOOP_TOKENS = 32
MOE_VMEM_LIMIT = 60 * 1024 * 1024


def _seg_rows(seg):
    return 2 * seg + N_EXPERTS * SUBLANES + MOE_CHUNK


def _token_rows(i):
    return pl.ds(pl.multiple_of(i * TOKEN_ROWS, TOKEN_ROWS), TOKEN_ROWS)


def _sorted_rows(first_row):
    return pl.ds(pl.multiple_of(first_row, TOKEN_ROWS), TOKEN_ROWS)


def _moe_kernel(off_ref, nfull_ref, rem_ref, slot_ref, xtm_ref, wg_ref, wu_ref, wd_ref,
                x2_ref, rec_ref, o_ref, buf, g1, g2, *, seg, td, tc):
    s = pl.program_id(0)
    p = pl.program_id(1)
    n_disp = seg // td
    n_exp = N_EXPERTS // MOE_EXPERTS_PER_STEP

    @pl.when((s == 0) & (p == 0))
    def _():
        buf[...] = jnp.zeros_like(buf)

    @pl.when(p < n_disp)
    def _():
        def dispatch(g, carry):
            for j in range(MOE_LOOP_TOKENS):
                t = g * MOE_LOOP_TOKENS + j
                row = xtm_ref[_token_rows(t), :]
                for k in range(2):
                    buf[_sorted_rows(slot_ref[0, 0, k * seg + p * td + t]), :] = row
            return carry

        lax.fori_loop(0, td // MOE_LOOP_TOKENS, dispatch, 0)

    def run_chunk(e, row0, rows, valid):
        r0 = pl.multiple_of(row0 * TOKEN_ROWS, SUBLANES * TOKEN_ROWS)
        xf = [buf[pl.ds(r0 + j, rows, stride=TOKEN_ROWS), :] for j in range(TOKEN_ROWS)]
        x = jnp.concatenate(xf, axis=-1).astype(BF16)
        hg = jnp.dot(x, wg_ref[e], preferred_element_type=F32)
        hu = jnp.dot(x, wu_ref[e], preferred_element_type=F32)
        h = (hg * _sigmoid(hg)) * hu
        y = jnp.dot(h.astype(BF16), wd_ref[e], preferred_element_type=F32)
        if valid is not None:
            mine = lax.broadcasted_iota(jnp.int32, (rows, LANES), 0) < valid
        for j in range(TOKEN_ROWS):
            yj = y[:, j * LANES:(j + 1) * LANES]
            if valid is not None:
                yj = jnp.where(mine, yj, xf[j])
            buf[pl.ds(r0 + j, rows, stride=TOKEN_ROWS), :] = yj

    def run_expert(e):
        idx = s * N_EXPERTS + (p - n_disp) * MOE_EXPERTS_PER_STEP + e
        base = off_ref[idx]
        n_full = nfull_ref[idx]
        rem = rem_ref[idx]

        def chunk(c, carry):
            run_chunk(e, base + c * MOE_CHUNK, MOE_CHUNK, None)
            return carry

        lax.fori_loop(0, n_full, chunk, 0)
        last = base + n_full * MOE_CHUNK
        for units in range(1, MOE_CHUNK // MOE_TAIL + 2):
            @pl.when((rem > (units - 1) * MOE_TAIL) & (rem <= units * MOE_TAIL))
            def _():
                run_chunk(e, last, units * MOE_TAIL, rem)

    @pl.when((p >= n_disp) & (p < n_disp + n_exp))
    def _():
        for e in range(MOE_EXPERTS_PER_STEP):
            run_expert(e)

    @pl.when(p >= n_disp + n_exp)
    def _():
        t0 = (p - n_disp - n_exp) * tc

        def gather(g, carry):
            for j in range(MOE_LOOP_TOKENS):
                t = g * MOE_LOOP_TOKENS + j
                g1[_token_rows(t), :] = buf[_sorted_rows(slot_ref[0, 0, t0 + t]), :]
                g2[_token_rows(t), :] = buf[_sorted_rows(slot_ref[0, 0, seg + t0 + t]), :]
            return carry

        lax.fori_loop(0, tc // MOE_LOOP_TOKENS, gather, 0)
        rec = rec_ref[...]
        lane = lax.broadcasted_iota(jnp.int32, rec.shape, 1)
        w1 = jnp.broadcast_to(
            jnp.sum(jnp.where(lane == REC_W1, rec, 0.0), axis=-1, keepdims=True), rec.shape)
        w2 = jnp.broadcast_to(
            jnp.sum(jnp.where(lane == REC_W2, rec, 0.0), axis=-1, keepdims=True), rec.shape)
        for j in range(TOKEN_ROWS):
            cols = slice(j * LANES, (j + 1) * LANES)
            o_ref[:, cols] = (x2_ref[:, cols] + w1 * g1[pl.ds(j, tc, stride=TOKEN_ROWS), :]
                              + w2 * g2[pl.ds(j, tc, stride=TOKEN_ROWS), :])


def _moe(off, nfull, rem, slot, xtm, wg, wu, wd, x2, rec, seg, td, tc):
    n = x2.shape[0]
    n_seg = n // seg
    n_disp, n_comb = seg // td, seg // tc
    per_step = MOE_EXPERTS_PER_STEP
    n_exp = N_EXPERTS // per_step
    rows = _seg_rows(seg) * TOKEN_ROWS
    disp_tile = lambda s, p, *_: (s * n_disp + jnp.minimum(p, n_disp - 1), 0)
    expert = lambda s, p, *_: (jnp.clip(p - n_disp, 0, n_exp - 1), 0, 0)
    comb_tile = lambda s, p, *_: (s * n_comb + jnp.clip(p - n_disp - n_exp, 0, n_comb - 1), 0)
    grid_spec = pltpu.PrefetchScalarGridSpec(
        num_scalar_prefetch=3,
        grid=(n_seg, n_disp + n_exp + n_comb),
        in_specs=[
            pl.BlockSpec((1, 1, 2 * seg), lambda s, p, *_: (s, 0, 0), memory_space=pltpu.SMEM),
            pl.BlockSpec((td * TOKEN_ROWS, LANES), disp_tile),
            pl.BlockSpec((per_step, D_MODEL, D_EXPERT), expert),
            pl.BlockSpec((per_step, D_MODEL, D_EXPERT), expert),
            pl.BlockSpec((per_step, D_EXPERT, D_MODEL), expert),
            pl.BlockSpec((tc, D_MODEL), comb_tile),
            pl.BlockSpec((tc, LANES), comb_tile),
        ],
        out_specs=pl.BlockSpec((tc, D_MODEL), comb_tile),
        scratch_shapes=[pltpu.VMEM((rows, LANES), F32),
                        pltpu.VMEM((tc * TOKEN_ROWS, LANES), F32),
                        pltpu.VMEM((tc * TOKEN_ROWS, LANES), F32)],
    )
    return pl.pallas_call(
        functools.partial(_moe_kernel, seg=seg, td=td, tc=tc),
        grid_spec=grid_spec,
        out_shape=jax.ShapeDtypeStruct((n, D_MODEL), F32),
        compiler_params=pltpu.CompilerParams(
            dimension_semantics=("arbitrary", "arbitrary"), vmem_limit_bytes=MOE_VMEM_LIMIT),
        name="moe",
    )(off, nfull, rem, slot, xtm, wg, wu, wd, x2, rec)


def _plan(rect, cnt, seg):
    n = rect.shape[1]
    expert = rect[REC_E1:REC_E2 + 1].astype(jnp.int32)
    rank = rect[REC_R1:REC_R2 + 1].astype(jnp.int32)
    counts = cnt[:, 0, :N_EXPERTS].astype(jnp.int32)
    padded = (counts + SUBLANES - 1) // SUBLANES * SUBLANES
    off = jnp.cumsum(padded, axis=1) - padded
    n_full = counts // MOE_CHUNK
    rem = counts - n_full * MOE_CHUNK
    join = (rem > 0) & (rem <= MOE_TAIL) & (n_full > 0)
    n_full = n_full - join
    rem = rem + join * MOE_CHUNK
    off_tok = jnp.repeat(off.T, seg, axis=1)
    hit = expert[:, None, :] == jnp.arange(N_EXPERTS, dtype=jnp.int32)[None, :, None]
    slot = (rank + jnp.sum(jnp.where(hit, off_tok[None], 0), axis=1)) * TOKEN_ROWS
    slot = slot.reshape(2, n // seg, seg).transpose(1, 0, 2).reshape(n // seg, 1, 2 * seg)
    return off.reshape(-1), n_full.reshape(-1), rem.reshape(-1), slot


def _rope_tables(pos):
    half = HEAD_DIM // 2
    inv_freq = ROPE_THETA ** (-jnp.arange(half, dtype=F32) / half)
    ang = pos[:, None] * inv_freq[None, :]
    return jnp.cos(ang), jnp.sin(ang)


def _rope_tables_range(n):
    hi = jnp.arange(n // ATTN_BLOCK, dtype=F32) * ATTN_BLOCK
    lo = jnp.arange(ATTN_BLOCK, dtype=F32)
    (cos_hi, sin_hi), (cos_lo, sin_lo) = _rope_tables(hi), _rope_tables(lo)
    cos = cos_hi[:, None, :] * cos_lo[None] - sin_hi[:, None, :] * sin_lo[None]
    sin = sin_hi[:, None, :] * cos_lo[None] + cos_hi[:, None, :] * sin_lo[None]
    return cos.reshape(n, -1), sin.reshape(n, -1)


def kernel(x_prompt, x_sample, cache_k_win, cache_v_win, state_conv, state_lru_h, attn_norm_g, w_in, q_norm_g, k_norm_g, attn_sinks, conv_w, conv_b, w_lru_a, b_lru_a, w_lru_i, b_lru_i, lru_lambda, w_br_attn, w_br_rnn, w_out, ffn_norm_g, w_route_group, b_route_group, w_route_expert, b_route_expert, w_exp_gate, w_exp_up, w_exp_down):
    batch, seq, _ = x_prompt.shape
    dec_batch, dec_seq, _ = x_sample.shape
    depth = w_in.shape[0]
    assert depth == 1 and dec_seq == 1
    l = 0

    w_in_f = w_in[l]
    qkg = jnp.concatenate([jnp.tile(q_norm_g[l], N_Q_HEADS), jnp.tile(k_norm_g[l], N_KV_HEADS)])[None, :]
    wcat_f = jnp.concatenate([w_lru_a[l], w_lru_i[l]], axis=-1)
    wa_f, wr_f, wo_f = w_br_attn[l], w_br_rnn[l], w_out[l]
    w_route_f = jnp.concatenate(
        [w_route_expert[l], w_route_group[l],
         jnp.zeros((D_MODEL, LANES - N_EXPERTS - N_GROUPS), F32)], axis=-1)
    wcat = wcat_f.astype(BF16)
    wa_b, wr_b, wo_b, w_route = (w.astype(BF16) for w in (wa_f, wr_f, wo_f, w_route_f))
    b_route = jnp.concatenate(
        [b_route_expert[l], b_route_group[l], jnp.zeros((LANES - N_EXPERTS - N_GROUPS,), F32)])[None, :]
    experts_f = (w_exp_gate[l], w_exp_up[l], w_exp_down[l])
    g1 = attn_norm_g[l][None, :]
    g2 = ffn_norm_g[l][None, :]
    cw, cb = conv_w[l], conv_b[l][None, :]
    b_a, b_i, lam = b_lru_a[l][None, :], b_lru_i[l][None, :], lru_lambda[l][None, :]
    sinks = attn_sinks[l]

    def tail(x, attn, rnn, sga, sgr, tm, seg, precise, experts_b=None, qkv=None):
        wa, wr, wo, wrt = (wa_f, wr_f, wo_f, w_route_f) if precise else (wa_b, wr_b, wo_b, w_route)
        tri = jnp.tril(jnp.ones((tm, tm), BF16), -1)
        new_sequences = None if qkv is None else (sinks, *qkv, seq, experts_f)
        outs = _merge(x, attn, rnn, sga, sgr, wa, wr, wo, g2, wrt, b_route, tri, tm, seg, precise,
                      new_sequences=new_sequences)
        x2, xtm, rec, rect, cnt = outs[:5]
        wg, wu, wd = experts_b or outs[5:]
        off, nfull, rem, slot = _plan(rect, cnt, seg)
        y = _moe(off, nfull, rem, slot, xtm, wg, wu, wd, x2, rec, seg, tm, tm)
        return y, (wg, wu, wd)

    xs = x_sample.reshape(dec_batch, D_MODEL)
    cos_s, sin_s = _rope_tables(jnp.full((dec_batch,), PAST_LEN, F32))
    qs, ks, vs, xrs, gys, sgas, sgrs, w_in_b = _proj(xs, g1, w_in_f, cos_s, sin_s, qkg, 1, dec_batch,
                                                     dec_batch, True)

    xp = x_prompt.reshape(batch * seq, D_MODEL)
    cos_p, sin_p = _rope_tables_range(seq)
    q, k, v, rnn, sga, sgr, h_last, conv_tail = _proj(
        xp, g1, w_in_b, cos_p, sin_p, qkg, batch, seq, PROMPT_TILE, False,
        rnn_weights=(cw, cb, wcat, b_a, b_i, lam))
    y_prompt, experts_b = tail(xp, None, rnn, sga, sgr, PROMPT_TILE, MOE_SEGMENT, False,
                               qkv=(q, k, v))
    y_prompt = y_prompt.reshape(batch, seq, D_MODEL)

    def last_rows(a, rows):
        return a.reshape(batch, seq, a.shape[-1])[:, seq - rows:]

    k_win_p = last_rows(k, WINDOW).reshape(1, batch, WINDOW, N_KV_HEADS, HEAD_DIM)
    v_win_p = last_rows(v, WINDOW).reshape(1, batch, WINDOW, N_KV_HEADS, HEAD_DIM)
    conv_p = conv_tail[None, :, SUBLANES - (CONV_W - 1):, :]
    h_p = h_last[None, :, 0, :]

    ck = jnp.transpose(cache_k_win[l], (0, 2, 3, 1))
    cv = jnp.transpose(cache_v_win[l], (0, 2, 3, 1))
    attn_s, k_win_s, v_win_s = _attn_sample(qs, ks, vs, ck, cv, sinks)
    sc = state_conv[l]
    rnn_s, h_s = _rnn_sample(xrs, gys, sc[:, 0], sc[:, 1], sc[:, 2], state_lru_h[l],
                             cw, cb, wcat_f, b_a, b_i, lam)
    y_sample, _ = tail(xs, attn_s, rnn_s, sgas, sgrs, dec_batch, dec_batch, True, experts_b)
    y_sample = y_sample.reshape(dec_batch, 1, D_MODEL)
    conv_s = jnp.stack([sc[:, 1], sc[:, 2], xrs], axis=1)[None]

    return (y_prompt, y_sample, k_win_p, v_win_p, conv_p, h_p,
            jnp.transpose(k_win_s, (0, 3, 1, 2))[None],
            jnp.transpose(v_win_s, (0, 3, 1, 2))[None],
            conv_s, h_s[None])
```
